```python
import math
import jax
import jax.numpy as jnp
from jax import lax
import numpy as np

D_MODEL = 1024
BATCH = 8
SEQ = 2048
DEPTH = 4
DEC_BATCH = 128
DEC_SEQ = 4
PAST_LEN = 16384
PAGE_SIZE = 128

F32 = jnp.float32
GROUP_W = D_MODEL // 4
CONV_K = 4
CHUNK = 128
EPS = 1e-6
LRU_W = GROUP_W
LRU_HEADS = 4
LRU_HD = LRU_W // LRU_HEADS
LRU_C = 8.0
S5_W = GROUP_W
S5_CH = 16
S5_G = S5_W // S5_CH
S5_N = 64
SSD_W = GROUP_W
SSD_HD = 64
SSD_H = SSD_W // SSD_HD
SSD_G = 2
SSD_N = 64
SSD_CONV_CH = SSD_W + 2 * SSD_G * SSD_N
RET_H = 4
RET_DK = 32
RET_DV = GROUP_W // RET_H
ROPE_BASE = 10000.0
N_GROUPS = 4
N_PER_GROUP = 4
N_EXPERTS = N_GROUPS * N_PER_GROUP
TOP_K_IN_GROUP = 2
D_EXPERT = D_MODEL // 4
N_MOD = 6
PROJ_SIZES = (LRU_W, LRU_W, S5_W, SSD_W, SSD_CONV_CH, SSD_H, RET_H * RET_DK, RET_H * RET_DK, GROUP_W, GROUP_W)
D_PROJ = sum(PROJ_SIZES)

kernel_name = 'hymba_hybrid_lru_s5_ssd_retention_hmoe_step'


def split_cols(x, sizes):
    out, start = [], 0
    for s in sizes:
        out.append(x[..., start:start + s])
        start += s
    return out


def rmsnorm(x, w=None):
    xf = x.astype(F32)
    y = xf * lax.rsqrt(jnp.mean(xf * xf, axis=-1, keepdims=True) + EPS)
    if w is not None:
        y = y * w.astype(F32)
    return y.astype(x.dtype)


def causal_conv(x, buf, w, b):
    L = x.shape[1]
    xp = jnp.concatenate([buf.astype(x.dtype), x], axis=1)
    y = b + w[0] * xp[:, 0:L]
    for k in range(1, CONV_K):
        y = y + w[k] * xp[:, k:k + L]
    return y, xp[:, xp.shape[1] - (CONV_K - 1):]


def combine_real(e1, e2):
    a1, b1 = e1
    a2, b2 = e2
    return a1 * a2, a2 * b1 + b2


def combine_complex(e1, e2):
    ar1, ai1, br1, bi1 = e1
    ar2, ai2, br2, bi2 = e2
    return (ar2 * ar1 - ai2 * ai1, ar2 * ai1 + ai2 * ar1,
            ar2 * br1 - ai2 * bi1 + br2, ar2 * bi1 + ai2 * br1 + bi2)


def rglru_mixer(xa, gate, buf, h0, conv_w, conv_b, wa, ba, wx, bx, lam, gain):
    B, L, _ = xa.shape
    xc, new_buf = causal_conv(xa, buf, conv_w, conv_b)
    xh = xc.reshape(B, L, LRU_HEADS, LRU_HD)
    r = jax.nn.sigmoid((jnp.einsum('blhi,hij->blhj', xh, wa).reshape(B, L, LRU_W) + ba).astype(F32))
    i = jax.nn.sigmoid((jnp.einsum('blhi,hij->blhj', xh, wx).reshape(B, L, LRU_W) + bx).astype(F32))
    log_a = -LRU_C * r * jax.nn.softplus(-lam.astype(F32))
    a = jnp.exp(log_a)
    b = jnp.sqrt(-jnp.expm1(2.0 * log_a)) * (i * xc.astype(F32))
    b = b.at[:, 0].add(a[:, 0] * h0.astype(F32))
    _, h = lax.associative_scan(combine_real, (a, b), axis=1)
    y = rmsnorm(h * jax.nn.gelu(gate.astype(F32)), gain)
    return y.astype(xa.dtype), new_buf, h[:, -1]


def s5_mixer(u, h0_re, h0_im, a_re, a_im, b_re, b_im, c_re, c_im, d, log_dt, w_glu, gain):
    B, L, _ = u.shape
    dt = jnp.exp(log_dt.astype(F32))[:, None]
    lr, li = a_re.astype(F32), a_im.astype(F32)
    mag = jnp.exp(lr * dt)
    ab_re, ab_im = mag * jnp.cos(li * dt), mag * jnp.sin(li * dt)
    den = lr * lr + li * li
    q_re = ((ab_re - 1.0) * lr + ab_im * li) / den
    q_im = (ab_im * lr - (ab_re - 1.0) * li) / den
    br, bi = b_re.astype(F32), b_im.astype(F32)
    bb_re = q_re[..., None] * br - q_im[..., None] * bi
    bb_im = q_re[..., None] * bi + q_im[..., None] * br
    uf = u.astype(F32)
    ug = uf.reshape(B, L, S5_G, S5_CH)
    bu_re = jnp.einsum('blgc,gnc->blgn', ug, bb_re)
    bu_im = jnp.einsum('blgc,gnc->blgn', ug, bb_im)
    h0r, h0i = h0_re.astype(F32), h0_im.astype(F32)
    bu_re = bu_re.at[:, 0].add(ab_re * h0r - ab_im * h0i)
    bu_im = bu_im.at[:, 0].add(ab_re * h0i + ab_im * h0r)
    ar = jnp.broadcast_to(ab_re, bu_re.shape)
    ai = jnp.broadcast_to(ab_im, bu_re.shape)
    _, _, h_re, h_im = lax.associative_scan(combine_complex, (ar, ai, bu_re, bu_im), axis=1)
    y = (jnp.einsum('blgn,gcn->blgc', h_re, c_re.astype(F32))
         - jnp.einsum('blgn,gcn->blgc', h_im, c_im.astype(F32)))
    y = jax.nn.gelu(y.reshape(B, L, S5_W) + d.astype(F32) * uf)
    y = y * jax.nn.sigmoid(y @ w_glu.astype(F32))
    return rmsnorm(y, gain).astype(u.dtype), h_re[:, -1], h_im[:, -1]


def ssd_chunked(x, dt, a, bm, cm, h0):
    B, L, H, P = x.shape
    Q = CHUNK if L % CHUNK == 0 else L
    nc = L // Q
    rep = H // SSD_G
    bm = jnp.repeat(bm, rep, axis=2).reshape(B, nc, Q, H, SSD_N)
    cm = jnp.repeat(cm, rep, axis=2).reshape(B, nc, Q, H, SSD_N)
    x = x.reshape(B, nc, Q, H, P)
    dt = dt.reshape(B, nc, Q, H)
    a_cum = jnp.cumsum(dt * a, axis=2)
    seg = a_cum[:, :, :, None, :] - a_cum[:, :, None, :, :]
    causal = jnp.tril(jnp.ones((Q, Q), dtype=bool))[None, None, :, :, None]
    decay = jnp.exp(jnp.where(causal, seg, -jnp.inf))
    scores = jnp.einsum('bcihn,bcjhn->bcijh', cm, bm) * decay * dt[:, :, None, :, :]
    y_diag = jnp.einsum('bcijh,bcjhp->bcihp', scores, x)
    w_end = jnp.exp(a_cum[:, :, -1:, :] - a_cum) * dt
    chunk_states = jnp.einsum('bcjhn,bcjh,bcjhp->bchpn', bm, w_end, x)
    chunk_decay = jnp.exp(a_cum[:, :, -1, :])

    def step(h, inp):
        s_c, g_c = inp
        return g_c[:, :, None, None] * h + s_c, h

    h_last, h_in = lax.scan(step, h0, (jnp.moveaxis(chunk_states, 1, 0), jnp.moveaxis(chunk_decay, 1, 0)))
    h_in = jnp.moveaxis(h_in, 0, 1)
    y_off = jnp.einsum('bcihn,bchpn->bcihp', cm, h_in) * jnp.exp(a_cum)[..., None]
    return (y_diag + y_off).reshape(B, L, H, P), h_last


def ssd_mixer(z, xbc, dt_raw, buf, h0, conv_w, conv_b, dt_bias, a_log, d, gain):
    B, L, _ = xbc.shape
    xbc, new_buf = causal_conv(xbc, buf, conv_w, conv_b)
    xbc = jax.nn.silu(xbc.astype(F32))
    xs, bm, cm = split_cols(xbc, (SSD_W, SSD_G * SSD_N, SSD_G * SSD_N))
    x = xs.reshape(B, L, SSD_H, SSD_HD)
    dt = jax.nn.softplus(dt_raw.astype(F32) + dt_bias.astype(F32))
    a = -jnp.exp(a_log.astype(F32))
    y, h_last = ssd_chunked(x, dt, a, bm.reshape(B, L, SSD_G, SSD_N), cm.reshape(B, L, SSD_G, SSD_N),
                            h0.astype(F32))
    y = (y + d.astype(F32)[:, None] * x).reshape(B, L, SSD_W) * jax.nn.silu(z.astype(F32))
    return rmsnorm(y, gain).astype(z.dtype), new_buf, h_last


def rope(x, pos):
    half = x.shape[-1] // 2
    inv = ROPE_BASE ** (-jnp.arange(half, dtype=F32) / half)
    ang = pos.astype(F32)[:, None] * inv
    cos, sin = jnp.cos(ang)[None, :, None, :], jnp.sin(ang)[None, :, None, :]
    x1, x2 = x[..., :half], x[..., half:]
    return jnp.concatenate([x1 * cos - x2 * sin, x1 * sin + x2 * cos], axis=-1)


def retention_log_decay():
    return jnp.log1p(-jnp.exp2(-5.0 - jnp.arange(RET_H, dtype=F32)))


def retention_chunked(q, k, v, s0):
    B, L, H, DK = q.shape
    DV = v.shape[-1]
    Q = CHUNK if L % CHUNK == 0 else L
    nc = L // Q
    lg = retention_log_decay()
    q = q.reshape(B, nc, Q, H, DK)
    k = k.reshape(B, nc, Q, H, DK)
    v = v.reshape(B, nc, Q, H, DV)
    idx = jnp.arange(Q, dtype=F32)
    rel = idx[:, None] - idx[None, :]
    dmask = jnp.where((rel >= 0)[..., None], jnp.exp(jnp.maximum(rel, 0.0)[..., None] * lg), 0.0)
    scores = jnp.einsum('bcihd,bcjhd->bcijh', q, k) * dmask
    intra = jnp.einsum('bcijh,bcjhe->bcihe', scores, v)
    k_dec = k * jnp.exp((Q - 1.0 - idx)[:, None] * lg)[:, :, None]
    kv = jnp.einsum('bcjhd,bcjhe->bchde', k_dec, v)
    chunk_decay = jnp.exp(Q * lg)[:, None, None]

    def step(s, kv_c):
        return chunk_decay * s + kv_c, s

    s_last, s_in = lax.scan(step, s0, jnp.moveaxis(kv, 1, 0))
    s_in = jnp.moveaxis(s_in, 0, 1)
    q_dec = q * jnp.exp((idx + 1.0)[:, None] * lg)[:, :, None]
    cross = jnp.einsum('bcihd,bchde->bcihe', q_dec, s_in)
    return (intra + cross).reshape(B, L, H, DV), s_last


def retention_mixer(q, k, v, g, s0, pos0, gain):
    B, L, _ = q.shape
    pos = pos0 + jnp.arange(L)
    q = rope(q.astype(F32).reshape(B, L, RET_H, RET_DK), pos)
    k = rope(k.astype(F32).reshape(B, L, RET_H, RET_DK), pos) * (RET_DK ** -0.5)
    v = v.astype(F32).reshape(B, L, RET_H, RET_DV)
    o, s_last = retention_chunked(q, k, v, s0.astype(F32))
    o = rmsnorm(o, gain.reshape(RET_H, RET_DV)).reshape(B, L, GROUP_W)
    return (jax.nn.silu(g.astype(F32)) * o).astype(g.dtype), s_last


def token_mixers(h, pos0, lru_buf, lru_h0, s5_h0_re, s5_h0_im, ssd_buf, ssd_h0, ret_s0,
                 w_in, lru_conv_w, lru_conv_b, lru_wa, lru_ba, lru_wx, lru_bx, lru_lambda,
                 s5_a_re, s5_a_im, s5_b_re, s5_b_im, s5_c_re, s5_c_im, s5_d, s5_log_dt, s5_w_glu,
                 ssd_conv_w, ssd_conv_b, ssd_dt_bias, ssd_a_log, ssd_d, mix_norm, w_out):
    proj = h @ w_in
    xa, ga, ub, zc, xbc, dtc, qd, kd, vd, gd = split_cols(proj, PROJ_SIZES)
    gn_a, gn_b, gn_c, gn_d = split_cols(mix_norm, (GROUP_W,) * 4)
    ya, lru_buf_n, lru_h_n = rglru_mixer(xa, ga, lru_buf, lru_h0, lru_conv_w, lru_conv_b,
                                         lru_wa, lru_ba, lru_wx, lru_bx, lru_lambda, gn_a)
    yb, s5_re_n, s5_im_n = s5_mixer(ub, s5_h0_re, s5_h0_im, s5_a_re, s5_a_im, s5_b_re, s5_b_im,
                                    s5_c_re, s5_c_im, s5_d, s5_log_dt, s5_w_glu, gn_b)
    yc, ssd_buf_n, ssd_h_n = ssd_mixer(zc, xbc, dtc, ssd_buf, ssd_h0, ssd_conv_w, ssd_conv_b,
                                       ssd_dt_bias, ssd_a_log, ssd_d, gn_c)
    yd, ret_n = retention_mixer(qd, kd, vd, gd, ret_s0, pos0, gn_d)
    out = jnp.concatenate([ya, yb, yc, yd], axis=-1).astype(h.dtype) @ w_out
    return out, (lru_buf_n, lru_h_n, s5_re_n, s5_im_n, ssd_buf_n, ssd_h_n, ret_n)


def hier_moe(h, w_rg, b_rg, w_re, b_re, w_g, w_u, w_d):
    B, L, D = h.shape
    t = h.reshape(B * L, D)
    lg = (t @ w_rg).astype(F32) + b_rg.astype(F32)
    pg = jax.nn.softmax(lg, axis=-1)
    onehot_g = jax.nn.one_hot(jnp.argmax(lg, axis=-1), N_GROUPS, dtype=F32)
    gate_g = jnp.sum(pg * onehot_g, axis=-1)
    le = jnp.einsum('td,dge->tge', t, w_re).astype(F32) + b_re.astype(F32)
    le = jnp.einsum('tge,tg->te', le, onehot_g)
    top_v, top_i = lax.top_k(le, TOP_K_IN_GROUP)
    top_w = jax.nn.softmax(top_v, axis=-1)
    w_grp = jnp.einsum('tk,tke->te', top_w, jax.nn.one_hot(top_i, N_PER_GROUP, dtype=F32))
    comb = (onehot_g[:, :, None] * w_grp[:, None, :] * gate_g[:, None, None]).reshape(B * L, N_EXPERTS)
    comb = comb.astype(t.dtype)
    out = jnp.zeros_like(t)
    for e in range(N_EXPERTS):
        he = jax.nn.silu(t @ w_g[e]) * (t @ w_u[e])
        out = out + comb[:, e:e + 1] * (he @ w_d[e])
    return out.reshape(B, L, D)


def trunk(x, c, states, pos0, ada_w, mix_w, moe_w, final_norm):
    w_ada, b_ada = ada_w
    new = tuple([] for _ in states)
    for l in range(DEPTH):
        mod = jax.nn.silu(c) @ w_ada[l] + b_ada[l]
        sh1, sc1, g1, sh2, sc2, g2 = [m[:, None, :] for m in jnp.split(mod, N_MOD, axis=-1)]
        h = rmsnorm(x) * (1.0 + sc1) + sh1
        mix, st = token_mixers(h, pos0, *[s[l] for s in states], *[w[l] for w in mix_w])
        x = x + g1 * mix
        h = rmsnorm(x) * (1.0 + sc2) + sh2
        x = x + g2 * hier_moe(h, *[w[l] for w in moe_w])
        for lst, s in zip(new, st):
            lst.append(s)
    return rmsnorm(x, final_norm), [jnp.stack(lst) for lst in new]


def setup_inputs(seed: int = 0) -> dict:
    key = jax.random.key(seed)
    ks = iter(jax.random.split(key, 64))
    D = D_MODEL

    def nrm(shape, scale=1.0):
        return scale * jax.random.normal(next(ks), shape, F32)

    def unif(shape, lo, hi):
        return jax.random.uniform(next(ks), shape, F32, lo, hi)

    x_prompt = nrm((BATCH, SEQ, D))
    x_sample = nrm((DEC_BATCH, DEC_SEQ, D))
    state_lru_conv = nrm((DEPTH, DEC_BATCH, CONV_K - 1, LRU_W))
    state_lru_h = nrm((DEPTH, DEC_BATCH, LRU_W), 0.5)
    state_s5_re = nrm((DEPTH, DEC_BATCH, S5_G, S5_N), 0.3)
    state_s5_im = nrm((DEPTH, DEC_BATCH, S5_G, S5_N), 0.3)
    state_ssd_conv = nrm((DEPTH, DEC_BATCH, CONV_K - 1, SSD_CONV_CH))
    state_ssd_h = nrm((DEPTH, DEC_BATCH, SSD_H, SSD_HD, SSD_N), 0.1)
    state_ret = nrm((DEPTH, DEC_BATCH, RET_H, RET_DK, RET_DV), 0.3)
    c_prompt = nrm((BATCH, D))
    c_sample = nrm((DEC_BATCH, D))
    a_pow = unif((DEPTH, LRU_W), 0.9, 0.999)
    a_base = a_pow ** (1.0 / LRU_C)
    dt_ssd = jnp.exp(unif((DEPTH, SSD_H), math.log(1e-3), math.log(1e-1)))
    return {
        'x_prompt': x_prompt, 'x_sample': x_sample,
        'state_lru_conv': state_lru_conv, 'state_lru_h': state_lru_h,
        'state_s5_re': state_s5_re, 'state_s5_im': state_s5_im,
        'state_ssd_conv': state_ssd_conv, 'state_ssd_h': state_ssd_h, 'state_ret': state_ret,
        'c_prompt': c_prompt, 'c_sample': c_sample,
        'w_ada': nrm((DEPTH, D, N_MOD * D), 0.5 * D ** -0.5),
        'b_ada': nrm((DEPTH, N_MOD * D), 0.02),
        'w_in': nrm((DEPTH, D, D_PROJ), D ** -0.5),
        'lru_conv_w': nrm((DEPTH, CONV_K, LRU_W), CONV_K ** -0.5),
        'lru_conv_b': nrm((DEPTH, LRU_W), 0.02),
        'lru_wa': nrm((DEPTH, LRU_HEADS, LRU_HD, LRU_HD), LRU_HD ** -0.5),
        'lru_ba': nrm((DEPTH, LRU_W), 0.02),
        'lru_wx': nrm((DEPTH, LRU_HEADS, LRU_HD, LRU_HD), LRU_HD ** -0.5),
        'lru_bx': nrm((DEPTH, LRU_W), 0.02),
        'lru_lambda': jnp.log(a_base) - jnp.log1p(-a_base),
        's5_a_re': -0.5 + nrm((DEPTH, S5_G, S5_N), 0.01),
        's5_a_im': jnp.pi * jnp.arange(S5_N, dtype=F32) + nrm((DEPTH, S5_G, S5_N), 0.01),
        's5_b_re': nrm((DEPTH, S5_G, S5_N, S5_CH), (2 * S5_CH) ** -0.5),
        's5_b_im': nrm((DEPTH, S5_G, S5_N, S5_CH), (2 * S5_CH) ** -0.5),
        's5_c_re': nrm((DEPTH, S5_G, S5_CH, S5_N), S5_N ** -0.5),
        's5_c_im': nrm((DEPTH, S5_G, S5_CH, S5_N), S5_N ** -0.5),
        's5_d': nrm((DEPTH, S5_W)),
        's5_log_dt': unif((DEPTH, S5_G), math.log(1e-3), math.log(1e-1)),
        's5_w_glu': nrm((DEPTH, S5_W, S5_W), S5_W ** -0.5),
        'ssd_conv_w': nrm((DEPTH, CONV_K, SSD_CONV_CH), CONV_K ** -0.5),
        'ssd_conv_b': nrm((DEPTH, SSD_CONV_CH), 0.02),
        'ssd_dt_bias': dt_ssd + jnp.log(-jnp.expm1(-dt_ssd)),
        'ssd_a_log': jnp.log(unif((DEPTH, SSD_H), 1.0, 16.0)),
        'ssd_d': 1.0 + nrm((DEPTH, SSD_H), 0.1),
        'mix_norm': 1.0 + nrm((DEPTH, D), 0.02),
        'w_out': nrm((DEPTH, D, D), D ** -0.5),
        'w_route_group': nrm((DEPTH, D, N_GROUPS), D ** -0.5),
        'b_route_group': nrm((DEPTH, N_GROUPS), 0.01),
        'w_route_exp': nrm((DEPTH, D, N_GROUPS, N_PER_GROUP), D ** -0.5),
        'b_route_exp': nrm((DEPTH, N_GROUPS, N_PER_GROUP), 0.01),
        'w_exp_gate': nrm((DEPTH, N_EXPERTS, D, D_EXPERT), D ** -0.5),
        'w_exp_up': nrm((DEPTH, N_EXPERTS, D, D_EXPERT), D ** -0.5),
        'w_exp_down': nrm((DEPTH, N_EXPERTS, D_EXPERT, D), D_EXPERT ** -0.5),
        'final_norm': 1.0 + nrm((D,), 0.02),
    }


def reference(x_prompt, x_sample, state_lru_conv, state_lru_h, state_s5_re, state_s5_im,
              state_ssd_conv, state_ssd_h, state_ret, c_prompt, c_sample,
              w_ada, b_ada, w_in, lru_conv_w, lru_conv_b, lru_wa, lru_ba, lru_wx, lru_bx, lru_lambda,
              s5_a_re, s5_a_im, s5_b_re, s5_b_im, s5_c_re, s5_c_im, s5_d, s5_log_dt, s5_w_glu,
              ssd_conv_w, ssd_conv_b, ssd_dt_bias, ssd_a_log, ssd_d, mix_norm, w_out,
              w_route_group, b_route_group, w_route_exp, b_route_exp,
              w_exp_gate, w_exp_up, w_exp_down, final_norm):
    states_s = (state_lru_conv, state_lru_h, state_s5_re, state_s5_im, state_ssd_conv, state_ssd_h, state_ret)
    bp = x_prompt.shape[0]
    states_p = tuple(jnp.zeros((DEPTH, bp) + s.shape[2:], x_prompt.dtype) for s in states_s)
    ada_w = (w_ada, b_ada)
    mix_w = (w_in, lru_conv_w, lru_conv_b, lru_wa, lru_ba, lru_wx, lru_bx, lru_lambda,
             s5_a_re, s5_a_im, s5_b_re, s5_b_im, s5_c_re, s5_c_im, s5_d, s5_log_dt, s5_w_glu,
             ssd_conv_w, ssd_conv_b, ssd_dt_bias, ssd_a_log, ssd_d, mix_norm, w_out)
    moe_w = (w_route_group, b_route_group, w_route_exp, b_route_exp, w_exp_gate, w_exp_up, w_exp_down)
    y_prompt, new_p = trunk(x_prompt, c_prompt, states_p, 0, ada_w, mix_w, moe_w, final_norm)
    y_sample, new_s = trunk(x_sample, c_sample, states_s, PAST_LEN, ada_w, mix_w, moe_w, final_norm)
    return (y_prompt, y_sample, new_p[0], new_s[0], new_p[1], new_s[1], new_p[2], new_s[2],
            new_p[3], new_s[3], new_p[4], new_s[4], new_p[5], new_s[5], new_p[6], new_s[6])
```

```python
import functools
import math

import numpy as np
import jax
import jax.numpy as jnp
from jax import lax
from jax.experimental import pallas as pl
from jax.experimental.pallas import tpu as pltpu

F32 = jnp.float32
MXU_DTYPE = jnp.bfloat16
HIGHEST = lax.Precision.HIGHEST

EPS = 1e-6
CONV_K = 4
LRU_HEADS = 4
LRU_C = 8.0
S5_CH = 16
S5_N = 64
SSD_HD = 64
SSD_G = 2
SSD_N = 64
RET_H = 4
RET_DK = 32
ROPE_BASE = 10000.0
N_GROUPS = 4
N_PER_GROUP = 4
N_EXPERTS = N_GROUPS * N_PER_GROUP
N_MOD = 6
PAST_LEN = 16384

LANES = 128
VMEM_LIMIT = 56 * 1024 * 1024


def _mm(a, b):
    return jnp.dot(a.astype(MXU_DTYPE), b.astype(MXU_DTYPE), preferred_element_type=F32)


def _mm_nt(a, b):
    return lax.dot_general(a.astype(MXU_DTYPE), b.astype(MXU_DTYPE), (((1,), (1,)), ((), ())),
                           preferred_element_type=F32)


def _mm_tn(a, b):
    return lax.dot_general(a.astype(MXU_DTYPE), b.astype(MXU_DTYPE), (((0,), (0,)), ((), ())),
                           preferred_element_type=F32)


def _rms(x):
    return x * lax.rsqrt(jnp.mean(x * x, axis=-1, keepdims=True) + EPS)


def _mod_body(c_ref, w_ref, b_ref, o_ref):
    c = c_ref[...]
    o_ref[...] = _mm(c * jax.nn.sigmoid(c), w_ref[...]) + b_ref[...]


def _modulation(c_all, w_ada, b_ada):
    depth, d, _ = w_ada.shape
    nb = c_all.shape[0]
    return pl.pallas_call(
        _mod_body,
        grid=(depth, N_MOD),
        in_specs=[
            pl.BlockSpec((nb, d), lambda l, k: (0, 0)),
            pl.BlockSpec((None, d, d), lambda l, k: (l, 0, k)),
            pl.BlockSpec((None, None, 1, d), lambda l, k: (l, k, 0, 0)),
        ],
        out_specs=pl.BlockSpec((None, None, nb, d), lambda l, k: (l, k, 0, 0)),
        out_shape=jax.ShapeDtypeStruct((depth, N_MOD, nb, d), F32),
        compiler_params=pltpu.CompilerParams(dimension_semantics=("arbitrary", "arbitrary"),
                                             vmem_limit_bytes=VMEM_LIMIT),
        name="modulation",
    )(c_all, w_ada, b_ada.reshape(depth, N_MOD, 1, d))


def _mod_spec(mod, k, tm, rows_per_seq):
    if mod.ndim == 5:
        tiles_per_seq = rows_per_seq // tm
        return pl.BlockSpec((None, None, None, 1, mod.shape[-1]),
                            lambda i, l: (l[0], k, i // tiles_per_seq, 0, 0))
    return pl.BlockSpec((None, None, tm, mod.shape[-1]), lambda i, l: (l[0], k, i, 0))


def _inproj_body(l_ref, x_ref, sc_ref, sh_ref, w_ref, o_ref):
    h = _rms(x_ref[...]) * (1.0 + sc_ref[...]) + sh_ref[...]
    o_ref[...] = _mm(h, w_ref[...])


def _inproj(l, x, mod, w_in, tm, rows_per_seq):
    t, d = x.shape
    dp = w_in.shape[-1]
    grid_spec = pltpu.PrefetchScalarGridSpec(
        num_scalar_prefetch=1,
        grid=(t // tm,),
        in_specs=[
            pl.BlockSpec((tm, d), lambda i, l: (i, 0)),
            _mod_spec(mod, 1, tm, rows_per_seq),
            _mod_spec(mod, 0, tm, rows_per_seq),
            pl.BlockSpec((None, d, dp), lambda i, l: (l[0], 0, 0)),
        ],
        out_specs=pl.BlockSpec((tm, dp), lambda i, l: (i, 0)),
    )
    return pl.pallas_call(
        _inproj_body,
        grid_spec=grid_spec,
        out_shape=jax.ShapeDtypeStruct((t, dp), F32),
        compiler_params=pltpu.CompilerParams(dimension_semantics=("arbitrary",), vmem_limit_bytes=VMEM_LIMIT),
        name="inproj",
    )(l, x, mod, mod, w_in)


V_LCB, V_BGATE, V_LAM, V_S5D, V_SCB, V_DTB, V_ALOG, V_SSDD, V_GAIN, V_LCW, V_SCW = 0, 1, 2, 3, 4, 5, 6, 7, 8, 9, 13
V_ROWS = 24

MIX_IN = ("proj", "trig", "tri", "dmask", "dec", "vec", "wgate", "bbm", "cre", "cim", "wglu", "pw",
          "lconv_i", "lh_i", "s5r_i", "s5i_i", "sconv_i", "sh_i", "ret_i")
MIX_OUT = ("y", "lconv_o", "lh_o", "s5r_o", "s5i_o", "sconv_o", "sh_o", "ret_o")


def _mix_body(l_ref, *refs, qt, ns, r, nlev):
    del l_ref
    g = dict(zip(MIX_IN + MIX_OUT, refs))
    gw = g["lh_i"].shape[-1]
    c = pl.program_id(1)

    @pl.when(c == 0)
    def _load_states():
        for nm in ("lconv", "lh", "s5r", "s5i", "sconv", "sh", "ret"):
            g[nm + "_o"][...] = g[nm + "_i"][...]

    proj = g["proj"]
    vec = g["vec"]
    row = lax.broadcasted_iota(jnp.int32, (qt, 1), 0)
    t = row & (r - 1)

    def down(x, d, fill=0.0):
        return jnp.where(t >= d, pltpu.roll(x, d, 0), fill)

    def up(x, d):
        return jnp.where(t + d < r, pltpu.roll(x, qt - d, 0), 0.0)

    def first_rows(prev):
        if ns == 1:
            prev = pltpu.roll(prev, 1, 0)
        return jnp.where(t == 0, prev, 0.0)

    def conv(xraw, buf_ref, w0, b):
        width = xraw.shape[-1]
        buf = buf_ref[...]
        acc = vec[b:b + 1, 0:width] + vec[w0 + 3:w0 + 4, 0:width] * xraw
        for m in range(1, CONV_K):
            prev = buf if m == 3 else pltpu.roll(buf, qt - (3 - m), 0)
            sh = jnp.where(t >= m, pltpu.roll(xraw, m, 0), prev)
            acc = acc + vec[w0 + 3 - m:w0 + 4 - m, 0:width] * sh
        buf_ref[...] = pltpu.roll(xraw, (qt - (r - 3)) % qt, 0)
        return acc

    if ns > 1:
        lane_seq = lax.broadcasted_iota(jnp.int32, (qt, ns * gw), 1) // gw
        row_seq = lax.broadcasted_iota(jnp.int32, (qt, ns * gw), 0) // r
        seqm = (lane_seq == row_seq).astype(F32)

    def expand(x):
        return x if ns == 1 else jnp.tile(x, (1, ns)) * seqm

    def fold(z):
        if ns == 1:
            return z
        z = z * seqm
        out = z[:, 0:gw]
        for s in range(1, ns):
            out = out + z[:, s * gw:(s + 1) * gw]
        return out

    def last_row_lanes(x):
        if ns == 1:
            return x[qt - 1:qt, :]
        return jnp.sum(jnp.where(t == r - 1, expand(x), 0.0), axis=0, keepdims=True)

    gain = vec[V_GAIN:V_GAIN + 1, :]

    xc = conv(proj[:, 0:gw], g["lconv_o"], V_LCW, V_LCB)
    pre = _mm(xc, g["wgate"][...]) + vec[V_BGATE:V_BGATE + 1, 0:2 * gw]
    rg = jax.nn.sigmoid(pre[:, 0:gw])
    ig = jax.nn.sigmoid(pre[:, gw:2 * gw])
    log_a = -LRU_C * rg * jax.nn.softplus(-vec[V_LAM:V_LAM + 1, 0:gw])
    a = jnp.exp(log_a)
    b = jnp.sqrt(-jnp.tanh(log_a) * (a * a + 1.0)) * (ig * xc)
    b = b + a * first_rows(g["lh_o"][...])
    for k in range(nlev):
        d = 1 << k
        b = a * down(b, d) + b
        a = a * down(a, d, 1.0)
    g["lh_o"][...] = b
    ya = _rms(b * jax.nn.gelu(proj[:, gw:2 * gw])) * gain[:, 0:gw]
    g["y"][:, 0:gw] = ya.astype(g["y"].dtype)

    sn = g["s5r_i"].shape[-1]
    u = proj[:, 2 * gw:3 * gw]
    bu = _mm(u, g["bbm"][...])
    pw = g["pw"]
    p_re, p_im = pw[0:1, :], pw[1:2, :]
    h0r, h0i = first_rows(g["s5r_o"][...]), first_rows(g["s5i_o"][...])
    hr = bu[:, 0:sn] + (p_re * h0r - p_im * h0i)
    hi = bu[:, sn:2 * sn] + (p_re * h0i + p_im * h0r)
    for k in range(nlev):
        d = 1 << k
        p_re, p_im = pw[2 * k:2 * k + 1, :], pw[2 * k + 1:2 * k + 2, :]
        sr, si = down(hr, d), down(hi, d)
        hr, hi = hr + (p_re * sr - p_im * si), hi + (p_re * si + p_im * sr)
    g["s5r_o"][...] = hr
    g["s5i_o"][...] = hi
    yb = _mm(hr, g["cre"][...]) - _mm(hi, g["cim"][...])
    yb = jax.nn.gelu(yb + vec[V_S5D:V_S5D + 1, 0:gw] * u)
    yb = yb * jax.nn.sigmoid(_mm(yb, g["wglu"][...]))
    g["y"][:, gw:2 * gw] = (_rms(yb) * gain[:, gw:2 * gw]).astype(g["y"].dtype)

    tri = g["tri"][...] > 0.0
    xbc = conv(proj[:, 4 * gw:6 * gw], g["sconv_o"], V_SCW, V_SCB)
    xbc = xbc * jax.nn.sigmoid(xbc)
    xs, bm, cm = xbc[:, 0:gw], xbc[:, gw:gw + LANES], xbc[:, gw + LANES:2 * gw]
    dt = jax.nn.softplus(proj[:, 9 * gw:10 * gw] + vec[V_DTB:V_DTB + 1, 0:gw])
    a_e = -jnp.exp(vec[V_ALOG:V_ALOG + 1, 0:gw])
    dta = dt * a_e
    acum = dta
    for k in range(nlev):
        acum = acum + down(acum, 1 << k)
    if ns == 1:
        alast = acum[qt - 1:qt, :]
    else:
        suf = dta
        for k in range(nlev):
            suf = suf + up(suf, 1 << k)
        alast = acum + suf - dta
    wend = jnp.exp(alast - acum) * dt
    acum_t = jnp.transpose(acum)
    lane = lax.broadcasted_iota(jnp.int32, (1, gw), 1)
    lane_b = lax.broadcasted_iota(jnp.int32, (1, LANES), 1)
    xdt = xs * dt
    ydiag = jnp.zeros((qt, gw), F32)
    for grp in range(SSD_G):
        gm = ((lane_b // SSD_N) == grp).astype(F32)
        cb = _mm_nt(cm * gm, bm)
        for hh in range(gw // SSD_HD // SSD_G):
            hd = grp * (gw // SSD_HD // SSD_G) + hh
            col = acum[:, hd * SSD_HD:hd * SSD_HD + 1]
            rowv = acum_t[hd * SSD_HD:hd * SSD_HD + 1, :]
            dec = jnp.where(tri, jnp.exp(col - rowv), 0.0)
            hm = ((lane // SSD_HD) == hd).astype(F32)
            ydiag = ydiag + _mm(cb * dec, xdt * hm)
    st = g["sh_o"][...]
    yoff = fold(_mm(cm, st)) * jnp.exp(acum)
    srow = lax.broadcasted_iota(jnp.int32, st.shape, 0) // SSD_N
    slane = (lax.broadcasted_iota(jnp.int32, st.shape, 1) % gw) // (SSD_HD * (gw // SSD_HD // SSD_G))
    upd = jnp.where(srow == slane, _mm_tn(bm, expand(xs * wend)), 0.0)
    g["sh_o"][...] = st * jnp.exp(last_row_lanes(acum)) + upd
    yc = (ydiag + yoff + vec[V_SSDD:V_SSDD + 1, 0:gw] * xs)
    z = proj[:, 3 * gw:4 * gw]
    yc = yc * (z * jax.nn.sigmoid(z))
    g["y"][:, 2 * gw:3 * gw] = (_rms(yc) * gain[:, 2 * gw:3 * gw]).astype(g["y"].dtype)

    trig = g["trig"]
    cos, sin_a, sin_b = trig[0], trig[1], trig[2]

    def rope(x):
        return x * cos + pltpu.roll(x, LANES - RET_DK // 2, 1) * sin_a + pltpu.roll(x, RET_DK // 2, 1) * sin_b

    q = rope(proj[:, 6 * gw:6 * gw + LANES])
    kk = rope(proj[:, 6 * gw + LANES:7 * gw]) * (RET_DK ** -0.5)
    v = proj[:, 7 * gw:8 * gw]
    dv = gw // RET_H
    intra = jnp.zeros((qt, gw), F32)
    for hd in range(RET_H):
        qm = ((lane_b // RET_DK) == hd).astype(F32)
        sc = _mm_nt(q * qm, kk) * g["dmask"][hd]
        vm = ((lane // dv) == hd).astype(F32)
        intra = intra + _mm(sc, v * vm)
    dec = g["dec"]
    rs = g["ret_o"][...]
    cross = fold(_mm(q * dec[0], rs))
    rrow = lax.broadcasted_iota(jnp.int32, rs.shape, 0) // RET_DK
    rlane = (lax.broadcasted_iota(jnp.int32, rs.shape, 1) % gw) // dv
    upd = jnp.where(rrow == rlane, _mm_tn(kk * dec[1], expand(v)), 0.0)
    g["ret_o"][...] = rs * dec[2][:, 0:1] + upd
    o = intra + cross
    o2 = o * o
    ms = jnp.zeros((qt, gw), F32)
    for hd in range(RET_H):
        vm = ((lane // dv) == hd).astype(F32)
        ms = ms + vm * jnp.sum(o2 * vm, axis=-1, keepdims=True)
    o = o * lax.rsqrt(ms * (1.0 / dv) + EPS) * gain[:, 3 * gw:4 * gw]
    gg = proj[:, 8 * gw:9 * gw]
    g["y"][:, 3 * gw:4 * gw] = (gg * jax.nn.sigmoid(gg) * o).astype(g["y"].dtype)


def _mix_consts(qt, ns, r, pos):
    idx = np.arange(qt)
    seq, tt = idx // r, idx % r
    same = seq[:, None] == seq[None, :]
    causal = same & (idx[:, None] >= idx[None, :])
    tri = causal.astype(np.float32)
    lg = np.log1p(-np.exp2(-5.0 - np.arange(RET_H, dtype=np.float64)))
    rel = (idx[:, None] - idx[None, :]).astype(np.float64)
    dmask = np.where(causal[None], np.exp(np.maximum(rel, 0.0)[None] * lg[:, None, None]), 0.0).astype(np.float32)
    lane_h = np.arange(LANES) // RET_DK
    qdec = np.exp((tt[:, None] + 1.0) * lg[lane_h][None, :])
    kdec = np.exp((r - 1.0 - tt[:, None]) * lg[lane_h][None, :])
    rdec = np.broadcast_to(np.exp(r * lg[lane_h])[:, None], (LANES, LANES))
    assert qt == LANES, "decay tables are stacked as (3, 128, 128)"
    dec = np.stack([qdec, kdec, rdec]).astype(np.float32)
    half = RET_DK // 2
    inv = ROPE_BASE ** (-jnp.arange(half, dtype=F32) / half)
    ang = pos.astype(F32)[:, None] * inv
    reps = LANES // half
    cos, sin = jnp.tile(jnp.cos(ang), (1, reps)), jnp.tile(jnp.sin(ang), (1, reps))
    first = (np.arange(LANES) % RET_DK) < half
    trig = jnp.stack([cos, jnp.where(first, -sin, 0.0), jnp.where(first, 0.0, sin)])
    return jnp.asarray(tri), jnp.asarray(dmask), jnp.asarray(dec), trig


def _mixer(l, proj, consts, mw, states, *, nb, nc, qt, ns, r):
    tri, dmask, dec, trig = consts
    nlev = int(math.log2(r))
    assert (1 << nlev) == r and r >= CONV_K
    dp = proj.shape[-1]
    d = mw["vec"].shape[-1]

    def layer_spec(a):
        return pl.BlockSpec((None,) + a.shape[1:], lambda b, c, l: (l[0],) + (0,) * (a.ndim - 1))

    def const_spec(a):
        return pl.BlockSpec(a.shape, lambda b, c, l: (0,) * a.ndim)

    def state_in_spec(a):
        return pl.BlockSpec((None, None) + a.shape[2:], lambda b, c, l: (l[0], b, 0, 0))

    def state_out_spec(a):
        return pl.BlockSpec((None,) + a.shape[2:], lambda b, c, l: (b, 0, 0))

    in_specs = [
        pl.BlockSpec((qt, dp), lambda b, c, l: (b * nc + c, 0)),
        pl.BlockSpec((3, qt, LANES), lambda b, c, l: (0, c, 0)),
        const_spec(tri), const_spec(dmask), const_spec(dec),
    ] + [layer_spec(mw[k]) for k in ("vec", "wgate", "bbm", "cre", "cim", "wglu", "pw")] \
      + [state_in_spec(s) for s in states]
    out_specs = [pl.BlockSpec((qt, d), lambda b, c, l: (b * nc + c, 0))] + [state_out_spec(s) for s in states]
    out_shape = [jax.ShapeDtypeStruct((nb * nc * qt, d), MXU_DTYPE)] + \
                [jax.ShapeDtypeStruct(s.shape[1:], F32) for s in states]
    grid_spec = pltpu.PrefetchScalarGridSpec(num_scalar_prefetch=1, grid=(nb, nc),
                                             in_specs=in_specs, out_specs=out_specs)
    return pl.pallas_call(
        functools.partial(_mix_body, qt=qt, ns=ns, r=r, nlev=nlev),
        grid_spec=grid_spec,
        out_shape=out_shape,
        compiler_params=pltpu.CompilerParams(dimension_semantics=("arbitrary", "arbitrary"),
                                             vmem_limit_bytes=VMEM_LIMIT),
        name="mixer",
    )(l, proj, trig, tri, dmask, dec, *[mw[k] for k in ("vec", "wgate", "bbm", "cre", "cim", "wglu", "pw")],
      *states)


def _post_body(l_ref, y_ref, x_ref, g1_ref, sc_ref, sh_ref, g2_ref, wo_ref, wr_ref, br_ref,
               wg_ref, wu_ref, wd_ref, o_ref, acc_ref):
    del l_ref
    x = x_ref[...] + g1_ref[...] * jnp.dot(y_ref[...], wo_ref[...], preferred_element_type=F32)
    h = _rms(x) * (1.0 + sc_ref[...]) + sh_ref[...]
    logits = jnp.dot(h, wr_ref[...], precision=HIGHEST, preferred_element_type=F32) + br_ref[...]
    col = [logits[:, k:k + 1] for k in range(N_GROUPS + N_EXPERTS)]

    def first_max(vals, allowed=None):
        neg = jnp.full_like(vals[0], -jnp.inf)
        cand = vals if allowed is None else [jnp.where(al > 0.0, v, neg) for v, al in zip(vals, allowed)]
        m = functools.reduce(jnp.maximum, cand)
        rem = jnp.ones_like(vals[0])
        hot = []
        for v in cand:
            f = jnp.where(v >= m, rem, 0.0)
            rem = rem - f
            hot.append(f)
        return hot, m

    grp, gmax = first_max(col[:N_GROUPS])
    gate = 1.0 / functools.reduce(lambda s, v: s + v, [jnp.exp(v - gmax) for v in col[:N_GROUPS]])
    le = [functools.reduce(lambda s, v: s + v,
                           [grp[gi] * col[N_GROUPS + gi * N_PER_GROUP + j] for gi in range(N_GROUPS)])
          for j in range(N_PER_GROUP)]
    top1, m1 = first_max(le)
    top2, m2 = first_max(le, [1.0 - f for f in top1])
    e2 = jnp.exp(m2 - m1)
    w1 = 1.0 / (1.0 + e2)
    w2 = e2 * w1
    wgrp = [top1[j] * w1 + top2[j] * w2 for j in range(N_PER_GROUP)]

    hb = h.astype(MXU_DTYPE)
    acc_ref[...] = jnp.zeros_like(acc_ref)
    for e in range(N_EXPERTS):
        comb = grp[e // N_PER_GROUP] * wgrp[e % N_PER_GROUP] * gate
        gt = jnp.dot(hb, wg_ref[e], preferred_element_type=F32)
        he = gt * jax.nn.sigmoid(gt) * jnp.dot(hb, wu_ref[e], preferred_element_type=F32)
        acc_ref[...] += jnp.dot((he * comb).astype(MXU_DTYPE), wd_ref[e], preferred_element_type=F32)
    o_ref[...] = x + g2_ref[...] * acc_ref[...]


def _post(l, y, x, mod, pw, tm, rows_per_seq):
    t, d = x.shape

    def layer_spec(a):
        return pl.BlockSpec((None,) + a.shape[1:], lambda i, l: (l[0],) + (0,) * (a.ndim - 1),
                            pipeline_mode=pl.Buffered(1))

    grid_spec = pltpu.PrefetchScalarGridSpec(
        num_scalar_prefetch=1,
        grid=(t // tm,),
        in_specs=[
            pl.BlockSpec((tm, d), lambda i, l: (i, 0)),
            pl.BlockSpec((tm, d), lambda i, l: (i, 0)),
            _mod_spec(mod, 2, tm, rows_per_seq),
            _mod_spec(mod, 4, tm, rows_per_seq),
            _mod_spec(mod, 3, tm, rows_per_seq),
            _mod_spec(mod, 5, tm, rows_per_seq),
        ] + [layer_spec(pw[k]) for k in ("w_out", "w_route", "b_route", "w_gate", "w_up", "w_down")],
        out_specs=pl.BlockSpec((tm, d), lambda i, l: (i, 0)),
        scratch_shapes=[pltpu.VMEM((tm, d), F32)],
    )
    return pl.pallas_call(
        _post_body,
        grid_spec=grid_spec,
        out_shape=jax.ShapeDtypeStruct((t, d), F32),
        compiler_params=pltpu.CompilerParams(dimension_semantics=("arbitrary",), vmem_limit_bytes=VMEM_LIMIT),
        name="post",
    )(l, y, x, mod, mod, mod, mod, *[pw[k] for k in ("w_out", "w_route", "b_route", "w_gate", "w_up", "w_down")])


def _final_body(x_ref, w_ref, o_ref):
    o_ref[...] = _rms(x_ref[...]) * w_ref[...]


def _final_norm(x, w, tm):
    t, d = x.shape
    return pl.pallas_call(
        _final_body,
        grid=(t // tm,),
        in_specs=[pl.BlockSpec((tm, d), lambda i: (i, 0)), pl.BlockSpec((1, d), lambda i: (0, 0))],
        out_specs=pl.BlockSpec((tm, d), lambda i: (i, 0)),
        out_shape=jax.ShapeDtypeStruct((t, d), F32),
        name="final_norm",
    )(x, w.reshape(1, d))


def _block_diag(w):
    depth, nblk, bi, bj = w.shape
    eye = jnp.eye(nblk, dtype=w.dtype)
    return (w[:, :, :, None, :] * eye[None, :, None, :, None]).reshape(depth, nblk * bi, nblk * bj)


def _pad_lanes(a, width):
    return jnp.pad(a, [(0, 0)] * (a.ndim - 1) + [(0, width - a.shape[-1])])


def _prep_weights(w_in, lru_conv_w, lru_conv_b, lru_wa, lru_ba, lru_wx, lru_bx, lru_lambda,
                  s5_a_re, s5_a_im, s5_b_re, s5_b_im, s5_c_re, s5_c_im, s5_d, s5_log_dt, s5_w_glu,
                  ssd_conv_w, ssd_conv_b, ssd_dt_bias, ssd_a_log, ssd_d, mix_norm, nlev_max):
    depth, d, _ = w_in.shape
    gw = d // 4
    n_ssd_h = gw // SSD_HD
    xbc_end = 4 * gw + gw + 2 * SSD_G * SSD_N
    dt_cols = jnp.repeat(w_in[..., xbc_end:xbc_end + n_ssd_h], SSD_HD, axis=-1)
    w_in_p = jnp.concatenate([w_in[..., :xbc_end], w_in[..., xbc_end + n_ssd_h:], dt_cols], axis=-1)

    def row(a):
        return _pad_lanes(a.reshape(depth, 1, -1), d)

    vec = jnp.concatenate([
        row(lru_conv_b), row(jnp.concatenate([lru_ba, lru_bx], -1)), row(lru_lambda), row(s5_d),
        row(ssd_conv_b), row(jnp.repeat(ssd_dt_bias, SSD_HD, -1)), row(jnp.repeat(ssd_a_log, SSD_HD, -1)),
        row(jnp.repeat(ssd_d, SSD_HD, -1)), row(mix_norm),
        _pad_lanes(lru_conv_w, d), _pad_lanes(ssd_conv_w, d),
        jnp.zeros((depth, V_ROWS - V_SCW - CONV_K, d), F32)], axis=1)

    wgate = jnp.concatenate([_block_diag(lru_wa), _block_diag(lru_wx)], axis=-1)

    dt = jnp.exp(s5_log_dt)[..., None]
    lr, li = s5_a_re, s5_a_im
    mag = jnp.exp(lr * dt)
    ab_re, ab_im = mag * jnp.cos(li * dt), mag * jnp.sin(li * dt)
    den = lr * lr + li * li
    q_re = ((ab_re - 1.0) * lr + ab_im * li) / den
    q_im = (ab_im * lr - (ab_re - 1.0) * li) / den
    bb_re = q_re[..., None] * s5_b_re - q_im[..., None] * s5_b_im
    bb_im = q_re[..., None] * s5_b_im + q_im[..., None] * s5_b_re
    bbm = jnp.concatenate([_block_diag(jnp.swapaxes(bb_re, -1, -2)), _block_diag(jnp.swapaxes(bb_im, -1, -2))], -1)
    cre = _block_diag(jnp.swapaxes(s5_c_re, -1, -2))
    cim = _block_diag(jnp.swapaxes(s5_c_im, -1, -2))
    pr, pi = ab_re.reshape(depth, -1), ab_im.reshape(depth, -1)
    rows = []
    for _ in range(nlev_max):
        rows += [pr, pi]
        pr, pi = pr * pr - pi * pi, 2.0 * pr * pi
    pw = jnp.stack(rows, axis=1)
    pw = jnp.pad(pw, ((0, 0), (0, (-pw.shape[1]) % 8), (0, 0)))

    mw = dict(vec=vec, wgate=wgate.astype(MXU_DTYPE), bbm=bbm.astype(MXU_DTYPE), cre=cre.astype(MXU_DTYPE),
              cim=cim.astype(MXU_DTYPE), wglu=s5_w_glu.astype(MXU_DTYPE), pw=pw)
    return w_in_p.astype(MXU_DTYPE), mw


def _states_to_tiles(lru_conv, lru_h, s5_re, s5_im, ssd_conv, ssd_h, ret, ns, r):
    depth, bsz = lru_h.shape[:2]
    nb = bsz // ns

    def conv_buf(buf):
        buf = jnp.pad(buf, ((0, 0), (0, 0), (0, r - buf.shape[2]), (0, 0)))
        return buf.reshape(depth, nb, ns * r, buf.shape[-1])

    def vec_state(h):
        h = h.reshape(depth, bsz, 1, -1)
        pad = ((0, 0), (0, 0), (r - 1, 0), (0, 0)) if ns == 1 else ((0, 0), (0, 0), (0, r - 1), (0, 0))
        return jnp.pad(h, pad).reshape(depth, nb, ns * r, h.shape[-1])

    n_h = ssd_h.shape[2]
    gh = (jnp.arange(SSD_G)[:, None] == jnp.arange(n_h)[None, :] // (n_h // SSD_G)).astype(F32)
    sh = ssd_h.reshape(depth, nb, ns, n_h, SSD_HD, SSD_N).transpose(0, 1, 5, 2, 3, 4)
    sh = sh[:, :, None] * gh[None, None, :, None, None, :, None]
    sh = sh.reshape(depth, nb, SSD_G * SSD_N, ns * n_h * SSD_HD)
    n_r, dk, dv = ret.shape[2:]
    eye = jnp.eye(n_r, dtype=F32)
    rt = ret.reshape(depth, nb, ns, n_r, dk, dv).transpose(0, 1, 3, 4, 2, 5)
    rt = rt[:, :, :, :, :, None, :] * eye[None, None, :, None, None, :, None]
    rt = rt.reshape(depth, nb, n_r * dk, ns * n_r * dv)
    return (conv_buf(lru_conv), vec_state(lru_h), vec_state(s5_re.reshape(depth, bsz, -1)),
            vec_state(s5_im.reshape(depth, bsz, -1)), conv_buf(ssd_conv), sh, rt)


def _states_from_tiles(tiles, ns, r, shapes):
    lconv, lh, s5r, s5i, sconv, sh, rt = tiles
    depth, nb = lh.shape[:2]
    bsz = nb * ns

    def conv_buf(buf):
        return buf.reshape(depth, bsz, r, buf.shape[-1])[:, :, :CONV_K - 1]

    def vec_state(h):
        return h.reshape(depth, bsz, r, h.shape[-1])[:, :, r - 1]

    n_h = shapes["ssd_h"][2]
    gh = (jnp.arange(SSD_G)[:, None] == jnp.arange(n_h)[None, :] // (n_h // SSD_G)).astype(F32)
    sh = sh.reshape(depth, nb, SSD_G, SSD_N, ns, n_h, SSD_HD)
    sh = jnp.sum(sh * gh[None, None, :, None, None, :, None], axis=2)
    sh = sh.transpose(0, 1, 3, 4, 5, 2).reshape(depth, bsz, n_h, SSD_HD, SSD_N)
    n_r, dk, dv = shapes["ret"][2:]
    eye = jnp.eye(n_r, dtype=F32)
    rt = rt.reshape(depth, nb, n_r, dk, ns, n_r, dv)
    rt = jnp.sum(rt * eye[None, None, :, None, None, :, None], axis=5)
    rt = rt.transpose(0, 1, 4, 2, 3, 5).reshape(depth, bsz, n_r, dk, dv)
    s5_shape = (depth, bsz) + tuple(shapes["s5"][2:])
    return (conv_buf(lconv), vec_state(lh), vec_state(s5r).reshape(s5_shape), vec_state(s5i).reshape(s5_shape),
            conv_buf(sconv), sh, rt)


def _trunk(x, mod, states, pos0, w_in_p, mw, pw, final_norm, *, qt, ns, tm):
    bsz, seq_len, d = x.shape
    depth = w_in_p.shape[0]
    r = qt // ns
    nc = seq_len // r if ns == 1 else 1
    assert (ns == 1 and seq_len % qt == 0) or (ns > 1 and r == seq_len and bsz % ns == 0)
    nb = bsz // ns
    tok = bsz * seq_len
    shapes = dict(ssd_h=states[5].shape, ret=states[6].shape, s5=states[2].shape)
    tiles = _states_to_tiles(*states, ns, r)
    pos = pos0 + (jnp.arange(seq_len) if ns == 1 else jnp.tile(jnp.arange(seq_len), ns))
    consts = _mix_consts(qt, ns, r, pos)
    rows_per_seq = seq_len

    def layer(xc, l):
        lv = jnp.reshape(l, (1,)).astype(jnp.int32)
        proj = _inproj(lv, xc, mod, w_in_p, tm, rows_per_seq)
        outs = _mixer(lv, proj, consts, mw, tiles, nb=nb, nc=nc, qt=qt, ns=ns, r=r)
        xn = _post(lv, outs[0], xc, mod, pw, tm, rows_per_seq)
        return xn, tuple(outs[1:])

    xf, new_tiles = lax.scan(layer, x.reshape(tok, d), jnp.arange(depth))
    y = _final_norm(xf, final_norm, tm).reshape(bsz, seq_len, d)
    return y, _states_from_tiles(new_tiles, ns, r, shapes)


def kernel(x_prompt, x_sample, state_lru_conv, state_lru_h, state_s5_re, state_s5_im, state_ssd_conv, state_ssd_h, state_ret, c_prompt, c_sample, w_ada, b_ada, w_in, lru_conv_w, lru_conv_b, lru_wa, lru_ba, lru_wx, lru_bx, lru_lambda, s5_a_re, s5_a_im, s5_b_re, s5_b_im, s5_c_re, s5_c_im, s5_d, s5_log_dt, s5_w_glu, ssd_conv_w, ssd_conv_b, ssd_dt_bias, ssd_a_log, ssd_d, mix_norm, w_out, w_route_group, b_route_group, w_route_exp, b_route_exp, w_exp_gate, w_exp_up, w_exp_down, final_norm):
    bp, lp, d = x_prompt.shape
    bs, ls, _ = x_sample.shape
    depth = w_in.shape[0]
    qt = LANES
    ns_s = qt // ls

    mod = _modulation(jnp.concatenate([c_prompt, c_sample], axis=0), w_ada, b_ada)
    mod_p = mod[:, :, :bp].reshape(depth, N_MOD, bp, 1, d)
    mod_s = jnp.repeat(mod[:, :, bp:], ls, axis=2)

    w_in_p, mw = _prep_weights(w_in, lru_conv_w, lru_conv_b, lru_wa, lru_ba, lru_wx, lru_bx, lru_lambda,
                               s5_a_re, s5_a_im, s5_b_re, s5_b_im, s5_c_re, s5_c_im, s5_d, s5_log_dt, s5_w_glu,
                               ssd_conv_w, ssd_conv_b, ssd_dt_bias, ssd_a_log, ssd_d, mix_norm,
                               nlev_max=int(math.log2(qt)))
    n_route = N_GROUPS + N_EXPERTS
    w_route = _pad_lanes(jnp.concatenate([w_route_group, w_route_exp.reshape(depth, d, N_EXPERTS)], -1), LANES)
    b_route = _pad_lanes(jnp.concatenate([b_route_group, b_route_exp.reshape(depth, N_EXPERTS)], -1), LANES)
    del n_route
    pw = dict(w_out=w_out.astype(MXU_DTYPE), w_route=w_route, b_route=b_route.reshape(depth, 1, LANES),
              w_gate=w_exp_gate.astype(MXU_DTYPE), w_up=w_exp_up.astype(MXU_DTYPE),
              w_down=w_exp_down.astype(MXU_DTYPE))

    states_s = (state_lru_conv, state_lru_h, state_s5_re, state_s5_im, state_ssd_conv, state_ssd_h, state_ret)
    states_p = tuple(jnp.zeros((depth, bp) + s.shape[2:], F32) for s in states_s)

    tm_p = min(512, lp)
    y_p, new_p = _trunk(x_prompt, mod_p, states_p, 0, w_in_p, mw, pw, final_norm, qt=qt, ns=1, tm=tm_p)
    y_s, new_s = _trunk(x_sample, mod_s, states_s, PAST_LEN, w_in_p, mw, pw, final_norm, qt=qt, ns=ns_s,
                        tm=min(512, bs * ls))
    out = [y_p, y_s]
    for a, b in zip(new_p, new_s):
        out += [a, b]
    return tuple(out)
```

```python
import functools
import math

import numpy as np
import jax
import jax.numpy as jnp
from jax import lax
from jax.experimental import pallas as pl
from jax.experimental.pallas import tpu as pltpu

F32 = jnp.float32
MXU_DTYPE = jnp.bfloat16
HIGHEST = lax.Precision.HIGHEST

EPS = 1e-6
CONV_K = 4
LRU_C = 8.0
SSD_HD = 64
SSD_G = 2
SSD_N = 64
RET_H = 4
RET_DK = 32
ROPE_BASE = 10000.0
N_GROUPS = 4
N_PER_GROUP = 4
N_EXPERTS = N_GROUPS * N_PER_GROUP
N_MOD = 6
PAST_LEN = 16384

LANES = 128
SUBLANES = 8
PACK = 16
VMEM_LIMIT = 56 * 1024 * 1024
EXPERT_ROWS = 128


def _mm(a, b):
    return jnp.dot(a.astype(MXU_DTYPE), b.astype(MXU_DTYPE), preferred_element_type=F32)


def _mm_nt(a, b):
    return lax.dot_general(a.astype(MXU_DTYPE), b.astype(MXU_DTYPE), (((1,), (1,)), ((), ())),
                           preferred_element_type=F32)


def _mm_tn(a, b):
    return lax.dot_general(a.astype(MXU_DTYPE), b.astype(MXU_DTYPE), (((0,), (0,)), ((), ())),
                           preferred_element_type=F32)


def _split3(x):
    x1 = x.astype(MXU_DTYPE)
    r1 = x - x1.astype(F32)
    x2 = r1.astype(MXU_DTYPE)
    x3 = (r1 - x2.astype(F32)).astype(MXU_DTYPE)
    return x1, x2, x3


def _select_rows(onehot, x, pieces=3):
    parts = _split3(x)[:pieces]
    out = jnp.dot(onehot, parts[0], preferred_element_type=F32)
    for p in parts[1:]:
        out = out + jnp.dot(onehot, p, preferred_element_type=F32)
    return out


def _rms(x):
    return x * lax.rsqrt(jnp.mean(x * x, axis=-1, keepdims=True) + EPS)


def _token_rows(v, tm):
    n = v.shape[0]
    return v if n in (1, tm) else jnp.tile(v, (tm // n, 1))


def _mod_body(c_ref, w_ref, b_ref, o_ref):
    c = c_ref[...]
    o_ref[...] = _mm(c * jax.nn.sigmoid(c), w_ref[...]) + b_ref[...]


def _modulation(c_all, w_ada, b_ada):
    depth, d, _ = w_ada.shape
    nb = c_all.shape[0]
    return pl.pallas_call(
        _mod_body,
        grid=(depth, N_MOD),
        in_specs=[
            pl.BlockSpec((nb, d), lambda l, k: (0, 0)),
            pl.BlockSpec((None, d, d), lambda l, k: (l, 0, k)),
            pl.BlockSpec((None, None, 1, d), lambda l, k: (l, k, 0, 0)),
        ],
        out_specs=pl.BlockSpec((None, None, nb, d), lambda l, k: (l, k, 0, 0)),
        out_shape=jax.ShapeDtypeStruct((depth, N_MOD, nb, d), F32),
        compiler_params=pltpu.CompilerParams(dimension_semantics=("arbitrary", "arbitrary"),
                                             vmem_limit_bytes=VMEM_LIMIT),
        name="modulation",
    )(c_all, w_ada, b_ada.reshape(depth, N_MOD, 1, d))


def _mod_spec(mod, k, tm, rows_per_seq):
    if mod.ndim == 5:
        tiles_per_seq = rows_per_seq // tm
        return pl.BlockSpec((None, None, None, 1, mod.shape[-1]),
                            lambda i, l: (l[0], k, i // tiles_per_seq, 0, 0))
    return pl.BlockSpec((None, None) + mod.shape[2:], lambda i, l: (l[0], k, 0, 0))


def _inproj_body(l_ref, x_ref, sc_ref, sh_ref, w_ref, o_ref):
    tm = x_ref.shape[0]
    h = _rms(x_ref[...]) * (1.0 + _token_rows(sc_ref[...], tm)) + _token_rows(sh_ref[...], tm)
    o_ref[...] = _mm(h, w_ref[...])


def _inproj(l, x, mod, w_in, tm, rows_per_seq):
    t, d = x.shape
    dp = w_in.shape[-1]
    grid_spec = pltpu.PrefetchScalarGridSpec(
        num_scalar_prefetch=1,
        grid=(t // tm,),
        in_specs=[
            pl.BlockSpec((tm, d), lambda i, l: (i, 0)),
            _mod_spec(mod, 1, tm, rows_per_seq),
            _mod_spec(mod, 0, tm, rows_per_seq),
            pl.BlockSpec((None, d, dp), lambda i, l: (l[0], 0, 0)),
        ],
        out_specs=pl.BlockSpec((tm, dp), lambda i, l: (i, 0)),
    )
    return pl.pallas_call(
        _inproj_body,
        grid_spec=grid_spec,
        out_shape=jax.ShapeDtypeStruct((t, dp), F32),
        compiler_params=pltpu.CompilerParams(dimension_semantics=("arbitrary",), vmem_limit_bytes=VMEM_LIMIT),
        name="inproj",
    )(l, x, mod, mod, w_in)


V_LCB, V_BGATE, V_LAM, V_S5D, V_SCB, V_DTB, V_ALOG, V_SSDD, V_GAIN, V_LCW, V_SCW = 0, 1, 2, 3, 4, 5, 6, 7, 8, 9, 13
V_ROWS = 24
STATE_NAMES = ("lconv", "lh", "s5r", "s5i", "sconv", "sh", "ret")
MIX_WEIGHTS = ("vec", "wgate", "bbm", "cre", "cim", "wglu", "pw", "pk")


def _mix_body(l_ref, *refs, names, qt, ns, r, zero_init):
    del l_ref
    g = dict(zip(names, refs))
    gw = g["lh_o"].shape[-1]
    sn = g["s5r_o"].shape[-1]
    nlev = int(math.log2(r))
    nk = qt // SUBLANES
    c = pl.program_id(1)

    @pl.when(c == 0)
    def _load_states():
        for nm in STATE_NAMES:
            o = g[nm + "_o"]
            o[...] = jnp.zeros(o.shape, o.dtype) if zero_init else g[nm + "_i"][...]

    proj = g["proj"]
    vec = g["vec"]
    pw = g["pw"]

    def pcols(a, b):
        v = proj[..., a:b]
        return v if ns == 1 else v.reshape(qt, b - a)

    def ysave(a, b, val):
        val = val.astype(g["y"].dtype)
        g["y"][..., a:b] = val if ns == 1 else val.reshape(r, ns, b - a)

    row = lax.broadcasted_iota(jnp.int32, (qt, 1), 0)
    t = row // ns
    sub = lax.broadcasted_iota(jnp.int32, (SUBLANES, 1), 0)

    def down(x, d, fill=0.0):
        return jnp.where(t >= d, pltpu.roll(x, d * ns, 0), fill)

    def up(x, d):
        return jnp.where(t + d < r, pltpu.roll(x, qt - d * ns, 0), 0.0)

    def sdown(x, d, fill=0.0):
        return jnp.where(sub >= d, pltpu.roll(x, d, 0), fill)

    def slabs(x):
        return [x[SUBLANES * k:SUBLANES * (k + 1)] for k in range(nk)]

    def first_rows(prev):
        if ns == 1:
            prev = pltpu.roll(prev, 1, 0)
        return jnp.where(t == 0, prev, 0.0)

    def conv(xraw, buf_ref, w0, b):
        width = xraw.shape[-1]
        buf = buf_ref[...]
        acc = vec[b:b + 1, 0:width] + vec[w0 + 3:w0 + 4, 0:width] * xraw
        for m in range(1, CONV_K):
            prev = buf if m == 3 else pltpu.roll(buf, qt - (3 - m) * ns, 0)
            sh = jnp.where(t >= m, pltpu.roll(xraw, m * ns, 0), prev)
            acc = acc + vec[w0 + 3 - m:w0 + 4 - m, 0:width] * sh
        buf_ref[...] = pltpu.roll(xraw, (qt - (r - 3) * ns) % qt, 0)
        return acc

    def seq_mask(width):
        lane_seq = lax.broadcasted_iota(jnp.int32, (qt, ns * width), 1) // width
        row_seq = lax.broadcasted_iota(jnp.int32, (qt, ns * width), 0) % ns
        return (lane_seq == row_seq).astype(F32)

    gain = vec[V_GAIN:V_GAIN + 1, :]
    if ns == 1:
        perm = g["perm"]
        to_blocked, to_time = perm[0], perm[1]

    xc = conv(pcols(0, gw), g["lconv_o"], V_LCW, V_LCB)
    ga = pcols(gw, 2 * gw)
    if ns == 1:
        xc = _select_rows(to_blocked, xc)
        ga = _select_rows(to_blocked, ga)
    pre = _mm(xc, g["wgate"][...]) + vec[V_BGATE:V_BGATE + 1, 0:2 * gw]
    rg = jax.nn.sigmoid(pre[:, 0:gw])
    ig = jax.nn.sigmoid(pre[:, gw:2 * gw])
    log_a = -LRU_C * rg * jax.nn.softplus(-vec[V_LAM:V_LAM + 1, 0:gw])
    a = jnp.exp(log_a)
    b = jnp.sqrt(-jnp.tanh(log_a) * (a * a + 1.0)) * (ig * xc)
    b = b + a * first_rows(g["lh_o"][...])
    if ns == 1:
        a_s, b_s = slabs(a), slabs(b)
        hs, ps = [b_s[0]], [a_s[0]]
        for k in range(1, nk):
            hs.append(a_s[k] * hs[-1] + b_s[k])
            ps.append(a_s[k] * ps[-1])
        e, ae = hs[-1], ps[-1]
        for j in range(int(math.log2(SUBLANES))):
            d = 1 << j
            e = ae * sdown(e, d) + e
            ae = ae * sdown(ae, d, 1.0)
        cin = sdown(e, 1)
        b = jnp.concatenate([hs[k] + ps[k] * cin for k in range(nk)], axis=0)
    else:
        for k in range(nlev):
            d = 1 << k
            b = a * down(b, d) + b
            a = a * down(a, d, 1.0)
    g["lh_o"][...] = b
    ya = _rms(b * jax.nn.gelu(ga)) * gain[:, 0:gw]

    u = pcols(2 * gw, 3 * gw)
    if ns == 1:
        u = _select_rows(to_blocked, u)
    bu = _mm(u, g["bbm"][...])
    p_re, p_im = pw[0:1, :], pw[1:2, :]
    h0r, h0i = first_rows(g["s5r_o"][...]), first_rows(g["s5i_o"][...])
    hr = bu[:, 0:sn] + (p_re * h0r - p_im * h0i)
    hi = bu[:, sn:2 * sn] + (p_re * h0i + p_im * h0r)
    if ns == 1:
        pk = g["pk"]
        br_s, bi_s = slabs(hr), slabs(hi)
        hrs, his = [br_s[0]], [bi_s[0]]
        for k in range(1, nk):
            hrs.append(br_s[k] + (p_re * hrs[-1] - p_im * his[-1]))
            his.append(bi_s[k] + (p_re * his[-1] + p_im * hrs[-2]))
        er, ei = hrs[-1], his[-1]
        lev0 = int(math.log2(nk))
        for j in range(int(math.log2(SUBLANES))):
            d = 1 << j
            q_re, q_im = pw[2 * (lev0 + j):2 * (lev0 + j) + 1, :], pw[2 * (lev0 + j) + 1:2 * (lev0 + j) + 2, :]
            sr, si = sdown(er, d), sdown(ei, d)
            er, ei = er + (q_re * sr - q_im * si), ei + (q_re * si + q_im * sr)
        cr, ci = sdown(er, 1), sdown(ei, 1)
        hr = jnp.concatenate([hrs[k] + (pk[k:k + 1, :] * cr - pk[nk + k:nk + k + 1, :] * ci) for k in range(nk)], 0)
        hi = jnp.concatenate([his[k] + (pk[k:k + 1, :] * ci + pk[nk + k:nk + k + 1, :] * cr) for k in range(nk)], 0)
    else:
        for k in range(nlev):
            d = 1 << k
            q_re, q_im = pw[2 * k:2 * k + 1, :], pw[2 * k + 1:2 * k + 2, :]
            sr, si = down(hr, d), down(hi, d)
            hr, hi = hr + (q_re * sr - q_im * si), hi + (q_re * si + q_im * sr)
    g["s5r_o"][...] = hr
    g["s5i_o"][...] = hi
    yb = _mm(hr, g["cre"][...]) - _mm(hi, g["cim"][...])
    yb = jax.nn.gelu(yb + vec[V_S5D:V_S5D + 1, 0:gw] * u)
    yb = yb * jax.nn.sigmoid(_mm(yb, g["wglu"][...]))
    yb = _rms(yb) * gain[:, gw:2 * gw]
    yab = jnp.concatenate([ya, yb], axis=1).astype(MXU_DTYPE)
    if ns == 1:
        yab = jnp.dot(to_time, yab, preferred_element_type=F32)
    ysave(0, 2 * gw, yab)

    tri = g["tri"][...] > 0.0
    xbc = conv(pcols(4 * gw, 6 * gw), g["sconv_o"], V_SCW, V_SCB)
    xbc = xbc * jax.nn.sigmoid(xbc)
    xs, bm, cm = xbc[:, 0:gw], xbc[:, gw:gw + LANES], xbc[:, gw + LANES:2 * gw]
    dt = jax.nn.softplus(pcols(9 * gw, 10 * gw) + vec[V_DTB:V_DTB + 1, 0:gw])
    a_e = -jnp.exp(vec[V_ALOG:V_ALOG + 1, 0:gw])
    dta = dt * a_e
    acum = dta
    for k in range(nlev):
        acum = acum + down(acum, 1 << k)
    if ns == 1:
        alast = acum[qt - 1:qt, :]
    else:
        suf = dta
        for k in range(nlev):
            suf = suf + up(suf, 1 << k)
        alast = acum + suf - dta
    wend = jnp.exp(alast - acum) * dt
    acum_t = jnp.transpose(acum)
    lane = lax.broadcasted_iota(jnp.int32, (1, gw), 1)
    lane_b = lax.broadcasted_iota(jnp.int32, (1, LANES), 1)
    heads_per_group = gw // SSD_HD // SSD_G
    xdt = xs * dt
    xw = xs * wend
    ydiag = jnp.zeros((qt, gw), F32)
    for grp in range(SSD_G):
        gm = ((lane_b // SSD_N) == grp).astype(F32)
        cb = _mm_nt(cm * gm, bm)
        for hh in range(heads_per_group):
            hd = grp * heads_per_group + hh
            col = acum[:, hd * SSD_HD:hd * SSD_HD + 1]
            rowv = acum_t[hd * SSD_HD:hd * SSD_HD + 1, :]
            dec = jnp.where(tri, jnp.exp(col - rowv), 0.0)
            hm = ((lane // SSD_HD) == hd).astype(F32)
            ydiag = ydiag + _mm(cb * dec, xdt * hm)
    st = g["sh_o"][...]
    srow = lax.broadcasted_iota(jnp.int32, st.shape, 0) // SSD_N
    if ns == 1:
        yoff = _mm(cm, st)
        slane = lax.broadcasted_iota(jnp.int32, st.shape, 1) // (SSD_HD * heads_per_group)
        upd = jnp.where(srow == slane, _mm_tn(bm, xw), 0.0)
        g["sh_o"][...] = st * jnp.exp(acum[qt - 1:qt, :]) + upd
    else:
        seqm = seq_mask(LANES)
        lastm = jnp.where(t == r - 1, seqm, 0.0)
        yo, upd, dl = [], None, None
        for grp in range(SSD_G):
            gm = ((lane_b // SSD_N) == grp).astype(F32)
            z = _mm(cm * gm, st) * seqm
            zf = z[:, 0:LANES]
            for s in range(1, ns):
                zf = zf + z[:, s * LANES:(s + 1) * LANES]
            yo.append(zf)
            sl = slice(grp * LANES, (grp + 1) * LANES)
            u_g = _mm_tn(bm, jnp.tile(xw[:, sl], (1, ns)) * seqm)
            d_g = jnp.exp(jnp.sum(jnp.tile(acum[:, sl], (1, ns)) * lastm, axis=0, keepdims=True))
            upd = u_g if grp == 0 else jnp.where(srow == grp, u_g, upd)
            dl = d_g if grp == 0 else jnp.where(srow == grp, d_g, dl)
        yoff = jnp.concatenate(yo, axis=1)
        g["sh_o"][...] = st * dl + upd
    yc = ydiag + yoff * jnp.exp(acum) + vec[V_SSDD:V_SSDD + 1, 0:gw] * xs
    z = pcols(3 * gw, 4 * gw)
    yc = yc * (z * jax.nn.sigmoid(z))
    ysave(2 * gw, 3 * gw, _rms(yc) * gain[:, 2 * gw:3 * gw])

    trig = g["trig"]
    cos, sin_a, sin_b = trig[0], trig[1], trig[2]

    def rope(x):
        return x * cos + pltpu.roll(x, LANES - RET_DK // 2, 1) * sin_a + pltpu.roll(x, RET_DK // 2, 1) * sin_b

    q = rope(pcols(6 * gw, 6 * gw + LANES))
    kk = rope(pcols(6 * gw + LANES, 7 * gw)) * (RET_DK ** -0.5)
    v = pcols(7 * gw, 8 * gw)
    dv = gw // RET_H
    intra = jnp.zeros((qt, gw), F32)
    for hd in range(RET_H):
        qm = ((lane_b // RET_DK) == hd).astype(F32)
        sc = _mm_nt(q * qm, kk) * g["dmask"][hd]
        vm = ((lane // dv) == hd).astype(F32)
        intra = intra + _mm(sc, v * vm)
    dec = g["dec"]
    qd, kd = q * dec[0], kk * dec[1]
    rs = g["ret_o"][...]
    if ns == 1:
        cross = _mm(qd, rs)
        rrow = lax.broadcasted_iota(jnp.int32, rs.shape, 0) // RET_DK
        rlane = lax.broadcasted_iota(jnp.int32, rs.shape, 1) // dv
        upd = jnp.where(rrow == rlane, _mm_tn(kd, v), 0.0)
    else:
        seqm = seq_mask(dv)
        lo = lane_b < dv
        pairs = []
        upd = jnp.zeros(rs.shape, F32)
        for pp in range(RET_H // 2):
            vp = v[:, pp * LANES:(pp + 1) * LANES]
            vsw = pltpu.roll(vp, dv, 1)
            tots = []
            for hh in range(2):
                qm = ((lane_b // RET_DK) == 2 * pp + hh).astype(F32)
                zc = _mm(qd * qm, rs) * seqm
                acc = zc[:, 0:LANES]
                for s2 in range(1, ns // 2):
                    acc = acc + zc[:, s2 * LANES:(s2 + 1) * LANES]
                tots.append(acc + pltpu.roll(acc, dv, 1))
                vh = jnp.where(lo, vp, vsw) if hh == 0 else jnp.where(lo, vsw, vp)
                upd = upd + _mm_tn(kd * qm, jnp.tile(vh, (1, ns // 2)) * seqm)
            pairs.append(jnp.where(lo, tots[0], tots[1]))
        cross = jnp.concatenate(pairs, axis=1)
    g["ret_o"][...] = rs * dec[2][:, 0:1] + upd
    o = intra + cross
    o2 = o * o
    ms = jnp.zeros((qt, gw), F32)
    for hd in range(RET_H):
        vm = ((lane // dv) == hd).astype(F32)
        ms = ms + vm * jnp.sum(o2 * vm, axis=-1, keepdims=True)
    o = o * lax.rsqrt(ms * (1.0 / dv) + EPS) * gain[:, 3 * gw:4 * gw]
    gg = pcols(8 * gw, 9 * gw)
    ysave(3 * gw, 4 * gw, gg * jax.nn.sigmoid(gg) * o)


def _mix_consts(qt, ns, r, nc, pos0):
    assert qt == LANES, "decay tables are stacked as (3, 128, 128)"
    idx = np.arange(qt)
    seq, tt = idx % ns, idx // ns
    causal = (seq[:, None] == seq[None, :]) & (tt[:, None] >= tt[None, :])
    tri = causal.astype(np.float32)
    lg = np.log1p(-np.exp2(-5.0 - np.arange(RET_H, dtype=np.float64)))
    rel = (tt[:, None] - tt[None, :]).astype(np.float64)
    dmask = np.where(causal[None], np.exp(np.maximum(rel, 0.0)[None] * lg[:, None, None]), 0.0).astype(np.float32)
    lane_h = np.arange(LANES) // RET_DK
    qdec = np.exp((tt[:, None] + 1.0) * lg[lane_h][None, :])
    kdec = np.exp((r - 1.0 - tt[:, None]) * lg[lane_h][None, :])
    rdec = np.broadcast_to(np.exp(r * lg[lane_h])[:, None], (LANES, LANES))
    dec = np.stack([qdec, kdec, rdec]).astype(np.float32)
    half = RET_DK // 2
    pos = pos0 + (jnp.arange(nc * qt) if ns == 1 else jnp.arange(qt) // ns)
    inv = ROPE_BASE ** (-jnp.arange(half, dtype=F32) / half)
    ang = pos.astype(F32)[:, None] * inv
    reps = LANES // half
    cos, sin = jnp.tile(jnp.cos(ang), (1, reps)), jnp.tile(jnp.sin(ang), (1, reps))
    first = (np.arange(LANES) % RET_DK) < half
    trig = jnp.stack([cos, jnp.where(first, -sin, 0.0), jnp.where(first, 0.0, sin)])
    nk = qt // SUBLANES
    to_blocked = np.zeros((qt, qt), np.float32)
    to_blocked[idx, (idx % SUBLANES) * nk + idx // SUBLANES] = 1.0
    perm = jnp.asarray(np.stack([to_blocked, to_blocked.T])).astype(MXU_DTYPE)
    return dict(trig=trig, tri=jnp.asarray(tri), dmask=jnp.asarray(dmask), dec=jnp.asarray(dec), perm=perm)


def _state_tile_shapes(nb, qt, ns, gw, sn):
    dv = gw // RET_H
    sh_lanes = gw if ns == 1 else ns * LANES
    ret_lanes = gw if ns == 1 else ns * dv
    return dict(lconv=(nb, qt, gw), lh=(nb, qt, gw), s5r=(nb, qt, sn), s5i=(nb, qt, sn), sconv=(nb, qt, 2 * gw),
                sh=(nb, SSD_G * SSD_N, sh_lanes), ret=(nb, RET_H * RET_DK, ret_lanes))


def _mixer(l, proj, consts, mw, tiles, *, nb, nc, qt, ns, r, d):
    dp = proj.shape[-1]
    gw = d // 4
    sn = mw["pw"].shape[-1]
    zero_init = tiles is None
    shapes = _state_tile_shapes(nb, qt, ns, gw, sn)

    def layer_spec(a):
        return pl.BlockSpec((None,) + a.shape[1:], lambda b, c, l: (l[0],) + (0,) * (a.ndim - 1))

    def const_spec(a):
        return pl.BlockSpec(a.shape, lambda b, c, l: (0,) * a.ndim)

    def state_spec(shape):
        return pl.BlockSpec((None,) + shape[1:], lambda b, c, l: (b, 0, 0))

    if ns == 1:
        io_spec = lambda w: pl.BlockSpec((qt, w), lambda b, c, l: (b * nc + c, 0))
        y_shape = (nb * nc * qt, d)
        cnames = ("trig", "tri", "dmask", "dec", "perm")
    else:
        proj = proj.reshape(r, nb * ns, dp)
        io_spec = lambda w: pl.BlockSpec((r, ns, w), lambda b, c, l: (0, b, 0))
        y_shape = (r, nb * ns, d)
        cnames = ("trig", "tri", "dmask", "dec")
    names = ("proj",) + cnames + MIX_WEIGHTS
    operands = [proj] + [consts[k] for k in cnames] + [mw[k] for k in MIX_WEIGHTS]
    in_specs = [io_spec(dp), pl.BlockSpec((3, qt, LANES), lambda b, c, l: (0, c, 0))] + \
               [const_spec(consts[k]) for k in cnames[1:]] + [layer_spec(mw[k]) for k in MIX_WEIGHTS]
    if not zero_init:
        names += tuple(nm + "_i" for nm in STATE_NAMES)
        operands += [tiles[nm] for nm in STATE_NAMES]
        in_specs += [state_spec(shapes[nm]) for nm in STATE_NAMES]
    names += ("y",) + tuple(nm + "_o" for nm in STATE_NAMES)
    out_specs = [io_spec(d)] + [state_spec(shapes[nm]) for nm in STATE_NAMES]
    out_shape = [jax.ShapeDtypeStruct(y_shape, MXU_DTYPE)] + \
                [jax.ShapeDtypeStruct(shapes[nm], F32) for nm in STATE_NAMES]
    grid_spec = pltpu.PrefetchScalarGridSpec(num_scalar_prefetch=1, grid=(nb, nc),
                                             in_specs=in_specs, out_specs=out_specs)
    outs = pl.pallas_call(
        functools.partial(_mix_body, names=names, qt=qt, ns=ns, r=r, zero_init=zero_init),
        grid_spec=grid_spec,
        out_shape=out_shape,
        compiler_params=pltpu.CompilerParams(dimension_semantics=("arbitrary", "arbitrary"),
                                             vmem_limit_bytes=VMEM_LIMIT),
        name="mixer",
    )(l, *operands)
    return outs[0].reshape(-1, d), dict(zip(STATE_NAMES, outs[1:]))


def _post_body(l_ref, y_ref, x_ref, g1_ref, sc_ref, sh_ref, g2_ref, lt_ref, wo_ref, wr_ref, br_ref,
               wg_ref, wu_ref, wd_ref, o_ref, xs_ref, cs_ref, os_ref):
    del l_ref
    tm, d = x_ref.shape
    tms = xs_ref.shape[0]
    x = x_ref[...] + _token_rows(g1_ref[...], tm) * jnp.dot(y_ref[...], wo_ref[...], preferred_element_type=F32)
    h = _rms(x) * (1.0 + _token_rows(sc_ref[...], tm)) + _token_rows(sh_ref[...], tm)
    logits = jnp.dot(h, wr_ref[...], precision=HIGHEST, preferred_element_type=F32) + br_ref[...]
    col = [logits[:, k:k + 1] for k in range(N_GROUPS + N_EXPERTS)]

    def first_max(vals, allowed=None):
        neg = jnp.full_like(vals[0], -jnp.inf)
        cand = vals if allowed is None else [jnp.where(al > 0.0, v, neg) for v, al in zip(vals, allowed)]
        m = functools.reduce(jnp.maximum, cand)
        rem = jnp.ones_like(vals[0])
        hot = []
        for v in cand:
            f = jnp.where(v >= m, rem, 0.0)
            rem = rem - f
            hot.append(f)
        return hot, m

    grp, gmax = first_max(col[:N_GROUPS])
    gate = 1.0 / functools.reduce(lambda s, v: s + v, [jnp.exp(v - gmax) for v in col[:N_GROUPS]])
    le = [functools.reduce(lambda s, v: s + v,
                           [grp[gi] * col[N_GROUPS + gi * N_PER_GROUP + j] for gi in range(N_GROUPS)])
          for j in range(N_PER_GROUP)]
    top1, m1 = first_max(le)
    top2, m2 = first_max(le, [1.0 - f for f in top1])
    e2 = jnp.exp(m2 - m1)
    w1 = 1.0 / (1.0 + e2)
    w2 = e2 * w1
    wgrp = [(top1[j] * w1 + top2[j] * w2) * gate for j in range(N_PER_GROUP)]

    lane = lax.broadcasted_iota(jnp.int32, (1, LANES), 1)
    ghot = functools.reduce(lambda s, v: s + v, [jnp.where(lane == gi, grp[gi], 0.0) for gi in range(N_GROUPS)])
    comb = functools.reduce(lambda s, v: s + v,
                            [jnp.where(lane == e, grp[e // N_PER_GROUP] * wgrp[e % N_PER_GROUP], 0.0)
                             for e in range(N_EXPERTS)])
    before = jnp.dot(lt_ref[...], ghot.astype(MXU_DTYPE), preferred_element_type=F32)
    count = jnp.sum(ghot, axis=0, keepdims=True)
    cnt = [count[:, gi:gi + 1] for gi in range(N_GROUPS)]
    base = [jnp.zeros((1, 1), F32)]
    for gi in range(1, N_GROUPS):
        base.append(jnp.floor((base[-1] + cnt[gi - 1] + (PACK - 1.0)) * (1.0 / PACK)) * PACK)
    pos = functools.reduce(lambda s, v: s + v,
                           [grp[gi] * (base[gi] + before[:, gi:gi + 1]) for gi in range(N_GROUPS)])
    pos_row = jnp.transpose(jnp.broadcast_to(pos, (tm, LANES)))[0:1, :]
    slot = lax.broadcasted_iota(jnp.int32, (tms, tm), 0).astype(F32)
    gather = jnp.where(slot == pos_row, 1.0, 0.0).astype(MXU_DTYPE)
    slot_l = lax.broadcasted_iota(jnp.int32, (tm, tms), 1).astype(F32)
    scatter = jnp.where(slot_l == pos, 1.0, 0.0).astype(MXU_DTYPE)

    xs_ref[...] = jnp.dot(gather, h.astype(MXU_DTYPE), preferred_element_type=F32).astype(xs_ref.dtype)
    cs_ref[...] = _select_rows(gather, comb)
    os_ref[...] = jnp.zeros(os_ref.shape, os_ref.dtype)
    for gi in range(N_GROUPS):
        start = base[gi][0, 0].astype(jnp.int32)
        nblk = jnp.floor((cnt[gi][0, 0] + (EXPERT_ROWS - 1.0)) * (1.0 / EXPERT_ROWS)).astype(jnp.int32)

        def block(k, carry, gi=gi, start=start):
            rows = pl.ds(pl.multiple_of(start + k * EXPERT_ROWS, PACK), EXPERT_ROWS)
            xb = xs_ref[rows, :]
            cb = cs_ref[rows, :]
            acc = jnp.zeros((EXPERT_ROWS, d), F32)
            for j in range(N_PER_GROUP):
                e = gi * N_PER_GROUP + j
                gt = jnp.dot(xb, wg_ref[e], preferred_element_type=F32)
                he = gt * jax.nn.sigmoid(gt) * jnp.dot(xb, wu_ref[e], preferred_element_type=F32)
                acc = acc + jnp.dot((he * cb[:, e:e + 1]).astype(MXU_DTYPE), wd_ref[e], preferred_element_type=F32)
            os_ref[rows, :] += acc
            return carry

        lax.fori_loop(0, nblk, block, 0)
    moe = _select_rows(scatter, os_ref[...], pieces=2)
    o_ref[...] = x + _token_rows(g2_ref[...], tm) * moe


def _post(l, y, x, mod, pw, tm, rows_per_seq):
    t, d = x.shape
    tms = tm + EXPERT_ROWS + PACK * N_GROUPS
    ltri = jnp.asarray(np.tril(np.ones((tm, tm), np.float32), -1)).astype(MXU_DTYPE)

    def layer_spec(a):
        return pl.BlockSpec((None,) + a.shape[1:], lambda i, l: (l[0],) + (0,) * (a.ndim - 1),
                            pipeline_mode=pl.Buffered(1))

    grid_spec = pltpu.PrefetchScalarGridSpec(
        num_scalar_prefetch=1,
        grid=(t // tm,),
        in_specs=[
            pl.BlockSpec((tm, d), lambda i, l: (i, 0)),
            pl.BlockSpec((tm, d), lambda i, l: (i, 0)),
            _mod_spec(mod, 2, tm, rows_per_seq),
            _mod_spec(mod, 4, tm, rows_per_seq),
            _mod_spec(mod, 3, tm, rows_per_seq),
            _mod_spec(mod, 5, tm, rows_per_seq),
            pl.BlockSpec((tm, tm), lambda i, l: (0, 0)),
        ] + [layer_spec(pw[k]) for k in ("w_out", "w_route", "b_route", "w_gate", "w_up", "w_down")],
        out_specs=pl.BlockSpec((tm, d), lambda i, l: (i, 0)),
        scratch_shapes=[pltpu.VMEM((tms, d), MXU_DTYPE), pltpu.VMEM((tms, LANES), F32), pltpu.VMEM((tms, d), F32)],
    )
    return pl.pallas_call(
        _post_body,
        grid_spec=grid_spec,
        out_shape=jax.ShapeDtypeStruct((t, d), F32),
        compiler_params=pltpu.CompilerParams(dimension_semantics=("arbitrary",), vmem_limit_bytes=VMEM_LIMIT),
        name="post",
    )(l, y, x, mod, mod, mod, mod, ltri,
      *[pw[k] for k in ("w_out", "w_route", "b_route", "w_gate", "w_up", "w_down")])


def _final_body(x_ref, w_ref, o_ref):
    o_ref[...] = _rms(x_ref[...]) * w_ref[...]


def _final_norm(x, w, tm):
    t, d = x.shape
    return pl.pallas_call(
        _final_body,
        grid=(t // tm,),
        in_specs=[pl.BlockSpec((tm, d), lambda i: (i, 0)), pl.BlockSpec((1, d), lambda i: (0, 0))],
        out_specs=pl.BlockSpec((tm, d), lambda i: (i, 0)),
        out_shape=jax.ShapeDtypeStruct((t, d), F32),
        name="final_norm",
    )(x, w.reshape(1, d))


def _block_diag(w):
    depth, nblk, bi, bj = w.shape
    eye = jnp.eye(nblk, dtype=w.dtype)
    return (w[:, :, :, None, :] * eye[None, :, None, :, None]).reshape(depth, nblk * bi, nblk * bj)


def _pad_lanes(a, width):
    return jnp.pad(a, [(0, 0)] * (a.ndim - 1) + [(0, width - a.shape[-1])])


def _prep_weights(w_in, lru_conv_w, lru_conv_b, lru_wa, lru_ba, lru_wx, lru_bx, lru_lambda,
                  s5_a_re, s5_a_im, s5_b_re, s5_b_im, s5_c_re, s5_c_im, s5_d, s5_log_dt, s5_w_glu,
                  ssd_conv_w, ssd_conv_b, ssd_dt_bias, ssd_a_log, ssd_d, mix_norm, qt):
    depth, d, _ = w_in.shape
    gw = d // 4
    n_ssd_h = gw // SSD_HD
    xbc_end = 4 * gw + gw + 2 * SSD_G * SSD_N
    dt_cols = jnp.repeat(w_in[..., xbc_end:xbc_end + n_ssd_h], SSD_HD, axis=-1)
    w_in_p = jnp.concatenate([w_in[..., :xbc_end], w_in[..., xbc_end + n_ssd_h:], dt_cols], axis=-1)

    def row(a):
        return _pad_lanes(a.reshape(depth, 1, -1), d)

    vec = jnp.concatenate([
        row(lru_conv_b), row(jnp.concatenate([lru_ba, lru_bx], -1)), row(lru_lambda), row(s5_d),
        row(ssd_conv_b), row(jnp.repeat(ssd_dt_bias, SSD_HD, -1)), row(jnp.repeat(ssd_a_log, SSD_HD, -1)),
        row(jnp.repeat(ssd_d, SSD_HD, -1)), row(mix_norm),
        _pad_lanes(lru_conv_w, d), _pad_lanes(ssd_conv_w, d),
        jnp.zeros((depth, V_ROWS - V_SCW - CONV_K, d), F32)], axis=1)

    wgate = jnp.concatenate([_block_diag(lru_wa), _block_diag(lru_wx)], axis=-1)

    dt = jnp.exp(s5_log_dt)[..., None]
    lr, li = s5_a_re, s5_a_im
    mag = jnp.exp(lr * dt)
    ab_re, ab_im = mag * jnp.cos(li * dt), mag * jnp.sin(li * dt)
    den = lr * lr + li * li
    q_re = ((ab_re - 1.0) * lr + ab_im * li) / den
    q_im = (ab_im * lr - (ab_re - 1.0) * li) / den
    bb_re = q_re[..., None] * s5_b_re - q_im[..., None] * s5_b_im
    bb_im = q_re[..., None] * s5_b_im + q_im[..., None] * s5_b_re
    bbm = jnp.concatenate([_block_diag(jnp.swapaxes(bb_re, -1, -2)), _block_diag(jnp.swapaxes(bb_im, -1, -2))], -1)
    cre = _block_diag(jnp.swapaxes(s5_c_re, -1, -2))
    cim = _block_diag(jnp.swapaxes(s5_c_im, -1, -2))
    a_re, a_im = ab_re.reshape(depth, -1), ab_im.reshape(depth, -1)
    pr, pi = a_re, a_im
    rows = []
    for _ in range(int(math.log2(qt))):
        rows += [pr, pi]
        pr, pi = pr * pr - pi * pi, 2.0 * pr * pi
    pw = jnp.stack(rows, axis=1)
    pw = jnp.pad(pw, ((0, 0), (0, (-pw.shape[1]) % SUBLANES), (0, 0)))
    pr, pi = a_re, a_im
    k_re, k_im = [], []
    for _ in range(qt // SUBLANES):
        k_re.append(pr)
        k_im.append(pi)
        pr, pi = pr * a_re - pi * a_im, pr * a_im + pi * a_re
    pk = jnp.stack(k_re + k_im, axis=1)

    mw = dict(vec=vec, wgate=wgate.astype(MXU_DTYPE), bbm=bbm.astype(MXU_DTYPE), cre=cre.astype(MXU_DTYPE),
              cim=cim.astype(MXU_DTYPE), wglu=s5_w_glu.astype(MXU_DTYPE), pw=pw, pk=pk)
    return w_in_p.astype(MXU_DTYPE), mw


def _group_of_head(n_h):
    return (jnp.arange(SSD_G)[:, None] == jnp.arange(n_h)[None, :] // (n_h // SSD_G)).astype(F32)


def _states_to_tiles(states, ns, r):
    lru_conv, lru_h, s5_re, s5_im, ssd_conv, ssd_h, ret = states
    bsz = lru_h.shape[0]
    nb = bsz // ns
    qt = ns * r

    def conv_buf(buf):
        buf = buf.reshape(nb, ns, CONV_K - 1, -1).transpose(0, 2, 1, 3)
        buf = jnp.pad(buf, ((0, 0), (0, r - (CONV_K - 1)), (0, 0), (0, 0)))
        return buf.reshape(nb, qt, -1)

    def vec_state(h):
        h = h.reshape(nb, ns, -1)
        return jnp.pad(h, ((0, 0), (0, qt - ns), (0, 0)))

    n_h = ssd_h.shape[1]
    sh = ssd_h.reshape(nb, ns, SSD_G, n_h // SSD_G, SSD_HD, SSD_N).transpose(0, 2, 5, 1, 3, 4)
    sh = sh.reshape(nb, SSD_G * SSD_N, ns * (n_h // SSD_G) * SSD_HD)
    n_r, dk, dv = ret.shape[1:]
    rt = ret.reshape(nb, ns, n_r, dk, dv).transpose(0, 2, 3, 1, 4).reshape(nb, n_r * dk, ns * dv)
    return dict(lconv=conv_buf(lru_conv), lh=vec_state(lru_h), s5r=vec_state(s5_re), s5i=vec_state(s5_im),
                sconv=conv_buf(ssd_conv), sh=sh, ret=rt)


def _states_from_tiles(tiles, ns, r, s5_shape):
    nb, qt = tiles["lh"].shape[:2]
    bsz = nb * ns

    def conv_buf(buf):
        return buf.reshape(nb, r, ns, -1)[:, :CONV_K - 1].transpose(0, 2, 1, 3).reshape(bsz, CONV_K - 1, -1)

    def vec_state(h):
        return h[:, qt - ns:].reshape(bsz, -1)

    sh, rt = tiles["sh"], tiles["ret"]
    gw = tiles["lh"].shape[-1]
    n_h = gw // SSD_HD
    dv = gw // RET_H
    if ns == 1:
        sh = sh.reshape(nb, SSD_G, SSD_N, n_h, SSD_HD)
        sh = jnp.sum(sh * _group_of_head(n_h)[None, :, None, :, None], axis=1).transpose(0, 2, 3, 1)
        rt = rt.reshape(nb, RET_H, RET_DK, RET_H, dv)
        rt = jnp.sum(rt * jnp.eye(RET_H, dtype=F32)[None, :, None, :, None], axis=3)
    else:
        sh = sh.reshape(nb, SSD_G, SSD_N, ns, n_h // SSD_G, SSD_HD).transpose(0, 3, 1, 4, 5, 2)
        sh = sh.reshape(bsz, n_h, SSD_HD, SSD_N)
        rt = rt.reshape(nb, RET_H, RET_DK, ns, dv).transpose(0, 3, 1, 2, 4).reshape(bsz, RET_H, RET_DK, dv)
    return (conv_buf(tiles["lconv"]), vec_state(tiles["lh"]), vec_state(tiles["s5r"]).reshape((bsz,) + s5_shape),
            vec_state(tiles["s5i"]).reshape((bsz,) + s5_shape), conv_buf(tiles["sconv"]), sh, rt)


def _trunk(x, mod, states, pos0, w_in_p, mw, pw, final_norm, s5_shape, *, qt, ns, tm):
    bsz, seq_len, d = x.shape
    depth = w_in_p.shape[0]
    r = qt // ns
    nc = seq_len // r if ns == 1 else 1
    assert (ns == 1 and seq_len % qt == 0 and states is None) or (ns > 1 and r == seq_len and bsz % ns == 0)
    assert r >= CONV_K and (r & (r - 1)) == 0
    nb = bsz // ns
    tok = bsz * seq_len
    consts = _mix_consts(qt, ns, r, nc, pos0)
    x2 = x.reshape(tok, d) if ns == 1 else x.transpose(1, 0, 2).reshape(tok, d)

    def layer(xc, l):
        lv = jnp.reshape(l, (1,)).astype(jnp.int32)
        proj = _inproj(lv, xc, mod, w_in_p, tm, seq_len)
        tiles = None
        if states is not None:
            tiles = _states_to_tiles([lax.dynamic_index_in_dim(s, l, 0, keepdims=False) for s in states], ns, r)
        y, new_tiles = _mixer(lv, proj, consts, mw, tiles, nb=nb, nc=nc, qt=qt, ns=ns, r=r, d=d)
        xn = _post(lv, y, xc, mod, pw, tm, seq_len)
        return xn, _states_from_tiles(new_tiles, ns, r, s5_shape)

    xf, new_states = lax.scan(layer, x2, jnp.arange(depth))
    y = _final_norm(xf, final_norm, tm)
    y = y.reshape(bsz, seq_len, d) if ns == 1 else y.reshape(seq_len, bsz, d).transpose(1, 0, 2)
    return y, new_states


def kernel(x_prompt, x_sample, state_lru_conv, state_lru_h, state_s5_re, state_s5_im, state_ssd_conv, state_ssd_h, state_ret, c_prompt, c_sample, w_ada, b_ada, w_in, lru_conv_w, lru_conv_b, lru_wa, lru_ba, lru_wx, lru_bx, lru_lambda, s5_a_re, s5_a_im, s5_b_re, s5_b_im, s5_c_re, s5_c_im, s5_d, s5_log_dt, s5_w_glu, ssd_conv_w, ssd_conv_b, ssd_dt_bias, ssd_a_log, ssd_d, mix_norm, w_out, w_route_group, b_route_group, w_route_exp, b_route_exp, w_exp_gate, w_exp_up, w_exp_down, final_norm):
    bp, lp, d = x_prompt.shape
    bs, ls, _ = x_sample.shape
    depth = w_in.shape[0]
    qt = LANES
    ns_s = qt // ls

    mod = _modulation(jnp.concatenate([c_prompt, c_sample], axis=0), w_ada, b_ada)
    mod_p = mod[:, :, :bp].reshape(depth, N_MOD, bp, 1, d)
    mod_s = mod[:, :, bp:]

    w_in_p, mw = _prep_weights(w_in, lru_conv_w, lru_conv_b, lru_wa, lru_ba, lru_wx, lru_bx, lru_lambda,
                               s5_a_re, s5_a_im, s5_b_re, s5_b_im, s5_c_re, s5_c_im, s5_d, s5_log_dt, s5_w_glu,
                               ssd_conv_w, ssd_conv_b, ssd_dt_bias, ssd_a_log, ssd_d, mix_norm, qt)
    w_route = _pad_lanes(jnp.concatenate([w_route_group, w_route_exp.reshape(depth, d, N_EXPERTS)], -1), LANES)
    b_route = _pad_lanes(jnp.concatenate([b_route_group, b_route_exp.reshape(depth, N_EXPERTS)], -1), LANES)
    pw = dict(w_out=w_out.astype(MXU_DTYPE), w_route=w_route, b_route=b_route.reshape(depth, 1, LANES),
              w_gate=w_exp_gate.astype(MXU_DTYPE), w_up=w_exp_up.astype(MXU_DTYPE),
              w_down=w_exp_down.astype(MXU_DTYPE))

    s5_shape = tuple(state_s5_re.shape[2:])
    states_s = (state_lru_conv, state_lru_h, state_s5_re.reshape(depth, bs, -1), state_s5_im.reshape(depth, bs, -1),
                state_ssd_conv, state_ssd_h, state_ret)
    y_p, new_p = _trunk(x_prompt, mod_p, None, 0, w_in_p, mw, pw, final_norm, s5_shape,
                        qt=qt, ns=1, tm=min(512, lp))
    y_s, new_s = _trunk(x_sample, mod_s, states_s, PAST_LEN, w_in_p, mw, pw, final_norm, s5_shape,
                        qt=qt, ns=ns_s, tm=bs * ls)
    out = [y_p, y_s]
    for a, b in zip(new_p, new_s):
        out += [a, b]
    return tuple(out)
```

```python
import functools
import math

import numpy as np
import jax
import jax.numpy as jnp
from jax import lax
from jax.experimental import pallas as pl
from jax.experimental.pallas import tpu as pltpu

F32 = jnp.float32
MXU_DTYPE = jnp.bfloat16
HIGHEST = lax.Precision.HIGHEST

EPS = 1e-6
CONV_K = 4
LRU_C = 8.0
SSD_HD = 64
SSD_G = 2
SSD_N = 64
RET_H = 4
RET_DK = 32
ROPE_BASE = 10000.0
N_GROUPS = 4
N_PER_GROUP = 4
N_EXPERTS = N_GROUPS * N_PER_GROUP
N_MOD = 6
PAST_LEN = 16384

LANES = 128
SUBLANES = 8
PACK = 16
VMEM_LIMIT = 56 * 1024 * 1024
EXPERT_ROWS = 128


def _mm(a, b):
    return jnp.dot(a.astype(MXU_DTYPE), b.astype(MXU_DTYPE), preferred_element_type=F32)


def _mm_nt(a, b):
    return lax.dot_general(a.astype(MXU_DTYPE), b.astype(MXU_DTYPE), (((1,), (1,)), ((), ())),
                           preferred_element_type=F32)


def _mm_tn(a, b):
    return lax.dot_general(a.astype(MXU_DTYPE), b.astype(MXU_DTYPE), (((0,), (0,)), ((), ())),
                           preferred_element_type=F32)


def _split3(x):
    x1 = x.astype(MXU_DTYPE)
    r1 = x - x1.astype(F32)
    x2 = r1.astype(MXU_DTYPE)
    x3 = (r1 - x2.astype(F32)).astype(MXU_DTYPE)
    return x1, x2, x3


def _select_rows(onehot, x, pieces=3):
    parts = _split3(x)[:pieces]
    out = jnp.dot(onehot, parts[0], preferred_element_type=F32)
    for p in parts[1:]:
        out = out + jnp.dot(onehot, p, preferred_element_type=F32)
    return out


def _rms(x):
    return x * lax.rsqrt(jnp.mean(x * x, axis=-1, keepdims=True) + EPS)


def _token_rows(v, tm):
    n = v.shape[0]
    return v if n in (1, tm) else jnp.tile(v, (tm // n, 1))


def _mod_body(c_ref, w_ref, b_ref, o_ref):
    c = c_ref[...]
    o_ref[...] = _mm(c * jax.nn.sigmoid(c), w_ref[...]) + b_ref[...]


def _modulation(c_all, w_ada, b_ada):
    depth, d, _ = w_ada.shape
    nb = c_all.shape[0]
    return pl.pallas_call(
        _mod_body,
        grid=(depth, N_MOD),
        in_specs=[
            pl.BlockSpec((nb, d), lambda l, k: (0, 0)),
            pl.BlockSpec((None, d, d), lambda l, k: (l, 0, k)),
            pl.BlockSpec((None, None, 1, d), lambda l, k: (l, k, 0, 0)),
        ],
        out_specs=pl.BlockSpec((None, None, nb, d), lambda l, k: (l, k, 0, 0)),
        out_shape=jax.ShapeDtypeStruct((depth, N_MOD, nb, d), F32),
        compiler_params=pltpu.CompilerParams(dimension_semantics=("arbitrary", "arbitrary"),
                                             vmem_limit_bytes=VMEM_LIMIT),
        name="modulation",
    )(c_all, w_ada, b_ada.reshape(depth, N_MOD, 1, d))


def _mod_spec(mod, k, tm, rows_per_seq):
    if mod.ndim == 5:
        tiles_per_seq = rows_per_seq // tm
        return pl.BlockSpec((None, None, None, 1, mod.shape[-1]),
                            lambda i, l: (l[0], k, i // tiles_per_seq, 0, 0))
    return pl.BlockSpec((None, None) + mod.shape[2:], lambda i, l: (l[0], k, 0, 0))


def _inproj_body(l_ref, x_ref, sc_ref, sh_ref, w_ref, o_ref):
    tm = x_ref.shape[0]
    h = _rms(x_ref[...]) * (1.0 + _token_rows(sc_ref[...], tm)) + _token_rows(sh_ref[...], tm)
    o_ref[...] = _mm(h, w_ref[...])


def _inproj(l, x, mod, w_in, tm, rows_per_seq):
    t, d = x.shape
    dp = w_in.shape[-1]
    grid_spec = pltpu.PrefetchScalarGridSpec(
        num_scalar_prefetch=1,
        grid=(t // tm,),
        in_specs=[
            pl.BlockSpec((tm, d), lambda i, l: (i, 0)),
            _mod_spec(mod, 1, tm, rows_per_seq),
            _mod_spec(mod, 0, tm, rows_per_seq),
            pl.BlockSpec((None, d, dp), lambda i, l: (l[0], 0, 0)),
        ],
        out_specs=pl.BlockSpec((tm, dp), lambda i, l: (i, 0)),
    )
    return pl.pallas_call(
        _inproj_body,
        grid_spec=grid_spec,
        out_shape=jax.ShapeDtypeStruct((t, dp), F32),
        compiler_params=pltpu.CompilerParams(dimension_semantics=("arbitrary",), vmem_limit_bytes=VMEM_LIMIT),
        name="inproj",
    )(l, x, mod, mod, w_in)


V_LCB, V_BGATE, V_LAM, V_S5D, V_SCB, V_DTB, V_ALOG, V_SSDD, V_GAIN, V_LCW, V_SCW = 0, 1, 2, 3, 4, 5, 6, 7, 8, 9, 13
V_ROWS = 24
STATE_NAMES = ("lconv", "lh", "s5r", "s5i", "sconv", "sh", "ret")
MIX_WEIGHTS = ("vec", "wgate", "bbm", "cre", "cim", "wglu", "pw", "pk")


def _mix_body(l_ref, *refs, names, qt, ns, r, zero_init):
    del l_ref
    g = dict(zip(names, refs))
    gw = g["lh_o"].shape[-1]
    sn = g["s5r_o"].shape[-1]
    nlev = int(math.log2(r))
    nk = qt // SUBLANES
    c = pl.program_id(1)

    @pl.when(c == 0)
    def _load_states():
        for nm in STATE_NAMES if ns == 1 else STATE_NAMES[:5]:
            o = g[nm + "_o"]
            o[...] = jnp.zeros(o.shape, o.dtype) if zero_init else g[nm + "_i"][...]

    proj = g["proj"]
    vec = g["vec"]
    pw = g["pw"]

    def pcols(a, b):
        v = proj[..., a:b]
        return v if ns == 1 else v.reshape(qt, b - a)

    def ysave(a, b, val):
        val = val.astype(g["y"].dtype)
        g["y"][..., a:b] = val if ns == 1 else val.reshape(r, ns, b - a)

    row = lax.broadcasted_iota(jnp.int32, (qt, 1), 0)
    t = row // ns
    sub = lax.broadcasted_iota(jnp.int32, (SUBLANES, 1), 0)

    def down(x, d, fill=0.0):
        return jnp.where(t >= d, pltpu.roll(x, d * ns, 0), fill)

    def up(x, d):
        return jnp.where(t + d < r, pltpu.roll(x, qt - d * ns, 0), 0.0)

    def sdown(x, d, fill=0.0):
        return jnp.where(sub >= d, pltpu.roll(x, d, 0), fill)

    def slabs(x):
        return [x[SUBLANES * k:SUBLANES * (k + 1)] for k in range(nk)]

    def first_rows(prev):
        if ns == 1:
            prev = pltpu.roll(prev, 1, 0)
        return jnp.where(t == 0, prev, 0.0)

    def conv(xraw, buf_ref, w0, b):
        width = xraw.shape[-1]
        buf = buf_ref[...]
        acc = vec[b:b + 1, 0:width] + vec[w0 + 3:w0 + 4, 0:width] * xraw
        for m in range(1, CONV_K):
            prev = buf if m == 3 else pltpu.roll(buf, qt - (3 - m) * ns, 0)
            sh = jnp.where(t >= m, pltpu.roll(xraw, m * ns, 0), prev)
            acc = acc + vec[w0 + 3 - m:w0 + 4 - m, 0:width] * sh
        buf_ref[...] = pltpu.roll(xraw, (qt - (r - 3) * ns) % qt, 0)
        return acc

    def seq_mask(width):
        lane_seq = lax.broadcasted_iota(jnp.int32, (qt, ns * width), 1) // width
        row_seq = lax.broadcasted_iota(jnp.int32, (qt, ns * width), 0) % ns
        return (lane_seq == row_seq).astype(F32)

    gain = vec[V_GAIN:V_GAIN + 1, :]
    if ns == 1:
        perm = g["perm"]
        to_blocked, to_time = perm[0], perm[1]

    xc = conv(pcols(0, gw), g["lconv_o"], V_LCW, V_LCB)
    ga = pcols(gw, 2 * gw)
    if ns == 1:
        xc = _select_rows(to_blocked, xc)
        ga = _select_rows(to_blocked, ga)
    pre = _mm(xc, g["wgate"][...]) + vec[V_BGATE:V_BGATE + 1, 0:2 * gw]
    rg = jax.nn.sigmoid(pre[:, 0:gw])
    ig = jax.nn.sigmoid(pre[:, gw:2 * gw])
    log_a = -LRU_C * rg * jax.nn.softplus(-vec[V_LAM:V_LAM + 1, 0:gw])
    a = jnp.exp(log_a)
    b = jnp.sqrt(-jnp.tanh(log_a) * (a * a + 1.0)) * (ig * xc)
    b = b + a * first_rows(g["lh_o"][...])
    if ns == 1:
        a_s, b_s = slabs(a), slabs(b)
        hs, ps = [b_s[0]], [a_s[0]]
        for k in range(1, nk):
            hs.append(a_s[k] * hs[-1] + b_s[k])
            ps.append(a_s[k] * ps[-1])
        e, ae = hs[-1], ps[-1]
        for j in range(int(math.log2(SUBLANES))):
            d = 1 << j
            e = ae * sdown(e, d) + e
            ae = ae * sdown(ae, d, 1.0)
        cin = sdown(e, 1)
        b = jnp.concatenate([hs[k] + ps[k] * cin for k in range(nk)], axis=0)
    else:
        for k in range(nlev):
            d = 1 << k
            b = a * down(b, d) + b
            a = a * down(a, d, 1.0)
    g["lh_o"][...] = b
    ya = _rms(b * jax.nn.gelu(ga)) * gain[:, 0:gw]

    u = pcols(2 * gw, 3 * gw)
    if ns == 1:
        u = _select_rows(to_blocked, u)
    bu = _mm(u, g["bbm"][...])
    p_re, p_im = pw[0:1, :], pw[1:2, :]
    h0r, h0i = first_rows(g["s5r_o"][...]), first_rows(g["s5i_o"][...])
    hr = bu[:, 0:sn] + (p_re * h0r - p_im * h0i)
    hi = bu[:, sn:2 * sn] + (p_re * h0i + p_im * h0r)
    if ns == 1:
        pk = g["pk"]
        br_s, bi_s = slabs(hr), slabs(hi)
        hrs, his = [br_s[0]], [bi_s[0]]
        for k in range(1, nk):
            hrs.append(br_s[k] + (p_re * hrs[-1] - p_im * his[-1]))
            his.append(bi_s[k] + (p_re * his[-1] + p_im * hrs[-2]))
        er, ei = hrs[-1], his[-1]
        lev0 = int(math.log2(nk))
        for j in range(int(math.log2(SUBLANES))):
            d = 1 << j
            q_re, q_im = pw[2 * (lev0 + j):2 * (lev0 + j) + 1, :], pw[2 * (lev0 + j) + 1:2 * (lev0 + j) + 2, :]
            sr, si = sdown(er, d), sdown(ei, d)
            er, ei = er + (q_re * sr - q_im * si), ei + (q_re * si + q_im * sr)
        cr, ci = sdown(er, 1), sdown(ei, 1)
        hr = jnp.concatenate([hrs[k] + (pk[k:k + 1, :] * cr - pk[nk + k:nk + k + 1, :] * ci) for k in range(nk)], 0)
        hi = jnp.concatenate([his[k] + (pk[k:k + 1, :] * ci + pk[nk + k:nk + k + 1, :] * cr) for k in range(nk)], 0)
    else:
        for k in range(nlev):
            d = 1 << k
            q_re, q_im = pw[2 * k:2 * k + 1, :], pw[2 * k + 1:2 * k + 2, :]
            sr, si = down(hr, d), down(hi, d)
            hr, hi = hr + (q_re * sr - q_im * si), hi + (q_re * si + q_im * sr)
    g["s5r_o"][...] = hr
    g["s5i_o"][...] = hi
    yb = _mm(hr, g["cre"][...]) - _mm(hi, g["cim"][...])
    yb = jax.nn.gelu(yb + vec[V_S5D:V_S5D + 1, 0:gw] * u)
    yb = yb * jax.nn.sigmoid(_mm(yb, g["wglu"][...]))
    yb = _rms(yb) * gain[:, gw:2 * gw]
    yab = jnp.concatenate([ya, yb], axis=1).astype(MXU_DTYPE)
    if ns == 1:
        yab = jnp.dot(to_time, yab, preferred_element_type=F32)
    ysave(0, 2 * gw, yab)

    tri = g["tri"][...] > 0.0
    xbc = conv(pcols(4 * gw, 6 * gw), g["sconv_o"], V_SCW, V_SCB)
    xbc = xbc * jax.nn.sigmoid(xbc)
    xs, bm, cm = xbc[:, 0:gw], xbc[:, gw:gw + LANES], xbc[:, gw + LANES:2 * gw]
    dt = jax.nn.softplus(pcols(9 * gw, 10 * gw) + vec[V_DTB:V_DTB + 1, 0:gw])
    a_e = -jnp.exp(vec[V_ALOG:V_ALOG + 1, 0:gw])
    dta = dt * a_e
    acum = dta
    for k in range(nlev):
        acum = acum + down(acum, 1 << k)
    if ns == 1:
        alast = acum[qt - 1:qt, :]
    else:
        suf = dta
        for k in range(nlev):
            suf = suf + up(suf, 1 << k)
        alast = acum + suf - dta
    wend = jnp.exp(alast - acum) * dt
    acum_t = jnp.transpose(acum)
    lane = lax.broadcasted_iota(jnp.int32, (1, gw), 1)
    lane_b = lax.broadcasted_iota(jnp.int32, (1, LANES), 1)
    heads_per_group = gw // SSD_HD // SSD_G
    xdt = xs * dt
    xw = xs * wend
    ydiag = jnp.zeros((qt, gw), F32)
    for grp in range(SSD_G):
        gm = ((lane_b // SSD_N) == grp).astype(F32)
        cb = _mm_nt(cm * gm, bm)
        for hh in range(heads_per_group):
            hd = grp * heads_per_group + hh
            col = acum[:, hd * SSD_HD:hd * SSD_HD + 1]
            rowv = acum_t[hd * SSD_HD:hd * SSD_HD + 1, :]
            dec = jnp.where(tri, jnp.exp(col - rowv), 0.0)
            hm = ((lane // SSD_HD) == hd).astype(F32)
            ydiag = ydiag + _mm(cb * dec, xdt * hm)
    if ns == 1:
        st = g["sh_o"][...]
        srow = lax.broadcasted_iota(jnp.int32, st.shape, 0) // SSD_N
        yoff = _mm(cm, st)
        slane = lax.broadcasted_iota(jnp.int32, st.shape, 1) // (SSD_HD * heads_per_group)
        upd = jnp.where(srow == slane, _mm_tn(bm, xw), 0.0)
        g["sh_o"][...] = st * jnp.exp(acum[qt - 1:qt, :]) + upd
    else:
        sh_i, sh_o = g["sh_i"], g["sh_o"]
        per_seq = gw // SSD_HD * SSD_HD
        cols = [[], []]
        for s in range(0, ns, 2):
            two = jnp.concatenate([sh_i[s * per_seq:(s + 1) * per_seq, :],
                                   sh_i[(s + 1) * per_seq:(s + 2) * per_seq, :]], axis=1)
            two = jnp.transpose(two)
            for grp in range(SSD_G):
                cols[grp] += [two[0:SSD_N, grp * LANES:(grp + 1) * LANES],
                              two[SSD_N:2 * SSD_N, grp * LANES:(grp + 1) * LANES]]
        st = jnp.concatenate([jnp.concatenate(cols[0], axis=1), jnp.concatenate(cols[1], axis=1)], axis=0)
        srow = lax.broadcasted_iota(jnp.int32, st.shape, 0) // SSD_N
        seqm = seq_mask(LANES)
        lastm = jnp.where(t == r - 1, seqm, 0.0)
        yo, upd, dl = [], None, None
        for grp in range(SSD_G):
            gm = ((lane_b // SSD_N) == grp).astype(F32)
            z = _mm(cm * gm, st) * seqm
            zf = z[:, 0:LANES]
            for s in range(1, ns):
                zf = zf + z[:, s * LANES:(s + 1) * LANES]
            yo.append(zf)
            sl = slice(grp * LANES, (grp + 1) * LANES)
            u_g = _mm_tn(bm, jnp.tile(xw[:, sl], (1, ns)) * seqm)
            d_g = jnp.exp(jnp.sum(jnp.tile(acum[:, sl], (1, ns)) * lastm, axis=0, keepdims=True))
            upd = u_g if grp == 0 else jnp.where(srow == grp, u_g, upd)
            dl = d_g if grp == 0 else jnp.where(srow == grp, d_g, dl)
        yoff = jnp.concatenate(yo, axis=1)
        st = st * dl + upd
        for s in range(0, ns, 2):
            two = jnp.concatenate(
                [jnp.concatenate([st[0:SSD_N, q * LANES:(q + 1) * LANES], st[SSD_N:2 * SSD_N, q * LANES:(q + 1) * LANES]],
                                 axis=1) for q in (s, s + 1)], axis=0)
            two = jnp.transpose(two)
            sh_o[s * per_seq:(s + 1) * per_seq, :] = two[:, 0:SSD_N]
            sh_o[(s + 1) * per_seq:(s + 2) * per_seq, :] = two[:, SSD_N:2 * SSD_N]
    yc = ydiag + yoff * jnp.exp(acum) + vec[V_SSDD:V_SSDD + 1, 0:gw] * xs
    z = pcols(3 * gw, 4 * gw)
    yc = yc * (z * jax.nn.sigmoid(z))
    ysave(2 * gw, 3 * gw, _rms(yc) * gain[:, 2 * gw:3 * gw])

    trig = g["trig"]
    cos, sin_a, sin_b = trig[0], trig[1], trig[2]

    def rope(x):
        return x * cos + pltpu.roll(x, LANES - RET_DK // 2, 1) * sin_a + pltpu.roll(x, RET_DK // 2, 1) * sin_b

    q = rope(pcols(6 * gw, 6 * gw + LANES))
    kk = rope(pcols(6 * gw + LANES, 7 * gw)) * (RET_DK ** -0.5)
    v = pcols(7 * gw, 8 * gw)
    dv = gw // RET_H
    intra = jnp.zeros((qt, gw), F32)
    for hd in range(RET_H):
        qm = ((lane_b // RET_DK) == hd).astype(F32)
        sc = _mm_nt(q * qm, kk) * g["dmask"][hd]
        vm = ((lane // dv) == hd).astype(F32)
        intra = intra + _mm(sc, v * vm)
    dec = g["dec"]
    qd, kd = q * dec[0], kk * dec[1]
    if ns == 1:
        rs = g["ret_o"][...]
        cross = _mm(qd, rs)
        rrow = lax.broadcasted_iota(jnp.int32, rs.shape, 0) // RET_DK
        rlane = lax.broadcasted_iota(jnp.int32, rs.shape, 1) // dv
        upd = jnp.where(rrow == rlane, _mm_tn(kd, v), 0.0)
    else:
        ret_i, ret_o = g["ret_i"], g["ret_o"]
        hk = RET_H * RET_DK
        rs = jnp.concatenate([ret_i[s * hk:(s + 1) * hk, :] for s in range(ns)], axis=1)
        seqm = seq_mask(dv)
        lo = lane_b < dv
        pairs = []
        upd = jnp.zeros(rs.shape, F32)
        for pp in range(RET_H // 2):
            vp = v[:, pp * LANES:(pp + 1) * LANES]
            vsw = pltpu.roll(vp, dv, 1)
            tots = []
            for hh in range(2):
                qm = ((lane_b // RET_DK) == 2 * pp + hh).astype(F32)
                zc = _mm(qd * qm, rs) * seqm
                acc = zc[:, 0:LANES]
                for s2 in range(1, ns // 2):
                    acc = acc + zc[:, s2 * LANES:(s2 + 1) * LANES]
                tots.append(acc + pltpu.roll(acc, dv, 1))
                vh = jnp.where(lo, vp, vsw) if hh == 0 else jnp.where(lo, vsw, vp)
                upd = upd + _mm_tn(kd * qm, jnp.tile(vh, (1, ns // 2)) * seqm)
            pairs.append(jnp.where(lo, tots[0], tots[1]))
        cross = jnp.concatenate(pairs, axis=1)
    rs = rs * dec[2][:, 0:1] + upd
    if ns == 1:
        g["ret_o"][...] = rs
    else:
        for s in range(ns):
            ret_o[s * hk:(s + 1) * hk, :] = rs[:, s * dv:(s + 1) * dv]
    o = intra + cross
    o2 = o * o
    ms = jnp.zeros((qt, gw), F32)
    for hd in range(RET_H):
        vm = ((lane // dv) == hd).astype(F32)
        ms = ms + vm * jnp.sum(o2 * vm, axis=-1, keepdims=True)
    o = o * lax.rsqrt(ms * (1.0 / dv) + EPS) * gain[:, 3 * gw:4 * gw]
    gg = pcols(8 * gw, 9 * gw)
    ysave(3 * gw, 4 * gw, gg * jax.nn.sigmoid(gg) * o)


def _mix_consts(qt, ns, r, nc, pos0):
    assert qt == LANES, "decay tables are stacked as (3, 128, 128)"
    idx = np.arange(qt)
    seq, tt = idx % ns, idx // ns
    causal = (seq[:, None] == seq[None, :]) & (tt[:, None] >= tt[None, :])
    tri = causal.astype(np.float32)
    lg = np.log1p(-np.exp2(-5.0 - np.arange(RET_H, dtype=np.float64)))
    rel = (tt[:, None] - tt[None, :]).astype(np.float64)
    dmask = np.where(causal[None], np.exp(np.maximum(rel, 0.0)[None] * lg[:, None, None]), 0.0).astype(np.float32)
    lane_h = np.arange(LANES) // RET_DK
    qdec = np.exp((tt[:, None] + 1.0) * lg[lane_h][None, :])
    kdec = np.exp((r - 1.0 - tt[:, None]) * lg[lane_h][None, :])
    rdec = np.broadcast_to(np.exp(r * lg[lane_h])[:, None], (LANES, LANES))
    dec = np.stack([qdec, kdec, rdec]).astype(np.float32)
    half = RET_DK // 2
    pos = pos0 + (jnp.arange(nc * qt) if ns == 1 else jnp.arange(qt) // ns)
    inv = ROPE_BASE ** (-jnp.arange(half, dtype=F32) / half)
    ang = pos.astype(F32)[:, None] * inv
    reps = LANES // half
    cos, sin = jnp.tile(jnp.cos(ang), (1, reps)), jnp.tile(jnp.sin(ang), (1, reps))
    first = (np.arange(LANES) % RET_DK) < half
    trig = jnp.stack([cos, jnp.where(first, -sin, 0.0), jnp.where(first, 0.0, sin)])
    nk = qt // SUBLANES
    to_blocked = np.zeros((qt, qt), np.float32)
    to_blocked[idx, (idx % SUBLANES) * nk + idx // SUBLANES] = 1.0
    perm = jnp.asarray(np.stack([to_blocked, to_blocked.T])).astype(MXU_DTYPE)
    return dict(trig=trig, tri=jnp.asarray(tri), dmask=jnp.asarray(dmask), dec=jnp.asarray(dec), perm=perm)


def _state_tile_shapes(nb, qt, ns, gw, sn):
    if ns == 1:
        sh, ret = (nb, SSD_G * SSD_N, gw), (nb, RET_H * RET_DK, gw)
    else:
        sh, ret = (nb, ns * gw, SSD_N), (nb, ns * RET_H * RET_DK, gw // RET_H)
    return dict(lconv=(nb, qt, gw), lh=(nb, qt, gw), s5r=(nb, qt, sn), s5i=(nb, qt, sn), sconv=(nb, qt, 2 * gw),
                sh=sh, ret=ret)


def _mixer(l, proj, consts, mw, tiles, *, nb, nc, qt, ns, r, d):
    dp = proj.shape[-1]
    gw = d // 4
    sn = mw["pw"].shape[-1]
    zero_init = tiles is None
    shapes = _state_tile_shapes(nb, qt, ns, gw, sn)

    def layer_spec(a):
        return pl.BlockSpec((None,) + a.shape[1:], lambda b, c, l: (l[0],) + (0,) * (a.ndim - 1))

    def const_spec(a):
        return pl.BlockSpec(a.shape, lambda b, c, l: (0,) * a.ndim)

    def state_spec(shape):
        return pl.BlockSpec((None,) + shape[1:], lambda b, c, l: (b, 0, 0))

    if ns == 1:
        io_spec = lambda w: pl.BlockSpec((qt, w), lambda b, c, l: (b * nc + c, 0))
        y_shape = (nb * nc * qt, d)
        cnames = ("trig", "tri", "dmask", "dec", "perm")
    else:
        proj = proj.reshape(r, nb * ns, dp)
        io_spec = lambda w: pl.BlockSpec((r, ns, w), lambda b, c, l: (0, b, 0))
        y_shape = (r, nb * ns, d)
        cnames = ("trig", "tri", "dmask", "dec")
    names = ("proj",) + cnames + MIX_WEIGHTS
    operands = [proj] + [consts[k] for k in cnames] + [mw[k] for k in MIX_WEIGHTS]
    in_specs = [io_spec(dp), pl.BlockSpec((3, qt, LANES), lambda b, c, l: (0, c, 0))] + \
               [const_spec(consts[k]) for k in cnames[1:]] + [layer_spec(mw[k]) for k in MIX_WEIGHTS]
    if not zero_init:
        names += tuple(nm + "_i" for nm in STATE_NAMES)
        operands += [tiles[nm] for nm in STATE_NAMES]
        in_specs += [state_spec(shapes[nm]) for nm in STATE_NAMES]
    names += ("y",) + tuple(nm + "_o" for nm in STATE_NAMES)
    out_specs = [io_spec(d)] + [state_spec(shapes[nm]) for nm in STATE_NAMES]
    out_shape = [jax.ShapeDtypeStruct(y_shape, MXU_DTYPE)] + \
                [jax.ShapeDtypeStruct(shapes[nm], F32) for nm in STATE_NAMES]
    grid_spec = pltpu.PrefetchScalarGridSpec(num_scalar_prefetch=1, grid=(nb, nc),
                                             in_specs=in_specs, out_specs=out_specs)
    outs = pl.pallas_call(
        functools.partial(_mix_body, names=names, qt=qt, ns=ns, r=r, zero_init=zero_init),
        grid_spec=grid_spec,
        out_shape=out_shape,
        compiler_params=pltpu.CompilerParams(dimension_semantics=("arbitrary", "arbitrary"),
                                             vmem_limit_bytes=VMEM_LIMIT),
        name="mixer",
    )(l, *operands)
    return outs[0].reshape(-1, d), dict(zip(STATE_NAMES, outs[1:]))


def _post_body(l_ref, y_ref, x_ref, g1_ref, sc_ref, sh_ref, g2_ref, lt_ref, wo_ref, wr_ref, br_ref,
               wg_ref, wu_ref, wd_ref, o_ref, xs_ref, cs_ref, os_ref):
    del l_ref
    tm, d = x_ref.shape
    tms = xs_ref.shape[0]
    x = x_ref[...] + _token_rows(g1_ref[...], tm) * jnp.dot(y_ref[...], wo_ref[...], preferred_element_type=F32)
    h = _rms(x) * (1.0 + _token_rows(sc_ref[...], tm)) + _token_rows(sh_ref[...], tm)
    hb = h.astype(MXU_DTYPE)
    h_lo = (h - hb.astype(F32)).astype(MXU_DTYPE)
    wr = wr_ref[...]
    l12 = jnp.dot(hb, wr, preferred_element_type=F32)
    logits = (l12[:, 0:LANES] + l12[:, LANES:2 * LANES]
              + jnp.dot(h_lo, wr[:, 0:LANES], preferred_element_type=F32) + br_ref[...])
    logits_t = jnp.transpose(logits)
    col = [logits_t[k:k + 1, :] for k in range(N_GROUPS + N_EXPERTS)]

    def first_max(vals, allowed=None):
        neg = jnp.full_like(vals[0], -jnp.inf)
        cand = vals if allowed is None else [jnp.where(al > 0.0, v, neg) for v, al in zip(vals, allowed)]
        m = functools.reduce(jnp.maximum, cand)
        rem = jnp.ones_like(vals[0])
        hot = []
        for v in cand:
            f = jnp.where(v >= m, rem, 0.0)
            rem = rem - f
            hot.append(f)
        return hot, m

    grp, gmax = first_max(col[:N_GROUPS])
    gate = 1.0 / functools.reduce(lambda s, v: s + v, [jnp.exp(v - gmax) for v in col[:N_GROUPS]])
    le = [functools.reduce(lambda s, v: s + v,
                           [grp[gi] * col[N_GROUPS + gi * N_PER_GROUP + j] for gi in range(N_GROUPS)])
          for j in range(N_PER_GROUP)]
    top1, m1 = first_max(le)
    top2, m2 = first_max(le, [1.0 - f for f in top1])
    e2 = jnp.exp(m2 - m1)
    w1 = 1.0 / (1.0 + e2)
    w2 = e2 * w1
    wgrp = [(top1[j] * w1 + top2[j] * w2) * gate for j in range(N_PER_GROUP)]

    sub8 = lax.broadcasted_iota(jnp.int32, (SUBLANES, 1), 0)
    sub_e = lax.broadcasted_iota(jnp.int32, (cs_ref.shape[1], 1), 0)
    ghot_t = functools.reduce(lambda s, v: s + v, [jnp.where(sub8 == gi, grp[gi], 0.0) for gi in range(N_GROUPS)])
    comb_t = functools.reduce(lambda s, v: s + v,
                              [jnp.where(sub_e == e, grp[e // N_PER_GROUP] * wgrp[e % N_PER_GROUP], 0.0)
                               for e in range(N_EXPERTS)])
    before = _mm_nt(ghot_t, lt_ref[...])
    count = jnp.sum(ghot_t, axis=1, keepdims=True)
    cnt = [count[gi:gi + 1, :] for gi in range(N_GROUPS)]
    base = [jnp.zeros((1, 1), F32)]
    for gi in range(1, N_GROUPS):
        base.append(jnp.floor((base[-1] + cnt[gi - 1] + (PACK - 1.0)) * (1.0 / PACK)) * PACK)
    pos_row = functools.reduce(lambda s, v: s + v,
                               [grp[gi] * (base[gi] + before[gi:gi + 1, :]) for gi in range(N_GROUPS)])
    pos = jnp.transpose(jnp.broadcast_to(pos_row, (LANES, tm)))[:, 0:1]
    slot = lax.broadcasted_iota(jnp.int32, (tms, tm), 0).astype(F32)
    gather = jnp.where(slot == pos_row, 1.0, 0.0).astype(MXU_DTYPE)
    slot_l = lax.broadcasted_iota(jnp.int32, (tm, tms), 1).astype(F32)
    scatter = jnp.where(slot_l == pos, 1.0, 0.0).astype(MXU_DTYPE)

    xs_ref[...] = jnp.dot(gather, hb, preferred_element_type=F32).astype(xs_ref.dtype)
    c1, c2, c3 = _split3(comb_t)
    nt = lambda p: lax.dot_general(gather, p, (((1,), (1,)), ((), ())), preferred_element_type=F32)
    cs_ref[...] = nt(c1) + nt(c2) + nt(c3)
    os_ref[...] = jnp.zeros(os_ref.shape, os_ref.dtype)
    for gi in range(N_GROUPS):
        start = base[gi][0, 0].astype(jnp.int32)
        nblk = jnp.floor((cnt[gi][0, 0] + (EXPERT_ROWS - 1.0)) * (1.0 / EXPERT_ROWS)).astype(jnp.int32)

        def block(k, carry, gi=gi, start=start):
            rows = pl.ds(pl.multiple_of(start + k * EXPERT_ROWS, PACK), EXPERT_ROWS)
            xb = xs_ref[rows, :]
            cb = cs_ref[rows, :]
            acc = jnp.zeros((EXPERT_ROWS, d), F32)
            for j in range(N_PER_GROUP):
                e = gi * N_PER_GROUP + j
                gt = jnp.dot(xb, wg_ref[e], preferred_element_type=F32)
                he = gt * jax.nn.sigmoid(gt) * jnp.dot(xb, wu_ref[e], preferred_element_type=F32)
                acc = acc + jnp.dot((he * cb[:, e:e + 1]).astype(MXU_DTYPE), wd_ref[e], preferred_element_type=F32)
            os_ref[rows, :] += acc
            return carry

        lax.fori_loop(0, nblk, block, 0)
    moe = _select_rows(scatter, os_ref[...], pieces=2)
    o_ref[...] = x + _token_rows(g2_ref[...], tm) * moe


def _post(l, y, x, mod, pw, tm, rows_per_seq):
    t, d = x.shape
    tms = tm + EXPERT_ROWS + PACK * N_GROUPS
    ltri = jnp.asarray(np.tril(np.ones((tm, tm), np.float32), -1)).astype(MXU_DTYPE)

    def layer_spec(a):
        return pl.BlockSpec((None,) + a.shape[1:], lambda i, l: (l[0],) + (0,) * (a.ndim - 1),
                            pipeline_mode=pl.Buffered(1))

    grid_spec = pltpu.PrefetchScalarGridSpec(
        num_scalar_prefetch=1,
        grid=(t // tm,),
        in_specs=[
            pl.BlockSpec((tm, d), lambda i, l: (i, 0)),
            pl.BlockSpec((tm, d), lambda i, l: (i, 0)),
            _mod_spec(mod, 2, tm, rows_per_seq),
            _mod_spec(mod, 4, tm, rows_per_seq),
            _mod_spec(mod, 3, tm, rows_per_seq),
            _mod_spec(mod, 5, tm, rows_per_seq),
            pl.BlockSpec((tm, tm), lambda i, l: (0, 0)),
        ] + [layer_spec(pw[k]) for k in ("w_out", "w_route", "b_route", "w_gate", "w_up", "w_down")],
        out_specs=pl.BlockSpec((tm, d), lambda i, l: (i, 0)),
        scratch_shapes=[pltpu.VMEM((tms, d), MXU_DTYPE), pltpu.VMEM((tms, 2 * N_EXPERTS), F32),
                        pltpu.VMEM((tms, d), F32)],
    )
    return pl.pallas_call(
        _post_body,
        grid_spec=grid_spec,
        out_shape=jax.ShapeDtypeStruct((t, d), F32),
        compiler_params=pltpu.CompilerParams(dimension_semantics=("arbitrary",), vmem_limit_bytes=VMEM_LIMIT),
        name="post",
    )(l, y, x, mod, mod, mod, mod, ltri,
      *[pw[k] for k in ("w_out", "w_route", "b_route", "w_gate", "w_up", "w_down")])


def _final_body(x_ref, w_ref, o_ref):
    o_ref[...] = _rms(x_ref[...]) * w_ref[...]


def _final_norm(x, w, tm):
    t, d = x.shape
    return pl.pallas_call(
        _final_body,
        grid=(t // tm,),
        in_specs=[pl.BlockSpec((tm, d), lambda i: (i, 0)), pl.BlockSpec((1, d), lambda i: (0, 0))],
        out_specs=pl.BlockSpec((tm, d), lambda i: (i, 0)),
        out_shape=jax.ShapeDtypeStruct((t, d), F32),
        name="final_norm",
    )(x, w.reshape(1, d))


def _block_diag(w):
    depth, nblk, bi, bj = w.shape
    eye = jnp.eye(nblk, dtype=w.dtype)
    return (w[:, :, :, None, :] * eye[None, :, None, :, None]).reshape(depth, nblk * bi, nblk * bj)


def _pad_lanes(a, width):
    return jnp.pad(a, [(0, 0)] * (a.ndim - 1) + [(0, width - a.shape[-1])])


def _prep_weights(w_in, lru_conv_w, lru_conv_b, lru_wa, lru_ba, lru_wx, lru_bx, lru_lambda,
                  s5_a_re, s5_a_im, s5_b_re, s5_b_im, s5_c_re, s5_c_im, s5_d, s5_log_dt, s5_w_glu,
                  ssd_conv_w, ssd_conv_b, ssd_dt_bias, ssd_a_log, ssd_d, mix_norm, qt):
    depth, d, _ = w_in.shape
    gw = d // 4
    n_ssd_h = gw // SSD_HD
    xbc_end = 4 * gw + gw + 2 * SSD_G * SSD_N
    dt_cols = jnp.repeat(w_in[..., xbc_end:xbc_end + n_ssd_h], SSD_HD, axis=-1)
    w_in_p = jnp.concatenate([w_in[..., :xbc_end], w_in[..., xbc_end + n_ssd_h:], dt_cols], axis=-1)

    def row(a):
        return _pad_lanes(a.reshape(depth, 1, -1), d)

    vec = jnp.concatenate([
        row(lru_conv_b), row(jnp.concatenate([lru_ba, lru_bx], -1)), row(lru_lambda), row(s5_d),
        row(ssd_conv_b), row(jnp.repeat(ssd_dt_bias, SSD_HD, -1)), row(jnp.repeat(ssd_a_log, SSD_HD, -1)),
        row(jnp.repeat(ssd_d, SSD_HD, -1)), row(mix_norm),
        _pad_lanes(lru_conv_w, d), _pad_lanes(ssd_conv_w, d),
        jnp.zeros((depth, V_ROWS - V_SCW - CONV_K, d), F32)], axis=1)

    wgate = jnp.concatenate([_block_diag(lru_wa), _block_diag(lru_wx)], axis=-1)

    dt = jnp.exp(s5_log_dt)[..., None]
    lr, li = s5_a_re, s5_a_im
    mag = jnp.exp(lr * dt)
    ab_re, ab_im = mag * jnp.cos(li * dt), mag * jnp.sin(li * dt)
    den = lr * lr + li * li
    q_re = ((ab_re - 1.0) * lr + ab_im * li) / den
    q_im = (ab_im * lr - (ab_re - 1.0) * li) / den
    bb_re = q_re[..., None] * s5_b_re - q_im[..., None] * s5_b_im
    bb_im = q_re[..., None] * s5_b_im + q_im[..., None] * s5_b_re
    bbm = jnp.concatenate([_block_diag(jnp.swapaxes(bb_re, -1, -2)), _block_diag(jnp.swapaxes(bb_im, -1, -2))], -1)
    cre = _block_diag(jnp.swapaxes(s5_c_re, -1, -2))
    cim = _block_diag(jnp.swapaxes(s5_c_im, -1, -2))
    a_re, a_im = ab_re.reshape(depth, -1), ab_im.reshape(depth, -1)
    pr, pi = a_re, a_im
    rows = []
    for _ in range(int(math.log2(qt))):
        rows += [pr, pi]
        pr, pi = pr * pr - pi * pi, 2.0 * pr * pi
    pw = jnp.stack(rows, axis=1)
    pw = jnp.pad(pw, ((0, 0), (0, (-pw.shape[1]) % SUBLANES), (0, 0)))
    pr, pi = a_re, a_im
    k_re, k_im = [], []
    for _ in range(qt // SUBLANES):
        k_re.append(pr)
        k_im.append(pi)
        pr, pi = pr * a_re - pi * a_im, pr * a_im + pi * a_re
    pk = jnp.stack(k_re + k_im, axis=1)

    mw = dict(vec=vec, wgate=wgate.astype(MXU_DTYPE), bbm=bbm.astype(MXU_DTYPE), cre=cre.astype(MXU_DTYPE),
              cim=cim.astype(MXU_DTYPE), wglu=s5_w_glu.astype(MXU_DTYPE), pw=pw, pk=pk)
    return w_in_p.astype(MXU_DTYPE), mw


def _group_of_head(n_h):
    return (jnp.arange(SSD_G)[:, None] == jnp.arange(n_h)[None, :] // (n_h // SSD_G)).astype(F32)


def _states_to_tiles(states, ns, r):
    lru_conv, lru_h, s5_re, s5_im, ssd_conv, ssd_h, ret = states
    bsz = lru_h.shape[0]
    nb = bsz // ns
    qt = ns * r

    def conv_buf(buf):
        buf = buf.reshape(nb, ns, CONV_K - 1, -1).transpose(0, 2, 1, 3)
        buf = jnp.pad(buf, ((0, 0), (0, r - (CONV_K - 1)), (0, 0), (0, 0)))
        return buf.reshape(nb, qt, -1)

    def vec_state(h):
        h = h.reshape(nb, ns, -1)
        return jnp.pad(h, ((0, 0), (0, qt - ns), (0, 0)))

    sh = ssd_h.reshape(nb, -1, ssd_h.shape[-1])
    rt = ret.reshape(nb, -1, ret.shape[-1])
    return dict(lconv=conv_buf(lru_conv), lh=vec_state(lru_h), s5r=vec_state(s5_re), s5i=vec_state(s5_im),
                sconv=conv_buf(ssd_conv), sh=sh, ret=rt)


def _states_from_tiles(tiles, ns, r, s5_shape):
    nb, qt = tiles["lh"].shape[:2]
    bsz = nb * ns

    def conv_buf(buf):
        return buf.reshape(nb, r, ns, -1)[:, :CONV_K - 1].transpose(0, 2, 1, 3).reshape(bsz, CONV_K - 1, -1)

    def vec_state(h):
        return h[:, qt - ns:].reshape(bsz, -1)

    sh, rt = tiles["sh"], tiles["ret"]
    gw = tiles["lh"].shape[-1]
    n_h = gw // SSD_HD
    dv = gw // RET_H
    if ns == 1:
        sh = sh.reshape(nb, SSD_G, SSD_N, n_h, SSD_HD)
        sh = jnp.sum(sh * _group_of_head(n_h)[None, :, None, :, None], axis=1).transpose(0, 2, 3, 1)
        rt = rt.reshape(nb, RET_H, RET_DK, RET_H, dv)
        rt = jnp.sum(rt * jnp.eye(RET_H, dtype=F32)[None, :, None, :, None], axis=3)
    else:
        sh = sh.reshape(bsz, n_h, SSD_HD, SSD_N)
        rt = rt.reshape(bsz, RET_H, RET_DK, dv)
    return (conv_buf(tiles["lconv"]), vec_state(tiles["lh"]), vec_state(tiles["s5r"]).reshape((bsz,) + s5_shape),
            vec_state(tiles["s5i"]).reshape((bsz,) + s5_shape), conv_buf(tiles["sconv"]), sh, rt)


def _trunk(x, mod, states, pos0, w_in_p, mw, pw, final_norm, s5_shape, *, qt, ns, tm):
    bsz, seq_len, d = x.shape
    depth = w_in_p.shape[0]
    r = qt // ns
    nc = seq_len // r if ns == 1 else 1
    assert (ns == 1 and seq_len % qt == 0 and states is None) or (ns > 1 and r == seq_len and bsz % ns == 0)
    assert r >= CONV_K and (r & (r - 1)) == 0
    nb = bsz // ns
    tok = bsz * seq_len
    consts = _mix_consts(qt, ns, r, nc, pos0)
    x2 = x.reshape(tok, d) if ns == 1 else x.transpose(1, 0, 2).reshape(tok, d)

    def layer(xc, l):
        lv = jnp.reshape(l, (1,)).astype(jnp.int32)
        proj = _inproj(lv, xc, mod, w_in_p, tm, seq_len)
        tiles = None
        if states is not None:
            tiles = _states_to_tiles([lax.dynamic_index_in_dim(s, l, 0, keepdims=False) for s in states], ns, r)
        y, new_tiles = _mixer(lv, proj, consts, mw, tiles, nb=nb, nc=nc, qt=qt, ns=ns, r=r, d=d)
        xn = _post(lv, y, xc, mod, pw, tm, seq_len)
        return xn, _states_from_tiles(new_tiles, ns, r, s5_shape)

    xf, new_states = lax.scan(layer, x2, jnp.arange(depth))
    y = _final_norm(xf, final_norm, tm)
    y = y.reshape(bsz, seq_len, d) if ns == 1 else y.reshape(seq_len, bsz, d).transpose(1, 0, 2)
    return y, new_states


def kernel(x_prompt, x_sample, state_lru_conv, state_lru_h, state_s5_re, state_s5_im, state_ssd_conv, state_ssd_h, state_ret, c_prompt, c_sample, w_ada, b_ada, w_in, lru_conv_w, lru_conv_b, lru_wa, lru_ba, lru_wx, lru_bx, lru_lambda, s5_a_re, s5_a_im, s5_b_re, s5_b_im, s5_c_re, s5_c_im, s5_d, s5_log_dt, s5_w_glu, ssd_conv_w, ssd_conv_b, ssd_dt_bias, ssd_a_log, ssd_d, mix_norm, w_out, w_route_group, b_route_group, w_route_exp, b_route_exp, w_exp_gate, w_exp_up, w_exp_down, final_norm):
    bp, lp, d = x_prompt.shape
    bs, ls, _ = x_sample.shape
    depth = w_in.shape[0]
    qt = LANES
    ns_s = qt // ls

    mod = _modulation(jnp.concatenate([c_prompt, c_sample], axis=0), w_ada, b_ada)
    mod_p = mod[:, :, :bp].reshape(depth, N_MOD, bp, 1, d)
    mod_s = mod[:, :, bp:]

    w_in_p, mw = _prep_weights(w_in, lru_conv_w, lru_conv_b, lru_wa, lru_ba, lru_wx, lru_bx, lru_lambda,
                               s5_a_re, s5_a_im, s5_b_re, s5_b_im, s5_c_re, s5_c_im, s5_d, s5_log_dt, s5_w_glu,
                               ssd_conv_w, ssd_conv_b, ssd_dt_bias, ssd_a_log, ssd_d, mix_norm, qt)
    w_route = _pad_lanes(jnp.concatenate([w_route_group, w_route_exp.reshape(depth, d, N_EXPERTS)], -1), LANES)
    b_route = _pad_lanes(jnp.concatenate([b_route_group, b_route_exp.reshape(depth, N_EXPERTS)], -1), LANES)
    w_route_hi = w_route.astype(MXU_DTYPE)
    w_route_lo = (w_route - w_route_hi.astype(F32)).astype(MXU_DTYPE)
    w_route = jnp.concatenate([w_route_hi, w_route_lo], axis=-1)
    pw = dict(w_out=w_out.astype(MXU_DTYPE), w_route=w_route, b_route=b_route.reshape(depth, 1, LANES),
              w_gate=w_exp_gate.astype(MXU_DTYPE), w_up=w_exp_up.astype(MXU_DTYPE),
              w_down=w_exp_down.astype(MXU_DTYPE))

    s5_shape = tuple(state_s5_re.shape[2:])
    states_s = (state_lru_conv, state_lru_h, state_s5_re.reshape(depth, bs, -1), state_s5_im.reshape(depth, bs, -1),
                state_ssd_conv, state_ssd_h, state_ret)
    y_p, new_p = _trunk(x_prompt, mod_p, None, 0, w_in_p, mw, pw, final_norm, s5_shape,
                        qt=qt, ns=1, tm=min(512, lp))
    y_s, new_s = _trunk(x_sample, mod_s, states_s, PAST_LEN, w_in_p, mw, pw, final_norm, s5_shape,
                        qt=qt, ns=ns_s, tm=bs * ls)
    out = [y_p, y_s]
    for a, b in zip(new_p, new_s):
        out += [a, b]
    return tuple(out)
```

```python
import functools
import math

import numpy as np
import jax
import jax.numpy as jnp
from jax import lax
from jax.experimental import pallas as pl
from jax.experimental.pallas import tpu as pltpu

F32 = jnp.float32
MXU_DTYPE = jnp.bfloat16
HIGHEST = lax.Precision.HIGHEST

EPS = 1e-6
CONV_K = 4
LRU_C = 8.0
SSD_HD = 64
SSD_G = 2
SSD_N = 64
RET_H = 4
RET_DK = 32
ROPE_BASE = 10000.0
N_GROUPS = 4
N_PER_GROUP = 4
N_EXPERTS = N_GROUPS * N_PER_GROUP
N_MOD = 6
PAST_LEN = 16384

LANES = 128
SUBLANES = 8
PACK = 16
VMEM_LIMIT = 56 * 1024 * 1024
EXPERT_ROWS = 128


def _mm(a, b):
    return jnp.dot(a.astype(MXU_DTYPE), b.astype(MXU_DTYPE), preferred_element_type=F32)


def _mm_nt(a, b):
    return lax.dot_general(a.astype(MXU_DTYPE), b.astype(MXU_DTYPE), (((1,), (1,)), ((), ())),
                           preferred_element_type=F32)


def _mm_tn(a, b):
    return lax.dot_general(a.astype(MXU_DTYPE), b.astype(MXU_DTYPE), (((0,), (0,)), ((), ())),
                           preferred_element_type=F32)


def _split3(x):
    x1 = x.astype(MXU_DTYPE)
    r1 = x - x1.astype(F32)
    x2 = r1.astype(MXU_DTYPE)
    x3 = (r1 - x2.astype(F32)).astype(MXU_DTYPE)
    return x1, x2, x3


def _select_rows(onehot, x, pieces=3):
    parts = _split3(x)[:pieces]
    out = jnp.dot(onehot, parts[0], preferred_element_type=F32)
    for p in parts[1:]:
        out = out + jnp.dot(onehot, p, preferred_element_type=F32)
    return out


def _sigmoid(x):
    return 0.5 * jnp.tanh(0.5 * x) + 0.5


def _rms(x):
    return x * lax.rsqrt(jnp.mean(x * x, axis=-1, keepdims=True) + EPS)


def _token_rows(v, tm):
    n = v.shape[0]
    return v if n in (1, tm) else jnp.tile(v, (tm // n, 1))


def _mod_body(c_ref, w_ref, b_ref, o_ref):
    c = c_ref[...]
    o_ref[...] = _mm(c * _sigmoid(c), w_ref[...]) + b_ref[...]


def _modulation(c_all, w_ada, b_ada):
    depth, d, _ = w_ada.shape
    nb = c_all.shape[0]
    return pl.pallas_call(
        _mod_body,
        grid=(depth, N_MOD),
        in_specs=[
            pl.BlockSpec((nb, d), lambda l, k: (0, 0)),
            pl.BlockSpec((None, d, d), lambda l, k: (l, 0, k)),
            pl.BlockSpec((None, None, 1, d), lambda l, k: (l, k, 0, 0)),
        ],
        out_specs=pl.BlockSpec((None, None, nb, d), lambda l, k: (l, k, 0, 0)),
        out_shape=jax.ShapeDtypeStruct((depth, N_MOD, nb, d), F32),
        compiler_params=pltpu.CompilerParams(dimension_semantics=("arbitrary", "arbitrary"),
                                             vmem_limit_bytes=VMEM_LIMIT),
        name="modulation",
    )(c_all, w_ada, b_ada.reshape(depth, N_MOD, 1, d))


def _mod_spec(mod, k, tm, rows_per_seq):
    if mod.ndim == 5:
        tiles_per_seq = rows_per_seq // tm
        return pl.BlockSpec((None, None, None, 1, mod.shape[-1]),
                            lambda i, l: (l[0], k, i // tiles_per_seq, 0, 0))
    return pl.BlockSpec((None, None) + mod.shape[2:], lambda i, l: (l[0], k, 0, 0))


def _inproj_body(l_ref, x_ref, sc_ref, sh_ref, w_ref, *rest):
    tm = x_ref.shape[0]
    o_ref = rest[-1]
    h = _rms(x_ref[...]) * (1.0 + _token_rows(sc_ref[...], tm)) + _token_rows(sh_ref[...], tm)
    hb = h.astype(MXU_DTYPE)
    if len(rest) == 2:
        to_blocked = rest[0][0]
        qt = to_blocked.shape[0]
        hb = jnp.concatenate([jnp.dot(to_blocked, hb[j * qt:(j + 1) * qt], preferred_element_type=F32)
                              for j in range(tm // qt)], axis=0).astype(MXU_DTYPE)
    o_ref[...] = jnp.dot(hb, w_ref[...], preferred_element_type=F32)


def _inproj(l, x, mod, w_in, tm, rows_per_seq, perm=None):
    t, d = x.shape
    dp = w_in.shape[-1]
    extra_specs = [] if perm is None else [pl.BlockSpec(perm.shape, lambda i, l: (0, 0, 0))]
    extra = [] if perm is None else [perm]
    grid_spec = pltpu.PrefetchScalarGridSpec(
        num_scalar_prefetch=1,
        grid=(t // tm,),
        in_specs=[
            pl.BlockSpec((tm, d), lambda i, l: (i, 0)),
            _mod_spec(mod, 1, tm, rows_per_seq),
            _mod_spec(mod, 0, tm, rows_per_seq),
            pl.BlockSpec((None, d, dp), lambda i, l: (l[0], 0, 0)),
        ] + extra_specs,
        out_specs=pl.BlockSpec((tm, dp), lambda i, l: (i, 0)),
    )
    return pl.pallas_call(
        _inproj_body,
        grid_spec=grid_spec,
        out_shape=jax.ShapeDtypeStruct((t, dp), F32),
        compiler_params=pltpu.CompilerParams(dimension_semantics=("arbitrary",), vmem_limit_bytes=VMEM_LIMIT),
        name="inproj",
    )(l, x, mod, mod, w_in, *extra)


V_LCB, V_BGATE, V_LAM, V_S5D, V_SCB, V_DTB, V_ALOG, V_SSDD, V_GAIN, V_LCW, V_SCW = 0, 1, 2, 3, 4, 5, 6, 7, 8, 9, 13
V_ROWS = 24
STATE_NAMES = ("lconv", "lh", "s5r", "s5i", "sconv", "sh", "ret")
MIX_WEIGHTS = ("vec", "wgate", "bbm", "cre", "cim", "wglu", "pw", "pk")


def _mix_body(l_ref, *refs, names, qt, ns, r, nc):
    del l_ref
    g = dict(zip(names, refs))
    gw = g["lh_o"].shape[-1]
    sn = g["s5r_o"].shape[-1]
    nlev = int(math.log2(r))
    nk = qt // SUBLANES
    c = pl.program_id(1)

    if ns == 1:
        @pl.when(c == 0)
        def _zero_carries():
            for nm in ("c_lh", "c_s5r", "c_s5i", "c_sh", "c_ret"):
                g[nm][...] = jnp.zeros(g[nm].shape, F32)
            for nm in ("cb_a", "cb_c"):
                g[nm][...] = jnp.zeros(g[nm].shape, F32)

    proj = g["proj"]
    vec = g["vec"]
    pw = g["pw"]

    def pcols(a, b):
        v = proj[..., a:b]
        return v if ns == 1 else v.reshape(qt, b - a)

    y_parts = []

    def ysave(a, b, val):
        val = val.astype(g["y"].dtype)
        if ns == 1:
            y_parts.append(val)
        else:
            g["y"][..., a:b] = val.reshape(r, ns, b - a)

    row = lax.broadcasted_iota(jnp.int32, (qt, 1), 0)
    t = (row % SUBLANES) * nk + row // SUBLANES if ns == 1 else row // ns
    sub = lax.broadcasted_iota(jnp.int32, (SUBLANES, 1), 0)

    def down(x, d, fill=0.0):
        return jnp.where(t >= d, pltpu.roll(x, d * ns, 0), fill)

    def up(x, d):
        return jnp.where(t + d < r, pltpu.roll(x, qt - d * ns, 0), 0.0)

    def sdown(x, d, fill=0.0):
        return jnp.where(sub >= d, pltpu.roll(x, d, 0), fill)

    def slabs(x):
        return [x[SUBLANES * k:SUBLANES * (k + 1)] for k in range(nk)]

    def first_rows(carry, init):
        if ns == 1:
            return jnp.where(row == 0, g[carry][...], 0.0)
        h0 = g[init][...]
        return jnp.concatenate([h0, jnp.zeros((qt - ns, h0.shape[1]), F32)], axis=0)

    def conv(xraw, nm, w0, b):
        width = xraw.shape[-1]
        acc = vec[b:b + 1, 0:width] + vec[w0 + 3:w0 + 4, 0:width] * xraw
        if ns == 1:
            cb = g["cb_" + nm]
            xs_k = slabs(xraw)
            wrap = [jnp.where(sub >= 1, pltpu.roll(xs_k[nk - j], 1, 0),
                              pltpu.roll(cb[(CONV_K - 1 - j) * SUBLANES:(CONV_K - j) * SUBLANES, :], 1, 0))
                    for j in range(1, CONV_K)]
            for m in range(1, CONV_K):
                sh = jnp.concatenate([wrap[m - k - 1] for k in range(m)] + xs_k[0:nk - m], axis=0)
                acc = acc + vec[w0 + 3 - m:w0 + 4 - m, 0:width] * sh
            cb[...] = xraw[qt - (CONV_K - 1) * SUBLANES:, :]
            g[{"a": "lconv_o", "c": "sconv_o"}[nm]][...] = jnp.concatenate(
                [xs_k[nk - j][SUBLANES - 1:SUBLANES, :] for j in range(CONV_K - 1, 0, -1)], axis=0)
            return acc
        o = g[{"a": "lconv_o", "c": "sconv_o"}[nm]]
        buf = g[{"a": "lconv_i", "c": "sconv_i"}[nm]][...].reshape((CONV_K - 1) * ns, width)
        buf = jnp.concatenate([buf, jnp.zeros((qt - (CONV_K - 1) * ns, width), F32)], axis=0)
        for m in range(1, CONV_K):
            prev = buf if m == 3 else pltpu.roll(buf, qt - (3 - m) * ns, 0)
            sh = jnp.where(t >= m, pltpu.roll(xraw, m * ns, 0), prev)
            acc = acc + vec[w0 + 3 - m:w0 + 4 - m, 0:width] * sh
        o[...] = xraw[(r - (CONV_K - 1)) * ns:, :].reshape(CONV_K - 1, ns, width)
        return acc

    def save_last(nm, h):
        if ns == 1:
            g["c_" + nm][...] = h[qt - 1:qt, :]
            g[nm + "_o"][...] = h[qt - 1:qt, :]
        else:
            g[nm + "_o"][...] = h[qt - ns:, :]

    def seq_mask(width):
        lane_seq = lax.broadcasted_iota(jnp.int32, (qt, ns * width), 1) // width
        row_seq = lax.broadcasted_iota(jnp.int32, (qt, ns * width), 0) % ns
        return (lane_seq == row_seq).astype(F32)

    gain = vec[V_GAIN:V_GAIN + 1, :]

    xc = conv(pcols(0, gw), "a", V_LCW, V_LCB)
    ga = pcols(gw, 2 * gw)
    pre = _mm(xc, g["wgate"][...]) + vec[V_BGATE:V_BGATE + 1, 0:2 * gw]
    rg = _sigmoid(pre[:, 0:gw])
    ig = _sigmoid(pre[:, gw:2 * gw])
    log_a = -LRU_C * rg * jax.nn.softplus(-vec[V_LAM:V_LAM + 1, 0:gw])
    a = jnp.exp(log_a)
    b = jnp.sqrt(-jnp.tanh(log_a) * (a * a + 1.0)) * (ig * xc)
    b = b + a * first_rows("c_lh", "lh_i")
    if ns == 1:
        a_s, b_s = slabs(a), slabs(b)
        hs, ps = [b_s[0]], [a_s[0]]
        for k in range(1, nk):
            hs.append(a_s[k] * hs[-1] + b_s[k])
            ps.append(a_s[k] * ps[-1])
        e, ae = hs[-1], ps[-1]
        for j in range(int(math.log2(SUBLANES))):
            d = 1 << j
            e = ae * sdown(e, d) + e
            ae = ae * sdown(ae, d, 1.0)
        cin = sdown(e, 1)
        b = jnp.concatenate([hs[k] + ps[k] * cin for k in range(nk)], axis=0)
    else:
        for k in range(nlev):
            d = 1 << k
            b = a * down(b, d) + b
            a = a * down(a, d, 1.0)
    save_last("lh", b)
    ya = _rms(b * jax.nn.gelu(ga)) * gain[:, 0:gw]

    u = pcols(2 * gw, 3 * gw)
    bu = _mm(u, g["bbm"][...])
    p_re, p_im = pw[0:1, :], pw[1:2, :]
    h0r, h0i = first_rows("c_s5r", "s5r_i"), first_rows("c_s5i", "s5i_i")
    hr = bu[:, 0:sn] + (p_re * h0r - p_im * h0i)
    hi = bu[:, sn:2 * sn] + (p_re * h0i + p_im * h0r)
    if ns == 1:
        pk = g["pk"]
        br_s, bi_s = slabs(hr), slabs(hi)
        hrs, his = [br_s[0]], [bi_s[0]]
        for k in range(1, nk):
            hrs.append(br_s[k] + (p_re * hrs[-1] - p_im * his[-1]))
            his.append(bi_s[k] + (p_re * his[-1] + p_im * hrs[-2]))
        er, ei = hrs[-1], his[-1]
        lev0 = int(math.log2(nk))
        for j in range(int(math.log2(SUBLANES))):
            d = 1 << j
            q_re, q_im = pw[2 * (lev0 + j):2 * (lev0 + j) + 1, :], pw[2 * (lev0 + j) + 1:2 * (lev0 + j) + 2, :]
            sr, si = sdown(er, d), sdown(ei, d)
            er, ei = er + (q_re * sr - q_im * si), ei + (q_re * si + q_im * sr)
        cr, ci = sdown(er, 1), sdown(ei, 1)
        hr = jnp.concatenate([hrs[k] + (pk[k:k + 1, :] * cr - pk[nk + k:nk + k + 1, :] * ci) for k in range(nk)], 0)
        hi = jnp.concatenate([his[k] + (pk[k:k + 1, :] * ci + pk[nk + k:nk + k + 1, :] * cr) for k in range(nk)], 0)
    else:
        for k in range(nlev):
            d = 1 << k
            q_re, q_im = pw[2 * k:2 * k + 1, :], pw[2 * k + 1:2 * k + 2, :]
            sr, si = down(hr, d), down(hi, d)
            hr, hi = hr + (q_re * sr - q_im * si), hi + (q_re * si + q_im * sr)
    save_last("s5r", hr)
    save_last("s5i", hi)
    yb = _mm(hr, g["cre"][...]) - _mm(hi, g["cim"][...])
    yb = jax.nn.gelu(yb + vec[V_S5D:V_S5D + 1, 0:gw] * u)
    yb = yb * _sigmoid(_mm(yb, g["wglu"][...]))
    yb = _rms(yb) * gain[:, gw:2 * gw]
    ysave(0, 2 * gw, jnp.concatenate([ya, yb], axis=1))

    tri = g["tri"][...] > 0.0
    xbc = conv(pcols(4 * gw, 6 * gw), "c", V_SCW, V_SCB)
    xbc = xbc * _sigmoid(xbc)
    xs, bm, cm = xbc[:, 0:gw], xbc[:, gw:gw + LANES], xbc[:, gw + LANES:2 * gw]
    dt = jax.nn.softplus(pcols(9 * gw, 10 * gw) + vec[V_DTB:V_DTB + 1, 0:gw])
    a_e = -jnp.exp(vec[V_ALOG:V_ALOG + 1, 0:gw])
    dta = dt * a_e
    if ns == 1:
        cs = slabs(dta)
        for k in range(1, nk):
            cs[k] = cs[k] + cs[k - 1]
        e = cs[-1]
        for j in range(int(math.log2(SUBLANES))):
            e = e + sdown(e, 1 << j)
        cin = sdown(e, 1)
        acum = jnp.concatenate([ck + cin for ck in cs], axis=0)
        alast = acum[qt - 1:qt, :]
    else:
        acum = dta
        for k in range(nlev):
            acum = acum + down(acum, 1 << k)
        suf = dta
        for k in range(nlev):
            suf = suf + up(suf, 1 << k)
        alast = acum + suf - dta
    wend = jnp.exp(alast - acum) * dt
    acum_t = jnp.transpose(acum)
    lane = lax.broadcasted_iota(jnp.int32, (1, gw), 1)
    lane_b = lax.broadcasted_iota(jnp.int32, (1, LANES), 1)
    heads_per_group = gw // SSD_HD // SSD_G
    xdt = xs * dt
    xw = xs * wend
    ydiag = jnp.zeros((qt, gw), F32)
    for grp in range(SSD_G):
        gm = ((lane_b // SSD_N) == grp).astype(F32)
        cb = _mm_nt(cm * gm, bm)
        for hh in range(heads_per_group):
            hd = grp * heads_per_group + hh
            col = acum[:, hd * SSD_HD:hd * SSD_HD + 1]
            rowv = acum_t[hd * SSD_HD:hd * SSD_HD + 1, :]
            dec = jnp.where(tri, jnp.exp(col - rowv), 0.0)
            hm = ((lane // SSD_HD) == hd).astype(F32)
            ydiag = ydiag + _mm(cb * dec, xdt * hm)
    if ns == 1:
        st = g["c_sh"][...]
        srow = lax.broadcasted_iota(jnp.int32, st.shape, 0) // SSD_N
        yoff = _mm(cm, st)
        slane = lax.broadcasted_iota(jnp.int32, st.shape, 1) // (SSD_HD * heads_per_group)
        upd = jnp.where(srow == slane, _mm_tn(bm, xw), 0.0)
        st = st * jnp.exp(acum[qt - 1:qt, :]) + upd
        g["c_sh"][...] = st

        @pl.when(c == nc - 1)
        def _final_ssd_state():
            own = jnp.concatenate([st[grp * SSD_N:(grp + 1) * SSD_N, grp * LANES:(grp + 1) * LANES]
                                   for grp in range(SSD_G)], axis=1)
            own = jnp.concatenate([own, jnp.zeros_like(own)], axis=0)
            g["sh_o"][...] = jnp.transpose(own)[:, 0:SSD_N]
    else:
        sh_i, sh_o = g["sh_i"], g["sh_o"]
        per_seq = gw // SSD_HD * SSD_HD
        cols = [[], []]
        for s in range(0, ns, 2):
            two = jnp.concatenate([sh_i[s * per_seq:(s + 1) * per_seq, :],
                                   sh_i[(s + 1) * per_seq:(s + 2) * per_seq, :]], axis=1)
            two = jnp.transpose(two)
            for grp in range(SSD_G):
                cols[grp] += [two[0:SSD_N, grp * LANES:(grp + 1) * LANES],
                              two[SSD_N:2 * SSD_N, grp * LANES:(grp + 1) * LANES]]
        st = jnp.concatenate([jnp.concatenate(cols[0], axis=1), jnp.concatenate(cols[1], axis=1)], axis=0)
        srow = lax.broadcasted_iota(jnp.int32, st.shape, 0) // SSD_N
        seqm = seq_mask(LANES)
        lastm = jnp.where(t == r - 1, seqm, 0.0)
        yo, upd, dl = [], None, None
        for grp in range(SSD_G):
            gm = ((lane_b // SSD_N) == grp).astype(F32)
            z = _mm(cm * gm, st) * seqm
            zf = z[:, 0:LANES]
            for s in range(1, ns):
                zf = zf + z[:, s * LANES:(s + 1) * LANES]
            yo.append(zf)
            sl = slice(grp * LANES, (grp + 1) * LANES)
            u_g = _mm_tn(bm, jnp.tile(xw[:, sl], (1, ns)) * seqm)
            d_g = jnp.exp(jnp.sum(jnp.tile(acum[:, sl], (1, ns)) * lastm, axis=0, keepdims=True))
            upd = u_g if grp == 0 else jnp.where(srow == grp, u_g, upd)
            dl = d_g if grp == 0 else jnp.where(srow == grp, d_g, dl)
        yoff = jnp.concatenate(yo, axis=1)
        st = st * dl + upd
        for s in range(0, ns, 2):
            two = jnp.concatenate(
                [jnp.concatenate([st[0:SSD_N, q * LANES:(q + 1) * LANES], st[SSD_N:2 * SSD_N, q * LANES:(q + 1) * LANES]],
                                 axis=1) for q in (s, s + 1)], axis=0)
            two = jnp.transpose(two)
            sh_o[s * per_seq:(s + 1) * per_seq, :] = two[:, 0:SSD_N]
            sh_o[(s + 1) * per_seq:(s + 2) * per_seq, :] = two[:, SSD_N:2 * SSD_N]
    yc = ydiag + yoff * jnp.exp(acum) + vec[V_SSDD:V_SSDD + 1, 0:gw] * xs
    z = pcols(3 * gw, 4 * gw)
    yc = yc * (z * _sigmoid(z))
    ysave(2 * gw, 3 * gw, _rms(yc) * gain[:, 2 * gw:3 * gw])

    trig = g["trig"]
    cos, sin_a, sin_b = trig[0], trig[1], trig[2]

    def rope(x):
        return x * cos + pltpu.roll(x, LANES - RET_DK // 2, 1) * sin_a + pltpu.roll(x, RET_DK // 2, 1) * sin_b

    q = rope(pcols(6 * gw, 6 * gw + LANES))
    kk = rope(pcols(6 * gw + LANES, 7 * gw)) * (RET_DK ** -0.5)
    v = pcols(7 * gw, 8 * gw)
    dv = gw // RET_H
    intra = jnp.zeros((qt, gw), F32)
    for hd in range(RET_H):
        qm = ((lane_b // RET_DK) == hd).astype(F32)
        sc = _mm_nt(q * qm, kk) * g["dmask"][hd]
        vm = ((lane // dv) == hd).astype(F32)
        intra = intra + _mm(sc, v * vm)
    dec = g["dec"]
    qd, kd = q * dec[0], kk * dec[1]
    if ns == 1:
        rs = g["c_ret"][...]
        cross = _mm(qd, rs)
        rrow = lax.broadcasted_iota(jnp.int32, rs.shape, 0) // RET_DK
        rlane = lax.broadcasted_iota(jnp.int32, rs.shape, 1) // dv
        upd = jnp.where(rrow == rlane, _mm_tn(kd, v), 0.0)
    else:
        ret_i, ret_o = g["ret_i"], g["ret_o"]
        hk = RET_H * RET_DK
        rs = jnp.concatenate([ret_i[s * hk:(s + 1) * hk, :] for s in range(ns)], axis=1)
        seqm = seq_mask(dv)
        lo = lane_b < dv
        pairs = []
        upd = jnp.zeros(rs.shape, F32)
        for pp in range(RET_H // 2):
            vp = v[:, pp * LANES:(pp + 1) * LANES]
            vsw = pltpu.roll(vp, dv, 1)
            tots = []
            for hh in range(2):
                qm = ((lane_b // RET_DK) == 2 * pp + hh).astype(F32)
                zc = _mm(qd * qm, rs) * seqm
                acc = zc[:, 0:LANES]
                for s2 in range(1, ns // 2):
                    acc = acc + zc[:, s2 * LANES:(s2 + 1) * LANES]
                tots.append(acc + pltpu.roll(acc, dv, 1))
                vh = jnp.where(lo, vp, vsw) if hh == 0 else jnp.where(lo, vsw, vp)
                upd = upd + _mm_tn(kd * qm, jnp.tile(vh, (1, ns // 2)) * seqm)
            pairs.append(jnp.where(lo, tots[0], tots[1]))
        cross = jnp.concatenate(pairs, axis=1)
    rs = rs * dec[2][:, 0:1] + upd
    if ns == 1:
        g["c_ret"][...] = rs

        @pl.when(c == nc - 1)
        def _final_ret_state():
            own = functools.reduce(lambda s, x: s + x, [rs[:, hd * dv:(hd + 1) * dv] for hd in range(RET_H)])
            g["ret_o"][...] = own
    else:
        for s in range(ns):
            ret_o[s * hk:(s + 1) * hk, :] = rs[:, s * dv:(s + 1) * dv]
    o = intra + cross
    o2 = o * o
    ms = jnp.zeros((qt, gw), F32)
    for hd in range(RET_H):
        vm = ((lane // dv) == hd).astype(F32)
        ms = ms + vm * jnp.sum(o2 * vm, axis=-1, keepdims=True)
    o = o * lax.rsqrt(ms * (1.0 / dv) + EPS) * gain[:, 3 * gw:4 * gw]
    gg = pcols(8 * gw, 9 * gw)
    ysave(3 * gw, 4 * gw, gg * _sigmoid(gg) * o)
    if ns == 1:
        y_all = jnp.concatenate(y_parts, axis=1)
        g["y"][...] = jnp.dot(g["perm"][1], y_all, preferred_element_type=F32).astype(g["y"].dtype)


def _mix_consts(qt, ns, r, nc, pos0):
    assert qt == LANES, "decay tables are stacked as (3, 128, 128)"
    idx = np.arange(qt)
    nk = qt // SUBLANES
    if ns == 1:
        seq, tt = np.zeros_like(idx), (idx % SUBLANES) * nk + idx // SUBLANES
    else:
        seq, tt = idx % ns, idx // ns
    causal = (seq[:, None] == seq[None, :]) & (tt[:, None] >= tt[None, :])
    tri = causal.astype(np.float32)
    lg = np.log1p(-np.exp2(-5.0 - np.arange(RET_H, dtype=np.float64)))
    rel = (tt[:, None] - tt[None, :]).astype(np.float64)
    dmask = np.where(causal[None], np.exp(np.maximum(rel, 0.0)[None] * lg[:, None, None]), 0.0).astype(np.float32)
    lane_h = np.arange(LANES) // RET_DK
    qdec = np.exp((tt[:, None] + 1.0) * lg[lane_h][None, :])
    kdec = np.exp((r - 1.0 - tt[:, None]) * lg[lane_h][None, :])
    rdec = np.broadcast_to(np.exp(r * lg[lane_h])[:, None], (LANES, LANES))
    dec = np.stack([qdec, kdec, rdec]).astype(np.float32)
    half = RET_DK // 2
    pos = pos0 + jnp.asarray((np.arange(nc)[:, None] * qt + tt[None, :]).reshape(-1))
    inv = ROPE_BASE ** (-jnp.arange(half, dtype=F32) / half)
    ang = pos.astype(F32)[:, None] * inv
    reps = LANES // half
    cos, sin = jnp.tile(jnp.cos(ang), (1, reps)), jnp.tile(jnp.sin(ang), (1, reps))
    first = (np.arange(LANES) % RET_DK) < half
    trig = jnp.stack([cos, jnp.where(first, -sin, 0.0), jnp.where(first, 0.0, sin)])
    to_blocked = np.zeros((qt, qt), np.float32)
    to_blocked[idx, (idx % SUBLANES) * nk + idx // SUBLANES] = 1.0
    perm = jnp.asarray(np.stack([to_blocked, to_blocked.T])).astype(MXU_DTYPE)
    return dict(trig=trig, tri=jnp.asarray(tri), dmask=jnp.asarray(dmask), dec=jnp.asarray(dec), perm=perm)


def _state_layout(depth, bsz, ns, gw, sn):
    hk, dv = RET_H * RET_DK, gw // RET_H
    if ns == 1:
        def per_seq(rows, w):
            return (depth, bsz, rows, w), (None, None, rows, w), lambda b, c, l: (l[0], b, 0, 0)
        return dict(lconv=per_seq(CONV_K - 1, gw), lh=per_seq(1, gw), s5r=per_seq(1, sn), s5i=per_seq(1, sn),
                    sconv=per_seq(CONV_K - 1, 2 * gw), sh=per_seq(gw, SSD_N), ret=per_seq(hk, dv))

    def flat(per, w):
        return (depth, bsz * per, w), (None, ns * per, w), lambda b, c, l: (l[0], b, 0)

    def conv(w):
        return (depth, CONV_K - 1, bsz, w), (None, CONV_K - 1, ns, w), lambda b, c, l: (l[0], 0, b, 0)
    return dict(lconv=conv(gw), lh=flat(1, gw), s5r=flat(1, sn), s5i=flat(1, sn), sconv=conv(2 * gw),
                sh=flat(gw, SSD_N), ret=flat(hk, dv))


def _mixer(l, proj, consts, mw, state_in, state_out, *, nb, nc, qt, ns, r, d):
    dp = proj.shape[-1]
    gw = d // 4
    sn = mw["pw"].shape[-1]
    depth = mw["pw"].shape[0]
    layout = _state_layout(depth, nb * ns, ns, gw, sn)
    assert (state_in is None) == (ns == 1)

    def layer_spec(a):
        return pl.BlockSpec((None,) + a.shape[1:], lambda b, c, l: (l[0],) + (0,) * (a.ndim - 1))

    def const_spec(a):
        return pl.BlockSpec(a.shape, lambda b, c, l: (0,) * a.ndim)

    def state_spec(nm):
        _, block, index = layout[nm]
        return pl.BlockSpec(block, index)

    if ns == 1:
        io_spec = lambda w: pl.BlockSpec((qt, w), lambda b, c, l: (b * nc + c, 0))
        y_shape = (nb * nc * qt, d)
        cnames = ("trig", "tri", "dmask", "dec", "perm")
    else:
        proj = proj.reshape(r, nb * ns, dp)
        io_spec = lambda w: pl.BlockSpec((r, ns, w), lambda b, c, l: (0, b, 0))
        y_shape = (r, nb * ns, d)
        cnames = ("trig", "tri", "dmask", "dec")
    names = ("proj",) + cnames + MIX_WEIGHTS
    operands = [proj] + [consts[k] for k in cnames] + [mw[k] for k in MIX_WEIGHTS]
    in_specs = [io_spec(dp), pl.BlockSpec((3, qt, LANES), lambda b, c, l: (0, c, 0))] + \
               [const_spec(consts[k]) for k in cnames[1:]] + [layer_spec(mw[k]) for k in MIX_WEIGHTS]
    if state_in is not None:
        names += tuple(nm + "_i" for nm in STATE_NAMES)
        operands += [state_in[nm] for nm in STATE_NAMES]
        in_specs += [state_spec(nm) for nm in STATE_NAMES]
    first_alias = 1 + len(operands)
    names += tuple(nm + "_alias" for nm in STATE_NAMES)
    operands += [state_out[nm] for nm in STATE_NAMES]
    in_specs += [pl.BlockSpec(memory_space=pl.ANY) for _ in STATE_NAMES]
    names += ("y",) + tuple(nm + "_o" for nm in STATE_NAMES)
    out_specs = [io_spec(d)] + [state_spec(nm) for nm in STATE_NAMES]
    out_shape = [jax.ShapeDtypeStruct(y_shape, MXU_DTYPE)] + \
                [jax.ShapeDtypeStruct(layout[nm][0], F32) for nm in STATE_NAMES]
    scratch = []
    if ns == 1:
        tail = (CONV_K - 1) * SUBLANES
        scratch_shapes = dict(cb_a=(tail, gw), cb_c=(tail, 2 * gw), c_lh=(1, gw), c_s5r=(1, sn),
                              c_s5i=(1, sn), c_sh=(SSD_G * SSD_N, gw), c_ret=(RET_H * RET_DK, gw))
        names += tuple(scratch_shapes)
        scratch = [pltpu.VMEM(shape, F32) for shape in scratch_shapes.values()]
    grid_spec = pltpu.PrefetchScalarGridSpec(num_scalar_prefetch=1, grid=(nb, nc),
                                             in_specs=in_specs, out_specs=out_specs, scratch_shapes=scratch)
    outs = pl.pallas_call(
        functools.partial(_mix_body, names=names, qt=qt, ns=ns, r=r, nc=nc),
        grid_spec=grid_spec,
        out_shape=out_shape,
        input_output_aliases={first_alias + k: 1 + k for k in range(len(STATE_NAMES))},
        compiler_params=pltpu.CompilerParams(dimension_semantics=("arbitrary", "arbitrary"),
                                             vmem_limit_bytes=VMEM_LIMIT),
        name="mixer",
    )(l, *operands)
    return outs[0].reshape(-1, d), dict(zip(STATE_NAMES, outs[1:]))


def _post_body(l_ref, y_ref, x_ref, g1_ref, sc_ref, sh_ref, g2_ref, lt_ref, wo_ref, wr_ref, br_ref,
               wg_ref, wu_ref, wd_ref, o_ref, xs_ref, cs_ref, os_ref):
    del l_ref
    tm, d = x_ref.shape
    tms = xs_ref.shape[0]
    x = x_ref[...] + _token_rows(g1_ref[...], tm) * jnp.dot(y_ref[...], wo_ref[...], preferred_element_type=F32)
    h = _rms(x) * (1.0 + _token_rows(sc_ref[...], tm)) + _token_rows(sh_ref[...], tm)
    hb = h.astype(MXU_DTYPE)
    h_lo = (h - hb.astype(F32)).astype(MXU_DTYPE)
    wr = wr_ref[...]
    l12 = jnp.dot(hb, wr, preferred_element_type=F32)
    logits = (l12[:, 0:LANES] + l12[:, LANES:2 * LANES]
              + jnp.dot(h_lo, wr[:, 0:LANES], preferred_element_type=F32) + br_ref[...])
    logits_t = jnp.transpose(logits)
    col = [logits_t[k:k + 1, :] for k in range(N_GROUPS + N_EXPERTS)]

    def first_max(vals, allowed=None):
        neg = jnp.full_like(vals[0], -jnp.inf)
        cand = vals if allowed is None else [jnp.where(al > 0.0, v, neg) for v, al in zip(vals, allowed)]
        m = functools.reduce(jnp.maximum, cand)
        rem = jnp.ones_like(vals[0])
        hot = []
        for v in cand:
            f = jnp.where(v >= m, rem, 0.0)
            rem = rem - f
            hot.append(f)
        return hot, m

    grp, gmax = first_max(col[:N_GROUPS])
    gate = 1.0 / functools.reduce(lambda s, v: s + v, [jnp.exp(v - gmax) for v in col[:N_GROUPS]])
    le = [functools.reduce(lambda s, v: s + v,
                           [grp[gi] * col[N_GROUPS + gi * N_PER_GROUP + j] for gi in range(N_GROUPS)])
          for j in range(N_PER_GROUP)]
    top1, m1 = first_max(le)
    top2, m2 = first_max(le, [1.0 - f for f in top1])
    e2 = jnp.exp(m2 - m1)
    w1 = 1.0 / (1.0 + e2)
    w2 = e2 * w1
    wgrp = [(top1[j] * w1 + top2[j] * w2) * gate for j in range(N_PER_GROUP)]

    sub8 = lax.broadcasted_iota(jnp.int32, (SUBLANES, 1), 0)
    sub_e = lax.broadcasted_iota(jnp.int32, (cs_ref.shape[1], 1), 0)
    ghot_t = functools.reduce(lambda s, v: s + v, [jnp.where(sub8 == gi, grp[gi], 0.0) for gi in range(N_GROUPS)])
    comb_t = functools.reduce(lambda s, v: s + v,
                              [jnp.where(sub_e == e, grp[e // N_PER_GROUP] * wgrp[e % N_PER_GROUP], 0.0)
                               for e in range(N_EXPERTS)])
    before = _mm_nt(ghot_t, lt_ref[...])
    count = jnp.sum(ghot_t, axis=1, keepdims=True)
    cnt = [count[gi:gi + 1, :] for gi in range(N_GROUPS)]
    base = [jnp.zeros((1, 1), F32)]
    for gi in range(1, N_GROUPS):
        base.append(jnp.floor((base[-1] + cnt[gi - 1] + (PACK - 1.0)) * (1.0 / PACK)) * PACK)
    pos_row = functools.reduce(lambda s, v: s + v,
                               [grp[gi] * (base[gi] + before[gi:gi + 1, :]) for gi in range(N_GROUPS)])
    pos = jnp.transpose(jnp.broadcast_to(pos_row, (LANES, tm)))[:, 0:1]
    slot = lax.broadcasted_iota(jnp.int32, (tms, tm), 0).astype(F32)
    gather = jnp.where(slot == pos_row, 1.0, 0.0).astype(MXU_DTYPE)
    slot_l = lax.broadcasted_iota(jnp.int32, (tm, tms), 1).astype(F32)
    scatter = jnp.where(slot_l == pos, 1.0, 0.0).astype(MXU_DTYPE)

    xs_ref[...] = jnp.dot(gather, hb, preferred_element_type=F32).astype(xs_ref.dtype)
    c1, c2, c3 = _split3(comb_t)
    nt = lambda p: lax.dot_general(gather, p, (((1,), (1,)), ((), ())), preferred_element_type=F32)
    cs_ref[...] = nt(c1) + nt(c2) + nt(c3)
    os_ref[...] = jnp.zeros(os_ref.shape, os_ref.dtype)
    for gi in range(N_GROUPS):
        start = base[gi][0, 0].astype(jnp.int32)
        nblk = jnp.floor((cnt[gi][0, 0] + (EXPERT_ROWS - 1.0)) * (1.0 / EXPERT_ROWS)).astype(jnp.int32)

        def block(k, carry, gi=gi, start=start):
            rows = pl.ds(pl.multiple_of(start + k * EXPERT_ROWS, PACK), EXPERT_ROWS)
            xb = xs_ref[rows, :]
            cb = cs_ref[rows, :]
            acc = jnp.zeros((EXPERT_ROWS, d), F32)
            for j in range(N_PER_GROUP):
                e = gi * N_PER_GROUP + j
                gt = jnp.dot(xb, wg_ref[e], preferred_element_type=F32)
                he = gt * _sigmoid(gt) * jnp.dot(xb, wu_ref[e], preferred_element_type=F32)
                acc = acc + jnp.dot((he * cb[:, e:e + 1]).astype(MXU_DTYPE), wd_ref[e], preferred_element_type=F32)
            os_ref[rows, :] += acc
            return carry

        lax.fori_loop(0, nblk, block, 0)
    moe = _select_rows(scatter, os_ref[...], pieces=2)
    o_ref[...] = x + _token_rows(g2_ref[...], tm) * moe


def _post(l, y, x, mod, pw, tm, rows_per_seq):
    t, d = x.shape
    tms = tm + EXPERT_ROWS + PACK * N_GROUPS
    ltri = jnp.asarray(np.tril(np.ones((tm, tm), np.float32), -1)).astype(MXU_DTYPE)

    def layer_spec(a):
        return pl.BlockSpec((None,) + a.shape[1:], lambda i, l: (l[0],) + (0,) * (a.ndim - 1),
                            pipeline_mode=pl.Buffered(1))

    grid_spec = pltpu.PrefetchScalarGridSpec(
        num_scalar_prefetch=1,
        grid=(t // tm,),
        in_specs=[
            pl.BlockSpec((tm, d), lambda i, l: (i, 0)),
            pl.BlockSpec((tm, d), lambda i, l: (i, 0)),
            _mod_spec(mod, 2, tm, rows_per_seq),
            _mod_spec(mod, 4, tm, rows_per_seq),
            _mod_spec(mod, 3, tm, rows_per_seq),
            _mod_spec(mod, 5, tm, rows_per_seq),
            pl.BlockSpec((tm, tm), lambda i, l: (0, 0)),
        ] + [layer_spec(pw[k]) for k in ("w_out", "w_route", "b_route", "w_gate", "w_up", "w_down")],
        out_specs=pl.BlockSpec((tm, d), lambda i, l: (i, 0)),
        scratch_shapes=[pltpu.VMEM((tms, d), MXU_DTYPE), pltpu.VMEM((tms, 2 * N_EXPERTS), F32),
                        pltpu.VMEM((tms, d), F32)],
    )
    return pl.pallas_call(
        _post_body,
        grid_spec=grid_spec,
        out_shape=jax.ShapeDtypeStruct((t, d), F32),
        compiler_params=pltpu.CompilerParams(dimension_semantics=("arbitrary",), vmem_limit_bytes=VMEM_LIMIT),
        name="post",
    )(l, y, x, mod, mod, mod, mod, ltri,
      *[pw[k] for k in ("w_out", "w_route", "b_route", "w_gate", "w_up", "w_down")])


def _final_body(x_ref, w_ref, o_ref):
    o_ref[...] = _rms(x_ref[...]) * w_ref[...]


def _final_norm(x, w, tm):
    t, d = x.shape
    return pl.pallas_call(
        _final_body,
        grid=(t // tm,),
        in_specs=[pl.BlockSpec((tm, d), lambda i: (i, 0)), pl.BlockSpec((1, d), lambda i: (0, 0))],
        out_specs=pl.BlockSpec((tm, d), lambda i: (i, 0)),
        out_shape=jax.ShapeDtypeStruct((t, d), F32),
        name="final_norm",
    )(x, w.reshape(1, d))


def _block_diag(w):
    depth, nblk, bi, bj = w.shape
    eye = jnp.eye(nblk, dtype=w.dtype)
    return (w[:, :, :, None, :] * eye[None, :, None, :, None]).reshape(depth, nblk * bi, nblk * bj)


def _pad_lanes(a, width):
    return jnp.pad(a, [(0, 0)] * (a.ndim - 1) + [(0, width - a.shape[-1])])


def _prep_weights(w_in, lru_conv_w, lru_conv_b, lru_wa, lru_ba, lru_wx, lru_bx, lru_lambda,
                  s5_a_re, s5_a_im, s5_b_re, s5_b_im, s5_c_re, s5_c_im, s5_d, s5_log_dt, s5_w_glu,
                  ssd_conv_w, ssd_conv_b, ssd_dt_bias, ssd_a_log, ssd_d, mix_norm, qt):
    depth, d, _ = w_in.shape
    gw = d // 4
    n_ssd_h = gw // SSD_HD
    xbc_end = 4 * gw + gw + 2 * SSD_G * SSD_N
    dt_cols = jnp.repeat(w_in[..., xbc_end:xbc_end + n_ssd_h], SSD_HD, axis=-1)
    w_in_p = jnp.concatenate([w_in[..., :xbc_end], w_in[..., xbc_end + n_ssd_h:], dt_cols], axis=-1)

    def row(a):
        return _pad_lanes(a.reshape(depth, 1, -1), d)

    vec = jnp.concatenate([
        row(lru_conv_b), row(jnp.concatenate([lru_ba, lru_bx], -1)), row(lru_lambda), row(s5_d),
        row(ssd_conv_b), row(jnp.repeat(ssd_dt_bias, SSD_HD, -1)), row(jnp.repeat(ssd_a_log, SSD_HD, -1)),
        row(jnp.repeat(ssd_d, SSD_HD, -1)), row(mix_norm),
        _pad_lanes(lru_conv_w, d), _pad_lanes(ssd_conv_w, d),
        jnp.zeros((depth, V_ROWS - V_SCW - CONV_K, d), F32)], axis=1)

    wgate = jnp.concatenate([_block_diag(lru_wa), _block_diag(lru_wx)], axis=-1)

    dt = jnp.exp(s5_log_dt)[..., None]
    lr, li = s5_a_re, s5_a_im
    mag = jnp.exp(lr * dt)
    ab_re, ab_im = mag * jnp.cos(li * dt), mag * jnp.sin(li * dt)
    den = lr * lr + li * li
    q_re = ((ab_re - 1.0) * lr + ab_im * li) / den
    q_im = (ab_im * lr - (ab_re - 1.0) * li) / den
    bb_re = q_re[..., None] * s5_b_re - q_im[..., None] * s5_b_im
    bb_im = q_re[..., None] * s5_b_im + q_im[..., None] * s5_b_re
    bbm = jnp.concatenate([_block_diag(jnp.swapaxes(bb_re, -1, -2)), _block_diag(jnp.swapaxes(bb_im, -1, -2))], -1)
    cre = _block_diag(jnp.swapaxes(s5_c_re, -1, -2))
    cim = _block_diag(jnp.swapaxes(s5_c_im, -1, -2))
    a_re, a_im = ab_re.reshape(depth, -1), ab_im.reshape(depth, -1)
    pr, pi = a_re, a_im
    rows = []
    for _ in range(int(math.log2(qt))):
        rows += [pr, pi]
        pr, pi = pr * pr - pi * pi, 2.0 * pr * pi
    pw = jnp.stack(rows, axis=1)
    pw = jnp.pad(pw, ((0, 0), (0, (-pw.shape[1]) % SUBLANES), (0, 0)))
    pr, pi = a_re, a_im
    k_re, k_im = [], []
    for _ in range(qt // SUBLANES):
        k_re.append(pr)
        k_im.append(pi)
        pr, pi = pr * a_re - pi * a_im, pr * a_im + pi * a_re
    pk = jnp.stack(k_re + k_im, axis=1)

    mw = dict(vec=vec, wgate=wgate.astype(MXU_DTYPE), bbm=bbm.astype(MXU_DTYPE), cre=cre.astype(MXU_DTYPE),
              cim=cim.astype(MXU_DTYPE), wglu=s5_w_glu.astype(MXU_DTYPE), pw=pw, pk=pk)
    return w_in_p.astype(MXU_DTYPE), mw


def _group_of_head(n_h):
    return (jnp.arange(SSD_G)[:, None] == jnp.arange(n_h)[None, :] // (n_h // SSD_G)).astype(F32)


def _states_to_layout(states, layout, ns):
    out = {}
    for nm, s in zip(STATE_NAMES, states):
        if nm in ("lconv", "sconv"):
            s = s.transpose(0, 2, 1, 3)
        out[nm] = s.reshape(layout[nm][0])
    return out


def _states_from_layout(arrs, ns, ref_shapes):
    out = []
    for nm, shape in zip(STATE_NAMES, ref_shapes):
        a = arrs[nm]
        if ns > 1 and nm in ("lconv", "sconv"):
            a = a.transpose(0, 2, 1, 3)
        out.append(a.reshape(shape))
    return tuple(out)


def _trunk(x, mod, states, ref_shapes, pos0, w_in_p, mw, pw, final_norm, *, qt, ns, tm):
    bsz, seq_len, d = x.shape
    depth = w_in_p.shape[0]
    gw = d // 4
    r = qt // ns
    nc = seq_len // r if ns == 1 else 1
    assert (ns == 1 and seq_len % qt == 0 and states is None) or (ns > 1 and r == seq_len and bsz % ns == 0)
    assert r >= CONV_K and (r & (r - 1)) == 0
    nb = bsz // ns
    tok = bsz * seq_len
    consts = _mix_consts(qt, ns, r, nc, pos0)
    layout = _state_layout(depth, bsz, ns, gw, mw["pw"].shape[-1])
    state_in = None if states is None else _states_to_layout(states, layout, ns)
    state_out = {nm: jnp.zeros(layout[nm][0], F32) for nm in STATE_NAMES}
    x2 = x.reshape(tok, d) if ns == 1 else x.transpose(1, 0, 2).reshape(tok, d)
    perm = consts["perm"] if ns == 1 else None

    def layer(carry, l):
        xc, st = carry
        lv = jnp.reshape(l, (1,)).astype(jnp.int32)
        proj = _inproj(lv, xc, mod, w_in_p, tm, seq_len, perm)
        y, st = _mixer(lv, proj, consts, mw, state_in, st, nb=nb, nc=nc, qt=qt, ns=ns, r=r, d=d)
        xn = _post(lv, y, xc, mod, pw, tm, seq_len)
        return (xn, st), None

    (xf, st), _ = lax.scan(layer, (x2, state_out), jnp.arange(depth))
    y = _final_norm(xf, final_norm, tm)
    y = y.reshape(bsz, seq_len, d) if ns == 1 else y.reshape(seq_len, bsz, d).transpose(1, 0, 2)
    return y, _states_from_layout(st, ns, ref_shapes)


def kernel(x_prompt, x_sample, state_lru_conv, state_lru_h, state_s5_re, state_s5_im, state_ssd_conv, state_ssd_h, state_ret, c_prompt, c_sample, w_ada, b_ada, w_in, lru_conv_w, lru_conv_b, lru_wa, lru_ba, lru_wx, lru_bx, lru_lambda, s5_a_re, s5_a_im, s5_b_re, s5_b_im, s5_c_re, s5_c_im, s5_d, s5_log_dt, s5_w_glu, ssd_conv_w, ssd_conv_b, ssd_dt_bias, ssd_a_log, ssd_d, mix_norm, w_out, w_route_group, b_route_group, w_route_exp, b_route_exp, w_exp_gate, w_exp_up, w_exp_down, final_norm):
    bp, lp, d = x_prompt.shape
    bs, ls, _ = x_sample.shape
    depth = w_in.shape[0]
    qt = LANES
    ns_s = qt // ls

    mod = _modulation(jnp.concatenate([c_prompt, c_sample], axis=0), w_ada, b_ada)
    mod_p = mod[:, :, :bp].reshape(depth, N_MOD, bp, 1, d)
    mod_s = mod[:, :, bp:]

    w_in_p, mw = _prep_weights(w_in, lru_conv_w, lru_conv_b, lru_wa, lru_ba, lru_wx, lru_bx, lru_lambda,
                               s5_a_re, s5_a_im, s5_b_re, s5_b_im, s5_c_re, s5_c_im, s5_d, s5_log_dt, s5_w_glu,
                               ssd_conv_w, ssd_conv_b, ssd_dt_bias, ssd_a_log, ssd_d, mix_norm, qt)
    w_route = _pad_lanes(jnp.concatenate([w_route_group, w_route_exp.reshape(depth, d, N_EXPERTS)], -1), LANES)
    b_route = _pad_lanes(jnp.concatenate([b_route_group, b_route_exp.reshape(depth, N_EXPERTS)], -1), LANES)
    w_route_hi = w_route.astype(MXU_DTYPE)
    w_route_lo = (w_route - w_route_hi.astype(F32)).astype(MXU_DTYPE)
    w_route = jnp.concatenate([w_route_hi, w_route_lo], axis=-1)
    pw = dict(w_out=w_out.astype(MXU_DTYPE), w_route=w_route, b_route=b_route.reshape(depth, 1, LANES),
              w_gate=w_exp_gate.astype(MXU_DTYPE), w_up=w_exp_up.astype(MXU_DTYPE),
              w_down=w_exp_down.astype(MXU_DTYPE))

    states_s = (state_lru_conv, state_lru_h, state_s5_re, state_s5_im, state_ssd_conv, state_ssd_h, state_ret)
    shapes_s = [s.shape for s in states_s]
    shapes_p = [(depth, bp) + s[2:] for s in shapes_s]
    y_p, new_p = _trunk(x_prompt, mod_p, None, shapes_p, 0, w_in_p, mw, pw, final_norm,
                        qt=qt, ns=1, tm=min(512, lp))
    y_s, new_s = _trunk(x_sample, mod_s, states_s, shapes_s, PAST_LEN, w_in_p, mw, pw, final_norm,
                        qt=qt, ns=ns_s, tm=bs * ls)
    out = [y_p, y_s]
    for a, b in zip(new_p, new_s):
        out += [a, b]
    return tuple(out)
```

```python
import functools
import math

import numpy as np
import jax
import jax.numpy as jnp
from jax import lax
from jax.experimental import pallas as pl
from jax.experimental.pallas import tpu as pltpu

F32 = jnp.float32
MXU_DTYPE = jnp.bfloat16
HIGHEST = lax.Precision.HIGHEST

EPS = 1e-6
CONV_K = 4
LRU_C = 8.0
SSD_HD = 64
SSD_G = 2
SSD_N = 64
RET_H = 4
RET_DK = 32
ROPE_BASE = 10000.0
N_GROUPS = 4
N_PER_GROUP = 4
N_EXPERTS = N_GROUPS * N_PER_GROUP
N_MOD = 6
PAST_LEN = 16384

LANES = 128
SUBLANES = 8
PACK = 16
VMEM_LIMIT = 56 * 1024 * 1024
EXPERT_ROWS = 144


def _mm(a, b):
    return jnp.dot(a.astype(MXU_DTYPE), b.astype(MXU_DTYPE), preferred_element_type=F32)


def _mm_nt(a, b):
    return lax.dot_general(a.astype(MXU_DTYPE), b.astype(MXU_DTYPE), (((1,), (1,)), ((), ())),
                           preferred_element_type=F32)


def _mm_tn(a, b):
    return lax.dot_general(a.astype(MXU_DTYPE), b.astype(MXU_DTYPE), (((0,), (0,)), ((), ())),
                           preferred_element_type=F32)


def _split3(x):
    x1 = x.astype(MXU_DTYPE)
    r1 = x - x1.astype(F32)
    x2 = r1.astype(MXU_DTYPE)
    x3 = (r1 - x2.astype(F32)).astype(MXU_DTYPE)
    return x1, x2, x3


def _select_rows(onehot, x, pieces=3):
    parts = _split3(x)[:pieces]
    out = jnp.dot(onehot, parts[0], preferred_element_type=F32)
    for p in parts[1:]:
        out = out + jnp.dot(onehot, p, preferred_element_type=F32)
    return out


def _sigmoid(x):
    return 0.5 * jnp.tanh(0.5 * x) + 0.5


def _rms(x):
    return x * lax.rsqrt(jnp.mean(x * x, axis=-1, keepdims=True) + EPS)


def _token_rows(v, tm):
    n = v.shape[0]
    return v if n in (1, tm) else jnp.tile(v, (tm // n, 1))


def _mod_body(c_ref, w_ref, b_ref, o_ref):
    c = c_ref[...]
    o_ref[...] = _mm(c * _sigmoid(c), w_ref[...]) + b_ref[...]


def _modulation(c_all, w_ada, b_ada):
    depth, d, _ = w_ada.shape
    nb = c_all.shape[0]
    return pl.pallas_call(
        _mod_body,
        grid=(depth, N_MOD),
        in_specs=[
            pl.BlockSpec((nb, d), lambda l, k: (0, 0)),
            pl.BlockSpec((None, d, d), lambda l, k: (l, 0, k)),
            pl.BlockSpec((None, None, 1, d), lambda l, k: (l, k, 0, 0)),
        ],
        out_specs=pl.BlockSpec((None, None, nb, d), lambda l, k: (l, k, 0, 0)),
        out_shape=jax.ShapeDtypeStruct((depth, N_MOD, nb, d), F32),
        compiler_params=pltpu.CompilerParams(dimension_semantics=("arbitrary", "arbitrary"),
                                             vmem_limit_bytes=VMEM_LIMIT),
        name="modulation",
    )(c_all, w_ada, b_ada.reshape(depth, N_MOD, 1, d))


def _mod_spec(mod, k, tm, rows_per_seq):
    if mod.ndim == 5:
        tiles_per_seq = rows_per_seq // tm
        return pl.BlockSpec((None, None, None, 1, mod.shape[-1]),
                            lambda i, l: (l[0], k, i // tiles_per_seq, 0, 0))
    return pl.BlockSpec((None, None) + mod.shape[2:], lambda i, l: (l[0], k, 0, 0))


def _inproj_body(l_ref, x_ref, sc_ref, sh_ref, w_ref, o_ref, *, blocked_chunk):
    tm = x_ref.shape[0]
    h = _rms(x_ref[...]) * (1.0 + _token_rows(sc_ref[...], tm)) + _token_rows(sh_ref[...], tm)
    hb = h.astype(MXU_DTYPE)
    if blocked_chunk is not None:
        qt, nk = blocked_chunk, blocked_chunk // SUBLANES
        r_i = lax.broadcasted_iota(jnp.int32, (qt, qt), 0)
        c_i = lax.broadcasted_iota(jnp.int32, (qt, qt), 1)
        to_blocked = jnp.where(c_i == (r_i % SUBLANES) * nk + r_i // SUBLANES, 1.0, 0.0).astype(MXU_DTYPE)
        hb = jnp.concatenate([jnp.dot(to_blocked, hb[j * qt:(j + 1) * qt], preferred_element_type=F32)
                              for j in range(tm // qt)], axis=0).astype(MXU_DTYPE)
    o_ref[...] = jnp.dot(hb, w_ref[...], preferred_element_type=F32)


def _inproj(l, x, mod, w_in, tm, rows_per_seq, blocked_chunk=None):
    t, d = x.shape
    dp = w_in.shape[-1]
    assert blocked_chunk is None or (mod.ndim == 5 and tm % blocked_chunk == 0)
    grid_spec = pltpu.PrefetchScalarGridSpec(
        num_scalar_prefetch=1,
        grid=(t // tm,),
        in_specs=[
            pl.BlockSpec((tm, d), lambda i, l: (i, 0)),
            _mod_spec(mod, 1, tm, rows_per_seq),
            _mod_spec(mod, 0, tm, rows_per_seq),
            pl.BlockSpec((None, d, dp), lambda i, l: (l[0], 0, 0)),
        ],
        out_specs=pl.BlockSpec((tm, dp), lambda i, l: (i, 0)),
    )
    return pl.pallas_call(
        functools.partial(_inproj_body, blocked_chunk=blocked_chunk),
        grid_spec=grid_spec,
        out_shape=jax.ShapeDtypeStruct((t, dp), F32),
        compiler_params=pltpu.CompilerParams(dimension_semantics=("arbitrary",), vmem_limit_bytes=VMEM_LIMIT),
        name="inproj",
    )(l, x, mod, mod, w_in)


V_LCB, V_BGATE, V_LAM, V_S5D, V_SCB, V_DTB, V_ALOG, V_SSDD, V_GAIN, V_LCW, V_SCW = 0, 1, 2, 3, 4, 5, 6, 7, 8, 9, 13
V_ROWS = 24
STATE_NAMES = ("lconv", "lh", "s5r", "s5i", "sconv", "sh", "ret")
MIX_WEIGHTS = ("vec", "wgate", "bbm", "cre", "cim", "wglu", "pw", "pk")


def _mix_body(l_ref, *refs, names, qt, ns, r, nc):
    del l_ref
    g = dict(zip(names, refs))
    gw = g["lh_o"].shape[-1]
    sn = g["s5r_o"].shape[-1]
    nlev = int(math.log2(r))
    nk = qt // SUBLANES
    c = pl.program_id(1)

    if ns == 1:
        @pl.when(c == 0)
        def _zero_carries():
            for nm in ("c_lh", "c_s5r", "c_s5i", "c_sh", "c_ret"):
                g[nm][...] = jnp.zeros(g[nm].shape, F32)
            for nm in ("cb_a", "cb_c"):
                g[nm][...] = jnp.zeros(g[nm].shape, F32)

    proj = g["proj"]
    vec = g["vec"]
    pw = g["pw"]

    def pcols(a, b):
        v = proj[..., a:b]
        return v if ns == 1 else v.reshape(qt, b - a)

    y_parts = []

    def ysave(a, b, val):
        val = val.astype(g["y"].dtype)
        if ns == 1:
            y_parts.append(val)
        else:
            g["y"][..., a:b] = val.reshape(r, ns, b - a)

    row = lax.broadcasted_iota(jnp.int32, (qt, 1), 0)
    t = (row % SUBLANES) * nk + row // SUBLANES if ns == 1 else row // ns
    sub = lax.broadcasted_iota(jnp.int32, (SUBLANES, 1), 0)

    def down(x, d, fill=0.0):
        return jnp.where(t >= d, pltpu.roll(x, d * ns, 0), fill)

    def up(x, d):
        return jnp.where(t + d < r, pltpu.roll(x, qt - d * ns, 0), 0.0)

    def sdown(x, d, fill=0.0):
        return jnp.where(sub >= d, pltpu.roll(x, d, 0), fill)

    def slabs(x):
        return [x[SUBLANES * k:SUBLANES * (k + 1)] for k in range(nk)]

    def first_rows(carry, init):
        if ns == 1:
            return jnp.where(row == 0, g[carry][...], 0.0)
        h0 = g[init][...]
        return jnp.concatenate([h0, jnp.zeros((qt - ns, h0.shape[1]), F32)], axis=0)

    def conv(xraw, nm, w0, b):
        width = xraw.shape[-1]
        acc = vec[b:b + 1, 0:width] + vec[w0 + 3:w0 + 4, 0:width] * xraw
        if ns == 1:
            cb = g["cb_" + nm]
            xs_k = slabs(xraw)
            wrap = [jnp.where(sub >= 1, pltpu.roll(xs_k[nk - j], 1, 0),
                              pltpu.roll(cb[(CONV_K - 1 - j) * SUBLANES:(CONV_K - j) * SUBLANES, :], 1, 0))
                    for j in range(1, CONV_K)]
            for m in range(1, CONV_K):
                sh = jnp.concatenate([wrap[m - k - 1] for k in range(m)] + xs_k[0:nk - m], axis=0)
                acc = acc + vec[w0 + 3 - m:w0 + 4 - m, 0:width] * sh
            cb[...] = xraw[qt - (CONV_K - 1) * SUBLANES:, :]
            g[{"a": "lconv_o", "c": "sconv_o"}[nm]][...] = jnp.concatenate(
                [xs_k[nk - j][SUBLANES - 1:SUBLANES, :] for j in range(CONV_K - 1, 0, -1)], axis=0)
            return acc
        o = g[{"a": "lconv_o", "c": "sconv_o"}[nm]]
        buf = g[{"a": "lconv_i", "c": "sconv_i"}[nm]][...].reshape((CONV_K - 1) * ns, width)
        buf = jnp.concatenate([buf, jnp.zeros((qt - (CONV_K - 1) * ns, width), F32)], axis=0)
        for m in range(1, CONV_K):
            prev = buf if m == 3 else pltpu.roll(buf, qt - (3 - m) * ns, 0)
            sh = jnp.where(t >= m, pltpu.roll(xraw, m * ns, 0), prev)
            acc = acc + vec[w0 + 3 - m:w0 + 4 - m, 0:width] * sh
        o[...] = xraw[(r - (CONV_K - 1)) * ns:, :].reshape(CONV_K - 1, ns, width)
        return acc

    def save_last(nm, h):
        if ns == 1:
            g["c_" + nm][...] = h[qt - 1:qt, :]
            g[nm + "_o"][...] = h[qt - 1:qt, :]
        else:
            g[nm + "_o"][...] = h[qt - ns:, :]

    def seq_mask(width):
        lane_seq = lax.broadcasted_iota(jnp.int32, (qt, ns * width), 1) // width
        row_seq = lax.broadcasted_iota(jnp.int32, (qt, ns * width), 0) % ns
        return (lane_seq == row_seq).astype(F32)

    gain = vec[V_GAIN:V_GAIN + 1, :]

    xc = conv(pcols(0, gw), "a", V_LCW, V_LCB)
    ga = pcols(gw, 2 * gw)
    pre = _mm(xc, g["wgate"][...]) + vec[V_BGATE:V_BGATE + 1, 0:2 * gw]
    rg = _sigmoid(pre[:, 0:gw])
    ig = _sigmoid(pre[:, gw:2 * gw])
    log_a = -LRU_C * rg * jax.nn.softplus(-vec[V_LAM:V_LAM + 1, 0:gw])
    a = jnp.exp(log_a)
    b = jnp.sqrt(-jnp.tanh(log_a) * (a * a + 1.0)) * (ig * xc)
    b = b + a * first_rows("c_lh", "lh_i")
    if ns == 1:
        a_s, b_s = slabs(a), slabs(b)
        hs, ps = [b_s[0]], [a_s[0]]
        for k in range(1, nk):
            hs.append(a_s[k] * hs[-1] + b_s[k])
            ps.append(a_s[k] * ps[-1])
        e, ae = hs[-1], ps[-1]
        for j in range(int(math.log2(SUBLANES))):
            d = 1 << j
            e = ae * sdown(e, d) + e
            ae = ae * sdown(ae, d, 1.0)
        cin = sdown(e, 1)
        b = jnp.concatenate([hs[k] + ps[k] * cin for k in range(nk)], axis=0)
    else:
        for k in range(nlev):
            d = 1 << k
            b = a * down(b, d) + b
            a = a * down(a, d, 1.0)
    save_last("lh", b)
    ya = _rms(b * jax.nn.gelu(ga)) * gain[:, 0:gw]

    u = pcols(2 * gw, 3 * gw)
    bu = _mm(u, g["bbm"][...])
    p_re, p_im = pw[0:1, :], pw[1:2, :]
    h0r, h0i = first_rows("c_s5r", "s5r_i"), first_rows("c_s5i", "s5i_i")
    hr = bu[:, 0:sn] + (p_re * h0r - p_im * h0i)
    hi = bu[:, sn:2 * sn] + (p_re * h0i + p_im * h0r)
    if ns == 1:
        pk = g["pk"]
        br_s, bi_s = slabs(hr), slabs(hi)
        hrs, his = [br_s[0]], [bi_s[0]]
        for k in range(1, nk):
            hrs.append(br_s[k] + (p_re * hrs[-1] - p_im * his[-1]))
            his.append(bi_s[k] + (p_re * his[-1] + p_im * hrs[-2]))
        er, ei = hrs[-1], his[-1]
        lev0 = int(math.log2(nk))
        for j in range(int(math.log2(SUBLANES))):
            d = 1 << j
            q_re, q_im = pw[2 * (lev0 + j):2 * (lev0 + j) + 1, :], pw[2 * (lev0 + j) + 1:2 * (lev0 + j) + 2, :]
            sr, si = sdown(er, d), sdown(ei, d)
            er, ei = er + (q_re * sr - q_im * si), ei + (q_re * si + q_im * sr)
        cr, ci = sdown(er, 1), sdown(ei, 1)
        hr = jnp.concatenate([hrs[k] + (pk[k:k + 1, :] * cr - pk[nk + k:nk + k + 1, :] * ci) for k in range(nk)], 0)
        hi = jnp.concatenate([his[k] + (pk[k:k + 1, :] * ci + pk[nk + k:nk + k + 1, :] * cr) for k in range(nk)], 0)
    else:
        for k in range(nlev):
            d = 1 << k
            q_re, q_im = pw[2 * k:2 * k + 1, :], pw[2 * k + 1:2 * k + 2, :]
            sr, si = down(hr, d), down(hi, d)
            hr, hi = hr + (q_re * sr - q_im * si), hi + (q_re * si + q_im * sr)
    save_last("s5r", hr)
    save_last("s5i", hi)
    yb = _mm(hr, g["cre"][...]) - _mm(hi, g["cim"][...])
    yb = jax.nn.gelu(yb + vec[V_S5D:V_S5D + 1, 0:gw] * u)
    yb = yb * _sigmoid(_mm(yb, g["wglu"][...]))
    yb = _rms(yb) * gain[:, gw:2 * gw]
    ysave(0, 2 * gw, jnp.concatenate([ya, yb], axis=1))

    tri = g["tri"][...] > 0.0
    xbc = conv(pcols(4 * gw, 6 * gw), "c", V_SCW, V_SCB)
    xbc = xbc * _sigmoid(xbc)
    xs, bm, cm = xbc[:, 0:gw], xbc[:, gw:gw + LANES], xbc[:, gw + LANES:2 * gw]
    dt = jax.nn.softplus(pcols(9 * gw, 10 * gw) + vec[V_DTB:V_DTB + 1, 0:gw])
    a_e = -jnp.exp(vec[V_ALOG:V_ALOG + 1, 0:gw])
    dta = dt * a_e
    if ns == 1:
        cs = slabs(dta)
        for k in range(1, nk):
            cs[k] = cs[k] + cs[k - 1]
        e = cs[-1]
        for j in range(int(math.log2(SUBLANES))):
            e = e + sdown(e, 1 << j)
        cin = sdown(e, 1)
        acum = jnp.concatenate([ck + cin for ck in cs], axis=0)
        alast = acum[qt - 1:qt, :]
    else:
        acum = dta
        for k in range(nlev):
            acum = acum + down(acum, 1 << k)
        suf = dta
        for k in range(nlev):
            suf = suf + up(suf, 1 << k)
        alast = acum + suf - dta
    wend = jnp.exp(alast - acum) * dt
    acum_t = jnp.transpose(acum)
    lane = lax.broadcasted_iota(jnp.int32, (1, gw), 1)
    lane_b = lax.broadcasted_iota(jnp.int32, (1, LANES), 1)
    heads_per_group = gw // SSD_HD // SSD_G
    xdt = xs * dt
    xw = xs * wend
    ydiag = jnp.zeros((qt, gw), F32)
    for grp in range(SSD_G):
        gm = ((lane_b // SSD_N) == grp).astype(F32)
        cb = _mm_nt(cm * gm, bm)
        for hh in range(heads_per_group):
            hd = grp * heads_per_group + hh
            col = acum[:, hd * SSD_HD:hd * SSD_HD + 1]
            rowv = acum_t[hd * SSD_HD:hd * SSD_HD + 1, :]
            dec = jnp.where(tri, jnp.exp(col - rowv), 0.0)
            hm = ((lane // SSD_HD) == hd).astype(F32)
            ydiag = ydiag + _mm(cb * dec, xdt * hm)
    if ns == 1:
        st = g["c_sh"][...]
        srow = lax.broadcasted_iota(jnp.int32, st.shape, 0) // SSD_N
        yoff = _mm(cm, st)
        slane = lax.broadcasted_iota(jnp.int32, st.shape, 1) // (SSD_HD * heads_per_group)
        upd = jnp.where(srow == slane, _mm_tn(bm, xw), 0.0)
        st = st * jnp.exp(acum[qt - 1:qt, :]) + upd
        g["c_sh"][...] = st

        @pl.when(c == nc - 1)
        def _final_ssd_state():
            own = jnp.concatenate([st[grp * SSD_N:(grp + 1) * SSD_N, grp * LANES:(grp + 1) * LANES]
                                   for grp in range(SSD_G)], axis=1)
            own = jnp.concatenate([own, jnp.zeros_like(own)], axis=0)
            g["sh_o"][...] = jnp.transpose(own)[:, 0:SSD_N].reshape(g["sh_o"].shape)
    else:
        sh_i, sh_o = g["sh_i"], g["sh_o"]
        per_seq = gw // SSD_HD * SSD_HD
        cols = [[], []]
        for s in range(0, ns, 2):
            two = jnp.concatenate([sh_i[s].reshape(per_seq, SSD_N), sh_i[s + 1].reshape(per_seq, SSD_N)], axis=1)
            two = jnp.transpose(two)
            for grp in range(SSD_G):
                cols[grp] += [two[0:SSD_N, grp * LANES:(grp + 1) * LANES],
                              two[SSD_N:2 * SSD_N, grp * LANES:(grp + 1) * LANES]]
        st = jnp.concatenate([jnp.concatenate(cols[0], axis=1), jnp.concatenate(cols[1], axis=1)], axis=0)
        srow = lax.broadcasted_iota(jnp.int32, st.shape, 0) // SSD_N
        seqm = seq_mask(LANES)
        lastm = jnp.where(t == r - 1, seqm, 0.0)
        yo, upd, dl = [], None, None
        for grp in range(SSD_G):
            gm = ((lane_b // SSD_N) == grp).astype(F32)
            z = _mm(cm * gm, st) * seqm
            zf = z[:, 0:LANES]
            for s in range(1, ns):
                zf = zf + z[:, s * LANES:(s + 1) * LANES]
            yo.append(zf)
            sl = slice(grp * LANES, (grp + 1) * LANES)
            u_g = _mm_tn(bm, jnp.tile(xw[:, sl], (1, ns)) * seqm)
            d_g = jnp.exp(jnp.sum(jnp.tile(acum[:, sl], (1, ns)) * lastm, axis=0, keepdims=True))
            upd = u_g if grp == 0 else jnp.where(srow == grp, u_g, upd)
            dl = d_g if grp == 0 else jnp.where(srow == grp, d_g, dl)
        yoff = jnp.concatenate(yo, axis=1)
        st = st * dl + upd
        for s in range(0, ns, 2):
            two = jnp.concatenate(
                [jnp.concatenate([st[0:SSD_N, q * LANES:(q + 1) * LANES], st[SSD_N:2 * SSD_N, q * LANES:(q + 1) * LANES]],
                                 axis=1) for q in (s, s + 1)], axis=0)
            two = jnp.transpose(two)
            sh_o[s] = two[:, 0:SSD_N].reshape(sh_o.shape[1:])
            sh_o[s + 1] = two[:, SSD_N:2 * SSD_N].reshape(sh_o.shape[1:])
    yc = ydiag + yoff * jnp.exp(acum) + vec[V_SSDD:V_SSDD + 1, 0:gw] * xs
    z = pcols(3 * gw, 4 * gw)
    yc = yc * (z * _sigmoid(z))
    ysave(2 * gw, 3 * gw, _rms(yc) * gain[:, 2 * gw:3 * gw])

    trig = g["trig"]
    cos, sin_a, sin_b = trig[0], trig[1], trig[2]

    def rope(x):
        return x * cos + pltpu.roll(x, LANES - RET_DK // 2, 1) * sin_a + pltpu.roll(x, RET_DK // 2, 1) * sin_b

    q = rope(pcols(6 * gw, 6 * gw + LANES))
    kk = rope(pcols(6 * gw + LANES, 7 * gw)) * (RET_DK ** -0.5)
    v = pcols(7 * gw, 8 * gw)
    dv = gw // RET_H
    intra = jnp.zeros((qt, gw), F32)
    for hd in range(RET_H):
        qm = ((lane_b // RET_DK) == hd).astype(F32)
        sc = _mm_nt(q * qm, kk) * g["dmask"][hd]
        vm = ((lane // dv) == hd).astype(F32)
        intra = intra + _mm(sc, v * vm)
    dec = g["dec"]
    qd, kd = q * dec[0], kk * dec[1]
    if ns == 1:
        rs = g["c_ret"][...]
        cross = _mm(qd, rs)
        rrow = lax.broadcasted_iota(jnp.int32, rs.shape, 0) // RET_DK
        rlane = lax.broadcasted_iota(jnp.int32, rs.shape, 1) // dv
        upd = jnp.where(rrow == rlane, _mm_tn(kd, v), 0.0)
    else:
        ret_i, ret_o = g["ret_i"], g["ret_o"]
        hk = RET_H * RET_DK
        rs = jnp.concatenate([ret_i[s].reshape(hk, dv) for s in range(ns)], axis=1)
        seqm = seq_mask(dv)
        lo = lane_b < dv
        pairs = []
        upd = jnp.zeros(rs.shape, F32)
        for pp in range(RET_H // 2):
            vp = v[:, pp * LANES:(pp + 1) * LANES]
            vsw = pltpu.roll(vp, dv, 1)
            tots = []
            for hh in range(2):
                qm = ((lane_b // RET_DK) == 2 * pp + hh).astype(F32)
                zc = _mm(qd * qm, rs) * seqm
                acc = zc[:, 0:LANES]
                for s2 in range(1, ns // 2):
                    acc = acc + zc[:, s2 * LANES:(s2 + 1) * LANES]
                tots.append(acc + pltpu.roll(acc, dv, 1))
                vh = jnp.where(lo, vp, vsw) if hh == 0 else jnp.where(lo, vsw, vp)
                upd = upd + _mm_tn(kd * qm, jnp.tile(vh, (1, ns // 2)) * seqm)
            pairs.append(jnp.where(lo, tots[0], tots[1]))
        cross = jnp.concatenate(pairs, axis=1)
    rs = rs * dec[2][:, 0:1] + upd
    if ns == 1:
        g["c_ret"][...] = rs

        @pl.when(c == nc - 1)
        def _final_ret_state():
            own = functools.reduce(lambda s, x: s + x, [rs[:, hd * dv:(hd + 1) * dv] for hd in range(RET_H)])
            g["ret_o"][...] = own.reshape(g["ret_o"].shape)
    else:
        for s in range(ns):
            ret_o[s] = rs[:, s * dv:(s + 1) * dv].reshape(ret_o.shape[1:])
    o = intra + cross
    o2 = o * o
    ms = jnp.zeros((qt, gw), F32)
    for hd in range(RET_H):
        vm = ((lane // dv) == hd).astype(F32)
        ms = ms + vm * jnp.sum(o2 * vm, axis=-1, keepdims=True)
    o = o * lax.rsqrt(ms * (1.0 / dv) + EPS) * gain[:, 3 * gw:4 * gw]
    gg = pcols(8 * gw, 9 * gw)
    ysave(3 * gw, 4 * gw, gg * _sigmoid(gg) * o)
    if ns == 1:
        y_all = jnp.concatenate(y_parts, axis=1)
        g["y"][...] = jnp.dot(g["perm"][1], y_all, preferred_element_type=F32).astype(g["y"].dtype)


def _mix_consts(qt, ns, r, nc, pos0):
    assert qt == LANES, "decay tables are stacked as (3, 128, 128)"
    idx = np.arange(qt)
    nk = qt // SUBLANES
    if ns == 1:
        seq, tt = np.zeros_like(idx), (idx % SUBLANES) * nk + idx // SUBLANES
    else:
        seq, tt = idx % ns, idx // ns
    causal = (seq[:, None] == seq[None, :]) & (tt[:, None] >= tt[None, :])
    tri = causal.astype(np.float32)
    lg = np.log1p(-np.exp2(-5.0 - np.arange(RET_H, dtype=np.float64)))
    rel = (tt[:, None] - tt[None, :]).astype(np.float64)
    dmask = np.where(causal[None], np.exp(np.maximum(rel, 0.0)[None] * lg[:, None, None]), 0.0).astype(np.float32)
    lane_h = np.arange(LANES) // RET_DK
    qdec = np.exp((tt[:, None] + 1.0) * lg[lane_h][None, :])
    kdec = np.exp((r - 1.0 - tt[:, None]) * lg[lane_h][None, :])
    rdec = np.broadcast_to(np.exp(r * lg[lane_h])[:, None], (LANES, LANES))
    dec = np.stack([qdec, kdec, rdec]).astype(np.float32)
    half = RET_DK // 2
    pos = pos0 + jnp.asarray((np.arange(nc)[:, None] * qt + tt[None, :]).reshape(-1))
    inv = ROPE_BASE ** (-jnp.arange(half, dtype=F32) / half)
    ang = pos.astype(F32)[:, None] * inv
    reps = LANES // half
    cos, sin = jnp.tile(jnp.cos(ang), (1, reps)), jnp.tile(jnp.sin(ang), (1, reps))
    first = (np.arange(LANES) % RET_DK) < half
    trig = jnp.stack([cos, jnp.where(first, -sin, 0.0), jnp.where(first, 0.0, sin)])
    to_blocked = np.zeros((qt, qt), np.float32)
    to_blocked[idx, (idx % SUBLANES) * nk + idx // SUBLANES] = 1.0
    perm = jnp.asarray(np.stack([to_blocked, to_blocked.T])).astype(MXU_DTYPE)
    return dict(trig=trig, tri=jnp.asarray(tri), dmask=jnp.asarray(dmask), dec=jnp.asarray(dec), perm=perm)


def _state_layout(depth, bsz, ns, gw, sn):
    hk, dv = RET_H * RET_DK, gw // RET_H
    if ns == 1:
        def per_seq(rows, w):
            return (depth, bsz, rows, w), (None, None, rows, w), lambda b, c, l: (l[0], b, 0, 0)
        def matrix(h, rows, w):
            return (depth, bsz, h, rows, w), (None, None, h, rows, w), lambda b, c, l: (l[0], b, 0, 0, 0)
        return dict(lconv=per_seq(CONV_K - 1, gw), lh=per_seq(1, gw), s5r=per_seq(1, sn), s5i=per_seq(1, sn),
                    sconv=per_seq(CONV_K - 1, 2 * gw), sh=matrix(gw // SSD_HD, SSD_HD, SSD_N),
                    ret=matrix(RET_H, RET_DK, dv))

    def flat(per, w):
        return (depth, bsz * per, w), (None, ns * per, w), lambda b, c, l: (l[0], b, 0)

    def conv(w):
        return (depth, CONV_K - 1, bsz, w), (None, CONV_K - 1, ns, w), lambda b, c, l: (l[0], 0, b, 0)
    def matrix(h, rows, w):
        return (depth, bsz, h, rows, w), (None, ns, h, rows, w), lambda b, c, l: (l[0], b, 0, 0, 0)
    return dict(lconv=conv(gw), lh=flat(1, gw), s5r=flat(1, sn), s5i=flat(1, sn), sconv=conv(2 * gw),
                sh=matrix(gw // SSD_HD, SSD_HD, SSD_N), ret=matrix(RET_H, RET_DK, dv))


def _mixer(l, proj, consts, mw, state_in, state_out, *, nb, nc, qt, ns, r, d):
    dp = proj.shape[-1]
    gw = d // 4
    sn = mw["pw"].shape[-1]
    depth = mw["pw"].shape[0]
    layout = _state_layout(depth, nb * ns, ns, gw, sn)
    assert (state_in is None) == (ns == 1)

    def layer_spec(a):
        return pl.BlockSpec((None,) + a.shape[1:], lambda b, c, l: (l[0],) + (0,) * (a.ndim - 1))

    def const_spec(a):
        return pl.BlockSpec(a.shape, lambda b, c, l: (0,) * a.ndim)

    def state_spec(nm):
        _, block, index = layout[nm]
        return pl.BlockSpec(block, index)

    if ns == 1:
        io_spec = lambda w: pl.BlockSpec((qt, w), lambda b, c, l: (b * nc + c, 0))
        y_shape = (nb * nc * qt, d)
        cnames = ("trig", "tri", "dmask", "dec", "perm")
    else:
        proj = proj.reshape(r, nb * ns, dp)
        io_spec = lambda w: pl.BlockSpec((r, ns, w), lambda b, c, l: (0, b, 0))
        y_shape = (r, nb * ns, d)
        cnames = ("trig", "tri", "dmask", "dec")
    names = ("proj",) + cnames + MIX_WEIGHTS
    operands = [proj] + [consts[k] for k in cnames] + [mw[k] for k in MIX_WEIGHTS]
    in_specs = [io_spec(dp), pl.BlockSpec((3, qt, LANES), lambda b, c, l: (0, c, 0))] + \
               [const_spec(consts[k]) for k in cnames[1:]] + [layer_spec(mw[k]) for k in MIX_WEIGHTS]
    if state_in is not None:
        names += tuple(nm + "_i" for nm in STATE_NAMES)
        operands += [state_in[nm] for nm in STATE_NAMES]
        in_specs += [state_spec(nm) for nm in STATE_NAMES]
    first_alias = 1 + len(operands)
    names += tuple(nm + "_alias" for nm in STATE_NAMES)
    operands += [state_out[nm] for nm in STATE_NAMES]
    in_specs += [pl.BlockSpec(memory_space=pl.ANY) for _ in STATE_NAMES]
    names += ("y",) + tuple(nm + "_o" for nm in STATE_NAMES)
    out_specs = [io_spec(d)] + [state_spec(nm) for nm in STATE_NAMES]
    out_shape = [jax.ShapeDtypeStruct(y_shape, MXU_DTYPE)] + \
                [jax.ShapeDtypeStruct(layout[nm][0], F32) for nm in STATE_NAMES]
    scratch = []
    if ns == 1:
        tail = (CONV_K - 1) * SUBLANES
        scratch_shapes = dict(cb_a=(tail, gw), cb_c=(tail, 2 * gw), c_lh=(1, gw), c_s5r=(1, sn),
                              c_s5i=(1, sn), c_sh=(SSD_G * SSD_N, gw), c_ret=(RET_H * RET_DK, gw))
        names += tuple(scratch_shapes)
        scratch = [pltpu.VMEM(shape, F32) for shape in scratch_shapes.values()]
    grid_spec = pltpu.PrefetchScalarGridSpec(num_scalar_prefetch=1, grid=(nb, nc),
                                             in_specs=in_specs, out_specs=out_specs, scratch_shapes=scratch)
    outs = pl.pallas_call(
        functools.partial(_mix_body, names=names, qt=qt, ns=ns, r=r, nc=nc),
        grid_spec=grid_spec,
        out_shape=out_shape,
        input_output_aliases={first_alias + k: 1 + k for k in range(len(STATE_NAMES))},
        compiler_params=pltpu.CompilerParams(dimension_semantics=("arbitrary", "arbitrary"),
                                             vmem_limit_bytes=VMEM_LIMIT),
        name="mixer",
    )(l, *operands)
    return outs[0].reshape(-1, d), dict(zip(STATE_NAMES, outs[1:]))


def _post_body(l_ref, y_ref, x_ref, g1_ref, sc_ref, sh_ref, g2_ref, lt_ref, wo_ref, wr_ref, br_ref,
               wgu_ref, wd_ref, o_ref, xs_ref, cs_ref, os_ref):
    del l_ref
    tm, d = x_ref.shape
    tms = xs_ref.shape[0]
    x = x_ref[...] + _token_rows(g1_ref[...], tm) * jnp.dot(y_ref[...], wo_ref[...], preferred_element_type=F32)
    h = _rms(x) * (1.0 + _token_rows(sc_ref[...], tm)) + _token_rows(sh_ref[...], tm)
    hb = h.astype(MXU_DTYPE)
    h_lo = (h - hb.astype(F32)).astype(MXU_DTYPE)
    wr = wr_ref[...]
    l12 = jnp.dot(hb, wr, preferred_element_type=F32)
    logits = (l12[:, 0:LANES] + l12[:, LANES:2 * LANES]
              + jnp.dot(h_lo, wr[:, 0:LANES], preferred_element_type=F32) + br_ref[...])
    logits_t = jnp.transpose(logits)
    col = [logits_t[k:k + 1, :] for k in range(N_GROUPS + N_EXPERTS)]

    def first_max(vals, allowed=None):
        neg = jnp.full_like(vals[0], -jnp.inf)
        cand = vals if allowed is None else [jnp.where(al > 0.0, v, neg) for v, al in zip(vals, allowed)]
        m = functools.reduce(jnp.maximum, cand)
        rem = jnp.ones_like(vals[0])
        hot = []
        for v in cand:
            f = jnp.where(v >= m, rem, 0.0)
            rem = rem - f
            hot.append(f)
        return hot, m

    grp, gmax = first_max(col[:N_GROUPS])
    gate = 1.0 / functools.reduce(lambda s, v: s + v, [jnp.exp(v - gmax) for v in col[:N_GROUPS]])
    le = [functools.reduce(lambda s, v: s + v,
                           [grp[gi] * col[N_GROUPS + gi * N_PER_GROUP + j] for gi in range(N_GROUPS)])
          for j in range(N_PER_GROUP)]
    top1, m1 = first_max(le)
    top2, m2 = first_max(le, [1.0 - f for f in top1])
    e2 = jnp.exp(m2 - m1)
    w1 = 1.0 / (1.0 + e2)
    w2 = e2 * w1
    wgrp = [(top1[j] * w1 + top2[j] * w2) * gate for j in range(N_PER_GROUP)]

    sub8 = lax.broadcasted_iota(jnp.int32, (SUBLANES, 1), 0)
    sub_e = lax.broadcasted_iota(jnp.int32, (cs_ref.shape[1], 1), 0)
    ghot_t = functools.reduce(lambda s, v: s + v, [jnp.where(sub8 == gi, grp[gi], 0.0) for gi in range(N_GROUPS)])
    comb_t = functools.reduce(lambda s, v: s + v,
                              [jnp.where(sub_e == e, grp[e // N_PER_GROUP] * wgrp[e % N_PER_GROUP], 0.0)
                               for e in range(N_EXPERTS)])
    before = _mm_nt(ghot_t, lt_ref[...])
    count = jnp.sum(ghot_t, axis=1, keepdims=True)
    cnt = [count[gi:gi + 1, :] for gi in range(N_GROUPS)]
    base = [jnp.zeros((1, 1), F32)]
    for gi in range(1, N_GROUPS):
        base.append(jnp.floor((base[-1] + cnt[gi - 1] + (PACK - 1.0)) * (1.0 / PACK)) * PACK)
    pos_row = functools.reduce(lambda s, v: s + v,
                               [grp[gi] * (base[gi] + before[gi:gi + 1, :]) for gi in range(N_GROUPS)])
    pos = jnp.transpose(jnp.broadcast_to(pos_row, (LANES, tm)))[:, 0:1]
    slot = lax.broadcasted_iota(jnp.int32, (tms, tm), 0).astype(F32)
    gather = jnp.where(slot == pos_row, 1.0, 0.0).astype(MXU_DTYPE)
    slot_l = lax.broadcasted_iota(jnp.int32, (tm, tms), 1).astype(F32)
    scatter = jnp.where(slot_l == pos, 1.0, 0.0).astype(MXU_DTYPE)

    xs_ref[...] = jnp.dot(gather, hb, preferred_element_type=F32).astype(xs_ref.dtype)
    c1, c2, c3 = _split3(comb_t)
    nt = lambda p: lax.dot_general(gather, p, (((1,), (1,)), ((), ())), preferred_element_type=F32)
    cs_ref[...] = nt(c1) + nt(c2) + nt(c3)
    os_ref[...] = jnp.zeros(os_ref.shape, os_ref.dtype)
    for gi in range(N_GROUPS):
        start = base[gi][0, 0].astype(jnp.int32)
        nblk = jnp.floor((cnt[gi][0, 0] + (EXPERT_ROWS - 1.0)) * (1.0 / EXPERT_ROWS)).astype(jnp.int32)

        def block(k, carry, gi=gi, start=start):
            rows = pl.ds(pl.multiple_of(start + k * EXPERT_ROWS, PACK), EXPERT_ROWS)
            xb = xs_ref[rows, :]
            cb = cs_ref[rows, :]
            acc = jnp.zeros((EXPERT_ROWS, d), F32)
            for j in range(N_PER_GROUP):
                e = gi * N_PER_GROUP + j
                gu = jnp.dot(xb, wgu_ref[e], preferred_element_type=F32)
                de = gu.shape[1] // 2
                gt = gu[:, 0:de]
                he = gt * _sigmoid(gt) * gu[:, de:]
                acc = acc + jnp.dot((he * cb[:, e:e + 1]).astype(MXU_DTYPE), wd_ref[e], preferred_element_type=F32)
            os_ref[rows, :] += acc
            return carry

        lax.fori_loop(0, nblk, block, 0)
    moe = _select_rows(scatter, os_ref[...], pieces=2)
    o_ref[...] = x + _token_rows(g2_ref[...], tm) * moe


def _post(l, y, x, mod, pw, tm, rows_per_seq):
    t, d = x.shape
    tms = tm + EXPERT_ROWS + PACK * N_GROUPS
    ltri = jnp.asarray(np.tril(np.ones((tm, tm), np.float32), -1)).astype(MXU_DTYPE)

    def layer_spec(a):
        return pl.BlockSpec((None,) + a.shape[1:], lambda i, l: (l[0],) + (0,) * (a.ndim - 1),
                            pipeline_mode=pl.Buffered(1))

    grid_spec = pltpu.PrefetchScalarGridSpec(
        num_scalar_prefetch=1,
        grid=(t // tm,),
        in_specs=[
            pl.BlockSpec((tm, d), lambda i, l: (i, 0)),
            pl.BlockSpec((tm, d), lambda i, l: (i, 0)),
            _mod_spec(mod, 2, tm, rows_per_seq),
            _mod_spec(mod, 4, tm, rows_per_seq),
            _mod_spec(mod, 3, tm, rows_per_seq),
            _mod_spec(mod, 5, tm, rows_per_seq),
            pl.BlockSpec((tm, tm), lambda i, l: (0, 0)),
        ] + [layer_spec(pw[k]) for k in ("w_out", "w_route", "b_route", "w_gate_up", "w_down")],
        out_specs=pl.BlockSpec((tm, d), lambda i, l: (i, 0)),
        scratch_shapes=[pltpu.VMEM((tms, d), MXU_DTYPE), pltpu.VMEM((tms, 2 * N_EXPERTS), F32),
                        pltpu.VMEM((tms, d), F32)],
    )
    return pl.pallas_call(
        _post_body,
        grid_spec=grid_spec,
        out_shape=jax.ShapeDtypeStruct((t, d), F32),
        compiler_params=pltpu.CompilerParams(dimension_semantics=("arbitrary",), vmem_limit_bytes=VMEM_LIMIT),
        name="post",
    )(l, y, x, mod, mod, mod, mod, ltri,
      *[pw[k] for k in ("w_out", "w_route", "b_route", "w_gate_up", "w_down")])


def _final_body(x_ref, w_ref, o_ref):
    o_ref[...] = _rms(x_ref[...]) * w_ref[...]


def _final_norm(x, w, tm):
    t, d = x.shape
    return pl.pallas_call(
        _final_body,
        grid=(t // tm,),
        in_specs=[pl.BlockSpec((tm, d), lambda i: (i, 0)), pl.BlockSpec((1, d), lambda i: (0, 0))],
        out_specs=pl.BlockSpec((tm, d), lambda i: (i, 0)),
        out_shape=jax.ShapeDtypeStruct((t, d), F32),
        name="final_norm",
    )(x, w.reshape(1, d))


def _block_diag(w):
    depth, nblk, bi, bj = w.shape
    eye = jnp.eye(nblk, dtype=w.dtype)
    return (w[:, :, :, None, :] * eye[None, :, None, :, None]).reshape(depth, nblk * bi, nblk * bj)


def _pad_lanes(a, width):
    return jnp.pad(a, [(0, 0)] * (a.ndim - 1) + [(0, width - a.shape[-1])])


def _prep_weights(w_in, lru_conv_w, lru_conv_b, lru_wa, lru_ba, lru_wx, lru_bx, lru_lambda,
                  s5_a_re, s5_a_im, s5_b_re, s5_b_im, s5_c_re, s5_c_im, s5_d, s5_log_dt, s5_w_glu,
                  ssd_conv_w, ssd_conv_b, ssd_dt_bias, ssd_a_log, ssd_d, mix_norm, qt):
    depth, d, _ = w_in.shape
    gw = d // 4
    n_ssd_h = gw // SSD_HD
    xbc_end = 4 * gw + gw + 2 * SSD_G * SSD_N
    dt_cols = jnp.repeat(w_in[..., xbc_end:xbc_end + n_ssd_h], SSD_HD, axis=-1)
    w_in_p = jnp.concatenate([w_in[..., :xbc_end], w_in[..., xbc_end + n_ssd_h:], dt_cols], axis=-1)

    def row(a):
        return _pad_lanes(a.reshape(depth, 1, -1), d)

    vec = jnp.concatenate([
        row(lru_conv_b), row(jnp.concatenate([lru_ba, lru_bx], -1)), row(lru_lambda), row(s5_d),
        row(ssd_conv_b), row(jnp.repeat(ssd_dt_bias, SSD_HD, -1)), row(jnp.repeat(ssd_a_log, SSD_HD, -1)),
        row(jnp.repeat(ssd_d, SSD_HD, -1)), row(mix_norm),
        _pad_lanes(lru_conv_w, d), _pad_lanes(ssd_conv_w, d),
        jnp.zeros((depth, V_ROWS - V_SCW - CONV_K, d), F32)], axis=1)

    wgate = jnp.concatenate([_block_diag(lru_wa), _block_diag(lru_wx)], axis=-1)

    dt = jnp.exp(s5_log_dt)[..., None]
    lr, li = s5_a_re, s5_a_im
    mag = jnp.exp(lr * dt)
    ab_re, ab_im = mag * jnp.cos(li * dt), mag * jnp.sin(li * dt)
    den = lr * lr + li * li
    q_re = ((ab_re - 1.0) * lr + ab_im * li) / den
    q_im = (ab_im * lr - (ab_re - 1.0) * li) / den
    bb_re = q_re[..., None] * s5_b_re - q_im[..., None] * s5_b_im
    bb_im = q_re[..., None] * s5_b_im + q_im[..., None] * s5_b_re
    bbm = jnp.concatenate([_block_diag(jnp.swapaxes(bb_re, -1, -2)), _block_diag(jnp.swapaxes(bb_im, -1, -2))], -1)
    cre = _block_diag(jnp.swapaxes(s5_c_re, -1, -2))
    cim = _block_diag(jnp.swapaxes(s5_c_im, -1, -2))
    a_re, a_im = ab_re.reshape(depth, -1), ab_im.reshape(depth, -1)
    pr, pi = a_re, a_im
    rows = []
    for _ in range(int(math.log2(qt))):
        rows += [pr, pi]
        pr, pi = pr * pr - pi * pi, 2.0 * pr * pi
    pw = jnp.stack(rows, axis=1)
    pw = jnp.pad(pw, ((0, 0), (0, (-pw.shape[1]) % SUBLANES), (0, 0)))
    pr, pi = a_re, a_im
    k_re, k_im = [], []
    for _ in range(qt // SUBLANES):
        k_re.append(pr)
        k_im.append(pi)
        pr, pi = pr * a_re - pi * a_im, pr * a_im + pi * a_re
    pk = jnp.stack(k_re + k_im, axis=1)

    mw = dict(vec=vec, wgate=wgate.astype(MXU_DTYPE), bbm=bbm.astype(MXU_DTYPE), cre=cre.astype(MXU_DTYPE),
              cim=cim.astype(MXU_DTYPE), wglu=s5_w_glu.astype(MXU_DTYPE), pw=pw, pk=pk)
    return w_in_p.astype(MXU_DTYPE), mw


def _group_of_head(n_h):
    return (jnp.arange(SSD_G)[:, None] == jnp.arange(n_h)[None, :] // (n_h // SSD_G)).astype(F32)


def _states_to_layout(states, layout, ns):
    out = {}
    for nm, s in zip(STATE_NAMES, states):
        if nm in ("lconv", "sconv"):
            s = s.transpose(0, 2, 1, 3)
        out[nm] = s.reshape(layout[nm][0])
    return out


def _states_from_layout(arrs, ns, ref_shapes):
    out = []
    for nm, shape in zip(STATE_NAMES, ref_shapes):
        a = arrs[nm]
        if ns > 1 and nm in ("lconv", "sconv"):
            a = a.transpose(0, 2, 1, 3)
        out.append(a.reshape(shape))
    return tuple(out)


def _trunk(x, mod, states, ref_shapes, pos0, w_in_p, mw, pw, final_norm, *, qt, ns, tm):
    bsz, seq_len, d = x.shape
    depth = w_in_p.shape[0]
    gw = d // 4
    r = qt // ns
    nc = seq_len // r if ns == 1 else 1
    assert (ns == 1 and seq_len % qt == 0 and states is None) or (ns > 1 and r == seq_len and bsz % ns == 0)
    assert r >= CONV_K and (r & (r - 1)) == 0
    nb = bsz // ns
    tok = bsz * seq_len
    consts = _mix_consts(qt, ns, r, nc, pos0)
    layout = _state_layout(depth, bsz, ns, gw, mw["pw"].shape[-1])
    state_in = None if states is None else _states_to_layout(states, layout, ns)
    state_out = {nm: jnp.zeros(layout[nm][0], F32) for nm in STATE_NAMES}
    x2 = x.reshape(tok, d) if ns == 1 else x.transpose(1, 0, 2).reshape(tok, d)
    blocked_chunk = qt if ns == 1 else None

    def layer(carry, l):
        xc, st = carry
        lv = jnp.reshape(l, (1,)).astype(jnp.int32)
        proj = _inproj(lv, xc, mod, w_in_p, tm, seq_len, blocked_chunk)
        y, st = _mixer(lv, proj, consts, mw, state_in, st, nb=nb, nc=nc, qt=qt, ns=ns, r=r, d=d)
        xn = _post(lv, y, xc, mod, pw, tm, seq_len)
        return (xn, st), None

    (xf, st), _ = lax.scan(layer, (x2, state_out), jnp.arange(depth))
    y = _final_norm(xf, final_norm, tm)
    y = y.reshape(bsz, seq_len, d) if ns == 1 else y.reshape(seq_len, bsz, d).transpose(1, 0, 2)
    return y, _states_from_layout(st, ns, ref_shapes)


def kernel(x_prompt, x_sample, state_lru_conv, state_lru_h, state_s5_re, state_s5_im, state_ssd_conv, state_ssd_h, state_ret, c_prompt, c_sample, w_ada, b_ada, w_in, lru_conv_w, lru_conv_b, lru_wa, lru_ba, lru_wx, lru_bx, lru_lambda, s5_a_re, s5_a_im, s5_b_re, s5_b_im, s5_c_re, s5_c_im, s5_d, s5_log_dt, s5_w_glu, ssd_conv_w, ssd_conv_b, ssd_dt_bias, ssd_a_log, ssd_d, mix_norm, w_out, w_route_group, b_route_group, w_route_exp, b_route_exp, w_exp_gate, w_exp_up, w_exp_down, final_norm):
    bp, lp, d = x_prompt.shape
    bs, ls, _ = x_sample.shape
    depth = w_in.shape[0]
    qt = LANES
    ns_s = qt // ls

    mod = _modulation(jnp.concatenate([c_prompt, c_sample], axis=0), w_ada, b_ada)
    mod_p = mod[:, :, :bp].reshape(depth, N_MOD, bp, 1, d)
    mod_s = mod[:, :, bp:]

    w_in_p, mw = _prep_weights(w_in, lru_conv_w, lru_conv_b, lru_wa, lru_ba, lru_wx, lru_bx, lru_lambda,
                               s5_a_re, s5_a_im, s5_b_re, s5_b_im, s5_c_re, s5_c_im, s5_d, s5_log_dt, s5_w_glu,
                               ssd_conv_w, ssd_conv_b, ssd_dt_bias, ssd_a_log, ssd_d, mix_norm, qt)
    w_route = _pad_lanes(jnp.concatenate([w_route_group, w_route_exp.reshape(depth, d, N_EXPERTS)], -1), LANES)
    b_route = _pad_lanes(jnp.concatenate([b_route_group, b_route_exp.reshape(depth, N_EXPERTS)], -1), LANES)
    w_route_hi = w_route.astype(MXU_DTYPE)
    w_route_lo = (w_route - w_route_hi.astype(F32)).astype(MXU_DTYPE)
    w_route = jnp.concatenate([w_route_hi, w_route_lo], axis=-1)
    pw = dict(w_out=w_out.astype(MXU_DTYPE), w_route=w_route, b_route=b_route.reshape(depth, 1, LANES),
              w_gate_up=jnp.concatenate([w_exp_gate, w_exp_up], axis=-1).astype(MXU_DTYPE),
              w_down=w_exp_down.astype(MXU_DTYPE))
    w_in_p, mw, pw = lax.optimization_barrier((w_in_p, mw, pw))

    states_s = (state_lru_conv, state_lru_h, state_s5_re, state_s5_im, state_ssd_conv, state_ssd_h, state_ret)
    shapes_s = [s.shape for s in states_s]
    shapes_p = [(depth, bp) + s[2:] for s in shapes_s]
    y_p, new_p = _trunk(x_prompt, mod_p, None, shapes_p, 0, w_in_p, mw, pw, final_norm,
                        qt=qt, ns=1, tm=min(512, lp))
    y_s, new_s = _trunk(x_sample, mod_s, states_s, shapes_s, PAST_LEN, w_in_p, mw, pw, final_norm,
                        qt=qt, ns=ns_s, tm=bs * ls)
    out = [y_p, y_s]
    for a, b in zip(new_p, new_s):
        out += [a, b]
    return tuple(out)
```

```python
import functools
import math

import numpy as np
import jax
import jax.numpy as jnp
from jax import lax
from jax.experimental import pallas as pl
from jax.experimental.pallas import tpu as pltpu

F32 = jnp.float32
MXU_DTYPE = jnp.bfloat16
HIGHEST = lax.Precision.HIGHEST

EPS = 1e-6
CONV_K = 4
LRU_C = 8.0
SSD_HD = 64
SSD_G = 2
SSD_N = 64
RET_H = 4
RET_DK = 32
ROPE_BASE = 10000.0
N_GROUPS = 4
N_PER_GROUP = 4
N_EXPERTS = N_GROUPS * N_PER_GROUP
N_MOD = 6
PAST_LEN = 16384

LANES = 128
SUBLANES = 8
PACK = 16
VMEM_LIMIT = 56 * 1024 * 1024
EXPERT_STEP = 64
EXPERT_SIZES = 4


def _mm(a, b):
    return jnp.dot(a.astype(MXU_DTYPE), b.astype(MXU_DTYPE), preferred_element_type=F32)


def _mm_nt(a, b):
    return lax.dot_general(a.astype(MXU_DTYPE), b.astype(MXU_DTYPE), (((1,), (1,)), ((), ())),
                           preferred_element_type=F32)


def _mm_tn(a, b):
    return lax.dot_general(a.astype(MXU_DTYPE), b.astype(MXU_DTYPE), (((0,), (0,)), ((), ())),
                           preferred_element_type=F32)


def _split3(x):
    x1 = x.astype(MXU_DTYPE)
    r1 = x - x1.astype(F32)
    x2 = r1.astype(MXU_DTYPE)
    x3 = (r1 - x2.astype(F32)).astype(MXU_DTYPE)
    return x1, x2, x3


def _select_rows(onehot, x, pieces=3):
    parts = _split3(x)[:pieces]
    out = jnp.dot(onehot, parts[0], preferred_element_type=F32)
    for p in parts[1:]:
        out = out + jnp.dot(onehot, p, preferred_element_type=F32)
    return out


def _sigmoid(x):
    return 0.5 * jnp.tanh(0.5 * x) + 0.5


def _rms(x):
    return x * lax.rsqrt(jnp.mean(x * x, axis=-1, keepdims=True) + EPS)


def _token_rows(v, tm):
    n = v.shape[0]
    return v if n in (1, tm) else jnp.tile(v, (tm // n, 1))


def _mod_body(c_ref, w_ref, b_ref, o_ref):
    c = c_ref[...]
    o_ref[...] = _mm(c * _sigmoid(c), w_ref[...]) + b_ref[...]


def _modulation(c_all, w_ada, b_ada):
    depth, d, _ = w_ada.shape
    nb = c_all.shape[0]
    return pl.pallas_call(
        _mod_body,
        grid=(depth, N_MOD),
        in_specs=[
            pl.BlockSpec((nb, d), lambda l, k: (0, 0)),
            pl.BlockSpec((None, d, d), lambda l, k: (l, 0, k)),
            pl.BlockSpec((None, None, 1, d), lambda l, k: (l, k, 0, 0)),
        ],
        out_specs=pl.BlockSpec((None, None, nb, d), lambda l, k: (l, k, 0, 0)),
        out_shape=jax.ShapeDtypeStruct((depth, N_MOD, nb, d), F32),
        compiler_params=pltpu.CompilerParams(dimension_semantics=("arbitrary", "arbitrary"),
                                             vmem_limit_bytes=VMEM_LIMIT),
        name="modulation",
    )(c_all, w_ada, b_ada.reshape(depth, N_MOD, 1, d))


def _mod_spec(mod, k, tm, rows_per_seq):
    if mod.ndim == 5:
        tiles_per_seq = rows_per_seq // tm
        return pl.BlockSpec((None, None, None, 1, mod.shape[-1]),
                            lambda i, l: (l[0], k, i // tiles_per_seq, 0, 0))
    return pl.BlockSpec((None, None) + mod.shape[2:], lambda i, l: (l[0], k, 0, 0))


def _inproj_body(l_ref, x_ref, sc_ref, sh_ref, w_ref, o_ref, *, blocked_chunk):
    tm = x_ref.shape[0]
    h = _rms(x_ref[...]) * (1.0 + _token_rows(sc_ref[...], tm)) + _token_rows(sh_ref[...], tm)
    hb = h.astype(MXU_DTYPE)
    if blocked_chunk is not None:
        qt, nk = blocked_chunk, blocked_chunk // SUBLANES
        r_i = lax.broadcasted_iota(jnp.int32, (qt, qt), 0)
        c_i = lax.broadcasted_iota(jnp.int32, (qt, qt), 1)
        to_blocked = jnp.where(c_i == (r_i % SUBLANES) * nk + r_i // SUBLANES, 1.0, 0.0).astype(MXU_DTYPE)
        hb = jnp.concatenate([jnp.dot(to_blocked, hb[j * qt:(j + 1) * qt], preferred_element_type=F32)
                              for j in range(tm // qt)], axis=0).astype(MXU_DTYPE)
    o_ref[...] = jnp.dot(hb, w_ref[...], preferred_element_type=F32)


def _inproj(l, x, mod, w_in, tm, rows_per_seq, blocked_chunk=None):
    t, d = x.shape
    dp = w_in.shape[-1]
    assert blocked_chunk is None or (mod.ndim == 5 and tm % blocked_chunk == 0)
    grid_spec = pltpu.PrefetchScalarGridSpec(
        num_scalar_prefetch=1,
        grid=(t // tm,),
        in_specs=[
            pl.BlockSpec((tm, d), lambda i, l: (i, 0)),
            _mod_spec(mod, 1, tm, rows_per_seq),
            _mod_spec(mod, 0, tm, rows_per_seq),
            pl.BlockSpec((None, d, dp), lambda i, l: (l[0], 0, 0)),
        ],
        out_specs=pl.BlockSpec((tm, dp), lambda i, l: (i, 0)),
    )
    return pl.pallas_call(
        functools.partial(_inproj_body, blocked_chunk=blocked_chunk),
        grid_spec=grid_spec,
        out_shape=jax.ShapeDtypeStruct((t, dp), F32),
        compiler_params=pltpu.CompilerParams(dimension_semantics=("arbitrary",), vmem_limit_bytes=VMEM_LIMIT),
        name="inproj",
    )(l, x, mod, mod, w_in)


V_LCB, V_BGATE, V_LAM, V_S5D, V_SCB, V_DTB, V_ALOG, V_SSDD, V_GAIN, V_LCW, V_SCW = 0, 1, 2, 3, 4, 5, 6, 7, 8, 9, 13
V_ROWS = 24
STATE_NAMES = ("lconv", "lh", "s5r", "s5i", "sconv", "sh", "ret")
MIX_WEIGHTS = ("vec", "wgate", "bbm", "cre", "cim", "wglu", "pw", "pk")


def _mix_body(l_ref, *refs, names, qt, ns, r, nc):
    del l_ref
    g = dict(zip(names, refs))
    gw = g["lh_o"].shape[-1]
    sn = g["s5r_o"].shape[-1]
    nlev = int(math.log2(r))
    nk = qt // SUBLANES
    c = pl.program_id(1)

    if ns == 1:
        @pl.when(c == 0)
        def _zero_carries():
            for nm in ("c_lh", "c_s5r", "c_s5i", "c_sh", "c_ret"):
                g[nm][...] = jnp.zeros(g[nm].shape, F32)
            for nm in ("cb_a", "cb_c"):
                g[nm][...] = jnp.zeros(g[nm].shape, F32)

    proj = g["proj"]
    vec = g["vec"]
    pw = g["pw"]

    def pcols(a, b):
        v = proj[..., a:b]
        return v if ns == 1 else v.reshape(qt, b - a)

    y_parts = []

    def ysave(a, b, val):
        val = val.astype(g["y"].dtype)
        if ns == 1:
            y_parts.append(val)
        else:
            g["y"][..., a:b] = val.reshape(r, ns, b - a)

    row = lax.broadcasted_iota(jnp.int32, (qt, 1), 0)
    t = (row % SUBLANES) * nk + row // SUBLANES if ns == 1 else row // ns
    sub = lax.broadcasted_iota(jnp.int32, (SUBLANES, 1), 0)

    def down(x, d, fill=0.0):
        return jnp.where(t >= d, pltpu.roll(x, d * ns, 0), fill)

    def up(x, d):
        return jnp.where(t + d < r, pltpu.roll(x, qt - d * ns, 0), 0.0)

    def sdown(x, d, fill=0.0):
        return jnp.where(sub >= d, pltpu.roll(x, d, 0), fill)

    def slabs(x):
        return [x[SUBLANES * k:SUBLANES * (k + 1)] for k in range(nk)]

    def first_rows(carry, init):
        if ns == 1:
            return jnp.where(row == 0, g[carry][...], 0.0)
        h0 = g[init][...]
        return jnp.concatenate([h0, jnp.zeros((qt - ns, h0.shape[1]), F32)], axis=0)

    def conv(xraw, nm, w0, b):
        width = xraw.shape[-1]
        acc = vec[b:b + 1, 0:width] + vec[w0 + 3:w0 + 4, 0:width] * xraw
        if ns == 1:
            cb = g["cb_" + nm]
            xs_k = slabs(xraw)
            wrap = [jnp.where(sub >= 1, pltpu.roll(xs_k[nk - j], 1, 0),
                              pltpu.roll(cb[(CONV_K - 1 - j) * SUBLANES:(CONV_K - j) * SUBLANES, :], 1, 0))
                    for j in range(1, CONV_K)]
            for m in range(1, CONV_K):
                sh = jnp.concatenate([wrap[m - k - 1] for k in range(m)] + xs_k[0:nk - m], axis=0)
                acc = acc + vec[w0 + 3 - m:w0 + 4 - m, 0:width] * sh
            cb[...] = xraw[qt - (CONV_K - 1) * SUBLANES:, :]
            g[{"a": "lconv_o", "c": "sconv_o"}[nm]][...] = jnp.concatenate(
                [xs_k[nk - j][SUBLANES - 1:SUBLANES, :] for j in range(CONV_K - 1, 0, -1)], axis=0)
            return acc
        o = g[{"a": "lconv_o", "c": "sconv_o"}[nm]]
        buf = g[{"a": "lconv_i", "c": "sconv_i"}[nm]][...].reshape((CONV_K - 1) * ns, width)
        buf = jnp.concatenate([buf, jnp.zeros((qt - (CONV_K - 1) * ns, width), F32)], axis=0)
        for m in range(1, CONV_K):
            prev = buf if m == 3 else pltpu.roll(buf, qt - (3 - m) * ns, 0)
            sh = jnp.where(t >= m, pltpu.roll(xraw, m * ns, 0), prev)
            acc = acc + vec[w0 + 3 - m:w0 + 4 - m, 0:width] * sh
        o[...] = xraw[(r - (CONV_K - 1)) * ns:, :].reshape(CONV_K - 1, ns, width)
        return acc

    def save_last(nm, h):
        if ns == 1:
            g["c_" + nm][...] = h[qt - 1:qt, :]
            g[nm + "_o"][...] = h[qt - 1:qt, :]
        else:
            g[nm + "_o"][...] = h[qt - ns:, :]

    def seq_mask(width):
        lane_seq = lax.broadcasted_iota(jnp.int32, (qt, ns * width), 1) // width
        row_seq = lax.broadcasted_iota(jnp.int32, (qt, ns * width), 0) % ns
        return (lane_seq == row_seq).astype(F32)

    gain = vec[V_GAIN:V_GAIN + 1, :]

    xc = conv(pcols(0, gw), "a", V_LCW, V_LCB)
    ga = pcols(gw, 2 * gw)
    pre = _mm(xc, g["wgate"][...]) + vec[V_BGATE:V_BGATE + 1, 0:2 * gw]
    rg = _sigmoid(pre[:, 0:gw])
    ig = _sigmoid(pre[:, gw:2 * gw])
    log_a = -LRU_C * rg * jax.nn.softplus(-vec[V_LAM:V_LAM + 1, 0:gw])
    a = jnp.exp(log_a)
    b = jnp.sqrt(-jnp.tanh(log_a) * (a * a + 1.0)) * (ig * xc)
    b = b + a * first_rows("c_lh", "lh_i")
    if ns == 1:
        a_s, b_s = slabs(a), slabs(b)
        hs, ps = [b_s[0]], [a_s[0]]
        for k in range(1, nk):
            hs.append(a_s[k] * hs[-1] + b_s[k])
            ps.append(a_s[k] * ps[-1])
        e, ae = hs[-1], ps[-1]
        for j in range(int(math.log2(SUBLANES))):
            d = 1 << j
            e = ae * sdown(e, d) + e
            ae = ae * sdown(ae, d, 1.0)
        cin = sdown(e, 1)
        b = jnp.concatenate([hs[k] + ps[k] * cin for k in range(nk)], axis=0)
    else:
        for k in range(nlev):
            d = 1 << k
            b = a * down(b, d) + b
            a = a * down(a, d, 1.0)
    save_last("lh", b)
    ya = _rms(b * jax.nn.gelu(ga)) * gain[:, 0:gw]

    u = pcols(2 * gw, 3 * gw)
    bu = _mm(u, g["bbm"][...])
    p_re, p_im = pw[0:1, :], pw[1:2, :]
    h0r, h0i = first_rows("c_s5r", "s5r_i"), first_rows("c_s5i", "s5i_i")
    hr = bu[:, 0:sn] + (p_re * h0r - p_im * h0i)
    hi = bu[:, sn:2 * sn] + (p_re * h0i + p_im * h0r)
    if ns == 1:
        pk = g["pk"]
        br_s, bi_s = slabs(hr), slabs(hi)
        hrs, his = [br_s[0]], [bi_s[0]]
        for k in range(1, nk):
            hrs.append(br_s[k] + (p_re * hrs[-1] - p_im * his[-1]))
            his.append(bi_s[k] + (p_re * his[-1] + p_im * hrs[-2]))
        er, ei = hrs[-1], his[-1]
        lev0 = int(math.log2(nk))
        for j in range(int(math.log2(SUBLANES))):
            d = 1 << j
            q_re, q_im = pw[2 * (lev0 + j):2 * (lev0 + j) + 1, :], pw[2 * (lev0 + j) + 1:2 * (lev0 + j) + 2, :]
            sr, si = sdown(er, d), sdown(ei, d)
            er, ei = er + (q_re * sr - q_im * si), ei + (q_re * si + q_im * sr)
        cr, ci = sdown(er, 1), sdown(ei, 1)
        hr = jnp.concatenate([hrs[k] + (pk[k:k + 1, :] * cr - pk[nk + k:nk + k + 1, :] * ci) for k in range(nk)], 0)
        hi = jnp.concatenate([his[k] + (pk[k:k + 1, :] * ci + pk[nk + k:nk + k + 1, :] * cr) for k in range(nk)], 0)
    else:
        for k in range(nlev):
            d = 1 << k
            q_re, q_im = pw[2 * k:2 * k + 1, :], pw[2 * k + 1:2 * k + 2, :]
            sr, si = down(hr, d), down(hi, d)
            hr, hi = hr + (q_re * sr - q_im * si), hi + (q_re * si + q_im * sr)
    save_last("s5r", hr)
    save_last("s5i", hi)
    yb = _mm(hr, g["cre"][...]) - _mm(hi, g["cim"][...])
    yb = jax.nn.gelu(yb + vec[V_S5D:V_S5D + 1, 0:gw] * u)
    yb = yb * _sigmoid(_mm(yb, g["wglu"][...]))
    yb = _rms(yb) * gain[:, gw:2 * gw]
    ysave(0, 2 * gw, jnp.concatenate([ya, yb], axis=1))

    tri = g["tri"][...] > 0.0
    xbc = conv(pcols(4 * gw, 6 * gw), "c", V_SCW, V_SCB)
    xbc = xbc * _sigmoid(xbc)
    xs, bm, cm = xbc[:, 0:gw], xbc[:, gw:gw + LANES], xbc[:, gw + LANES:2 * gw]
    dt = jax.nn.softplus(pcols(9 * gw, 10 * gw) + vec[V_DTB:V_DTB + 1, 0:gw])
    a_e = -jnp.exp(vec[V_ALOG:V_ALOG + 1, 0:gw])
    dta = dt * a_e
    if ns == 1:
        cs = slabs(dta)
        for k in range(1, nk):
            cs[k] = cs[k] + cs[k - 1]
        e = cs[-1]
        for j in range(int(math.log2(SUBLANES))):
            e = e + sdown(e, 1 << j)
        cin = sdown(e, 1)
        acum = jnp.concatenate([ck + cin for ck in cs], axis=0)
        alast = acum[qt - 1:qt, :]
    else:
        acum = dta
        for k in range(nlev):
            acum = acum + down(acum, 1 << k)
        suf = dta
        for k in range(nlev):
            suf = suf + up(suf, 1 << k)
        alast = acum + suf - dta
    wend = jnp.exp(alast - acum) * dt
    acum_t = jnp.transpose(acum)
    lane = lax.broadcasted_iota(jnp.int32, (1, gw), 1)
    lane_b = lax.broadcasted_iota(jnp.int32, (1, LANES), 1)
    heads_per_group = gw // SSD_HD // SSD_G
    xdt = xs * dt
    xw = xs * wend
    ydiag = jnp.zeros((qt, gw), F32)
    for grp in range(SSD_G):
        gm = ((lane_b // SSD_N) == grp).astype(F32)
        cb = _mm_nt(cm * gm, bm)
        for hh in range(heads_per_group):
            hd = grp * heads_per_group + hh
            col = acum[:, hd * SSD_HD:hd * SSD_HD + 1]
            rowv = acum_t[hd * SSD_HD:hd * SSD_HD + 1, :]
            dec = jnp.where(tri, jnp.exp(col - rowv), 0.0)
            hm = ((lane // SSD_HD) == hd).astype(F32)
            ydiag = ydiag + _mm(cb * dec, xdt * hm)
    if ns == 1:
        st = g["c_sh"][...]
        srow = lax.broadcasted_iota(jnp.int32, st.shape, 0) // SSD_N
        yoff = _mm(cm, st)
        slane = lax.broadcasted_iota(jnp.int32, st.shape, 1) // (SSD_HD * heads_per_group)
        upd = jnp.where(srow == slane, _mm_tn(bm, xw), 0.0)
        st = st * jnp.exp(acum[qt - 1:qt, :]) + upd
        g["c_sh"][...] = st

        @pl.when(c == nc - 1)
        def _final_ssd_state():
            own = jnp.concatenate([st[grp * SSD_N:(grp + 1) * SSD_N, grp * LANES:(grp + 1) * LANES]
                                   for grp in range(SSD_G)], axis=1)
            own = jnp.concatenate([own, jnp.zeros_like(own)], axis=0)
            g["sh_o"][...] = jnp.transpose(own)[:, 0:SSD_N].reshape(g["sh_o"].shape)
    else:
        sh_i, sh_o = g["sh_i"], g["sh_o"]
        per_seq = gw // SSD_HD * SSD_HD
        cols = [[], []]
        for s in range(0, ns, 2):
            two = jnp.concatenate([sh_i[s].reshape(per_seq, SSD_N), sh_i[s + 1].reshape(per_seq, SSD_N)], axis=1)
            two = jnp.transpose(two)
            for grp in range(SSD_G):
                cols[grp] += [two[0:SSD_N, grp * LANES:(grp + 1) * LANES],
                              two[SSD_N:2 * SSD_N, grp * LANES:(grp + 1) * LANES]]
        st = jnp.concatenate([jnp.concatenate(cols[0], axis=1), jnp.concatenate(cols[1], axis=1)], axis=0)
        srow = lax.broadcasted_iota(jnp.int32, st.shape, 0) // SSD_N
        seqm = seq_mask(LANES)
        lastm = jnp.where(t == r - 1, seqm, 0.0)
        yo, upd, dl = [], None, None
        for grp in range(SSD_G):
            gm = ((lane_b // SSD_N) == grp).astype(F32)
            z = _mm(cm * gm, st) * seqm
            zf = z[:, 0:LANES]
            for s in range(1, ns):
                zf = zf + z[:, s * LANES:(s + 1) * LANES]
            yo.append(zf)
            sl = slice(grp * LANES, (grp + 1) * LANES)
            u_g = _mm_tn(bm, jnp.tile(xw[:, sl], (1, ns)) * seqm)
            d_g = jnp.exp(jnp.sum(jnp.tile(acum[:, sl], (1, ns)) * lastm, axis=0, keepdims=True))
            upd = u_g if grp == 0 else jnp.where(srow == grp, u_g, upd)
            dl = d_g if grp == 0 else jnp.where(srow == grp, d_g, dl)
        yoff = jnp.concatenate(yo, axis=1)
        st = st * dl + upd
        for s in range(0, ns, 2):
            two = jnp.concatenate(
                [jnp.concatenate([st[0:SSD_N, q * LANES:(q + 1) * LANES], st[SSD_N:2 * SSD_N, q * LANES:(q + 1) * LANES]],
                                 axis=1) for q in (s, s + 1)], axis=0)
            two = jnp.transpose(two)
            sh_o[s] = two[:, 0:SSD_N].reshape(sh_o.shape[1:])
            sh_o[s + 1] = two[:, SSD_N:2 * SSD_N].reshape(sh_o.shape[1:])
    yc = ydiag + yoff * jnp.exp(acum) + vec[V_SSDD:V_SSDD + 1, 0:gw] * xs
    z = pcols(3 * gw, 4 * gw)
    yc = yc * (z * _sigmoid(z))
    ysave(2 * gw, 3 * gw, _rms(yc) * gain[:, 2 * gw:3 * gw])

    trig = g["trig"]
    cos, sin_a, sin_b = trig[0], trig[1], trig[2]

    def rope(x):
        return x * cos + pltpu.roll(x, LANES - RET_DK // 2, 1) * sin_a + pltpu.roll(x, RET_DK // 2, 1) * sin_b

    q = rope(pcols(6 * gw, 6 * gw + LANES))
    kk = rope(pcols(6 * gw + LANES, 7 * gw)) * (RET_DK ** -0.5)
    v = pcols(7 * gw, 8 * gw)
    dv = gw // RET_H
    intra = jnp.zeros((qt, gw), F32)
    for hd in range(RET_H):
        qm = ((lane_b // RET_DK) == hd).astype(F32)
        sc = _mm_nt(q * qm, kk) * g["dmask"][hd]
        vm = ((lane // dv) == hd).astype(F32)
        intra = intra + _mm(sc, v * vm)
    dec = g["dec"]
    qd, kd = q * dec[0], kk * dec[1]
    if ns == 1:
        rs = g["c_ret"][...]
        cross = _mm(qd, rs)
        rrow = lax.broadcasted_iota(jnp.int32, rs.shape, 0) // RET_DK
        rlane = lax.broadcasted_iota(jnp.int32, rs.shape, 1) // dv
        upd = jnp.where(rrow == rlane, _mm_tn(kd, v), 0.0)
    else:
        ret_i, ret_o = g["ret_i"], g["ret_o"]
        hk = RET_H * RET_DK
        rs = jnp.concatenate([ret_i[s].reshape(hk, dv) for s in range(ns)], axis=1)
        seqm = seq_mask(dv)
        lo = lane_b < dv
        pairs = []
        upd = jnp.zeros(rs.shape, F32)
        for pp in range(RET_H // 2):
            vp = v[:, pp * LANES:(pp + 1) * LANES]
            vsw = pltpu.roll(vp, dv, 1)
            tots = []
            for hh in range(2):
                qm = ((lane_b // RET_DK) == 2 * pp + hh).astype(F32)
                zc = _mm(qd * qm, rs) * seqm
                acc = zc[:, 0:LANES]
                for s2 in range(1, ns // 2):
                    acc = acc + zc[:, s2 * LANES:(s2 + 1) * LANES]
                tots.append(acc + pltpu.roll(acc, dv, 1))
                vh = jnp.where(lo, vp, vsw) if hh == 0 else jnp.where(lo, vsw, vp)
                upd = upd + _mm_tn(kd * qm, jnp.tile(vh, (1, ns // 2)) * seqm)
            pairs.append(jnp.where(lo, tots[0], tots[1]))
        cross = jnp.concatenate(pairs, axis=1)
    rs = rs * dec[2][:, 0:1] + upd
    if ns == 1:
        g["c_ret"][...] = rs

        @pl.when(c == nc - 1)
        def _final_ret_state():
            own = functools.reduce(lambda s, x: s + x, [rs[:, hd * dv:(hd + 1) * dv] for hd in range(RET_H)])
            g["ret_o"][...] = own.reshape(g["ret_o"].shape)
    else:
        for s in range(ns):
            ret_o[s] = rs[:, s * dv:(s + 1) * dv].reshape(ret_o.shape[1:])
    o = intra + cross
    o2 = o * o
    ms = jnp.zeros((qt, gw), F32)
    for hd in range(RET_H):
        vm = ((lane // dv) == hd).astype(F32)
        ms = ms + vm * jnp.sum(o2 * vm, axis=-1, keepdims=True)
    o = o * lax.rsqrt(ms * (1.0 / dv) + EPS) * gain[:, 3 * gw:4 * gw]
    gg = pcols(8 * gw, 9 * gw)
    ysave(3 * gw, 4 * gw, gg * _sigmoid(gg) * o)
    if ns == 1:
        y_all = jnp.concatenate(y_parts, axis=1)
        g["y"][...] = jnp.dot(g["perm"][1], y_all, preferred_element_type=F32).astype(g["y"].dtype)


def _mix_consts(qt, ns, r, nc, pos0):
    assert qt == LANES, "decay tables are stacked as (3, 128, 128)"
    idx = np.arange(qt)
    nk = qt // SUBLANES
    if ns == 1:
        seq, tt = np.zeros_like(idx), (idx % SUBLANES) * nk + idx // SUBLANES
    else:
        seq, tt = idx % ns, idx // ns
    causal = (seq[:, None] == seq[None, :]) & (tt[:, None] >= tt[None, :])
    tri = causal.astype(np.float32)
    lg = np.log1p(-np.exp2(-5.0 - np.arange(RET_H, dtype=np.float64)))
    rel = (tt[:, None] - tt[None, :]).astype(np.float64)
    dmask = np.where(causal[None], np.exp(np.maximum(rel, 0.0)[None] * lg[:, None, None]), 0.0).astype(np.float32)
    lane_h = np.arange(LANES) // RET_DK
    qdec = np.exp((tt[:, None] + 1.0) * lg[lane_h][None, :])
    kdec = np.exp((r - 1.0 - tt[:, None]) * lg[lane_h][None, :])
    rdec = np.broadcast_to(np.exp(r * lg[lane_h])[:, None], (LANES, LANES))
    dec = np.stack([qdec, kdec, rdec]).astype(np.float32)
    half = RET_DK // 2
    pos = pos0 + jnp.asarray((np.arange(nc)[:, None] * qt + tt[None, :]).reshape(-1))
    inv = ROPE_BASE ** (-jnp.arange(half, dtype=F32) / half)
    ang = pos.astype(F32)[:, None] * inv
    reps = LANES // half
    cos, sin = jnp.tile(jnp.cos(ang), (1, reps)), jnp.tile(jnp.sin(ang), (1, reps))
    first = (np.arange(LANES) % RET_DK) < half
    trig = jnp.stack([cos, jnp.where(first, -sin, 0.0), jnp.where(first, 0.0, sin)])
    to_blocked = np.zeros((qt, qt), np.float32)
    to_blocked[idx, (idx % SUBLANES) * nk + idx // SUBLANES] = 1.0
    perm = jnp.asarray(np.stack([to_blocked, to_blocked.T])).astype(MXU_DTYPE)
    return dict(trig=trig, tri=jnp.asarray(tri), dmask=jnp.asarray(dmask), dec=jnp.asarray(dec), perm=perm)


def _state_layout(depth, bsz, ns, gw, sn):
    hk, dv = RET_H * RET_DK, gw // RET_H
    if ns == 1:
        def per_seq(rows, w):
            return (depth, bsz, rows, w), (None, None, rows, w), lambda b, c, l: (l[0], b, 0, 0)
        def matrix(h, rows, w):
            return (depth, bsz, h, rows, w), (None, None, h, rows, w), lambda b, c, l: (l[0], b, 0, 0, 0)
        return dict(lconv=per_seq(CONV_K - 1, gw), lh=per_seq(1, gw), s5r=per_seq(1, sn), s5i=per_seq(1, sn),
                    sconv=per_seq(CONV_K - 1, 2 * gw), sh=matrix(gw // SSD_HD, SSD_HD, SSD_N),
                    ret=matrix(RET_H, RET_DK, dv))

    def flat(per, w):
        return (depth, bsz * per, w), (None, ns * per, w), lambda b, c, l: (l[0], b, 0)

    def conv(w):
        return (depth, CONV_K - 1, bsz, w), (None, CONV_K - 1, ns, w), lambda b, c, l: (l[0], 0, b, 0)
    def matrix(h, rows, w):
        return (depth, bsz, h, rows, w), (None, ns, h, rows, w), lambda b, c, l: (l[0], b, 0, 0, 0)
    return dict(lconv=conv(gw), lh=flat(1, gw), s5r=flat(1, sn), s5i=flat(1, sn), sconv=conv(2 * gw),
                sh=matrix(gw // SSD_HD, SSD_HD, SSD_N), ret=matrix(RET_H, RET_DK, dv))


def _mixer(l, proj, consts, mw, state_in, state_out, *, nb, nc, qt, ns, r, d):
    dp = proj.shape[-1]
    gw = d // 4
    sn = mw["pw"].shape[-1]
    depth = mw["pw"].shape[0]
    layout = _state_layout(depth, nb * ns, ns, gw, sn)
    assert (state_in is None) == (ns == 1)

    def layer_spec(a):
        return pl.BlockSpec((None,) + a.shape[1:], lambda b, c, l: (l[0],) + (0,) * (a.ndim - 1))

    def const_spec(a):
        return pl.BlockSpec(a.shape, lambda b, c, l: (0,) * a.ndim)

    def state_spec(nm):
        _, block, index = layout[nm]
        return pl.BlockSpec(block, index)

    if ns == 1:
        io_spec = lambda w: pl.BlockSpec((qt, w), lambda b, c, l: (b * nc + c, 0))
        y_shape = (nb * nc * qt, d)
        cnames = ("trig", "tri", "dmask", "dec", "perm")
    else:
        proj = proj.reshape(r, nb * ns, dp)
        io_spec = lambda w: pl.BlockSpec((r, ns, w), lambda b, c, l: (0, b, 0))
        y_shape = (r, nb * ns, d)
        cnames = ("trig", "tri", "dmask", "dec")
    names = ("proj",) + cnames + MIX_WEIGHTS
    operands = [proj] + [consts[k] for k in cnames] + [mw[k] for k in MIX_WEIGHTS]
    in_specs = [io_spec(dp), pl.BlockSpec((3, qt, LANES), lambda b, c, l: (0, c, 0))] + \
               [const_spec(consts[k]) for k in cnames[1:]] + [layer_spec(mw[k]) for k in MIX_WEIGHTS]
    if state_in is not None:
        names += tuple(nm + "_i" for nm in STATE_NAMES)
        operands += [state_in[nm] for nm in STATE_NAMES]
        in_specs += [state_spec(nm) for nm in STATE_NAMES]
    first_alias = 1 + len(operands)
    names += tuple(nm + "_alias" for nm in STATE_NAMES)
    operands += [state_out[nm] for nm in STATE_NAMES]
    in_specs += [pl.BlockSpec(memory_space=pl.ANY) for _ in STATE_NAMES]
    names += ("y",) + tuple(nm + "_o" for nm in STATE_NAMES)
    out_specs = [io_spec(d)] + [state_spec(nm) for nm in STATE_NAMES]
    out_shape = [jax.ShapeDtypeStruct(y_shape, MXU_DTYPE)] + \
                [jax.ShapeDtypeStruct(layout[nm][0], F32) for nm in STATE_NAMES]
    scratch = []
    if ns == 1:
        tail = (CONV_K - 1) * SUBLANES
        scratch_shapes = dict(cb_a=(tail, gw), cb_c=(tail, 2 * gw), c_lh=(1, gw), c_s5r=(1, sn),
                              c_s5i=(1, sn), c_sh=(SSD_G * SSD_N, gw), c_ret=(RET_H * RET_DK, gw))
        names += tuple(scratch_shapes)
        scratch = [pltpu.VMEM(shape, F32) for shape in scratch_shapes.values()]
    grid_spec = pltpu.PrefetchScalarGridSpec(num_scalar_prefetch=1, grid=(nb, nc),
                                             in_specs=in_specs, out_specs=out_specs, scratch_shapes=scratch)
    outs = pl.pallas_call(
        functools.partial(_mix_body, names=names, qt=qt, ns=ns, r=r, nc=nc),
        grid_spec=grid_spec,
        out_shape=out_shape,
        input_output_aliases={first_alias + k: 1 + k for k in range(len(STATE_NAMES))},
        compiler_params=pltpu.CompilerParams(dimension_semantics=("arbitrary", "arbitrary"),
                                             vmem_limit_bytes=VMEM_LIMIT),
        name="mixer",
    )(l, *operands)
    return outs[0].reshape(-1, d), dict(zip(STATE_NAMES, outs[1:]))


def _post_body(l_ref, y_ref, x_ref, g1_ref, sc_ref, sh_ref, g2_ref, lt_ref, wo_ref, wr_ref, br_ref,
               wg_ref, wu_ref, wd_ref, o_ref, xs_ref, cs_ref, os_ref, sm_ref):
    del l_ref
    tm, d = x_ref.shape
    tms = xs_ref.shape[0]
    x = x_ref[...] + _token_rows(g1_ref[...], tm) * jnp.dot(y_ref[...], wo_ref[...], preferred_element_type=F32)
    h = _rms(x) * (1.0 + _token_rows(sc_ref[...], tm)) + _token_rows(sh_ref[...], tm)
    hb = h.astype(MXU_DTYPE)
    h_lo = (h - hb.astype(F32)).astype(MXU_DTYPE)
    wr = wr_ref[...]
    l12 = jnp.dot(hb, wr, preferred_element_type=F32)
    logits = (l12[:, 0:LANES] + l12[:, LANES:2 * LANES]
              + jnp.dot(h_lo, wr[:, 0:LANES], preferred_element_type=F32) + br_ref[...])
    logits_t = jnp.transpose(logits)
    col = [logits_t[k:k + 1, :] for k in range(N_GROUPS + N_EXPERTS)]

    def first_max(vals, allowed=None):
        neg = jnp.full_like(vals[0], -jnp.inf)
        cand = vals if allowed is None else [jnp.where(al > 0.0, v, neg) for v, al in zip(vals, allowed)]
        m = functools.reduce(jnp.maximum, cand)
        rem = jnp.ones_like(vals[0])
        hot = []
        for v in cand:
            f = jnp.where(v >= m, rem, 0.0)
            rem = rem - f
            hot.append(f)
        return hot, m

    grp, gmax = first_max(col[:N_GROUPS])
    gate = 1.0 / functools.reduce(lambda s, v: s + v, [jnp.exp(v - gmax) for v in col[:N_GROUPS]])
    le = [functools.reduce(lambda s, v: s + v,
                           [grp[gi] * col[N_GROUPS + gi * N_PER_GROUP + j] for gi in range(N_GROUPS)])
          for j in range(N_PER_GROUP)]
    top1, m1 = first_max(le)
    top2, m2 = first_max(le, [1.0 - f for f in top1])
    e2 = jnp.exp(m2 - m1)
    w1 = 1.0 / (1.0 + e2)
    w2 = e2 * w1
    wgrp = [(top1[j] * w1 + top2[j] * w2) * gate for j in range(N_PER_GROUP)]

    sub8 = lax.broadcasted_iota(jnp.int32, (SUBLANES, 1), 0)
    ghot_t = functools.reduce(lambda s, v: s + v, [jnp.where(sub8 == gi, grp[gi], 0.0) for gi in range(N_GROUPS)])
    before = _mm_nt(ghot_t, lt_ref[...])
    count = jnp.sum(ghot_t, axis=1, keepdims=True)
    cnt = [count[gi:gi + 1, :] for gi in range(N_GROUPS)]
    base = [jnp.zeros((1, 1), F32)]
    for gi in range(1, N_GROUPS):
        base.append(jnp.floor((base[-1] + cnt[gi - 1] + (PACK - 1.0)) * (1.0 / PACK)) * PACK)
    pos_row = functools.reduce(lambda s, v: s + v,
                               [grp[gi] * (base[gi] + before[gi:gi + 1, :]) for gi in range(N_GROUPS)])
    pos = jnp.transpose(jnp.broadcast_to(pos_row, (LANES, tm)))[:, 0:1]
    slot = lax.broadcasted_iota(jnp.int32, (tms, tm), 0).astype(F32)
    gather = jnp.where(slot == pos_row, 1.0, 0.0).astype(MXU_DTYPE)
    slot_l = lax.broadcasted_iota(jnp.int32, (tm, tms), 1).astype(F32)
    scatter = jnp.where(slot_l == pos, 1.0, 0.0).astype(MXU_DTYPE)

    xs_ref[...] = jnp.dot(gather, hb, preferred_element_type=F32).astype(xs_ref.dtype)
    nt = lambda p: lax.dot_general(gather, p, (((1,), (1,)), ((), ())), preferred_element_type=F32)
    for gi in range(N_GROUPS):
        comb_g = functools.reduce(lambda s, v: s + v,
                                  [jnp.where(sub8 == j, grp[gi] * wgrp[j], 0.0) for j in range(N_PER_GROUP)])
        c1, c2, c3 = _split3(comb_g)
        cs_ref[gi] = nt(c1) + nt(c2) + nt(c3)
        sm_ref[0, gi] = base[gi][0, 0].astype(jnp.int32)
        sm_ref[1, gi] = cnt[gi][0, 0].astype(jnp.int32)
    os_ref[...] = jnp.zeros(os_ref.shape, os_ref.dtype)

    def expert_block(size):
        def run(gi, r0):
            rows = pl.ds(pl.multiple_of(r0, PACK), size)
            xb = xs_ref[rows, :]
            cb = cs_ref[gi, rows, :]
            acc = jnp.zeros((size, d), F32)
            for j in range(N_PER_GROUP):
                e = gi * N_PER_GROUP + j
                gt = jnp.dot(xb, wg_ref[e], preferred_element_type=F32)
                he = gt * _sigmoid(gt) * jnp.dot(xb, wu_ref[e], preferred_element_type=F32)
                acc = acc + jnp.dot((he * cb[:, j:j + 1]).astype(MXU_DTYPE), wd_ref[e], preferred_element_type=F32)
            os_ref[rows, :] += acc
            return jnp.int32(0)
        return run

    blocks = [expert_block((k + 1) * EXPERT_STEP) for k in range(EXPERT_SIZES)]

    def group(gi, carry):
        start, total = sm_ref[0, gi], sm_ref[1, gi]

        def step(state):
            off, rem = state
            k = jnp.minimum((rem - 1) // EXPERT_STEP, EXPERT_SIZES - 1)
            lax.switch(k, [functools.partial(b, gi, start + off) for b in blocks])
            return off + (k + 1) * EXPERT_STEP, rem - (k + 1) * EXPERT_STEP

        lax.while_loop(lambda state: state[1] > 0, step, (jnp.int32(0), total))
        return carry

    lax.fori_loop(0, N_GROUPS, group, 0)
    moe = _select_rows(scatter, os_ref[...], pieces=2)
    o_ref[...] = x + _token_rows(g2_ref[...], tm) * moe


def _post(l, y, x, mod, pw, tm, rows_per_seq):
    t, d = x.shape
    tms = tm + EXPERT_STEP * EXPERT_SIZES + PACK * N_GROUPS
    ltri = jnp.asarray(np.tril(np.ones((tm, tm), np.float32), -1)).astype(MXU_DTYPE)

    def layer_spec(a):
        return pl.BlockSpec((None,) + a.shape[1:], lambda i, l: (l[0],) + (0,) * (a.ndim - 1),
                            pipeline_mode=pl.Buffered(1))

    grid_spec = pltpu.PrefetchScalarGridSpec(
        num_scalar_prefetch=1,
        grid=(t // tm,),
        in_specs=[
            pl.BlockSpec((tm, d), lambda i, l: (i, 0)),
            pl.BlockSpec((tm, d), lambda i, l: (i, 0)),
            _mod_spec(mod, 2, tm, rows_per_seq),
            _mod_spec(mod, 4, tm, rows_per_seq),
            _mod_spec(mod, 3, tm, rows_per_seq),
            _mod_spec(mod, 5, tm, rows_per_seq),
            pl.BlockSpec((tm, tm), lambda i, l: (0, 0)),
        ] + [layer_spec(pw[k]) for k in ("w_out", "w_route", "b_route", "w_gate", "w_up", "w_down")],
        out_specs=pl.BlockSpec((tm, d), lambda i, l: (i, 0)),
        scratch_shapes=[pltpu.VMEM((tms, d), MXU_DTYPE), pltpu.VMEM((N_GROUPS, tms, SUBLANES), F32),
                        pltpu.VMEM((tms, d), F32), pltpu.SMEM((2, N_GROUPS), jnp.int32)],
    )
    return pl.pallas_call(
        _post_body,
        grid_spec=grid_spec,
        out_shape=jax.ShapeDtypeStruct((t, d), F32),
        compiler_params=pltpu.CompilerParams(dimension_semantics=("arbitrary",), vmem_limit_bytes=VMEM_LIMIT),
        name="post",
    )(l, y, x, mod, mod, mod, mod, ltri,
      *[pw[k] for k in ("w_out", "w_route", "b_route", "w_gate", "w_up", "w_down")])


def _final_body(x_ref, w_ref, o_ref):
    o_ref[...] = _rms(x_ref[...]) * w_ref[...]


def _final_norm(x, w, tm):
    t, d = x.shape
    return pl.pallas_call(
        _final_body,
        grid=(t // tm,),
        in_specs=[pl.BlockSpec((tm, d), lambda i: (i, 0)), pl.BlockSpec((1, d), lambda i: (0, 0))],
        out_specs=pl.BlockSpec((tm, d), lambda i: (i, 0)),
        out_shape=jax.ShapeDtypeStruct((t, d), F32),
        name="final_norm",
    )(x, w.reshape(1, d))


def _block_diag(w):
    depth, nblk, bi, bj = w.shape
    eye = jnp.eye(nblk, dtype=w.dtype)
    return (w[:, :, :, None, :] * eye[None, :, None, :, None]).reshape(depth, nblk * bi, nblk * bj)


def _pad_lanes(a, width):
    return jnp.pad(a, [(0, 0)] * (a.ndim - 1) + [(0, width - a.shape[-1])])


def _prep_weights(w_in, lru_conv_w, lru_conv_b, lru_wa, lru_ba, lru_wx, lru_bx, lru_lambda,
                  s5_a_re, s5_a_im, s5_b_re, s5_b_im, s5_c_re, s5_c_im, s5_d, s5_log_dt, s5_w_glu,
                  ssd_conv_w, ssd_conv_b, ssd_dt_bias, ssd_a_log, ssd_d, mix_norm, qt):
    depth, d, _ = w_in.shape
    gw = d // 4
    n_ssd_h = gw // SSD_HD
    xbc_end = 4 * gw + gw + 2 * SSD_G * SSD_N
    dt_cols = jnp.repeat(w_in[..., xbc_end:xbc_end + n_ssd_h], SSD_HD, axis=-1)
    w_in_p = jnp.concatenate([w_in[..., :xbc_end], w_in[..., xbc_end + n_ssd_h:], dt_cols], axis=-1)

    def row(a):
        return _pad_lanes(a.reshape(depth, 1, -1), d)

    vec = jnp.concatenate([
        row(lru_conv_b), row(jnp.concatenate([lru_ba, lru_bx], -1)), row(lru_lambda), row(s5_d),
        row(ssd_conv_b), row(jnp.repeat(ssd_dt_bias, SSD_HD, -1)), row(jnp.repeat(ssd_a_log, SSD_HD, -1)),
        row(jnp.repeat(ssd_d, SSD_HD, -1)), row(mix_norm),
        _pad_lanes(lru_conv_w, d), _pad_lanes(ssd_conv_w, d),
        jnp.zeros((depth, V_ROWS - V_SCW - CONV_K, d), F32)], axis=1)

    wgate = jnp.concatenate([_block_diag(lru_wa), _block_diag(lru_wx)], axis=-1)

    dt = jnp.exp(s5_log_dt)[..., None]
    lr, li = s5_a_re, s5_a_im
    mag = jnp.exp(lr * dt)
    ab_re, ab_im = mag * jnp.cos(li * dt), mag * jnp.sin(li * dt)
    den = lr * lr + li * li
    q_re = ((ab_re - 1.0) * lr + ab_im * li) / den
    q_im = (ab_im * lr - (ab_re - 1.0) * li) / den
    bb_re = q_re[..., None] * s5_b_re - q_im[..., None] * s5_b_im
    bb_im = q_re[..., None] * s5_b_im + q_im[..., None] * s5_b_re
    bbm = jnp.concatenate([_block_diag(jnp.swapaxes(bb_re, -1, -2)), _block_diag(jnp.swapaxes(bb_im, -1, -2))], -1)
    cre = _block_diag(jnp.swapaxes(s5_c_re, -1, -2))
    cim = _block_diag(jnp.swapaxes(s5_c_im, -1, -2))
    a_re, a_im = ab_re.reshape(depth, -1), ab_im.reshape(depth, -1)
    pr, pi = a_re, a_im
    rows = []
    for _ in range(int(math.log2(qt))):
        rows += [pr, pi]
        pr, pi = pr * pr - pi * pi, 2.0 * pr * pi
    pw = jnp.stack(rows, axis=1)
    pw = jnp.pad(pw, ((0, 0), (0, (-pw.shape[1]) % SUBLANES), (0, 0)))
    pr, pi = a_re, a_im
    k_re, k_im = [], []
    for _ in range(qt // SUBLANES):
        k_re.append(pr)
        k_im.append(pi)
        pr, pi = pr * a_re - pi * a_im, pr * a_im + pi * a_re
    pk = jnp.stack(k_re + k_im, axis=1)

    mw = dict(vec=vec, wgate=wgate.astype(MXU_DTYPE), bbm=bbm.astype(MXU_DTYPE), cre=cre.astype(MXU_DTYPE),
              cim=cim.astype(MXU_DTYPE), wglu=s5_w_glu.astype(MXU_DTYPE), pw=pw, pk=pk)
    return w_in_p.astype(MXU_DTYPE), mw


def _group_of_head(n_h):
    return (jnp.arange(SSD_G)[:, None] == jnp.arange(n_h)[None, :] // (n_h // SSD_G)).astype(F32)


def _states_to_layout(states, layout, ns):
    out = {}
    for nm, s in zip(STATE_NAMES, states):
        if nm in ("lconv", "sconv"):
            s = s.transpose(0, 2, 1, 3)
        out[nm] = s.reshape(layout[nm][0])
    return out


def _states_from_layout(arrs, ns, ref_shapes):
    out = []
    for nm, shape in zip(STATE_NAMES, ref_shapes):
        a = arrs[nm]
        if ns > 1 and nm in ("lconv", "sconv"):
            a = a.transpose(0, 2, 1, 3)
        out.append(a.reshape(shape))
    return tuple(out)


def _trunk(x, mod, states, ref_shapes, pos0, w_in_p, mw, pw, final_norm, *, qt, ns, tm):
    bsz, seq_len, d = x.shape
    depth = w_in_p.shape[0]
    gw = d // 4
    r = qt // ns
    nc = seq_len // r if ns == 1 else 1
    assert (ns == 1 and seq_len % qt == 0 and states is None) or (ns > 1 and r == seq_len and bsz % ns == 0)
    assert r >= CONV_K and (r & (r - 1)) == 0
    nb = bsz // ns
    tok = bsz * seq_len
    consts = _mix_consts(qt, ns, r, nc, pos0)
    layout = _state_layout(depth, bsz, ns, gw, mw["pw"].shape[-1])
    state_in = None if states is None else _states_to_layout(states, layout, ns)
    state_out = {nm: jnp.zeros(layout[nm][0], F32) for nm in STATE_NAMES}
    x2 = x.reshape(tok, d) if ns == 1 else x.transpose(1, 0, 2).reshape(tok, d)
    blocked_chunk = qt if ns == 1 else None

    def layer(carry, l):
        xc, st = carry
        lv = jnp.reshape(l, (1,)).astype(jnp.int32)
        proj = _inproj(lv, xc, mod, w_in_p, tm, seq_len, blocked_chunk)
        y, st = _mixer(lv, proj, consts, mw, state_in, st, nb=nb, nc=nc, qt=qt, ns=ns, r=r, d=d)
        xn = _post(lv, y, xc, mod, pw, tm, seq_len)
        return (xn, st), None

    (xf, st), _ = lax.scan(layer, (x2, state_out), jnp.arange(depth))
    y = _final_norm(xf, final_norm, tm)
    y = y.reshape(bsz, seq_len, d) if ns == 1 else y.reshape(seq_len, bsz, d).transpose(1, 0, 2)
    return y, _states_from_layout(st, ns, ref_shapes)


def kernel(x_prompt, x_sample, state_lru_conv, state_lru_h, state_s5_re, state_s5_im, state_ssd_conv, state_ssd_h, state_ret, c_prompt, c_sample, w_ada, b_ada, w_in, lru_conv_w, lru_conv_b, lru_wa, lru_ba, lru_wx, lru_bx, lru_lambda, s5_a_re, s5_a_im, s5_b_re, s5_b_im, s5_c_re, s5_c_im, s5_d, s5_log_dt, s5_w_glu, ssd_conv_w, ssd_conv_b, ssd_dt_bias, ssd_a_log, ssd_d, mix_norm, w_out, w_route_group, b_route_group, w_route_exp, b_route_exp, w_exp_gate, w_exp_up, w_exp_down, final_norm):
    bp, lp, d = x_prompt.shape
    bs, ls, _ = x_sample.shape
    depth = w_in.shape[0]
    qt = LANES
    ns_s = qt // ls

    mod = _modulation(jnp.concatenate([c_prompt, c_sample], axis=0), w_ada, b_ada)
    mod_p = mod[:, :, :bp].reshape(depth, N_MOD, bp, 1, d)
    mod_s = mod[:, :, bp:]

    w_in_p, mw = _prep_weights(w_in, lru_conv_w, lru_conv_b, lru_wa, lru_ba, lru_wx, lru_bx, lru_lambda,
                               s5_a_re, s5_a_im, s5_b_re, s5_b_im, s5_c_re, s5_c_im, s5_d, s5_log_dt, s5_w_glu,
                               ssd_conv_w, ssd_conv_b, ssd_dt_bias, ssd_a_log, ssd_d, mix_norm, qt)
    w_route = _pad_lanes(jnp.concatenate([w_route_group, w_route_exp.reshape(depth, d, N_EXPERTS)], -1), LANES)
    b_route = _pad_lanes(jnp.concatenate([b_route_group, b_route_exp.reshape(depth, N_EXPERTS)], -1), LANES)
    w_route_hi = w_route.astype(MXU_DTYPE)
    w_route_lo = (w_route - w_route_hi.astype(F32)).astype(MXU_DTYPE)
    w_route = jnp.concatenate([w_route_hi, w_route_lo], axis=-1)
    pw = dict(w_out=w_out.astype(MXU_DTYPE), w_route=w_route, b_route=b_route.reshape(depth, 1, LANES),
              w_gate=w_exp_gate.astype(MXU_DTYPE), w_up=w_exp_up.astype(MXU_DTYPE),
              w_down=w_exp_down.astype(MXU_DTYPE))
    w_in_p, mw, pw = lax.optimization_barrier((w_in_p, mw, pw))

    states_s = (state_lru_conv, state_lru_h, state_s5_re, state_s5_im, state_ssd_conv, state_ssd_h, state_ret)
    shapes_s = [s.shape for s in states_s]
    shapes_p = [(depth, bp) + s[2:] for s in shapes_s]
    y_p, new_p = _trunk(x_prompt, mod_p, None, shapes_p, 0, w_in_p, mw, pw, final_norm,
                        qt=qt, ns=1, tm=min(512, lp))
    y_s, new_s = _trunk(x_sample, mod_s, states_s, shapes_s, PAST_LEN, w_in_p, mw, pw, final_norm,
                        qt=qt, ns=ns_s, tm=bs * ls)
    out = [y_p, y_s]
    for a, b in zip(new_p, new_s):
        out += [a, b]
    return tuple(out)
```

```python
import functools
import math

import numpy as np
import jax
import jax.numpy as jnp
from jax import lax
from jax.experimental import pallas as pl
from jax.experimental.pallas import tpu as pltpu

F32 = jnp.float32
MXU_DTYPE = jnp.bfloat16
HIGHEST = lax.Precision.HIGHEST

EPS = 1e-6
CONV_K = 4
LRU_C = 8.0
SSD_HD = 64
SSD_G = 2
SSD_N = 64
RET_H = 4
RET_DK = 32
ROPE_BASE = 10000.0
N_GROUPS = 4
N_PER_GROUP = 4
N_EXPERTS = N_GROUPS * N_PER_GROUP
N_MOD = 6
PAST_LEN = 16384

LANES = 128
SUBLANES = 8
PACK = 16
VMEM_LIMIT = 56 * 1024 * 1024
EXPERT_ROWS = 144


def _mm(a, b):
    return jnp.dot(a.astype(MXU_DTYPE), b.astype(MXU_DTYPE), preferred_element_type=F32)


def _mm_nt(a, b):
    return lax.dot_general(a.astype(MXU_DTYPE), b.astype(MXU_DTYPE), (((1,), (1,)), ((), ())),
                           preferred_element_type=F32)


def _mm_tn(a, b):
    return lax.dot_general(a.astype(MXU_DTYPE), b.astype(MXU_DTYPE), (((0,), (0,)), ((), ())),
                           preferred_element_type=F32)


def _split3(x):
    x1 = x.astype(MXU_DTYPE)
    r1 = x - x1.astype(F32)
    x2 = r1.astype(MXU_DTYPE)
    x3 = (r1 - x2.astype(F32)).astype(MXU_DTYPE)
    return x1, x2, x3


def _select_rows(onehot, x, pieces=3):
    parts = _split3(x)[:pieces]
    out = jnp.dot(onehot, parts[0], preferred_element_type=F32)
    for p in parts[1:]:
        out = out + jnp.dot(onehot, p, preferred_element_type=F32)
    return out


def _sigmoid(x):
    return 0.5 * jnp.tanh(0.5 * x) + 0.5


def _rms(x):
    return x * lax.rsqrt(jnp.mean(x * x, axis=-1, keepdims=True) + EPS)


def _token_rows(v, tm):
    n = v.shape[0]
    return v if n in (1, tm) else jnp.tile(v, (tm // n, 1))


def _mod_body(c_ref, w_ref, b_ref, first_ref, rest_ref):
    c = c_ref[...]
    m = _mm(c * _sigmoid(c), w_ref[...]) + b_ref[...]
    n_first = first_ref.shape[0]
    first_ref[...] = m[0:n_first]
    rest_ref[...] = m[n_first:]


def _modulation(c_all, n_first, w_ada, b_ada):
    depth, d, _ = w_ada.shape
    nb = c_all.shape[0]
    return pl.pallas_call(
        _mod_body,
        grid=(depth, N_MOD),
        in_specs=[
            pl.BlockSpec((nb, d), lambda l, k: (0, 0)),
            pl.BlockSpec((None, d, d), lambda l, k: (l, 0, k)),
            pl.BlockSpec((None, None, 1, d), lambda l, k: (l, k, 0, 0)),
        ],
        out_specs=[pl.BlockSpec((None, None, n_first, d), lambda l, k: (l, k, 0, 0)),
                   pl.BlockSpec((None, None, nb - n_first, d), lambda l, k: (l, k, 0, 0))],
        out_shape=[jax.ShapeDtypeStruct((depth, N_MOD, n_first, d), F32),
                   jax.ShapeDtypeStruct((depth, N_MOD, nb - n_first, d), F32)],
        compiler_params=pltpu.CompilerParams(dimension_semantics=("arbitrary", "arbitrary"),
                                             vmem_limit_bytes=VMEM_LIMIT),
        name="modulation",
    )(c_all, w_ada, b_ada.reshape(depth, N_MOD, 1, d))


def _mod_spec(mod, k, tm, rows_per_seq):
    if mod.ndim == 5:
        tiles_per_seq = rows_per_seq // tm
        return pl.BlockSpec((None, None, None, 1, mod.shape[-1]),
                            lambda i, l: (l[0], k, i // tiles_per_seq, 0, 0))
    return pl.BlockSpec((None, None) + mod.shape[2:], lambda i, l: (l[0], k, 0, 0))


def _inproj_body(l_ref, x_ref, sc_ref, sh_ref, w_ref, o_ref, *, blocked_chunk):
    tm = x_ref.shape[0]
    h = _rms(x_ref[...]) * (1.0 + _token_rows(sc_ref[...], tm)) + _token_rows(sh_ref[...], tm)
    hb = h.astype(MXU_DTYPE)
    if blocked_chunk is not None:
        qt, nk = blocked_chunk, blocked_chunk // SUBLANES
        r_i = lax.broadcasted_iota(jnp.int32, (qt, qt), 0)
        c_i = lax.broadcasted_iota(jnp.int32, (qt, qt), 1)
        to_blocked = jnp.where(c_i == (r_i % SUBLANES) * nk + r_i // SUBLANES, 1.0, 0.0).astype(MXU_DTYPE)
        hb = jnp.concatenate([jnp.dot(to_blocked, hb[j * qt:(j + 1) * qt], preferred_element_type=F32)
                              for j in range(tm // qt)], axis=0).astype(MXU_DTYPE)
    o_ref[...] = jnp.dot(hb, w_ref[...], preferred_element_type=F32)


def _inproj(l, x, mod, w_in, tm, rows_per_seq, blocked_chunk=None):
    t, d = x.shape
    dp = w_in.shape[-1]
    assert blocked_chunk is None or (mod.ndim == 5 and tm % blocked_chunk == 0)
    grid_spec = pltpu.PrefetchScalarGridSpec(
        num_scalar_prefetch=1,
        grid=(t // tm,),
        in_specs=[
            pl.BlockSpec((tm, d), lambda i, l: (i, 0)),
            _mod_spec(mod, 1, tm, rows_per_seq),
            _mod_spec(mod, 0, tm, rows_per_seq),
            pl.BlockSpec((None, d, dp), lambda i, l: (l[0], 0, 0)),
        ],
        out_specs=pl.BlockSpec((tm, dp), lambda i, l: (i, 0)),
    )
    return pl.pallas_call(
        functools.partial(_inproj_body, blocked_chunk=blocked_chunk),
        grid_spec=grid_spec,
        out_shape=jax.ShapeDtypeStruct((t, dp), F32),
        compiler_params=pltpu.CompilerParams(dimension_semantics=("arbitrary",), vmem_limit_bytes=VMEM_LIMIT),
        name="inproj",
    )(l, x, mod, mod, w_in)


V_LCB, V_BGATE, V_LAM, V_S5D, V_SCB, V_DTB, V_ALOG, V_SSDD, V_GAIN, V_LCW, V_SCW = 0, 1, 2, 3, 4, 5, 6, 7, 8, 9, 13
V_ROWS = 24
STATE_NAMES = ("lconv", "lh", "s5r", "s5i", "sconv", "sh", "ret")
MIX_WEIGHTS = ("vec", "wgate", "bbm", "cre", "cim", "wglu", "pw", "pk")


def _mix_body(l_ref, *refs, names, qt, ns, r, nc):
    del l_ref
    g = dict(zip(names, refs))
    gw = g["lh_o"].shape[-1]
    sn = g["s5r_o"].shape[-1]
    nlev = int(math.log2(r))
    nk = qt // SUBLANES
    c = pl.program_id(1)

    if ns == 1:
        @pl.when(c == 0)
        def _zero_carries():
            for nm in ("c_lh", "c_s5r", "c_s5i", "c_sh", "c_ret"):
                g[nm][...] = jnp.zeros(g[nm].shape, F32)
            for nm in ("cb_a", "cb_c"):
                g[nm][...] = jnp.zeros(g[nm].shape, F32)

    proj = g["proj"]
    vec = g["vec"]
    pw = g["pw"]

    def pcols(a, b):
        v = proj[..., a:b]
        return v if ns == 1 else v.reshape(qt, b - a)

    y_parts = []

    def ysave(a, b, val):
        val = val.astype(g["y"].dtype)
        if ns == 1:
            y_parts.append(val)
        else:
            g["y"][..., a:b] = val.reshape(r, ns, b - a)

    row = lax.broadcasted_iota(jnp.int32, (qt, 1), 0)
    t = (row % SUBLANES) * nk + row // SUBLANES if ns == 1 else row // ns
    sub = lax.broadcasted_iota(jnp.int32, (SUBLANES, 1), 0)

    def down(x, d, fill=0.0):
        return jnp.where(t >= d, pltpu.roll(x, d * ns, 0), fill)

    def up(x, d):
        return jnp.where(t + d < r, pltpu.roll(x, qt - d * ns, 0), 0.0)

    def sdown(x, d, fill=0.0):
        return jnp.where(sub >= d, pltpu.roll(x, d, 0), fill)

    def slabs(x):
        return [x[SUBLANES * k:SUBLANES * (k + 1)] for k in range(nk)]

    def first_rows(carry, init):
        if ns == 1:
            return jnp.where(row == 0, g[carry][...], 0.0)
        h0 = g[init][...]
        return jnp.concatenate([h0, jnp.zeros((qt - ns, h0.shape[1]), F32)], axis=0)

    def conv(xraw, nm, w0, b):
        width = xraw.shape[-1]
        acc = vec[b:b + 1, 0:width] + vec[w0 + 3:w0 + 4, 0:width] * xraw
        if ns == 1:
            cb = g["cb_" + nm]
            xs_k = slabs(xraw)
            wrap = [jnp.where(sub >= 1, pltpu.roll(xs_k[nk - j], 1, 0),
                              pltpu.roll(cb[(CONV_K - 1 - j) * SUBLANES:(CONV_K - j) * SUBLANES, :], 1, 0))
                    for j in range(1, CONV_K)]
            for m in range(1, CONV_K):
                sh = jnp.concatenate([wrap[m - k - 1] for k in range(m)] + xs_k[0:nk - m], axis=0)
                acc = acc + vec[w0 + 3 - m:w0 + 4 - m, 0:width] * sh
            cb[...] = xraw[qt - (CONV_K - 1) * SUBLANES:, :]
            g[{"a": "lconv_o", "c": "sconv_o"}[nm]][...] = jnp.concatenate(
                [xs_k[nk - j][SUBLANES - 1:SUBLANES, :] for j in range(CONV_K - 1, 0, -1)], axis=0)
            return acc
        o = g[{"a": "lconv_o", "c": "sconv_o"}[nm]]
        buf = g[{"a": "lconv_i", "c": "sconv_i"}[nm]][...].reshape((CONV_K - 1) * ns, width)
        buf = jnp.concatenate([buf, jnp.zeros((qt - (CONV_K - 1) * ns, width), F32)], axis=0)
        for m in range(1, CONV_K):
            prev = buf if m == 3 else pltpu.roll(buf, qt - (3 - m) * ns, 0)
            sh = jnp.where(t >= m, pltpu.roll(xraw, m * ns, 0), prev)
            acc = acc + vec[w0 + 3 - m:w0 + 4 - m, 0:width] * sh
        o[...] = xraw[(r - (CONV_K - 1)) * ns:, :].reshape(CONV_K - 1, ns, width)
        return acc

    def save_last(nm, h):
        if ns == 1:
            g["c_" + nm][...] = h[qt - 1:qt, :]
            g[nm + "_o"][...] = h[qt - 1:qt, :]
        else:
            g[nm + "_o"][...] = h[qt - ns:, :]

    def seq_mask(width):
        lane_seq = lax.broadcasted_iota(jnp.int32, (qt, ns * width), 1) // width
        row_seq = lax.broadcasted_iota(jnp.int32, (qt, ns * width), 0) % ns
        return (lane_seq == row_seq).astype(F32)

    gain = vec[V_GAIN:V_GAIN + 1, :]

    xc = conv(pcols(0, gw), "a", V_LCW, V_LCB)
    ga = pcols(gw, 2 * gw)
    pre = _mm(xc, g["wgate"][...]) + vec[V_BGATE:V_BGATE + 1, 0:2 * gw]
    rg = _sigmoid(pre[:, 0:gw])
    ig = _sigmoid(pre[:, gw:2 * gw])
    log_a = -LRU_C * rg * jax.nn.softplus(-vec[V_LAM:V_LAM + 1, 0:gw])
    a = jnp.exp(log_a)
    b = jnp.sqrt(-jnp.tanh(log_a) * (a * a + 1.0)) * (ig * xc)
    b = b + a * first_rows("c_lh", "lh_i")
    if ns == 1:
        a_s, b_s = slabs(a), slabs(b)
        hs, ps = [b_s[0]], [a_s[0]]
        for k in range(1, nk):
            hs.append(a_s[k] * hs[-1] + b_s[k])
            ps.append(a_s[k] * ps[-1])
        e, ae = hs[-1], ps[-1]
        for j in range(int(math.log2(SUBLANES))):
            d = 1 << j
            e = ae * sdown(e, d) + e
            ae = ae * sdown(ae, d, 1.0)
        cin = sdown(e, 1)
        b = jnp.concatenate([hs[k] + ps[k] * cin for k in range(nk)], axis=0)
    else:
        for k in range(nlev):
            d = 1 << k
            b = a * down(b, d) + b
            a = a * down(a, d, 1.0)
    save_last("lh", b)
    ya = _rms(b * jax.nn.gelu(ga)) * gain[:, 0:gw]

    u = pcols(2 * gw, 3 * gw)
    bu = _mm(u, g["bbm"][...])
    p_re, p_im = pw[0:1, :], pw[1:2, :]
    h0r, h0i = first_rows("c_s5r", "s5r_i"), first_rows("c_s5i", "s5i_i")
    hr = bu[:, 0:sn] + (p_re * h0r - p_im * h0i)
    hi = bu[:, sn:2 * sn] + (p_re * h0i + p_im * h0r)
    if ns == 1:
        pk = g["pk"]
        br_s, bi_s = slabs(hr), slabs(hi)
        hrs, his = [br_s[0]], [bi_s[0]]
        for k in range(1, nk):
            hrs.append(br_s[k] + (p_re * hrs[-1] - p_im * his[-1]))
            his.append(bi_s[k] + (p_re * his[-1] + p_im * hrs[-2]))
        er, ei = hrs[-1], his[-1]
        lev0 = int(math.log2(nk))
        for j in range(int(math.log2(SUBLANES))):
            d = 1 << j
            q_re, q_im = pw[2 * (lev0 + j):2 * (lev0 + j) + 1, :], pw[2 * (lev0 + j) + 1:2 * (lev0 + j) + 2, :]
            sr, si = sdown(er, d), sdown(ei, d)
            er, ei = er + (q_re * sr - q_im * si), ei + (q_re * si + q_im * sr)
        cr, ci = sdown(er, 1), sdown(ei, 1)
        hr = jnp.concatenate([hrs[k] + (pk[k:k + 1, :] * cr - pk[nk + k:nk + k + 1, :] * ci) for k in range(nk)], 0)
        hi = jnp.concatenate([his[k] + (pk[k:k + 1, :] * ci + pk[nk + k:nk + k + 1, :] * cr) for k in range(nk)], 0)
    else:
        for k in range(nlev):
            d = 1 << k
            q_re, q_im = pw[2 * k:2 * k + 1, :], pw[2 * k + 1:2 * k + 2, :]
            sr, si = down(hr, d), down(hi, d)
            hr, hi = hr + (q_re * sr - q_im * si), hi + (q_re * si + q_im * sr)
    save_last("s5r", hr)
    save_last("s5i", hi)
    yb = _mm(hr, g["cre"][...]) - _mm(hi, g["cim"][...])
    yb = jax.nn.gelu(yb + vec[V_S5D:V_S5D + 1, 0:gw] * u)
    yb = yb * _sigmoid(_mm(yb, g["wglu"][...]))
    yb = _rms(yb) * gain[:, gw:2 * gw]
    ysave(0, 2 * gw, jnp.concatenate([ya, yb], axis=1))

    tri = g["tri"][...] > 0.0
    xbc = conv(pcols(4 * gw, 6 * gw), "c", V_SCW, V_SCB)
    xbc = xbc * _sigmoid(xbc)
    xs, bm, cm = xbc[:, 0:gw], xbc[:, gw:gw + LANES], xbc[:, gw + LANES:2 * gw]
    dt = jax.nn.softplus(pcols(9 * gw, 10 * gw) + vec[V_DTB:V_DTB + 1, 0:gw])
    a_e = -jnp.exp(vec[V_ALOG:V_ALOG + 1, 0:gw])
    dta = dt * a_e
    if ns == 1:
        cs = slabs(dta)
        for k in range(1, nk):
            cs[k] = cs[k] + cs[k - 1]
        e = cs[-1]
        for j in range(int(math.log2(SUBLANES))):
            e = e + sdown(e, 1 << j)
        cin = sdown(e, 1)
        acum = jnp.concatenate([ck + cin for ck in cs], axis=0)
        alast = acum[qt - 1:qt, :]
    else:
        acum = dta
        for k in range(nlev):
            acum = acum + down(acum, 1 << k)
        suf = dta
        for k in range(nlev):
            suf = suf + up(suf, 1 << k)
        alast = acum + suf - dta
    wend = jnp.exp(alast - acum) * dt
    acum_t = jnp.transpose(acum)
    lane = lax.broadcasted_iota(jnp.int32, (1, gw), 1)
    lane_b = lax.broadcasted_iota(jnp.int32, (1, LANES), 1)
    heads_per_group = gw // SSD_HD // SSD_G
    xdt = xs * dt
    xw = xs * wend
    ydiag = jnp.zeros((qt, gw), F32)
    for grp in range(SSD_G):
        gm = ((lane_b // SSD_N) == grp).astype(F32)
        cb = _mm_nt(cm * gm, bm)
        for hh in range(heads_per_group):
            hd = grp * heads_per_group + hh
            col = acum[:, hd * SSD_HD:hd * SSD_HD + 1]
            rowv = acum_t[hd * SSD_HD:hd * SSD_HD + 1, :]
            dec = jnp.where(tri, jnp.exp(col - rowv), 0.0)
            hm = ((lane // SSD_HD) == hd).astype(F32)
            ydiag = ydiag + _mm(cb * dec, xdt * hm)
    if ns == 1:
        st = g["c_sh"][...]
        srow = lax.broadcasted_iota(jnp.int32, st.shape, 0) // SSD_N
        yoff = _mm(cm, st)
        slane = lax.broadcasted_iota(jnp.int32, st.shape, 1) // (SSD_HD * heads_per_group)
        upd = jnp.where(srow == slane, _mm_tn(bm, xw), 0.0)
        st = st * jnp.exp(acum[qt - 1:qt, :]) + upd
        g["c_sh"][...] = st

        @pl.when(c == nc - 1)
        def _final_ssd_state():
            own = jnp.concatenate([st[grp * SSD_N:(grp + 1) * SSD_N, grp * LANES:(grp + 1) * LANES]
                                   for grp in range(SSD_G)], axis=1)
            own = jnp.concatenate([own, jnp.zeros_like(own)], axis=0)
            g["sh_o"][...] = jnp.transpose(own)[:, 0:SSD_N].reshape(g["sh_o"].shape)
    else:
        sh_i, sh_o = g["sh_i"], g["sh_o"]
        per_seq = gw // SSD_HD * SSD_HD
        cols = [[], []]
        for s in range(0, ns, 2):
            two = jnp.concatenate([sh_i[s].reshape(per_seq, SSD_N), sh_i[s + 1].reshape(per_seq, SSD_N)], axis=1)
            two = jnp.transpose(two)
            for grp in range(SSD_G):
                cols[grp] += [two[0:SSD_N, grp * LANES:(grp + 1) * LANES],
                              two[SSD_N:2 * SSD_N, grp * LANES:(grp + 1) * LANES]]
        st = jnp.concatenate([jnp.concatenate(cols[0], axis=1), jnp.concatenate(cols[1], axis=1)], axis=0)
        srow = lax.broadcasted_iota(jnp.int32, st.shape, 0) // SSD_N
        seqm = seq_mask(LANES)
        lastm = jnp.where(t == r - 1, seqm, 0.0)
        yo, upd, dl = [], None, None
        for grp in range(SSD_G):
            gm = ((lane_b // SSD_N) == grp).astype(F32)
            z = _mm(cm * gm, st) * seqm
            zf = z[:, 0:LANES]
            for s in range(1, ns):
                zf = zf + z[:, s * LANES:(s + 1) * LANES]
            yo.append(zf)
            sl = slice(grp * LANES, (grp + 1) * LANES)
            u_g = _mm_tn(bm, jnp.tile(xw[:, sl], (1, ns)) * seqm)
            d_g = jnp.exp(jnp.sum(jnp.tile(acum[:, sl], (1, ns)) * lastm, axis=0, keepdims=True))
            upd = u_g if grp == 0 else jnp.where(srow == grp, u_g, upd)
            dl = d_g if grp == 0 else jnp.where(srow == grp, d_g, dl)
        yoff = jnp.concatenate(yo, axis=1)
        st = st * dl + upd
        for s in range(0, ns, 2):
            two = jnp.concatenate(
                [jnp.concatenate([st[0:SSD_N, q * LANES:(q + 1) * LANES], st[SSD_N:2 * SSD_N, q * LANES:(q + 1) * LANES]],
                                 axis=1) for q in (s, s + 1)], axis=0)
            two = jnp.transpose(two)
            sh_o[s] = two[:, 0:SSD_N].reshape(sh_o.shape[1:])
            sh_o[s + 1] = two[:, SSD_N:2 * SSD_N].reshape(sh_o.shape[1:])
    yc = ydiag + yoff * jnp.exp(acum) + vec[V_SSDD:V_SSDD + 1, 0:gw] * xs
    z = pcols(3 * gw, 4 * gw)
    yc = yc * (z * _sigmoid(z))
    ysave(2 * gw, 3 * gw, _rms(yc) * gain[:, 2 * gw:3 * gw])

    trig = g["trig"]
    cos, sin_a, sin_b = trig[0], trig[1], trig[2]

    def rope(x):
        return x * cos + pltpu.roll(x, LANES - RET_DK // 2, 1) * sin_a + pltpu.roll(x, RET_DK // 2, 1) * sin_b

    q = rope(pcols(6 * gw, 6 * gw + LANES))
    kk = rope(pcols(6 * gw + LANES, 7 * gw)) * (RET_DK ** -0.5)
    v = pcols(7 * gw, 8 * gw)
    dv = gw // RET_H
    intra = jnp.zeros((qt, gw), F32)
    for hd in range(RET_H):
        qm = ((lane_b // RET_DK) == hd).astype(F32)
        sc = _mm_nt(q * qm, kk) * g["dmask"][hd]
        vm = ((lane // dv) == hd).astype(F32)
        intra = intra + _mm(sc, v * vm)
    dec = g["dec"]
    qd, kd = q * dec[0], kk * dec[1]
    if ns == 1:
        rs = g["c_ret"][...]
        cross = _mm(qd, rs)
        rrow = lax.broadcasted_iota(jnp.int32, rs.shape, 0) // RET_DK
        rlane = lax.broadcasted_iota(jnp.int32, rs.shape, 1) // dv
        upd = jnp.where(rrow == rlane, _mm_tn(kd, v), 0.0)
    else:
        ret_i, ret_o = g["ret_i"], g["ret_o"]
        hk = RET_H * RET_DK
        rs = jnp.concatenate([ret_i[s].reshape(hk, dv) for s in range(ns)], axis=1)
        seqm = seq_mask(dv)
        lo = lane_b < dv
        pairs = []
        upd = jnp.zeros(rs.shape, F32)
        for pp in range(RET_H // 2):
            vp = v[:, pp * LANES:(pp + 1) * LANES]
            vsw = pltpu.roll(vp, dv, 1)
            tots = []
            for hh in range(2):
                qm = ((lane_b // RET_DK) == 2 * pp + hh).astype(F32)
                zc = _mm(qd * qm, rs) * seqm
                acc = zc[:, 0:LANES]
                for s2 in range(1, ns // 2):
                    acc = acc + zc[:, s2 * LANES:(s2 + 1) * LANES]
                tots.append(acc + pltpu.roll(acc, dv, 1))
                vh = jnp.where(lo, vp, vsw) if hh == 0 else jnp.where(lo, vsw, vp)
                upd = upd + _mm_tn(kd * qm, jnp.tile(vh, (1, ns // 2)) * seqm)
            pairs.append(jnp.where(lo, tots[0], tots[1]))
        cross = jnp.concatenate(pairs, axis=1)
    rs = rs * dec[2][:, 0:1] + upd
    if ns == 1:
        g["c_ret"][...] = rs

        @pl.when(c == nc - 1)
        def _final_ret_state():
            own = functools.reduce(lambda s, x: s + x, [rs[:, hd * dv:(hd + 1) * dv] for hd in range(RET_H)])
            g["ret_o"][...] = own.reshape(g["ret_o"].shape)
    else:
        for s in range(ns):
            ret_o[s] = rs[:, s * dv:(s + 1) * dv].reshape(ret_o.shape[1:])
    o = intra + cross
    o2 = o * o
    ms = jnp.zeros((qt, gw), F32)
    for hd in range(RET_H):
        vm = ((lane // dv) == hd).astype(F32)
        ms = ms + vm * jnp.sum(o2 * vm, axis=-1, keepdims=True)
    o = o * lax.rsqrt(ms * (1.0 / dv) + EPS) * gain[:, 3 * gw:4 * gw]
    gg = pcols(8 * gw, 9 * gw)
    ysave(3 * gw, 4 * gw, gg * _sigmoid(gg) * o)
    if ns == 1:
        y_all = jnp.concatenate(y_parts, axis=1)
        g["y"][...] = jnp.dot(g["perm"][1], y_all, preferred_element_type=F32).astype(g["y"].dtype)


def _mix_consts(qt, ns, r, nc, pos0):
    assert qt == LANES, "decay tables are stacked as (3, 128, 128)"
    idx = np.arange(qt)
    nk = qt // SUBLANES
    if ns == 1:
        seq, tt = np.zeros_like(idx), (idx % SUBLANES) * nk + idx // SUBLANES
    else:
        seq, tt = idx % ns, idx // ns
    causal = (seq[:, None] == seq[None, :]) & (tt[:, None] >= tt[None, :])
    tri = causal.astype(np.float32)
    lg = np.log1p(-np.exp2(-5.0 - np.arange(RET_H, dtype=np.float64)))
    rel = (tt[:, None] - tt[None, :]).astype(np.float64)
    dmask = np.where(causal[None], np.exp(np.maximum(rel, 0.0)[None] * lg[:, None, None]), 0.0).astype(np.float32)
    lane_h = np.arange(LANES) // RET_DK
    qdec = np.exp((tt[:, None] + 1.0) * lg[lane_h][None, :])
    kdec = np.exp((r - 1.0 - tt[:, None]) * lg[lane_h][None, :])
    rdec = np.broadcast_to(np.exp(r * lg[lane_h])[:, None], (LANES, LANES))
    dec = np.stack([qdec, kdec, rdec]).astype(np.float32)
    half = RET_DK // 2
    pos = pos0 + jnp.asarray((np.arange(nc)[:, None] * qt + tt[None, :]).reshape(-1))
    inv = ROPE_BASE ** (-jnp.arange(half, dtype=F32) / half)
    ang = pos.astype(F32)[:, None] * inv
    reps = LANES // half
    cos, sin = jnp.tile(jnp.cos(ang), (1, reps)), jnp.tile(jnp.sin(ang), (1, reps))
    first = (np.arange(LANES) % RET_DK) < half
    trig = jnp.stack([cos, jnp.where(first, -sin, 0.0), jnp.where(first, 0.0, sin)])
    to_blocked = np.zeros((qt, qt), np.float32)
    to_blocked[idx, (idx % SUBLANES) * nk + idx // SUBLANES] = 1.0
    perm = jnp.asarray(np.stack([to_blocked, to_blocked.T])).astype(MXU_DTYPE)
    return dict(trig=trig, tri=jnp.asarray(tri), dmask=jnp.asarray(dmask), dec=jnp.asarray(dec), perm=perm)


def _state_layout(depth, bsz, ns, gw, sn):
    hk, dv = RET_H * RET_DK, gw // RET_H
    if ns == 1:
        def per_seq(rows, w):
            return (depth, bsz, rows, w), (None, None, rows, w), lambda b, c, l: (l[0], b, 0, 0)
        def matrix(h, rows, w):
            return (depth, bsz, h, rows, w), (None, None, h, rows, w), lambda b, c, l: (l[0], b, 0, 0, 0)
        return dict(lconv=per_seq(CONV_K - 1, gw), lh=per_seq(1, gw), s5r=per_seq(1, sn), s5i=per_seq(1, sn),
                    sconv=per_seq(CONV_K - 1, 2 * gw), sh=matrix(gw // SSD_HD, SSD_HD, SSD_N),
                    ret=matrix(RET_H, RET_DK, dv))

    def flat(per, w):
        return (depth, bsz * per, w), (None, ns * per, w), lambda b, c, l: (l[0], b, 0)

    def conv(w):
        return (depth, CONV_K - 1, bsz, w), (None, CONV_K - 1, ns, w), lambda b, c, l: (l[0], 0, b, 0)
    def matrix(h, rows, w):
        return (depth, bsz, h, rows, w), (None, ns, h, rows, w), lambda b, c, l: (l[0], b, 0, 0, 0)
    return dict(lconv=conv(gw), lh=flat(1, gw), s5r=flat(1, sn), s5i=flat(1, sn), sconv=conv(2 * gw),
                sh=matrix(gw // SSD_HD, SSD_HD, SSD_N), ret=matrix(RET_H, RET_DK, dv))


def _mixer(l, proj, consts, mw, state_in, state_out, *, nb, nc, qt, ns, r, d):
    dp = proj.shape[-1]
    gw = d // 4
    sn = mw["pw"].shape[-1]
    depth = mw["pw"].shape[0]
    layout = _state_layout(depth, nb * ns, ns, gw, sn)
    assert (state_in is None) == (ns == 1)

    def layer_spec(a):
        return pl.BlockSpec((None,) + a.shape[1:], lambda b, c, l: (l[0],) + (0,) * (a.ndim - 1))

    def const_spec(a):
        return pl.BlockSpec(a.shape, lambda b, c, l: (0,) * a.ndim)

    def state_spec(nm):
        _, block, index = layout[nm]
        return pl.BlockSpec(block, index)

    if ns == 1:
        io_spec = lambda w: pl.BlockSpec((qt, w), lambda b, c, l: (b * nc + c, 0))
        y_shape = (nb * nc * qt, d)
        cnames = ("trig", "tri", "dmask", "dec", "perm")
    else:
        proj = proj.reshape(r, nb * ns, dp)
        io_spec = lambda w: pl.BlockSpec((r, ns, w), lambda b, c, l: (0, b, 0))
        y_shape = (r, nb * ns, d)
        cnames = ("trig", "tri", "dmask", "dec")
    names = ("proj",) + cnames + MIX_WEIGHTS
    operands = [proj] + [consts[k] for k in cnames] + [mw[k] for k in MIX_WEIGHTS]
    in_specs = [io_spec(dp), pl.BlockSpec((3, qt, LANES), lambda b, c, l: (0, c, 0))] + \
               [const_spec(consts[k]) for k in cnames[1:]] + [layer_spec(mw[k]) for k in MIX_WEIGHTS]
    if state_in is not None:
        names += tuple(nm + "_i" for nm in STATE_NAMES)
        operands += [state_in[nm] for nm in STATE_NAMES]
        in_specs += [state_spec(nm) for nm in STATE_NAMES]
    first_alias = 1 + len(operands)
    names += tuple(nm + "_alias" for nm in STATE_NAMES)
    operands += [state_out[nm] for nm in STATE_NAMES]
    in_specs += [pl.BlockSpec(memory_space=pl.ANY) for _ in STATE_NAMES]
    names += ("y",) + tuple(nm + "_o" for nm in STATE_NAMES)
    out_specs = [io_spec(d)] + [state_spec(nm) for nm in STATE_NAMES]
    out_shape = [jax.ShapeDtypeStruct(y_shape, MXU_DTYPE)] + \
                [jax.ShapeDtypeStruct(layout[nm][0], F32) for nm in STATE_NAMES]
    scratch = []
    if ns == 1:
        tail = (CONV_K - 1) * SUBLANES
        scratch_shapes = dict(cb_a=(tail, gw), cb_c=(tail, 2 * gw), c_lh=(1, gw), c_s5r=(1, sn),
                              c_s5i=(1, sn), c_sh=(SSD_G * SSD_N, gw), c_ret=(RET_H * RET_DK, gw))
        names += tuple(scratch_shapes)
        scratch = [pltpu.VMEM(shape, F32) for shape in scratch_shapes.values()]
    grid_spec = pltpu.PrefetchScalarGridSpec(num_scalar_prefetch=1, grid=(nb, nc),
                                             in_specs=in_specs, out_specs=out_specs, scratch_shapes=scratch)
    outs = pl.pallas_call(
        functools.partial(_mix_body, names=names, qt=qt, ns=ns, r=r, nc=nc),
        grid_spec=grid_spec,
        out_shape=out_shape,
        input_output_aliases={first_alias + k: 1 + k for k in range(len(STATE_NAMES))},
        compiler_params=pltpu.CompilerParams(dimension_semantics=("arbitrary", "arbitrary"),
                                             vmem_limit_bytes=VMEM_LIMIT),
        name="mixer",
    )(l, *operands)
    return outs[0].reshape(-1, d), dict(zip(STATE_NAMES, outs[1:]))


def _post_body(l_ref, y_ref, x_ref, g1_ref, sc_ref, sh_ref, g2_ref, lt_ref, wo_ref, wr_ref, br_ref,
               wg_ref, wu_ref, wd_ref, *rest):
    o_ref, xs_ref, cs_ref, os_ref = rest[-4:]
    del l_ref
    tm, d = x_ref.shape
    tms = xs_ref.shape[0]
    x = x_ref[...] + _token_rows(g1_ref[...], tm) * jnp.dot(y_ref[...], wo_ref[...], preferred_element_type=F32)
    h = _rms(x) * (1.0 + _token_rows(sc_ref[...], tm)) + _token_rows(sh_ref[...], tm)
    hb = h.astype(MXU_DTYPE)
    h_lo = (h - hb.astype(F32)).astype(MXU_DTYPE)
    wr = wr_ref[...]
    l12 = jnp.dot(hb, wr, preferred_element_type=F32)
    logits = (l12[:, 0:LANES] + l12[:, LANES:2 * LANES]
              + jnp.dot(h_lo, wr[:, 0:LANES], preferred_element_type=F32) + br_ref[...])
    logits_t = jnp.transpose(logits)
    col = [logits_t[k:k + 1, :] for k in range(N_GROUPS + N_EXPERTS)]

    def first_max(vals, allowed=None):
        neg = jnp.full_like(vals[0], -jnp.inf)
        cand = vals if allowed is None else [jnp.where(al > 0.0, v, neg) for v, al in zip(vals, allowed)]
        m = functools.reduce(jnp.maximum, cand)
        rem = jnp.ones_like(vals[0])
        hot = []
        for v in cand:
            f = jnp.where(v >= m, rem, 0.0)
            rem = rem - f
            hot.append(f)
        return hot, m

    grp, gmax = first_max(col[:N_GROUPS])
    gate = 1.0 / functools.reduce(lambda s, v: s + v, [jnp.exp(v - gmax) for v in col[:N_GROUPS]])
    le = [functools.reduce(lambda s, v: s + v,
                           [grp[gi] * col[N_GROUPS + gi * N_PER_GROUP + j] for gi in range(N_GROUPS)])
          for j in range(N_PER_GROUP)]
    top1, m1 = first_max(le)
    top2, m2 = first_max(le, [1.0 - f for f in top1])
    e2 = jnp.exp(m2 - m1)
    w1 = 1.0 / (1.0 + e2)
    w2 = e2 * w1
    wgrp = [(top1[j] * w1 + top2[j] * w2) * gate for j in range(N_PER_GROUP)]

    sub8 = lax.broadcasted_iota(jnp.int32, (SUBLANES, 1), 0)
    ghot_t = functools.reduce(lambda s, v: s + v, [jnp.where(sub8 == gi, grp[gi], 0.0) for gi in range(N_GROUPS)])
    before = _mm_nt(ghot_t, lt_ref[...])
    count = jnp.sum(ghot_t, axis=1, keepdims=True)
    cnt = [count[gi:gi + 1, :] for gi in range(N_GROUPS)]
    base = [jnp.zeros((1, 1), F32)]
    for gi in range(1, N_GROUPS):
        base.append(jnp.floor((base[-1] + cnt[gi - 1] + (PACK - 1.0)) * (1.0 / PACK)) * PACK)
    pos_row = functools.reduce(lambda s, v: s + v,
                               [grp[gi] * (base[gi] + before[gi:gi + 1, :]) for gi in range(N_GROUPS)])
    pos = jnp.transpose(jnp.broadcast_to(pos_row, (LANES, tm)))[:, 0:1]
    slot = lax.broadcasted_iota(jnp.int32, (tms, tm), 0).astype(F32)
    gather = jnp.where(slot == pos_row, 1.0, 0.0).astype(MXU_DTYPE)
    slot_l = lax.broadcasted_iota(jnp.int32, (tm, tms), 1).astype(F32)
    scatter = jnp.where(slot_l == pos, 1.0, 0.0).astype(MXU_DTYPE)

    xs_ref[...] = jnp.dot(gather, hb, preferred_element_type=F32).astype(xs_ref.dtype)
    sub_e = lax.broadcasted_iota(jnp.int32, (cs_ref.shape[1], 1), 0)
    comb_t = functools.reduce(lambda s, v: s + v,
                              [jnp.where(sub_e == e, grp[e // N_PER_GROUP] * wgrp[e % N_PER_GROUP], 0.0)
                               for e in range(N_EXPERTS)])
    c1, c2, c3 = _split3(comb_t)
    nt = lambda p: lax.dot_general(gather, p, (((1,), (1,)), ((), ())), preferred_element_type=F32)
    cs_ref[...] = nt(c1) + nt(c2) + nt(c3)
    os_ref[...] = jnp.zeros(os_ref.shape, os_ref.dtype)
    for gi in range(N_GROUPS):
        start = base[gi][0, 0].astype(jnp.int32)
        nblk = jnp.floor((cnt[gi][0, 0] + (EXPERT_ROWS - 1.0)) * (1.0 / EXPERT_ROWS)).astype(jnp.int32)

        def block(k, carry, gi=gi, start=start):
            rows = pl.ds(pl.multiple_of(start + k * EXPERT_ROWS, PACK), EXPERT_ROWS)
            xb = xs_ref[rows, :]
            cb = cs_ref[rows, :]
            acc = jnp.zeros((EXPERT_ROWS, d), F32)
            for j in range(N_PER_GROUP):
                e = gi * N_PER_GROUP + j
                gt = jnp.dot(xb, wg_ref[e], preferred_element_type=F32)
                he = gt * _sigmoid(gt) * jnp.dot(xb, wu_ref[e], preferred_element_type=F32)
                acc = acc + jnp.dot((he * cb[:, e:e + 1]).astype(MXU_DTYPE), wd_ref[e], preferred_element_type=F32)
            os_ref[rows, :] += acc
            return carry

        lax.fori_loop(0, nblk, block, 0)
    moe = _select_rows(scatter, os_ref[...], pieces=2)
    out = x + _token_rows(g2_ref[...], tm) * moe
    if len(rest) == 5:
        out = _rms(out) * rest[0][...]
    o_ref[...] = out


def _post(l, y, x, mod, pw, tm, rows_per_seq, final_w=None):
    t, d = x.shape
    final = [] if final_w is None else [final_w.reshape(1, d)]
    tms = tm + EXPERT_ROWS + PACK * N_GROUPS
    ltri = jnp.asarray(np.tril(np.ones((tm, tm), np.float32), -1)).astype(MXU_DTYPE)

    def layer_spec(a):
        return pl.BlockSpec((None,) + a.shape[1:], lambda i, l: (l[0],) + (0,) * (a.ndim - 1),
                            pipeline_mode=pl.Buffered(1))

    grid_spec = pltpu.PrefetchScalarGridSpec(
        num_scalar_prefetch=1,
        grid=(t // tm,),
        in_specs=[
            pl.BlockSpec((tm, d), lambda i, l: (i, 0)),
            pl.BlockSpec((tm, d), lambda i, l: (i, 0)),
            _mod_spec(mod, 2, tm, rows_per_seq),
            _mod_spec(mod, 4, tm, rows_per_seq),
            _mod_spec(mod, 3, tm, rows_per_seq),
            _mod_spec(mod, 5, tm, rows_per_seq),
            pl.BlockSpec((tm, tm), lambda i, l: (0, 0)),
        ] + [layer_spec(pw[k]) for k in ("w_out", "w_route", "b_route", "w_gate", "w_up", "w_down")]
          + [pl.BlockSpec((1, d), lambda i, l: (0, 0)) for _ in final],
        out_specs=pl.BlockSpec((tm, d), lambda i, l: (i, 0)),
        scratch_shapes=[pltpu.VMEM((tms, d), MXU_DTYPE), pltpu.VMEM((tms, 2 * N_EXPERTS), F32),
                        pltpu.VMEM((tms, d), F32)],
    )
    return pl.pallas_call(
        _post_body,
        grid_spec=grid_spec,
        out_shape=jax.ShapeDtypeStruct((t, d), F32),
        compiler_params=pltpu.CompilerParams(dimension_semantics=("arbitrary",), vmem_limit_bytes=VMEM_LIMIT),
        name="post",
    )(l, y, x, mod, mod, mod, mod, ltri,
      *[pw[k] for k in ("w_out", "w_route", "b_route", "w_gate", "w_up", "w_down")], *final)


def _block_diag(w):
    depth, nblk, bi, bj = w.shape
    eye = jnp.eye(nblk, dtype=w.dtype)
    return (w[:, :, :, None, :] * eye[None, :, None, :, None]).reshape(depth, nblk * bi, nblk * bj)


def _pad_lanes(a, width):
    return jnp.pad(a, [(0, 0)] * (a.ndim - 1) + [(0, width - a.shape[-1])])


def _prep_weights(w_in, lru_conv_w, lru_conv_b, lru_wa, lru_ba, lru_wx, lru_bx, lru_lambda,
                  s5_a_re, s5_a_im, s5_b_re, s5_b_im, s5_c_re, s5_c_im, s5_d, s5_log_dt, s5_w_glu,
                  ssd_conv_w, ssd_conv_b, ssd_dt_bias, ssd_a_log, ssd_d, mix_norm, qt):
    depth, d, _ = w_in.shape
    gw = d // 4
    n_ssd_h = gw // SSD_HD
    xbc_end = 4 * gw + gw + 2 * SSD_G * SSD_N
    dt_cols = jnp.repeat(w_in[..., xbc_end:xbc_end + n_ssd_h], SSD_HD, axis=-1)
    w_in_p = jnp.concatenate([w_in[..., :xbc_end], w_in[..., xbc_end + n_ssd_h:], dt_cols], axis=-1)

    def row(a):
        return _pad_lanes(a.reshape(depth, 1, -1), d)

    vec = jnp.concatenate([
        row(lru_conv_b), row(jnp.concatenate([lru_ba, lru_bx], -1)), row(lru_lambda), row(s5_d),
        row(ssd_conv_b), row(jnp.repeat(ssd_dt_bias, SSD_HD, -1)), row(jnp.repeat(ssd_a_log, SSD_HD, -1)),
        row(jnp.repeat(ssd_d, SSD_HD, -1)), row(mix_norm),
        _pad_lanes(lru_conv_w, d), _pad_lanes(ssd_conv_w, d),
        jnp.zeros((depth, V_ROWS - V_SCW - CONV_K, d), F32)], axis=1)

    wgate = jnp.concatenate([_block_diag(lru_wa), _block_diag(lru_wx)], axis=-1)

    dt = jnp.exp(s5_log_dt)[..., None]
    lr, li = s5_a_re, s5_a_im
    mag = jnp.exp(lr * dt)
    ab_re, ab_im = mag * jnp.cos(li * dt), mag * jnp.sin(li * dt)
    den = lr * lr + li * li
    q_re = ((ab_re - 1.0) * lr + ab_im * li) / den
    q_im = (ab_im * lr - (ab_re - 1.0) * li) / den
    bb_re = q_re[..., None] * s5_b_re - q_im[..., None] * s5_b_im
    bb_im = q_re[..., None] * s5_b_im + q_im[..., None] * s5_b_re
    bbm = jnp.concatenate([_block_diag(jnp.swapaxes(bb_re, -1, -2)), _block_diag(jnp.swapaxes(bb_im, -1, -2))], -1)
    cre = _block_diag(jnp.swapaxes(s5_c_re, -1, -2))
    cim = _block_diag(jnp.swapaxes(s5_c_im, -1, -2))
    a_re, a_im = ab_re.reshape(depth, -1), ab_im.reshape(depth, -1)
    pr, pi = a_re, a_im
    rows = []
    for _ in range(int(math.log2(qt))):
        rows += [pr, pi]
        pr, pi = pr * pr - pi * pi, 2.0 * pr * pi
    pw = jnp.stack(rows, axis=1)
    pw = jnp.pad(pw, ((0, 0), (0, (-pw.shape[1]) % SUBLANES), (0, 0)))
    pr, pi = a_re, a_im
    k_re, k_im = [], []
    for _ in range(qt // SUBLANES):
        k_re.append(pr)
        k_im.append(pi)
        pr, pi = pr * a_re - pi * a_im, pr * a_im + pi * a_re
    pk = jnp.stack(k_re + k_im, axis=1)

    mw = dict(vec=vec, wgate=wgate.astype(MXU_DTYPE), bbm=bbm.astype(MXU_DTYPE), cre=cre.astype(MXU_DTYPE),
              cim=cim.astype(MXU_DTYPE), wglu=s5_w_glu.astype(MXU_DTYPE), pw=pw, pk=pk)
    return w_in_p.astype(MXU_DTYPE), mw


def _states_to_layout(states, layout, ns):
    out = {}
    for nm, s in zip(STATE_NAMES, states):
        if nm in ("lconv", "sconv"):
            s = s.transpose(0, 2, 1, 3)
        out[nm] = s.reshape(layout[nm][0])
    return out


def _states_from_layout(arrs, ns, ref_shapes):
    out = []
    for nm, shape in zip(STATE_NAMES, ref_shapes):
        a = arrs[nm]
        if ns > 1 and nm in ("lconv", "sconv"):
            a = a.transpose(0, 2, 1, 3)
        out.append(a.reshape(shape))
    return tuple(out)


def _trunk(x, mod, states, ref_shapes, pos0, w_in_p, mw, pw, final_norm, *, qt, ns, tm):
    bsz, seq_len, d = x.shape
    depth = w_in_p.shape[0]
    gw = d // 4
    r = qt // ns
    nc = seq_len // r if ns == 1 else 1
    assert (ns == 1 and seq_len % qt == 0 and states is None) or (ns > 1 and r == seq_len and bsz % ns == 0)
    assert r >= CONV_K and (r & (r - 1)) == 0
    nb = bsz // ns
    tok = bsz * seq_len
    consts = _mix_consts(qt, ns, r, nc, pos0)
    layout = _state_layout(depth, bsz, ns, gw, mw["pw"].shape[-1])
    state_in = None if states is None else _states_to_layout(states, layout, ns)
    state_out = {nm: jnp.zeros(layout[nm][0], F32) for nm in STATE_NAMES}
    x2 = x.reshape(tok, d) if ns == 1 else x.transpose(1, 0, 2).reshape(tok, d)
    blocked_chunk = qt if ns == 1 else None

    def layer(carry, l, final_w=None):
        xc, st = carry
        lv = jnp.reshape(l, (1,)).astype(jnp.int32)
        proj = _inproj(lv, xc, mod, w_in_p, tm, seq_len, blocked_chunk)
        y, st = _mixer(lv, proj, consts, mw, state_in, st, nb=nb, nc=nc, qt=qt, ns=ns, r=r, d=d)
        xn = _post(lv, y, xc, mod, pw, tm, seq_len, final_w)
        return (xn, st), None

    assert depth >= 2
    carry, _ = layer((x2, state_out), jnp.int32(0))
    carry, _ = lax.scan(layer, carry, jnp.arange(1, depth - 1))
    (y, st), _ = layer(carry, jnp.int32(depth - 1), final_norm)
    y = y.reshape(bsz, seq_len, d) if ns == 1 else y.reshape(seq_len, bsz, d).transpose(1, 0, 2)
    return y, _states_from_layout(st, ns, ref_shapes)


def kernel(x_prompt, x_sample, state_lru_conv, state_lru_h, state_s5_re, state_s5_im, state_ssd_conv, state_ssd_h, state_ret, c_prompt, c_sample, w_ada, b_ada, w_in, lru_conv_w, lru_conv_b, lru_wa, lru_ba, lru_wx, lru_bx, lru_lambda, s5_a_re, s5_a_im, s5_b_re, s5_b_im, s5_c_re, s5_c_im, s5_d, s5_log_dt, s5_w_glu, ssd_conv_w, ssd_conv_b, ssd_dt_bias, ssd_a_log, ssd_d, mix_norm, w_out, w_route_group, b_route_group, w_route_exp, b_route_exp, w_exp_gate, w_exp_up, w_exp_down, final_norm):
    bp, lp, d = x_prompt.shape
    bs, ls, _ = x_sample.shape
    depth = w_in.shape[0]
    qt = LANES
    ns_s = qt // ls

    mod_p, mod_s = _modulation(jnp.concatenate([c_prompt, c_sample], axis=0), bp, w_ada, b_ada)
    mod_p = mod_p.reshape(depth, N_MOD, bp, 1, d)

    w_in_p, mw = _prep_weights(w_in, lru_conv_w, lru_conv_b, lru_wa, lru_ba, lru_wx, lru_bx, lru_lambda,
                               s5_a_re, s5_a_im, s5_b_re, s5_b_im, s5_c_re, s5_c_im, s5_d, s5_log_dt, s5_w_glu,
                               ssd_conv_w, ssd_conv_b, ssd_dt_bias, ssd_a_log, ssd_d, mix_norm, qt)
    w_route = _pad_lanes(jnp.concatenate([w_route_group, w_route_exp.reshape(depth, d, N_EXPERTS)], -1), LANES)
    b_route = _pad_lanes(jnp.concatenate([b_route_group, b_route_exp.reshape(depth, N_EXPERTS)], -1), LANES)
    w_route_hi = w_route.astype(MXU_DTYPE)
    w_route_lo = (w_route - w_route_hi.astype(F32)).astype(MXU_DTYPE)
    w_route = jnp.concatenate([w_route_hi, w_route_lo], axis=-1)
    pw = dict(w_out=w_out.astype(MXU_DTYPE), w_route=w_route, b_route=b_route.reshape(depth, 1, LANES),
              w_gate=w_exp_gate.astype(MXU_DTYPE), w_up=w_exp_up.astype(MXU_DTYPE),
              w_down=w_exp_down.astype(MXU_DTYPE))
    w_in_p, mw, pw = lax.optimization_barrier((w_in_p, mw, pw))

    states_s = (state_lru_conv, state_lru_h, state_s5_re, state_s5_im, state_ssd_conv, state_ssd_h, state_ret)
    shapes_s = [s.shape for s in states_s]
    shapes_p = [(depth, bp) + s[2:] for s in shapes_s]
    y_p, new_p = _trunk(x_prompt, mod_p, None, shapes_p, 0, w_in_p, mw, pw, final_norm,
                        qt=qt, ns=1, tm=min(512, lp))
    y_s, new_s = _trunk(x_sample, mod_s, states_s, shapes_s, PAST_LEN, w_in_p, mw, pw, final_norm,
                        qt=qt, ns=ns_s, tm=bs * ls)
    out = [y_p, y_s]
    for a, b in zip(new_p, new_s):
        out += [a, b]
    return tuple(out)
```

```python
import functools
import math

import numpy as np
import jax
import jax.numpy as jnp
from jax import lax
from jax.experimental import pallas as pl
from jax.experimental.pallas import tpu as pltpu

F32 = jnp.float32
MXU_DTYPE = jnp.bfloat16
HIGHEST = lax.Precision.HIGHEST

EPS = 1e-6
CONV_K = 4
LRU_C = 8.0
SSD_HD = 64
SSD_G = 2
SSD_N = 64
RET_H = 4
RET_DK = 32
ROPE_BASE = 10000.0
N_GROUPS = 4
N_PER_GROUP = 4
N_EXPERTS = N_GROUPS * N_PER_GROUP
N_MOD = 6
PAST_LEN = 16384

LANES = 128
SUBLANES = 8
PACK = 16
VMEM_LIMIT = 56 * 1024 * 1024
EXPERT_ROWS = 144


def _mm(a, b):
    return jnp.dot(a.astype(MXU_DTYPE), b.astype(MXU_DTYPE), preferred_element_type=F32)


def _mm_nt(a, b):
    return lax.dot_general(a.astype(MXU_DTYPE), b.astype(MXU_DTYPE), (((1,), (1,)), ((), ())),
                           preferred_element_type=F32)


def _mm_tn(a, b):
    return lax.dot_general(a.astype(MXU_DTYPE), b.astype(MXU_DTYPE), (((0,), (0,)), ((), ())),
                           preferred_element_type=F32)


def _split3(x):
    x1 = x.astype(MXU_DTYPE)
    r1 = x - x1.astype(F32)
    x2 = r1.astype(MXU_DTYPE)
    x3 = (r1 - x2.astype(F32)).astype(MXU_DTYPE)
    return x1, x2, x3


def _select_rows(onehot, x, pieces=3):
    parts = _split3(x)[:pieces]
    out = jnp.dot(onehot, parts[0], preferred_element_type=F32)
    for p in parts[1:]:
        out = out + jnp.dot(onehot, p, preferred_element_type=F32)
    return out


def _sigmoid(x):
    return 0.5 * jnp.tanh(0.5 * x) + 0.5


def _rms(x):
    return x * lax.rsqrt(jnp.mean(x * x, axis=-1, keepdims=True) + EPS)


def _token_rows(v, tm):
    n = v.shape[0]
    return v if n in (1, tm) else jnp.tile(v, (tm // n, 1))


def _mod_body(c_ref, w_ref, b_ref, first_ref, rest_ref):
    c = c_ref[...]
    m = _mm(c * _sigmoid(c), w_ref[...]) + b_ref[...]
    n_first = first_ref.shape[0]
    first_ref[...] = m[0:n_first]
    rest_ref[...] = m[n_first:]


def _modulation(c_all, n_first, w_ada, b_ada):
    depth, d, _ = w_ada.shape
    nb = c_all.shape[0]
    return pl.pallas_call(
        _mod_body,
        grid=(depth, N_MOD),
        in_specs=[
            pl.BlockSpec((nb, d), lambda l, k: (0, 0)),
            pl.BlockSpec((None, d, d), lambda l, k: (l, 0, k)),
            pl.BlockSpec((None, None, 1, d), lambda l, k: (l, k, 0, 0)),
        ],
        out_specs=[pl.BlockSpec((None, None, n_first, d), lambda l, k: (l, k, 0, 0)),
                   pl.BlockSpec((None, None, nb - n_first, d), lambda l, k: (l, k, 0, 0))],
        out_shape=[jax.ShapeDtypeStruct((depth, N_MOD, n_first, d), F32),
                   jax.ShapeDtypeStruct((depth, N_MOD, nb - n_first, d), F32)],
        compiler_params=pltpu.CompilerParams(dimension_semantics=("arbitrary", "arbitrary"),
                                             vmem_limit_bytes=VMEM_LIMIT),
        name="modulation",
    )(c_all, w_ada, b_ada.reshape(depth, N_MOD, 1, d))


def _mod_spec(mod, k, tm, rows_per_seq):
    if mod.ndim == 5:
        tiles_per_seq = rows_per_seq // tm
        return pl.BlockSpec((None, None, None, 1, mod.shape[-1]),
                            lambda i, l: (l[0], k, i // tiles_per_seq, 0, 0))
    return pl.BlockSpec((None, None) + mod.shape[2:], lambda i, l: (l[0], k, 0, 0))


def _inproj_body(l_ref, x_ref, sc_ref, sh_ref, w_ref, o_ref, *, blocked_chunk):
    tm = x_ref.shape[0]
    h = _rms(x_ref[...]) * (1.0 + _token_rows(sc_ref[...], tm)) + _token_rows(sh_ref[...], tm)
    hb = h.astype(MXU_DTYPE)
    if blocked_chunk is not None:
        qt, nk = blocked_chunk, blocked_chunk // SUBLANES
        r_i = lax.broadcasted_iota(jnp.int32, (qt, qt), 0)
        c_i = lax.broadcasted_iota(jnp.int32, (qt, qt), 1)
        to_blocked = jnp.where(c_i == (r_i % SUBLANES) * nk + r_i // SUBLANES, 1.0, 0.0).astype(MXU_DTYPE)
        hb = jnp.concatenate([jnp.dot(to_blocked, hb[j * qt:(j + 1) * qt], preferred_element_type=F32)
                              for j in range(tm // qt)], axis=0).astype(MXU_DTYPE)
    o_ref[...] = jnp.dot(hb, w_ref[...], preferred_element_type=F32)


def _inproj(l, x, mod, w_in, tm, rows_per_seq, blocked_chunk=None):
    t, d = x.shape
    dp = w_in.shape[-1]
    assert blocked_chunk is None or (mod.ndim == 5 and tm % blocked_chunk == 0)
    grid_spec = pltpu.PrefetchScalarGridSpec(
        num_scalar_prefetch=1,
        grid=(t // tm,),
        in_specs=[
            pl.BlockSpec((tm, d), lambda i, l: (i, 0)),
            _mod_spec(mod, 1, tm, rows_per_seq),
            _mod_spec(mod, 0, tm, rows_per_seq),
            pl.BlockSpec((None, d, dp), lambda i, l: (l[0], 0, 0)),
        ],
        out_specs=pl.BlockSpec((tm, dp), lambda i, l: (i, 0)),
    )
    return pl.pallas_call(
        functools.partial(_inproj_body, blocked_chunk=blocked_chunk),
        grid_spec=grid_spec,
        out_shape=jax.ShapeDtypeStruct((t, dp), F32),
        compiler_params=pltpu.CompilerParams(dimension_semantics=("arbitrary",), vmem_limit_bytes=VMEM_LIMIT),
        name="inproj",
    )(l, x, mod, mod, w_in)


V_LCB, V_BGATE, V_LAM, V_S5D, V_SCB, V_DTB, V_ALOG, V_SSDD, V_GAIN, V_LCW, V_SCW = 0, 1, 2, 3, 4, 5, 6, 7, 8, 9, 13
V_ROWS = 24
STATE_NAMES = ("lconv", "lh", "s5r", "s5i", "sconv", "sh", "ret")
MIX_WEIGHTS = ("vec", "wgate", "bbm", "cre", "cim", "wglu", "pw", "pk")


def _mix_body(l_ref, *refs, names, qt, ns, r, nc):
    del l_ref
    g = dict(zip(names, refs))
    gw = g["lh_o"].shape[-1]
    sn = g["s5r_o"].shape[-1]
    nlev = int(math.log2(r))
    nk = qt // SUBLANES
    c = pl.program_id(1)

    if ns == 1:
        @pl.when(c == 0)
        def _zero_carries():
            for nm in ("c_lh", "c_s5r", "c_s5i", "c_sh", "c_ret"):
                g[nm][...] = jnp.zeros(g[nm].shape, F32)
            for nm in ("cb_a", "cb_c"):
                g[nm][...] = jnp.zeros(g[nm].shape, F32)

    proj = g["proj"]
    vec = g["vec"]
    pw = g["pw"]

    def pcols(a, b):
        v = proj[..., a:b]
        return v if ns == 1 else v.reshape(qt, b - a)

    y_parts = []

    def ysave(a, b, val):
        val = val.astype(g["y"].dtype)
        if ns == 1:
            y_parts.append(val)
        else:
            g["y"][..., a:b] = val.reshape(r, ns, b - a)

    row = lax.broadcasted_iota(jnp.int32, (qt, 1), 0)
    t = (row % SUBLANES) * nk + row // SUBLANES if ns == 1 else row // ns
    sub = lax.broadcasted_iota(jnp.int32, (SUBLANES, 1), 0)

    def down(x, d, fill=0.0):
        return jnp.where(t >= d, pltpu.roll(x, d * ns, 0), fill)

    def up(x, d):
        return jnp.where(t + d < r, pltpu.roll(x, qt - d * ns, 0), 0.0)

    def sdown(x, d, fill=0.0):
        return jnp.where(sub >= d, pltpu.roll(x, d, 0), fill)

    def slabs(x):
        return [x[SUBLANES * k:SUBLANES * (k + 1)] for k in range(nk)]

    def first_rows(carry, init):
        if ns == 1:
            return jnp.where(row == 0, g[carry][...], 0.0)
        h0 = g[init][...]
        return jnp.concatenate([h0, jnp.zeros((qt - ns, h0.shape[1]), F32)], axis=0)

    def conv(xraw, nm, w0, b):
        width = xraw.shape[-1]
        acc = vec[b:b + 1, 0:width] + vec[w0 + 3:w0 + 4, 0:width] * xraw
        if ns == 1:
            cb = g["cb_" + nm]
            xs_k = slabs(xraw)
            wrap = [jnp.where(sub >= 1, pltpu.roll(xs_k[nk - j], 1, 0),
                              pltpu.roll(cb[(CONV_K - 1 - j) * SUBLANES:(CONV_K - j) * SUBLANES, :], 1, 0))
                    for j in range(1, CONV_K)]
            for m in range(1, CONV_K):
                sh = jnp.concatenate([wrap[m - k - 1] for k in range(m)] + xs_k[0:nk - m], axis=0)
                acc = acc + vec[w0 + 3 - m:w0 + 4 - m, 0:width] * sh
            cb[...] = xraw[qt - (CONV_K - 1) * SUBLANES:, :]
            g[{"a": "lconv_o", "c": "sconv_o"}[nm]][...] = jnp.concatenate(
                [xs_k[nk - j][SUBLANES - 1:SUBLANES, :] for j in range(CONV_K - 1, 0, -1)], axis=0)
            return acc
        o = g[{"a": "lconv_o", "c": "sconv_o"}[nm]]
        buf = g[{"a": "lconv_i", "c": "sconv_i"}[nm]][...].reshape((CONV_K - 1) * ns, width)
        buf = jnp.concatenate([buf, jnp.zeros((qt - (CONV_K - 1) * ns, width), F32)], axis=0)
        for m in range(1, CONV_K):
            prev = buf if m == 3 else pltpu.roll(buf, qt - (3 - m) * ns, 0)
            sh = jnp.where(t >= m, pltpu.roll(xraw, m * ns, 0), prev)
            acc = acc + vec[w0 + 3 - m:w0 + 4 - m, 0:width] * sh
        o[...] = xraw[(r - (CONV_K - 1)) * ns:, :].reshape(CONV_K - 1, ns, width)
        return acc

    def save_last(nm, h):
        if ns == 1:
            g["c_" + nm][...] = h[qt - 1:qt, :]
            g[nm + "_o"][...] = h[qt - 1:qt, :]
        else:
            g[nm + "_o"][...] = h[qt - ns:, :]

    def seq_mask(width):
        lane_seq = lax.broadcasted_iota(jnp.int32, (qt, ns * width), 1) // width
        row_seq = lax.broadcasted_iota(jnp.int32, (qt, ns * width), 0) % ns
        return (lane_seq == row_seq).astype(F32)

    gain = vec[V_GAIN:V_GAIN + 1, :]

    xc = conv(pcols(0, gw), "a", V_LCW, V_LCB)
    ga = pcols(gw, 2 * gw)
    pre = _mm(xc, g["wgate"][...]) + vec[V_BGATE:V_BGATE + 1, 0:2 * gw]
    rg = _sigmoid(pre[:, 0:gw])
    ig = _sigmoid(pre[:, gw:2 * gw])
    log_a = -LRU_C * rg * jax.nn.softplus(-vec[V_LAM:V_LAM + 1, 0:gw])
    a = jnp.exp(log_a)
    b = jnp.sqrt(-jnp.tanh(log_a) * (a * a + 1.0)) * (ig * xc)
    b = b + a * first_rows("c_lh", "lh_i")
    if ns == 1:
        a_s, b_s = slabs(a), slabs(b)
        hs, ps = [b_s[0]], [a_s[0]]
        for k in range(1, nk):
            hs.append(a_s[k] * hs[-1] + b_s[k])
            ps.append(a_s[k] * ps[-1])
        e, ae = hs[-1], ps[-1]
        for j in range(int(math.log2(SUBLANES))):
            d = 1 << j
            e = ae * sdown(e, d) + e
            ae = ae * sdown(ae, d, 1.0)
        cin = sdown(e, 1)
        b = jnp.concatenate([hs[k] + ps[k] * cin for k in range(nk)], axis=0)
    else:
        for k in range(nlev):
            d = 1 << k
            b = a * down(b, d) + b
            a = a * down(a, d, 1.0)
    save_last("lh", b)
    ya = _rms(b * jax.nn.gelu(ga)) * gain[:, 0:gw]

    u = pcols(2 * gw, 3 * gw)
    bu = _mm(u, g["bbm"][...])
    p_re, p_im = pw[0:1, :], pw[1:2, :]
    h0r, h0i = first_rows("c_s5r", "s5r_i"), first_rows("c_s5i", "s5i_i")
    hr = bu[:, 0:sn] + (p_re * h0r - p_im * h0i)
    hi = bu[:, sn:2 * sn] + (p_re * h0i + p_im * h0r)
    if ns == 1:
        pk = g["pk"]
        br_s, bi_s = slabs(hr), slabs(hi)
        hrs, his = [br_s[0]], [bi_s[0]]
        for k in range(1, nk):
            hrs.append(br_s[k] + (p_re * hrs[-1] - p_im * his[-1]))
            his.append(bi_s[k] + (p_re * his[-1] + p_im * hrs[-2]))
        er, ei = hrs[-1], his[-1]
        lev0 = int(math.log2(nk))
        for j in range(int(math.log2(SUBLANES))):
            d = 1 << j
            q_re, q_im = pw[2 * (lev0 + j):2 * (lev0 + j) + 1, :], pw[2 * (lev0 + j) + 1:2 * (lev0 + j) + 2, :]
            sr, si = sdown(er, d), sdown(ei, d)
            er, ei = er + (q_re * sr - q_im * si), ei + (q_re * si + q_im * sr)
        cr, ci = sdown(er, 1), sdown(ei, 1)
        hr = jnp.concatenate([hrs[k] + (pk[k:k + 1, :] * cr - pk[nk + k:nk + k + 1, :] * ci) for k in range(nk)], 0)
        hi = jnp.concatenate([his[k] + (pk[k:k + 1, :] * ci + pk[nk + k:nk + k + 1, :] * cr) for k in range(nk)], 0)
    else:
        for k in range(nlev):
            d = 1 << k
            q_re, q_im = pw[2 * k:2 * k + 1, :], pw[2 * k + 1:2 * k + 2, :]
            sr, si = down(hr, d), down(hi, d)
            hr, hi = hr + (q_re * sr - q_im * si), hi + (q_re * si + q_im * sr)
    save_last("s5r", hr)
    save_last("s5i", hi)
    yb = _mm(hr, g["cre"][...]) - _mm(hi, g["cim"][...])
    yb = jax.nn.gelu(yb + vec[V_S5D:V_S5D + 1, 0:gw] * u)
    yb = yb * _sigmoid(_mm(yb, g["wglu"][...]))
    yb = _rms(yb) * gain[:, gw:2 * gw]
    ysave(0, 2 * gw, jnp.concatenate([ya, yb], axis=1))

    tri = g["tri"][...] > 0.0
    xbc = conv(pcols(4 * gw, 6 * gw), "c", V_SCW, V_SCB)
    xbc = xbc * _sigmoid(xbc)
    xs, bm, cm = xbc[:, 0:gw], xbc[:, gw:gw + LANES], xbc[:, gw + LANES:2 * gw]
    dt = jax.nn.softplus(pcols(9 * gw, 10 * gw) + vec[V_DTB:V_DTB + 1, 0:gw])
    a_e = -jnp.exp(vec[V_ALOG:V_ALOG + 1, 0:gw])
    dta = dt * a_e
    if ns == 1:
        cs = slabs(dta)
        for k in range(1, nk):
            cs[k] = cs[k] + cs[k - 1]
        e = cs[-1]
        for j in range(int(math.log2(SUBLANES))):
            e = e + sdown(e, 1 << j)
        cin = sdown(e, 1)
        acum = jnp.concatenate([ck + cin for ck in cs], axis=0)
        alast = acum[qt - 1:qt, :]
    else:
        acum = dta
        for k in range(nlev):
            acum = acum + down(acum, 1 << k)
        suf = dta
        for k in range(nlev):
            suf = suf + up(suf, 1 << k)
        alast = acum + suf - dta
    wend = jnp.exp(alast - acum) * dt
    acum_t = jnp.transpose(acum)
    lane = lax.broadcasted_iota(jnp.int32, (1, gw), 1)
    lane_b = lax.broadcasted_iota(jnp.int32, (1, LANES), 1)
    heads_per_group = gw // SSD_HD // SSD_G
    xdt = xs * dt
    xw = xs * wend
    ydiag = jnp.zeros((qt, gw), F32)
    for grp in range(SSD_G):
        gm = ((lane_b // SSD_N) == grp).astype(F32)
        cb = _mm_nt(cm * gm, bm)
        for hh in range(heads_per_group):
            hd = grp * heads_per_group + hh
            col = acum[:, hd * SSD_HD:hd * SSD_HD + 1]
            rowv = acum_t[hd * SSD_HD:hd * SSD_HD + 1, :]
            dec = jnp.where(tri, jnp.exp(col - rowv), 0.0)
            hm = ((lane // SSD_HD) == hd).astype(F32)
            ydiag = ydiag + _mm(cb * dec, xdt * hm)
    if ns == 1:
        st = g["c_sh"][...]
        srow = lax.broadcasted_iota(jnp.int32, st.shape, 0) // SSD_N
        yoff = _mm(cm, st)
        slane = lax.broadcasted_iota(jnp.int32, st.shape, 1) // (SSD_HD * heads_per_group)
        upd = jnp.where(srow == slane, _mm_tn(bm, xw), 0.0)
        st = st * jnp.exp(acum[qt - 1:qt, :]) + upd
        g["c_sh"][...] = st

        @pl.when(c == nc - 1)
        def _final_ssd_state():
            own = jnp.concatenate([st[grp * SSD_N:(grp + 1) * SSD_N, grp * LANES:(grp + 1) * LANES]
                                   for grp in range(SSD_G)], axis=1)
            own = jnp.concatenate([own, jnp.zeros_like(own)], axis=0)
            g["sh_o"][...] = jnp.transpose(own)[:, 0:SSD_N].reshape(g["sh_o"].shape)
    else:
        sh_i, sh_o = g["sh_i"], g["sh_o"]
        per_seq = gw // SSD_HD * SSD_HD
        cols = [[], []]
        for s in range(0, ns, 2):
            two = jnp.concatenate([sh_i[s].reshape(per_seq, SSD_N), sh_i[s + 1].reshape(per_seq, SSD_N)], axis=1)
            two = jnp.transpose(two)
            for grp in range(SSD_G):
                cols[grp] += [two[0:SSD_N, grp * LANES:(grp + 1) * LANES],
                              two[SSD_N:2 * SSD_N, grp * LANES:(grp + 1) * LANES]]
        st = jnp.concatenate([jnp.concatenate(cols[0], axis=1), jnp.concatenate(cols[1], axis=1)], axis=0)
        srow = lax.broadcasted_iota(jnp.int32, st.shape, 0) // SSD_N
        seqm = seq_mask(LANES)
        lastm = jnp.where(t == r - 1, seqm, 0.0)
        yo, upd, dl = [], None, None
        for grp in range(SSD_G):
            gm = ((lane_b // SSD_N) == grp).astype(F32)
            z = _mm(cm * gm, st) * seqm
            zf = z[:, 0:LANES]
            for s in range(1, ns):
                zf = zf + z[:, s * LANES:(s + 1) * LANES]
            yo.append(zf)
            sl = slice(grp * LANES, (grp + 1) * LANES)
            u_g = _mm_tn(bm, jnp.tile(xw[:, sl], (1, ns)) * seqm)
            d_g = jnp.exp(jnp.sum(jnp.tile(acum[:, sl], (1, ns)) * lastm, axis=0, keepdims=True))
            upd = u_g if grp == 0 else jnp.where(srow == grp, u_g, upd)
            dl = d_g if grp == 0 else jnp.where(srow == grp, d_g, dl)
        yoff = jnp.concatenate(yo, axis=1)
        st = st * dl + upd
        for s in range(0, ns, 2):
            two = jnp.concatenate(
                [jnp.concatenate([st[0:SSD_N, q * LANES:(q + 1) * LANES], st[SSD_N:2 * SSD_N, q * LANES:(q + 1) * LANES]],
                                 axis=1) for q in (s, s + 1)], axis=0)
            two = jnp.transpose(two)
            sh_o[s] = two[:, 0:SSD_N].reshape(sh_o.shape[1:])
            sh_o[s + 1] = two[:, SSD_N:2 * SSD_N].reshape(sh_o.shape[1:])
    yc = ydiag + yoff * jnp.exp(acum) + vec[V_SSDD:V_SSDD + 1, 0:gw] * xs
    z = pcols(3 * gw, 4 * gw)
    yc = yc * (z * _sigmoid(z))
    ysave(2 * gw, 3 * gw, _rms(yc) * gain[:, 2 * gw:3 * gw])

    trig = g["trig"]
    cos, sin_a, sin_b = trig[0], trig[1], trig[2]

    def rope(x):
        return x * cos + pltpu.roll(x, LANES - RET_DK // 2, 1) * sin_a + pltpu.roll(x, RET_DK // 2, 1) * sin_b

    q = rope(pcols(6 * gw, 6 * gw + LANES))
    kk = rope(pcols(6 * gw + LANES, 7 * gw)) * (RET_DK ** -0.5)
    v = pcols(7 * gw, 8 * gw)
    dv = gw // RET_H
    intra = jnp.zeros((qt, gw), F32)
    for hd in range(RET_H):
        qm = ((lane_b // RET_DK) == hd).astype(F32)
        sc = _mm_nt(q * qm, kk) * g["dmask"][hd]
        vm = ((lane // dv) == hd).astype(F32)
        intra = intra + _mm(sc, v * vm)
    dec = g["dec"]
    qd, kd = q * dec[0], kk * dec[1]
    if ns == 1:
        rs = g["c_ret"][...]
        cross = _mm(qd, rs)
        rrow = lax.broadcasted_iota(jnp.int32, rs.shape, 0) // RET_DK
        rlane = lax.broadcasted_iota(jnp.int32, rs.shape, 1) // dv
        upd = jnp.where(rrow == rlane, _mm_tn(kd, v), 0.0)
    else:
        ret_i, ret_o = g["ret_i"], g["ret_o"]
        hk = RET_H * RET_DK
        rs = jnp.concatenate([ret_i[s].reshape(hk, dv) for s in range(ns)], axis=1)
        seqm = seq_mask(dv)
        lo = lane_b < dv
        pairs = []
        upd = jnp.zeros(rs.shape, F32)
        for pp in range(RET_H // 2):
            vp = v[:, pp * LANES:(pp + 1) * LANES]
            vsw = pltpu.roll(vp, dv, 1)
            tots = []
            for hh in range(2):
                qm = ((lane_b // RET_DK) == 2 * pp + hh).astype(F32)
                zc = _mm(qd * qm, rs) * seqm
                acc = zc[:, 0:LANES]
                for s2 in range(1, ns // 2):
                    acc = acc + zc[:, s2 * LANES:(s2 + 1) * LANES]
                tots.append(acc + pltpu.roll(acc, dv, 1))
                vh = jnp.where(lo, vp, vsw) if hh == 0 else jnp.where(lo, vsw, vp)
                upd = upd + _mm_tn(kd * qm, jnp.tile(vh, (1, ns // 2)) * seqm)
            pairs.append(jnp.where(lo, tots[0], tots[1]))
        cross = jnp.concatenate(pairs, axis=1)
    rs = rs * dec[2][:, 0:1] + upd
    if ns == 1:
        g["c_ret"][...] = rs

        @pl.when(c == nc - 1)
        def _final_ret_state():
            own = functools.reduce(lambda s, x: s + x, [rs[:, hd * dv:(hd + 1) * dv] for hd in range(RET_H)])
            g["ret_o"][...] = own.reshape(g["ret_o"].shape)
    else:
        for s in range(ns):
            ret_o[s] = rs[:, s * dv:(s + 1) * dv].reshape(ret_o.shape[1:])
    o = intra + cross
    o2 = o * o
    ms = jnp.zeros((qt, gw), F32)
    for hd in range(RET_H):
        vm = ((lane // dv) == hd).astype(F32)
        ms = ms + vm * jnp.sum(o2 * vm, axis=-1, keepdims=True)
    o = o * lax.rsqrt(ms * (1.0 / dv) + EPS) * gain[:, 3 * gw:4 * gw]
    gg = pcols(8 * gw, 9 * gw)
    ysave(3 * gw, 4 * gw, gg * _sigmoid(gg) * o)
    if ns == 1:
        y_all = jnp.concatenate(y_parts, axis=1)
        g["y"][...] = jnp.dot(g["perm"][1], y_all, preferred_element_type=F32).astype(g["y"].dtype)


def _mix_consts(qt, ns, r, nc, pos0):
    assert qt == LANES, "decay tables are stacked as (3, 128, 128)"
    idx = np.arange(qt)
    nk = qt // SUBLANES
    if ns == 1:
        seq, tt = np.zeros_like(idx), (idx % SUBLANES) * nk + idx // SUBLANES
    else:
        seq, tt = idx % ns, idx // ns
    causal = (seq[:, None] == seq[None, :]) & (tt[:, None] >= tt[None, :])
    tri = causal.astype(np.float32)
    lg = np.log1p(-np.exp2(-5.0 - np.arange(RET_H, dtype=np.float64)))
    rel = (tt[:, None] - tt[None, :]).astype(np.float64)
    dmask = np.where(causal[None], np.exp(np.maximum(rel, 0.0)[None] * lg[:, None, None]), 0.0).astype(np.float32)
    lane_h = np.arange(LANES) // RET_DK
    qdec = np.exp((tt[:, None] + 1.0) * lg[lane_h][None, :])
    kdec = np.exp((r - 1.0 - tt[:, None]) * lg[lane_h][None, :])
    rdec = np.broadcast_to(np.exp(r * lg[lane_h])[:, None], (LANES, LANES))
    dec = np.stack([qdec, kdec, rdec]).astype(np.float32)
    half = RET_DK // 2
    pos = pos0 + jnp.asarray((np.arange(nc)[:, None] * qt + tt[None, :]).reshape(-1))
    inv = ROPE_BASE ** (-jnp.arange(half, dtype=F32) / half)
    ang = pos.astype(F32)[:, None] * inv
    reps = LANES // half
    cos, sin = jnp.tile(jnp.cos(ang), (1, reps)), jnp.tile(jnp.sin(ang), (1, reps))
    first = (np.arange(LANES) % RET_DK) < half
    trig = jnp.stack([cos, jnp.where(first, -sin, 0.0), jnp.where(first, 0.0, sin)])
    to_blocked = np.zeros((qt, qt), np.float32)
    to_blocked[idx, (idx % SUBLANES) * nk + idx // SUBLANES] = 1.0
    perm = jnp.asarray(np.stack([to_blocked, to_blocked.T])).astype(MXU_DTYPE)
    return dict(trig=trig, tri=jnp.asarray(tri), dmask=jnp.asarray(dmask), dec=jnp.asarray(dec), perm=perm)


def _state_layout(depth, bsz, ns, gw, sn):
    hk, dv = RET_H * RET_DK, gw // RET_H
    if ns == 1:
        def per_seq(rows, w):
            return (depth, bsz, rows, w), (None, None, rows, w), lambda b, c, l: (l[0], b, 0, 0)
        def matrix(h, rows, w):
            return (depth, bsz, h, rows, w), (None, None, h, rows, w), lambda b, c, l: (l[0], b, 0, 0, 0)
        return dict(lconv=per_seq(CONV_K - 1, gw), lh=per_seq(1, gw), s5r=per_seq(1, sn), s5i=per_seq(1, sn),
                    sconv=per_seq(CONV_K - 1, 2 * gw), sh=matrix(gw // SSD_HD, SSD_HD, SSD_N),
                    ret=matrix(RET_H, RET_DK, dv))

    def flat(per, w):
        return (depth, bsz * per, w), (None, ns * per, w), lambda b, c, l: (l[0], b, 0)

    def conv(w):
        return (depth, CONV_K - 1, bsz, w), (None, CONV_K - 1, ns, w), lambda b, c, l: (l[0], 0, b, 0)
    def matrix(h, rows, w):
        return (depth, bsz, h, rows, w), (None, ns, h, rows, w), lambda b, c, l: (l[0], b, 0, 0, 0)
    return dict(lconv=conv(gw), lh=flat(1, gw), s5r=flat(1, sn), s5i=flat(1, sn), sconv=conv(2 * gw),
                sh=matrix(gw // SSD_HD, SSD_HD, SSD_N), ret=matrix(RET_H, RET_DK, dv))


def _mixer(l, proj, consts, mw, state_in, state_out, *, nb, nc, qt, ns, r, d):
    dp = proj.shape[-1]
    gw = d // 4
    sn = mw["pw"].shape[-1]
    depth = mw["pw"].shape[0]
    layout = _state_layout(depth, nb * ns, ns, gw, sn)
    assert (state_in is None) == (ns == 1)

    def layer_spec(a):
        return pl.BlockSpec((None,) + a.shape[1:], lambda b, c, l: (l[0],) + (0,) * (a.ndim - 1))

    def const_spec(a):
        return pl.BlockSpec(a.shape, lambda b, c, l: (0,) * a.ndim)

    def state_spec(nm):
        _, block, index = layout[nm]
        return pl.BlockSpec(block, index)

    if ns == 1:
        io_spec = lambda w: pl.BlockSpec((qt, w), lambda b, c, l: (b * nc + c, 0))
        y_shape = (nb * nc * qt, d)
        cnames = ("trig", "tri", "dmask", "dec", "perm")
    else:
        proj = proj.reshape(r, nb * ns, dp)
        io_spec = lambda w: pl.BlockSpec((r, ns, w), lambda b, c, l: (0, b, 0))
        y_shape = (r, nb * ns, d)
        cnames = ("trig", "tri", "dmask", "dec")
    names = ("proj",) + cnames + MIX_WEIGHTS
    operands = [proj] + [consts[k] for k in cnames] + [mw[k] for k in MIX_WEIGHTS]
    in_specs = [io_spec(dp), pl.BlockSpec((3, qt, LANES), lambda b, c, l: (0, c, 0))] + \
               [const_spec(consts[k]) for k in cnames[1:]] + [layer_spec(mw[k]) for k in MIX_WEIGHTS]
    if state_in is not None:
        names += tuple(nm + "_i" for nm in STATE_NAMES)
        operands += [state_in[nm] for nm in STATE_NAMES]
        in_specs += [state_spec(nm) for nm in STATE_NAMES]
    first_alias = 1 + len(operands)
    names += tuple(nm + "_alias" for nm in STATE_NAMES)
    operands += [state_out[nm] for nm in STATE_NAMES]
    in_specs += [pl.BlockSpec(memory_space=pl.ANY) for _ in STATE_NAMES]
    names += ("y",) + tuple(nm + "_o" for nm in STATE_NAMES)
    out_specs = [io_spec(d)] + [state_spec(nm) for nm in STATE_NAMES]
    out_shape = [jax.ShapeDtypeStruct(y_shape, MXU_DTYPE)] + \
                [jax.ShapeDtypeStruct(layout[nm][0], F32) for nm in STATE_NAMES]
    scratch = []
    if ns == 1:
        tail = (CONV_K - 1) * SUBLANES
        scratch_shapes = dict(cb_a=(tail, gw), cb_c=(tail, 2 * gw), c_lh=(1, gw), c_s5r=(1, sn),
                              c_s5i=(1, sn), c_sh=(SSD_G * SSD_N, gw), c_ret=(RET_H * RET_DK, gw))
        names += tuple(scratch_shapes)
        scratch = [pltpu.VMEM(shape, F32) for shape in scratch_shapes.values()]
    grid_spec = pltpu.PrefetchScalarGridSpec(num_scalar_prefetch=1, grid=(nb, nc),
                                             in_specs=in_specs, out_specs=out_specs, scratch_shapes=scratch)
    outs = pl.pallas_call(
        functools.partial(_mix_body, names=names, qt=qt, ns=ns, r=r, nc=nc),
        grid_spec=grid_spec,
        out_shape=out_shape,
        input_output_aliases={first_alias + k: 1 + k for k in range(len(STATE_NAMES))},
        compiler_params=pltpu.CompilerParams(dimension_semantics=("arbitrary", "arbitrary"),
                                             vmem_limit_bytes=VMEM_LIMIT),
        name="mixer",
    )(l, *operands)
    return outs[0].reshape(-1, d), dict(zip(STATE_NAMES, outs[1:]))


def _post_body(l_ref, y_ref, x_ref, g1_ref, sc_ref, sh_ref, g2_ref, lt_ref, wo_ref, wr_ref, br_ref,
               wg_ref, wu_ref, wd_ref, *rest):
    o_ref, xs_ref, cs_ref, os_ref = rest[-4:]
    del l_ref
    tm, d = x_ref.shape
    tms = xs_ref.shape[0]
    x = x_ref[...] + _token_rows(g1_ref[...], tm) * jnp.dot(y_ref[...], wo_ref[...], preferred_element_type=F32)
    h = _rms(x) * (1.0 + _token_rows(sc_ref[...], tm)) + _token_rows(sh_ref[...], tm)
    hb = h.astype(MXU_DTYPE)
    h_lo = (h - hb.astype(F32)).astype(MXU_DTYPE)
    wr = wr_ref[...]
    l12 = jnp.dot(hb, wr, preferred_element_type=F32)
    logits = (l12[:, 0:LANES] + l12[:, LANES:2 * LANES]
              + jnp.dot(h_lo, wr[:, 0:LANES], preferred_element_type=F32) + br_ref[...])
    logits_t = jnp.transpose(logits)
    col = [logits_t[k:k + 1, :] for k in range(N_GROUPS + N_EXPERTS)]

    def first_max(vals, allowed=None):
        neg = jnp.full_like(vals[0], -jnp.inf)
        cand = vals if allowed is None else [jnp.where(al > 0.0, v, neg) for v, al in zip(vals, allowed)]
        m = functools.reduce(jnp.maximum, cand)
        rem = jnp.ones_like(vals[0])
        hot = []
        for v in cand:
            f = jnp.where(v >= m, rem, 0.0)
            rem = rem - f
            hot.append(f)
        return hot, m

    grp, gmax = first_max(col[:N_GROUPS])
    gate = 1.0 / functools.reduce(lambda s, v: s + v, [jnp.exp(v - gmax) for v in col[:N_GROUPS]])
    le = [functools.reduce(lambda s, v: s + v,
                           [grp[gi] * col[N_GROUPS + gi * N_PER_GROUP + j] for gi in range(N_GROUPS)])
          for j in range(N_PER_GROUP)]
    top1, m1 = first_max(le)
    top2, m2 = first_max(le, [1.0 - f for f in top1])
    e2 = jnp.exp(m2 - m1)
    w1 = 1.0 / (1.0 + e2)
    w2 = e2 * w1
    wgrp = [(top1[j] * w1 + top2[j] * w2) * gate for j in range(N_PER_GROUP)]

    sub8 = lax.broadcasted_iota(jnp.int32, (SUBLANES, 1), 0)
    ghot_t = functools.reduce(lambda s, v: s + v, [jnp.where(sub8 == gi, grp[gi], 0.0) for gi in range(N_GROUPS)])
    before = _mm_nt(ghot_t, lt_ref[...])
    count = jnp.sum(ghot_t, axis=1, keepdims=True)
    cnt = [count[gi:gi + 1, :] for gi in range(N_GROUPS)]
    base = [jnp.zeros((1, 1), F32)]
    for gi in range(1, N_GROUPS):
        base.append(jnp.floor((base[-1] + cnt[gi - 1] + (PACK - 1.0)) * (1.0 / PACK)) * PACK)
    pos_row = functools.reduce(lambda s, v: s + v,
                               [grp[gi] * (base[gi] + before[gi:gi + 1, :]) for gi in range(N_GROUPS)])
    pos = jnp.transpose(jnp.broadcast_to(pos_row, (LANES, tm)))[:, 0:1]
    slot = lax.broadcasted_iota(jnp.int32, (tms, tm), 0).astype(F32)
    gather = jnp.where(slot == pos_row, 1.0, 0.0).astype(MXU_DTYPE)
    slot_l = lax.broadcasted_iota(jnp.int32, (tm, tms), 1).astype(F32)
    scatter = jnp.where(slot_l == pos, 1.0, 0.0).astype(MXU_DTYPE)

    xs_ref[...] = jnp.dot(gather, hb, preferred_element_type=F32).astype(xs_ref.dtype)
    sub_e = lax.broadcasted_iota(jnp.int32, (cs_ref.shape[1], 1), 0)
    comb_t = functools.reduce(lambda s, v: s + v,
                              [jnp.where(sub_e == e, grp[e // N_PER_GROUP] * wgrp[e % N_PER_GROUP], 0.0)
                               for e in range(N_EXPERTS)])
    c1, c2, c3 = _split3(comb_t)
    nt = lambda p: lax.dot_general(gather, p, (((1,), (1,)), ((), ())), preferred_element_type=F32)
    cs_ref[...] = nt(c1) + nt(c2) + nt(c3)
    os_ref[...] = jnp.zeros(os_ref.shape, os_ref.dtype)
    for gi in range(N_GROUPS):
        start = base[gi][0, 0].astype(jnp.int32)
        nblk = jnp.floor((cnt[gi][0, 0] + (EXPERT_ROWS - 1.0)) * (1.0 / EXPERT_ROWS)).astype(jnp.int32)

        def block(k, carry, gi=gi, start=start):
            rows = pl.ds(pl.multiple_of(start + k * EXPERT_ROWS, PACK), EXPERT_ROWS)
            xb = xs_ref[rows, :]
            cb = cs_ref[rows, :]
            acc = jnp.zeros((EXPERT_ROWS, d), F32)
            for j in range(N_PER_GROUP):
                e = gi * N_PER_GROUP + j
                gt = jnp.dot(xb, wg_ref[e], preferred_element_type=F32)
                he = gt * _sigmoid(gt) * jnp.dot(xb, wu_ref[e], preferred_element_type=F32)
                acc = acc + jnp.dot((he * cb[:, e:e + 1]).astype(MXU_DTYPE), wd_ref[e], preferred_element_type=F32)
            os_ref[rows, :] += acc
            return carry

        lax.fori_loop(0, nblk, block, 0)
    moe = _select_rows(scatter, os_ref[...], pieces=2)
    out = x + _token_rows(g2_ref[...], tm) * moe
    if len(rest) == 5:
        out = _rms(out) * rest[0][...]
    o_ref[...] = out


def _post(l, y, x, mod, pw, tm, rows_per_seq, final_w=None, in_place=False):
    t, d = x.shape
    final = [] if final_w is None else [final_w.reshape(1, d)]
    tms = tm + EXPERT_ROWS + PACK * N_GROUPS
    ltri = jnp.asarray(np.tril(np.ones((tm, tm), np.float32), -1)).astype(MXU_DTYPE)

    def layer_spec(a):
        return pl.BlockSpec((None,) + a.shape[1:], lambda i, l: (l[0],) + (0,) * (a.ndim - 1),
                            pipeline_mode=pl.Buffered(1))

    grid_spec = pltpu.PrefetchScalarGridSpec(
        num_scalar_prefetch=1,
        grid=(t // tm,),
        in_specs=[
            pl.BlockSpec((tm, d), lambda i, l: (i, 0)),
            pl.BlockSpec((tm, d), lambda i, l: (i, 0)),
            _mod_spec(mod, 2, tm, rows_per_seq),
            _mod_spec(mod, 4, tm, rows_per_seq),
            _mod_spec(mod, 3, tm, rows_per_seq),
            _mod_spec(mod, 5, tm, rows_per_seq),
            pl.BlockSpec((tm, tm), lambda i, l: (0, 0)),
        ] + [layer_spec(pw[k]) for k in ("w_out", "w_route", "b_route", "w_gate", "w_up", "w_down")]
          + [pl.BlockSpec((1, d), lambda i, l: (0, 0)) for _ in final],
        out_specs=pl.BlockSpec((tm, d), lambda i, l: (i, 0)),
        scratch_shapes=[pltpu.VMEM((tms, d), MXU_DTYPE), pltpu.VMEM((tms, 2 * N_EXPERTS), F32),
                        pltpu.VMEM((tms, d), F32)],
    )
    return pl.pallas_call(
        _post_body,
        grid_spec=grid_spec,
        out_shape=jax.ShapeDtypeStruct((t, d), F32),
        input_output_aliases={2: 0} if in_place else {},
        compiler_params=pltpu.CompilerParams(dimension_semantics=("arbitrary",), vmem_limit_bytes=VMEM_LIMIT),
        name="post",
    )(l, y, x, mod, mod, mod, mod, ltri,
      *[pw[k] for k in ("w_out", "w_route", "b_route", "w_gate", "w_up", "w_down")], *final)


def _block_diag(w):
    depth, nblk, bi, bj = w.shape
    eye = jnp.eye(nblk, dtype=w.dtype)
    return (w[:, :, :, None, :] * eye[None, :, None, :, None]).reshape(depth, nblk * bi, nblk * bj)


def _pad_lanes(a, width):
    return jnp.pad(a, [(0, 0)] * (a.ndim - 1) + [(0, width - a.shape[-1])])


def _prep_weights(w_in, lru_conv_w, lru_conv_b, lru_wa, lru_ba, lru_wx, lru_bx, lru_lambda,
                  s5_a_re, s5_a_im, s5_b_re, s5_b_im, s5_c_re, s5_c_im, s5_d, s5_log_dt, s5_w_glu,
                  ssd_conv_w, ssd_conv_b, ssd_dt_bias, ssd_a_log, ssd_d, mix_norm, qt):
    depth, d, _ = w_in.shape
    gw = d // 4
    n_ssd_h = gw // SSD_HD
    xbc_end = 4 * gw + gw + 2 * SSD_G * SSD_N
    dt_cols = jnp.repeat(w_in[..., xbc_end:xbc_end + n_ssd_h], SSD_HD, axis=-1)
    w_in_p = jnp.concatenate([w_in[..., :xbc_end], w_in[..., xbc_end + n_ssd_h:], dt_cols], axis=-1)

    def row(a):
        return _pad_lanes(a.reshape(depth, 1, -1), d)

    vec = jnp.concatenate([
        row(lru_conv_b), row(jnp.concatenate([lru_ba, lru_bx], -1)), row(lru_lambda), row(s5_d),
        row(ssd_conv_b), row(jnp.repeat(ssd_dt_bias, SSD_HD, -1)), row(jnp.repeat(ssd_a_log, SSD_HD, -1)),
        row(jnp.repeat(ssd_d, SSD_HD, -1)), row(mix_norm),
        _pad_lanes(lru_conv_w, d), _pad_lanes(ssd_conv_w, d),
        jnp.zeros((depth, V_ROWS - V_SCW - CONV_K, d), F32)], axis=1)

    wgate = jnp.concatenate([_block_diag(lru_wa), _block_diag(lru_wx)], axis=-1)

    dt = jnp.exp(s5_log_dt)[..., None]
    lr, li = s5_a_re, s5_a_im
    mag = jnp.exp(lr * dt)
    ab_re, ab_im = mag * jnp.cos(li * dt), mag * jnp.sin(li * dt)
    den = lr * lr + li * li
    q_re = ((ab_re - 1.0) * lr + ab_im * li) / den
    q_im = (ab_im * lr - (ab_re - 1.0) * li) / den
    bb_re = q_re[..., None] * s5_b_re - q_im[..., None] * s5_b_im
    bb_im = q_re[..., None] * s5_b_im + q_im[..., None] * s5_b_re
    bbm = jnp.concatenate([_block_diag(jnp.swapaxes(bb_re, -1, -2)), _block_diag(jnp.swapaxes(bb_im, -1, -2))], -1)
    cre = _block_diag(jnp.swapaxes(s5_c_re, -1, -2))
    cim = _block_diag(jnp.swapaxes(s5_c_im, -1, -2))
    a_re, a_im = ab_re.reshape(depth, -1), ab_im.reshape(depth, -1)
    pr, pi = a_re, a_im
    rows = []
    for _ in range(int(math.log2(qt))):
        rows += [pr, pi]
        pr, pi = pr * pr - pi * pi, 2.0 * pr * pi
    pw = jnp.stack(rows, axis=1)
    pw = jnp.pad(pw, ((0, 0), (0, (-pw.shape[1]) % SUBLANES), (0, 0)))
    pr, pi = a_re, a_im
    k_re, k_im = [], []
    for _ in range(qt // SUBLANES):
        k_re.append(pr)
        k_im.append(pi)
        pr, pi = pr * a_re - pi * a_im, pr * a_im + pi * a_re
    pk = jnp.stack(k_re + k_im, axis=1)

    mw = dict(vec=vec, wgate=wgate.astype(MXU_DTYPE), bbm=bbm.astype(MXU_DTYPE), cre=cre.astype(MXU_DTYPE),
              cim=cim.astype(MXU_DTYPE), wglu=s5_w_glu.astype(MXU_DTYPE), pw=pw, pk=pk)
    return w_in_p.astype(MXU_DTYPE), mw


def _states_to_layout(states, layout, ns):
    out = {}
    for nm, s in zip(STATE_NAMES, states):
        if nm in ("lconv", "sconv"):
            s = s.transpose(0, 2, 1, 3)
        out[nm] = s.reshape(layout[nm][0])
    return out


def _states_from_layout(arrs, ns, ref_shapes):
    out = []
    for nm, shape in zip(STATE_NAMES, ref_shapes):
        a = arrs[nm]
        if ns > 1 and nm in ("lconv", "sconv"):
            a = a.transpose(0, 2, 1, 3)
        out.append(a.reshape(shape))
    return tuple(out)


def _trunk(x, mod, states, ref_shapes, pos0, w_in_p, mw, pw, final_norm, *, qt, ns, tm):
    bsz, seq_len, d = x.shape
    depth = w_in_p.shape[0]
    gw = d // 4
    r = qt // ns
    nc = seq_len // r if ns == 1 else 1
    assert (ns == 1 and seq_len % qt == 0 and states is None) or (ns > 1 and r == seq_len and bsz % ns == 0)
    assert r >= CONV_K and (r & (r - 1)) == 0
    nb = bsz // ns
    tok = bsz * seq_len
    consts = _mix_consts(qt, ns, r, nc, pos0)
    layout = _state_layout(depth, bsz, ns, gw, mw["pw"].shape[-1])
    state_in = None if states is None else _states_to_layout(states, layout, ns)
    state_out = {nm: jnp.zeros(layout[nm][0], F32) for nm in STATE_NAMES}
    x2 = x.reshape(tok, d) if ns == 1 else x.transpose(1, 0, 2).reshape(tok, d)
    blocked_chunk = qt if ns == 1 else None

    def layer(carry, l, final_w=None, in_place=True):
        xc, st = carry
        lv = jnp.reshape(l, (1,)).astype(jnp.int32)
        proj = _inproj(lv, xc, mod, w_in_p, tm, seq_len, blocked_chunk)
        y, st = _mixer(lv, proj, consts, mw, state_in, st, nb=nb, nc=nc, qt=qt, ns=ns, r=r, d=d)
        xn = _post(lv, y, xc, mod, pw, tm, seq_len, final_w, in_place)
        return (xn, st), None

    assert depth >= 2
    carry, _ = layer((x2, state_out), jnp.int32(0), in_place=False)
    carry, _ = lax.scan(layer, carry, jnp.arange(1, depth - 1))
    (y, st), _ = layer(carry, jnp.int32(depth - 1), final_norm)
    y = y.reshape(bsz, seq_len, d) if ns == 1 else y.reshape(seq_len, bsz, d).transpose(1, 0, 2)
    return y, _states_from_layout(st, ns, ref_shapes)


def kernel(x_prompt, x_sample, state_lru_conv, state_lru_h, state_s5_re, state_s5_im, state_ssd_conv, state_ssd_h, state_ret, c_prompt, c_sample, w_ada, b_ada, w_in, lru_conv_w, lru_conv_b, lru_wa, lru_ba, lru_wx, lru_bx, lru_lambda, s5_a_re, s5_a_im, s5_b_re, s5_b_im, s5_c_re, s5_c_im, s5_d, s5_log_dt, s5_w_glu, ssd_conv_w, ssd_conv_b, ssd_dt_bias, ssd_a_log, ssd_d, mix_norm, w_out, w_route_group, b_route_group, w_route_exp, b_route_exp, w_exp_gate, w_exp_up, w_exp_down, final_norm):
    bp, lp, d = x_prompt.shape
    bs, ls, _ = x_sample.shape
    depth = w_in.shape[0]
    qt = LANES
    ns_s = qt // ls

    mod_p, mod_s = _modulation(jnp.concatenate([c_prompt, c_sample], axis=0), bp, w_ada, b_ada)
    mod_p = mod_p.reshape(depth, N_MOD, bp, 1, d)

    w_in_p, mw = _prep_weights(w_in, lru_conv_w, lru_conv_b, lru_wa, lru_ba, lru_wx, lru_bx, lru_lambda,
                               s5_a_re, s5_a_im, s5_b_re, s5_b_im, s5_c_re, s5_c_im, s5_d, s5_log_dt, s5_w_glu,
                               ssd_conv_w, ssd_conv_b, ssd_dt_bias, ssd_a_log, ssd_d, mix_norm, qt)
    w_route = _pad_lanes(jnp.concatenate([w_route_group, w_route_exp.reshape(depth, d, N_EXPERTS)], -1), LANES)
    b_route = _pad_lanes(jnp.concatenate([b_route_group, b_route_exp.reshape(depth, N_EXPERTS)], -1), LANES)
    w_route_hi = w_route.astype(MXU_DTYPE)
    w_route_lo = (w_route - w_route_hi.astype(F32)).astype(MXU_DTYPE)
    w_route = jnp.concatenate([w_route_hi, w_route_lo], axis=-1)
    pw = dict(w_out=w_out.astype(MXU_DTYPE), w_route=w_route, b_route=b_route.reshape(depth, 1, LANES),
              w_gate=w_exp_gate.astype(MXU_DTYPE), w_up=w_exp_up.astype(MXU_DTYPE),
              w_down=w_exp_down.astype(MXU_DTYPE))
    w_in_p, mw, pw = lax.optimization_barrier((w_in_p, mw, pw))

    states_s = (state_lru_conv, state_lru_h, state_s5_re, state_s5_im, state_ssd_conv, state_ssd_h, state_ret)
    shapes_s = [s.shape for s in states_s]
    shapes_p = [(depth, bp) + s[2:] for s in shapes_s]
    y_p, new_p = _trunk(x_prompt, mod_p, None, shapes_p, 0, w_in_p, mw, pw, final_norm,
                        qt=qt, ns=1, tm=min(512, lp))
    y_s, new_s = _trunk(x_sample, mod_s, states_s, shapes_s, PAST_LEN, w_in_p, mw, pw, final_norm,
                        qt=qt, ns=ns_s, tm=bs * ls)
    out = [y_p, y_s]
    for a, b in zip(new_p, new_s):
        out += [a, b]
    return tuple(out)
```

```python
import functools
import math

import numpy as np
import jax
import jax.numpy as jnp
from jax import lax
from jax.experimental import pallas as pl
from jax.experimental.pallas import tpu as pltpu

F32 = jnp.float32
MXU_DTYPE = jnp.bfloat16
HIGHEST = lax.Precision.HIGHEST

EPS = 1e-6
CONV_K = 4
LRU_C = 8.0
SSD_HD = 64
SSD_G = 2
SSD_N = 64
RET_H = 4
RET_DK = 32
ROPE_BASE = 10000.0
N_GROUPS = 4
N_PER_GROUP = 4
N_EXPERTS = N_GROUPS * N_PER_GROUP
N_MOD = 6
PAST_LEN = 16384

LANES = 128
SUBLANES = 8
PACK = 16
VMEM_LIMIT = 56 * 1024 * 1024
EXPERT_ROWS = 144


def _mm(a, b):
    return jnp.dot(a.astype(MXU_DTYPE), b.astype(MXU_DTYPE), preferred_element_type=F32)


def _mm_nt(a, b):
    return lax.dot_general(a.astype(MXU_DTYPE), b.astype(MXU_DTYPE), (((1,), (1,)), ((), ())),
                           preferred_element_type=F32)


def _mm_tn(a, b):
    return lax.dot_general(a.astype(MXU_DTYPE), b.astype(MXU_DTYPE), (((0,), (0,)), ((), ())),
                           preferred_element_type=F32)


def _split3(x):
    x1 = x.astype(MXU_DTYPE)
    r1 = x - x1.astype(F32)
    x2 = r1.astype(MXU_DTYPE)
    x3 = (r1 - x2.astype(F32)).astype(MXU_DTYPE)
    return x1, x2, x3


def _select_rows(onehot, x, pieces=3):
    parts = _split3(x)[:pieces]
    out = jnp.dot(onehot, parts[0], preferred_element_type=F32)
    for p in parts[1:]:
        out = out + jnp.dot(onehot, p, preferred_element_type=F32)
    return out


def _sigmoid(x):
    return 0.5 * jnp.tanh(0.5 * x) + 0.5


def _rms(x):
    return x * lax.rsqrt(jnp.mean(x * x, axis=-1, keepdims=True) + EPS)


def _token_rows(v, tm):
    n = v.shape[0]
    return v if n in (1, tm) else jnp.tile(v, (tm // n, 1))


def _mod_body(c_ref, w_ref, b_ref, first_ref, rest_ref):
    c = c_ref[...]
    m = _mm(c * _sigmoid(c), w_ref[...]) + b_ref[...]
    n_first = first_ref.shape[0]
    first_ref[...] = m[0:n_first]
    rest_ref[...] = m[n_first:]


def _modulation(c_all, n_first, w_ada, b_ada):
    depth, d, _ = w_ada.shape
    nb = c_all.shape[0]
    return pl.pallas_call(
        _mod_body,
        grid=(depth, N_MOD),
        in_specs=[
            pl.BlockSpec((nb, d), lambda l, k: (0, 0)),
            pl.BlockSpec((None, d, d), lambda l, k: (l, 0, k)),
            pl.BlockSpec((None, None, 1, d), lambda l, k: (l, k, 0, 0)),
        ],
        out_specs=[pl.BlockSpec((None, None, n_first, d), lambda l, k: (l, k, 0, 0)),
                   pl.BlockSpec((None, None, nb - n_first, d), lambda l, k: (l, k, 0, 0))],
        out_shape=[jax.ShapeDtypeStruct((depth, N_MOD, n_first, d), F32),
                   jax.ShapeDtypeStruct((depth, N_MOD, nb - n_first, d), F32)],
        compiler_params=pltpu.CompilerParams(dimension_semantics=("arbitrary", "arbitrary"),
                                             vmem_limit_bytes=VMEM_LIMIT),
        name="modulation",
    )(c_all, w_ada, b_ada.reshape(depth, N_MOD, 1, d))


def _mod_spec(mod, k, tm, rows_per_seq):
    if mod.ndim == 5:
        tiles_per_seq = rows_per_seq // tm
        return pl.BlockSpec((None, None, None, 1, mod.shape[-1]),
                            lambda i, l: (l[0], k, i // tiles_per_seq, 0, 0))
    return pl.BlockSpec((None, None) + mod.shape[2:], lambda i, l: (l[0], k, 0, 0))


def _inproj_body(l_ref, x_ref, sc_ref, sh_ref, w_ref, o_ref, *, blocked_chunk):
    tm = x_ref.shape[0]
    h = _rms(x_ref[...]) * (1.0 + _token_rows(sc_ref[...], tm)) + _token_rows(sh_ref[...], tm)
    hb = h.astype(MXU_DTYPE)
    if blocked_chunk is not None:
        qt, nk = blocked_chunk, blocked_chunk // SUBLANES
        r_i = lax.broadcasted_iota(jnp.int32, (qt, qt), 0)
        c_i = lax.broadcasted_iota(jnp.int32, (qt, qt), 1)
        to_blocked = jnp.where(c_i == (r_i % SUBLANES) * nk + r_i // SUBLANES, 1.0, 0.0).astype(MXU_DTYPE)
        hb = jnp.concatenate([jnp.dot(to_blocked, hb[j * qt:(j + 1) * qt], preferred_element_type=F32)
                              for j in range(tm // qt)], axis=0).astype(MXU_DTYPE)
    o_ref[...] = jnp.dot(hb, w_ref[...], preferred_element_type=F32)


def _inproj(l, x, mod, w_in, tm, rows_per_seq, blocked_chunk=None):
    t, d = x.shape
    dp = w_in.shape[-1]
    assert blocked_chunk is None or (mod.ndim == 5 and tm % blocked_chunk == 0)
    grid_spec = pltpu.PrefetchScalarGridSpec(
        num_scalar_prefetch=1,
        grid=(t // tm,),
        in_specs=[
            pl.BlockSpec((tm, d), lambda i, l: (i, 0)),
            _mod_spec(mod, 1, tm, rows_per_seq),
            _mod_spec(mod, 0, tm, rows_per_seq),
            pl.BlockSpec((None, d, dp), lambda i, l: (l[0], 0, 0)),
        ],
        out_specs=pl.BlockSpec((tm, dp), lambda i, l: (i, 0)),
    )
    return pl.pallas_call(
        functools.partial(_inproj_body, blocked_chunk=blocked_chunk),
        grid_spec=grid_spec,
        out_shape=jax.ShapeDtypeStruct((t, dp), F32),
        compiler_params=pltpu.CompilerParams(dimension_semantics=("arbitrary",), vmem_limit_bytes=VMEM_LIMIT),
        name="inproj",
    )(l, x, mod, mod, w_in)


V_LCB, V_BGATE, V_LAM, V_S5D, V_SCB, V_DTB, V_ALOG, V_SSDD, V_GAIN, V_LCW, V_SCW = 0, 1, 2, 3, 4, 5, 6, 7, 8, 9, 13
V_ROWS = 24
STATE_NAMES = ("lconv", "lh", "s5r", "s5i", "sconv", "sh", "ret")
MIX_WEIGHTS = ("vec", "wgate", "bbm", "cre", "cim", "wglu", "pw", "pk")


def _mix_body(l_ref, *refs, names, qt, ns, r, nc):
    del l_ref
    g = dict(zip(names, refs))
    gw = g["lh_o"].shape[-1]
    sn = g["s5r_o"].shape[-1]
    nlev = int(math.log2(r))
    nk = qt // SUBLANES
    c = pl.program_id(1)

    if ns == 1:
        @pl.when(c == 0)
        def _zero_carries():
            for nm in ("c_lh", "c_s5r", "c_s5i", "c_sh", "c_ret"):
                g[nm][...] = jnp.zeros(g[nm].shape, F32)
            for nm in ("cb_a", "cb_c"):
                g[nm][...] = jnp.zeros(g[nm].shape, F32)

    proj = g["proj"]
    vec = g["vec"]
    pw = g["pw"]

    def pcols(a, b):
        v = proj[..., a:b]
        return v if ns == 1 else v.reshape(qt, b - a)

    y_parts = []

    def ysave(a, b, val):
        val = val.astype(g["y"].dtype)
        if ns == 1:
            y_parts.append(val)
        else:
            g["y"][..., a:b] = val.reshape(r, ns, b - a)

    row = lax.broadcasted_iota(jnp.int32, (qt, 1), 0)
    t = (row % SUBLANES) * nk + row // SUBLANES if ns == 1 else row // ns
    sub = lax.broadcasted_iota(jnp.int32, (SUBLANES, 1), 0)

    def down(x, d, fill=0.0):
        return jnp.where(t >= d, pltpu.roll(x, d * ns, 0), fill)

    def up(x, d):
        return jnp.where(t + d < r, pltpu.roll(x, qt - d * ns, 0), 0.0)

    def sdown(x, d, fill=0.0):
        return jnp.where(sub >= d, pltpu.roll(x, d, 0), fill)

    def slabs(x):
        return [x[SUBLANES * k:SUBLANES * (k + 1)] for k in range(nk)]

    def first_rows(carry, init):
        if ns == 1:
            return jnp.where(row == 0, g[carry][...], 0.0)
        h0 = g[init][...]
        return jnp.concatenate([h0, jnp.zeros((qt - ns, h0.shape[1]), F32)], axis=0)

    def conv(xraw, nm, w0, b):
        width = xraw.shape[-1]
        acc = vec[b:b + 1, 0:width] + vec[w0 + 3:w0 + 4, 0:width] * xraw
        if ns == 1:
            cb = g["cb_" + nm]
            xs_k = slabs(xraw)
            wrap = [jnp.where(sub >= 1, pltpu.roll(xs_k[nk - j], 1, 0),
                              pltpu.roll(cb[(CONV_K - 1 - j) * SUBLANES:(CONV_K - j) * SUBLANES, :], 1, 0))
                    for j in range(1, CONV_K)]
            for m in range(1, CONV_K):
                sh = jnp.concatenate([wrap[m - k - 1] for k in range(m)] + xs_k[0:nk - m], axis=0)
                acc = acc + vec[w0 + 3 - m:w0 + 4 - m, 0:width] * sh
            cb[...] = xraw[qt - (CONV_K - 1) * SUBLANES:, :]
            g[{"a": "lconv_o", "c": "sconv_o"}[nm]][...] = jnp.concatenate(
                [xs_k[nk - j][SUBLANES - 1:SUBLANES, :] for j in range(CONV_K - 1, 0, -1)], axis=0)
            return acc
        o = g[{"a": "lconv_o", "c": "sconv_o"}[nm]]
        buf = g[{"a": "lconv_i", "c": "sconv_i"}[nm]][...].reshape((CONV_K - 1) * ns, width)
        buf = jnp.concatenate([buf, jnp.zeros((qt - (CONV_K - 1) * ns, width), F32)], axis=0)
        for m in range(1, CONV_K):
            prev = buf if m == 3 else pltpu.roll(buf, qt - (3 - m) * ns, 0)
            sh = jnp.where(t >= m, pltpu.roll(xraw, m * ns, 0), prev)
            acc = acc + vec[w0 + 3 - m:w0 + 4 - m, 0:width] * sh
        o[...] = xraw[(r - (CONV_K - 1)) * ns:, :].reshape(CONV_K - 1, ns, width)
        return acc

    def save_last(nm, h):
        if ns == 1:
            g["c_" + nm][...] = h[qt - 1:qt, :]
            g[nm + "_o"][...] = h[qt - 1:qt, :]
        else:
            g[nm + "_o"][...] = h[qt - ns:, :]

    def seq_mask(width):
        lane_seq = lax.broadcasted_iota(jnp.int32, (qt, ns * width), 1) // width
        row_seq = lax.broadcasted_iota(jnp.int32, (qt, ns * width), 0) % ns
        return (lane_seq == row_seq).astype(F32)

    gain = vec[V_GAIN:V_GAIN + 1, :]

    xc = conv(pcols(0, gw), "a", V_LCW, V_LCB)
    ga = pcols(gw, 2 * gw)
    pre = _mm(xc, g["wgate"][...]) + vec[V_BGATE:V_BGATE + 1, 0:2 * gw]
    rg = _sigmoid(pre[:, 0:gw])
    ig = _sigmoid(pre[:, gw:2 * gw])
    log_a = -LRU_C * rg * jax.nn.softplus(-vec[V_LAM:V_LAM + 1, 0:gw])
    a = jnp.exp(log_a)
    b = jnp.sqrt(-jnp.tanh(log_a) * (a * a + 1.0)) * (ig * xc)
    b = b + a * first_rows("c_lh", "lh_i")
    if ns == 1:
        a_s, b_s = slabs(a), slabs(b)
        hs, ps = [b_s[0]], [a_s[0]]
        for k in range(1, nk):
            hs.append(a_s[k] * hs[-1] + b_s[k])
            ps.append(a_s[k] * ps[-1])
        e, ae = hs[-1], ps[-1]
        for j in range(int(math.log2(SUBLANES))):
            d = 1 << j
            e = ae * sdown(e, d) + e
            ae = ae * sdown(ae, d, 1.0)
        cin = sdown(e, 1)
        b = jnp.concatenate([hs[k] + ps[k] * cin for k in range(nk)], axis=0)
    else:
        for k in range(nlev):
            d = 1 << k
            b = a * down(b, d) + b
            a = a * down(a, d, 1.0)
    save_last("lh", b)
    ya = _rms(b * jax.nn.gelu(ga)) * gain[:, 0:gw]

    u = pcols(2 * gw, 3 * gw)
    bu = _mm(u, g["bbm"][...])
    p_re, p_im = pw[0:1, :], pw[1:2, :]
    h0r, h0i = first_rows("c_s5r", "s5r_i"), first_rows("c_s5i", "s5i_i")
    hr = bu[:, 0:sn] + (p_re * h0r - p_im * h0i)
    hi = bu[:, sn:2 * sn] + (p_re * h0i + p_im * h0r)
    if ns == 1:
        pk = g["pk"]
        br_s, bi_s = slabs(hr), slabs(hi)
        hrs, his = [br_s[0]], [bi_s[0]]
        for k in range(1, nk):
            hrs.append(br_s[k] + (p_re * hrs[-1] - p_im * his[-1]))
            his.append(bi_s[k] + (p_re * his[-1] + p_im * hrs[-2]))
        er, ei = hrs[-1], his[-1]
        lev0 = int(math.log2(nk))
        for j in range(int(math.log2(SUBLANES))):
            d = 1 << j
            q_re, q_im = pw[2 * (lev0 + j):2 * (lev0 + j) + 1, :], pw[2 * (lev0 + j) + 1:2 * (lev0 + j) + 2, :]
            sr, si = sdown(er, d), sdown(ei, d)
            er, ei = er + (q_re * sr - q_im * si), ei + (q_re * si + q_im * sr)
        cr, ci = sdown(er, 1), sdown(ei, 1)
        hr = jnp.concatenate([hrs[k] + (pk[k:k + 1, :] * cr - pk[nk + k:nk + k + 1, :] * ci) for k in range(nk)], 0)
        hi = jnp.concatenate([his[k] + (pk[k:k + 1, :] * ci + pk[nk + k:nk + k + 1, :] * cr) for k in range(nk)], 0)
    else:
        for k in range(nlev):
            d = 1 << k
            q_re, q_im = pw[2 * k:2 * k + 1, :], pw[2 * k + 1:2 * k + 2, :]
            sr, si = down(hr, d), down(hi, d)
            hr, hi = hr + (q_re * sr - q_im * si), hi + (q_re * si + q_im * sr)
    save_last("s5r", hr)
    save_last("s5i", hi)
    yb = _mm(hr, g["cre"][...]) - _mm(hi, g["cim"][...])
    yb = jax.nn.gelu(yb + vec[V_S5D:V_S5D + 1, 0:gw] * u)
    yb = yb * _sigmoid(_mm(yb, g["wglu"][...]))
    yb = _rms(yb) * gain[:, gw:2 * gw]
    ysave(0, 2 * gw, jnp.concatenate([ya, yb], axis=1))

    tri = g["tri"][...] > 0.0
    xbc = conv(pcols(4 * gw, 6 * gw), "c", V_SCW, V_SCB)
    xbc = xbc * _sigmoid(xbc)
    xs, bm, cm = xbc[:, 0:gw], xbc[:, gw:gw + LANES], xbc[:, gw + LANES:2 * gw]
    dt = jax.nn.softplus(pcols(9 * gw, 10 * gw) + vec[V_DTB:V_DTB + 1, 0:gw])
    a_e = -jnp.exp(vec[V_ALOG:V_ALOG + 1, 0:gw])
    dta = dt * a_e
    if ns == 1:
        cs = slabs(dta)
        for k in range(1, nk):
            cs[k] = cs[k] + cs[k - 1]
        e = cs[-1]
        for j in range(int(math.log2(SUBLANES))):
            e = e + sdown(e, 1 << j)
        cin = sdown(e, 1)
        acum = jnp.concatenate([ck + cin for ck in cs], axis=0)
        alast = acum[qt - 1:qt, :]
    else:
        acum = dta
        for k in range(nlev):
            acum = acum + down(acum, 1 << k)
        suf = dta
        for k in range(nlev):
            suf = suf + up(suf, 1 << k)
        alast = acum + suf - dta
    wend = jnp.exp(alast - acum) * dt
    acum_t = jnp.transpose(acum)
    lane = lax.broadcasted_iota(jnp.int32, (1, gw), 1)
    lane_b = lax.broadcasted_iota(jnp.int32, (1, LANES), 1)
    heads_per_group = gw // SSD_HD // SSD_G
    xdt = xs * dt
    xw = xs * wend
    ydiag = jnp.zeros((qt, gw), F32)
    for grp in range(SSD_G):
        gm = ((lane_b // SSD_N) == grp).astype(F32)
        cb = _mm_nt(cm * gm, bm)
        for hh in range(heads_per_group):
            hd = grp * heads_per_group + hh
            col = acum[:, hd * SSD_HD:hd * SSD_HD + 1]
            rowv = acum_t[hd * SSD_HD:hd * SSD_HD + 1, :]
            dec = jnp.where(tri, jnp.exp(col - rowv), 0.0)
            hm = ((lane // SSD_HD) == hd).astype(F32)
            ydiag = ydiag + _mm(cb * dec, xdt * hm)
    if ns == 1:
        st = g["c_sh"][...]
        srow = lax.broadcasted_iota(jnp.int32, st.shape, 0) // SSD_N
        yoff = _mm(cm, st)
        slane = lax.broadcasted_iota(jnp.int32, st.shape, 1) // (SSD_HD * heads_per_group)
        upd = jnp.where(srow == slane, _mm_tn(bm, xw), 0.0)
        st = st * jnp.exp(acum[qt - 1:qt, :]) + upd
        g["c_sh"][...] = st

        @pl.when(c == nc - 1)
        def _final_ssd_state():
            own = jnp.concatenate([st[grp * SSD_N:(grp + 1) * SSD_N, grp * LANES:(grp + 1) * LANES]
                                   for grp in range(SSD_G)], axis=1)
            own = jnp.concatenate([own, jnp.zeros_like(own)], axis=0)
            g["sh_o"][...] = jnp.transpose(own)[:, 0:SSD_N].reshape(g["sh_o"].shape)
    else:
        sh_i, sh_o = g["sh_i"], g["sh_o"]
        per_seq = gw // SSD_HD * SSD_HD
        cols = [[], []]
        for s in range(0, ns, 2):
            two = jnp.concatenate([sh_i[s].reshape(per_seq, SSD_N), sh_i[s + 1].reshape(per_seq, SSD_N)], axis=1)
            two = jnp.transpose(two)
            for grp in range(SSD_G):
                cols[grp] += [two[0:SSD_N, grp * LANES:(grp + 1) * LANES],
                              two[SSD_N:2 * SSD_N, grp * LANES:(grp + 1) * LANES]]
        st = jnp.concatenate([jnp.concatenate(cols[0], axis=1), jnp.concatenate(cols[1], axis=1)], axis=0)
        srow = lax.broadcasted_iota(jnp.int32, st.shape, 0) // SSD_N
        seqm = seq_mask(LANES)
        lastm = jnp.where(t == r - 1, seqm, 0.0)
        yo, upd, dl = [], None, None
        for grp in range(SSD_G):
            gm = ((lane_b // SSD_N) == grp).astype(F32)
            z = _mm(cm * gm, st) * seqm
            zf = z[:, 0:LANES]
            for s in range(1, ns):
                zf = zf + z[:, s * LANES:(s + 1) * LANES]
            yo.append(zf)
            sl = slice(grp * LANES, (grp + 1) * LANES)
            u_g = _mm_tn(bm, jnp.tile(xw[:, sl], (1, ns)) * seqm)
            d_g = jnp.exp(jnp.sum(jnp.tile(acum[:, sl], (1, ns)) * lastm, axis=0, keepdims=True))
            upd = u_g if grp == 0 else jnp.where(srow == grp, u_g, upd)
            dl = d_g if grp == 0 else jnp.where(srow == grp, d_g, dl)
        yoff = jnp.concatenate(yo, axis=1)
        st = st * dl + upd
        for s in range(0, ns, 2):
            two = jnp.concatenate(
                [jnp.concatenate([st[0:SSD_N, q * LANES:(q + 1) * LANES], st[SSD_N:2 * SSD_N, q * LANES:(q + 1) * LANES]],
                                 axis=1) for q in (s, s + 1)], axis=0)
            two = jnp.transpose(two)
            sh_o[s] = two[:, 0:SSD_N].reshape(sh_o.shape[1:])
            sh_o[s + 1] = two[:, SSD_N:2 * SSD_N].reshape(sh_o.shape[1:])
    yc = ydiag + yoff * jnp.exp(acum) + vec[V_SSDD:V_SSDD + 1, 0:gw] * xs
    z = pcols(3 * gw, 4 * gw)
    yc = yc * (z * _sigmoid(z))
    ysave(2 * gw, 3 * gw, _rms(yc) * gain[:, 2 * gw:3 * gw])

    trig = g["trig"]
    cos, sin_a, sin_b = trig[0], trig[1], trig[2]

    def rope(x):
        return x * cos + pltpu.roll(x, LANES - RET_DK // 2, 1) * sin_a + pltpu.roll(x, RET_DK // 2, 1) * sin_b

    q = rope(pcols(6 * gw, 6 * gw + LANES))
    kk = rope(pcols(6 * gw + LANES, 7 * gw)) * (RET_DK ** -0.5)
    v = pcols(7 * gw, 8 * gw)
    dv = gw // RET_H
    intra = jnp.zeros((qt, gw), F32)
    for hd in range(RET_H):
        qm = ((lane_b // RET_DK) == hd).astype(F32)
        sc = _mm_nt(q * qm, kk) * g["dmask"][hd]
        vm = ((lane // dv) == hd).astype(F32)
        intra = intra + _mm(sc, v * vm)
    dec = g["dec"]
    qd, kd = q * dec[0], kk * dec[1]
    if ns == 1:
        rs = g["c_ret"][...]
        cross = _mm(qd, rs)
        rrow = lax.broadcasted_iota(jnp.int32, rs.shape, 0) // RET_DK
        rlane = lax.broadcasted_iota(jnp.int32, rs.shape, 1) // dv
        upd = jnp.where(rrow == rlane, _mm_tn(kd, v), 0.0)
    else:
        ret_i, ret_o = g["ret_i"], g["ret_o"]
        hk = RET_H * RET_DK
        rs = jnp.concatenate([ret_i[s].reshape(hk, dv) for s in range(ns)], axis=1)
        seqm = seq_mask(dv)
        lo = lane_b < dv
        pairs = []
        upd = jnp.zeros(rs.shape, F32)
        for pp in range(RET_H // 2):
            vp = v[:, pp * LANES:(pp + 1) * LANES]
            vsw = pltpu.roll(vp, dv, 1)
            tots = []
            for hh in range(2):
                qm = ((lane_b // RET_DK) == 2 * pp + hh).astype(F32)
                zc = _mm(qd * qm, rs) * seqm
                acc = zc[:, 0:LANES]
                for s2 in range(1, ns // 2):
                    acc = acc + zc[:, s2 * LANES:(s2 + 1) * LANES]
                tots.append(acc + pltpu.roll(acc, dv, 1))
                vh = jnp.where(lo, vp, vsw) if hh == 0 else jnp.where(lo, vsw, vp)
                upd = upd + _mm_tn(kd * qm, jnp.tile(vh, (1, ns // 2)) * seqm)
            pairs.append(jnp.where(lo, tots[0], tots[1]))
        cross = jnp.concatenate(pairs, axis=1)
    rs = rs * dec[2][:, 0:1] + upd
    if ns == 1:
        g["c_ret"][...] = rs

        @pl.when(c == nc - 1)
        def _final_ret_state():
            own = functools.reduce(lambda s, x: s + x, [rs[:, hd * dv:(hd + 1) * dv] for hd in range(RET_H)])
            g["ret_o"][...] = own.reshape(g["ret_o"].shape)
    else:
        for s in range(ns):
            ret_o[s] = rs[:, s * dv:(s + 1) * dv].reshape(ret_o.shape[1:])
    o = intra + cross
    o2 = o * o
    ms = jnp.zeros((qt, gw), F32)
    for hd in range(RET_H):
        vm = ((lane // dv) == hd).astype(F32)
        ms = ms + vm * jnp.sum(o2 * vm, axis=-1, keepdims=True)
    o = o * lax.rsqrt(ms * (1.0 / dv) + EPS) * gain[:, 3 * gw:4 * gw]
    gg = pcols(8 * gw, 9 * gw)
    ysave(3 * gw, 4 * gw, gg * _sigmoid(gg) * o)
    if ns == 1:
        y_all = jnp.concatenate(y_parts, axis=1)
        g["y"][...] = jnp.dot(g["perm"][1], y_all, preferred_element_type=F32).astype(g["y"].dtype)


def _mix_consts(qt, ns, r, nc, pos0):
    assert qt == LANES, "decay tables are stacked as (3, 128, 128)"
    idx = np.arange(qt)
    nk = qt // SUBLANES
    if ns == 1:
        seq, tt = np.zeros_like(idx), (idx % SUBLANES) * nk + idx // SUBLANES
    else:
        seq, tt = idx % ns, idx // ns
    causal = (seq[:, None] == seq[None, :]) & (tt[:, None] >= tt[None, :])
    tri = causal.astype(np.float32)
    lg = np.log1p(-np.exp2(-5.0 - np.arange(RET_H, dtype=np.float64)))
    rel = (tt[:, None] - tt[None, :]).astype(np.float64)
    dmask = np.where(causal[None], np.exp(np.maximum(rel, 0.0)[None] * lg[:, None, None]), 0.0).astype(np.float32)
    lane_h = np.arange(LANES) // RET_DK
    qdec = np.exp((tt[:, None] + 1.0) * lg[lane_h][None, :])
    kdec = np.exp((r - 1.0 - tt[:, None]) * lg[lane_h][None, :])
    rdec = np.broadcast_to(np.exp(r * lg[lane_h])[:, None], (LANES, LANES))
    dec = np.stack([qdec, kdec, rdec]).astype(np.float32)
    half = RET_DK // 2
    pos = pos0 + jnp.asarray((np.arange(nc)[:, None] * qt + tt[None, :]).reshape(-1))
    inv = ROPE_BASE ** (-jnp.arange(half, dtype=F32) / half)
    ang = pos.astype(F32)[:, None] * inv
    reps = LANES // half
    cos, sin = jnp.tile(jnp.cos(ang), (1, reps)), jnp.tile(jnp.sin(ang), (1, reps))
    first = (np.arange(LANES) % RET_DK) < half
    trig = jnp.stack([cos, jnp.where(first, -sin, 0.0), jnp.where(first, 0.0, sin)])
    to_blocked = np.zeros((qt, qt), np.float32)
    to_blocked[idx, (idx % SUBLANES) * nk + idx // SUBLANES] = 1.0
    perm = jnp.asarray(np.stack([to_blocked, to_blocked.T])).astype(MXU_DTYPE)
    return dict(trig=trig, tri=jnp.asarray(tri), dmask=jnp.asarray(dmask), dec=jnp.asarray(dec), perm=perm)


def _state_layout(depth, bsz, ns, gw, sn):
    hk, dv = RET_H * RET_DK, gw // RET_H
    if ns == 1:
        def per_seq(rows, w):
            return (depth, bsz, rows, w), (None, None, rows, w), lambda b, c, l: (l[0], b, 0, 0)
        def matrix(h, rows, w):
            return (depth, bsz, h, rows, w), (None, None, h, rows, w), lambda b, c, l: (l[0], b, 0, 0, 0)
        return dict(lconv=per_seq(CONV_K - 1, gw), lh=per_seq(1, gw), s5r=per_seq(1, sn), s5i=per_seq(1, sn),
                    sconv=per_seq(CONV_K - 1, 2 * gw), sh=matrix(gw // SSD_HD, SSD_HD, SSD_N),
                    ret=matrix(RET_H, RET_DK, dv))

    def flat(per, w):
        return (depth, bsz * per, w), (None, ns * per, w), lambda b, c, l: (l[0], b, 0)

    def conv(w):
        return (depth, CONV_K - 1, bsz, w), (None, CONV_K - 1, ns, w), lambda b, c, l: (l[0], 0, b, 0)
    def matrix(h, rows, w):
        return (depth, bsz, h, rows, w), (None, ns, h, rows, w), lambda b, c, l: (l[0], b, 0, 0, 0)
    return dict(lconv=conv(gw), lh=flat(1, gw), s5r=flat(1, sn), s5i=flat(1, sn), sconv=conv(2 * gw),
                sh=matrix(gw // SSD_HD, SSD_HD, SSD_N), ret=matrix(RET_H, RET_DK, dv))


def _mixer(l, proj, consts, mw, state_in, state_out, *, nb, nc, qt, ns, r, d):
    dp = proj.shape[-1]
    gw = d // 4
    sn = mw["pw"].shape[-1]
    depth = mw["pw"].shape[0]
    layout = _state_layout(depth, nb * ns, ns, gw, sn)
    assert (state_in is None) == (ns == 1)

    def layer_spec(a):
        return pl.BlockSpec((None,) + a.shape[1:], lambda b, c, l: (l[0],) + (0,) * (a.ndim - 1))

    def const_spec(a):
        return pl.BlockSpec(a.shape, lambda b, c, l: (0,) * a.ndim)

    def state_spec(nm):
        _, block, index = layout[nm]
        return pl.BlockSpec(block, index)

    if ns == 1:
        io_spec = lambda w: pl.BlockSpec((qt, w), lambda b, c, l: (b * nc + c, 0))
        y_shape = (nb * nc * qt, d)
        cnames = ("trig", "tri", "dmask", "dec", "perm")
    else:
        proj = proj.reshape(r, nb * ns, dp)
        io_spec = lambda w: pl.BlockSpec((r, ns, w), lambda b, c, l: (0, b, 0))
        y_shape = (r, nb * ns, d)
        cnames = ("trig", "tri", "dmask", "dec")
    names = ("proj",) + cnames + MIX_WEIGHTS
    operands = [proj] + [consts[k] for k in cnames] + [mw[k] for k in MIX_WEIGHTS]
    in_specs = [io_spec(dp), pl.BlockSpec((3, qt, LANES), lambda b, c, l: (0, c, 0))] + \
               [const_spec(consts[k]) for k in cnames[1:]] + [layer_spec(mw[k]) for k in MIX_WEIGHTS]
    if state_in is not None:
        names += tuple(nm + "_i" for nm in STATE_NAMES)
        operands += [state_in[nm] for nm in STATE_NAMES]
        in_specs += [state_spec(nm) for nm in STATE_NAMES]
    first_alias = 1 + len(operands)
    names += tuple(nm + "_alias" for nm in STATE_NAMES)
    operands += [state_out[nm] for nm in STATE_NAMES]
    in_specs += [pl.BlockSpec(memory_space=pl.ANY) for _ in STATE_NAMES]
    names += ("y",) + tuple(nm + "_o" for nm in STATE_NAMES)
    out_specs = [io_spec(d)] + [state_spec(nm) for nm in STATE_NAMES]
    out_shape = [jax.ShapeDtypeStruct(y_shape, MXU_DTYPE)] + \
                [jax.ShapeDtypeStruct(layout[nm][0], F32) for nm in STATE_NAMES]
    scratch = []
    if ns == 1:
        tail = (CONV_K - 1) * SUBLANES
        scratch_shapes = dict(cb_a=(tail, gw), cb_c=(tail, 2 * gw), c_lh=(1, gw), c_s5r=(1, sn),
                              c_s5i=(1, sn), c_sh=(SSD_G * SSD_N, gw), c_ret=(RET_H * RET_DK, gw))
        names += tuple(scratch_shapes)
        scratch = [pltpu.VMEM(shape, F32) for shape in scratch_shapes.values()]
    grid_spec = pltpu.PrefetchScalarGridSpec(num_scalar_prefetch=1, grid=(nb, nc),
                                             in_specs=in_specs, out_specs=out_specs, scratch_shapes=scratch)
    outs = pl.pallas_call(
        functools.partial(_mix_body, names=names, qt=qt, ns=ns, r=r, nc=nc),
        grid_spec=grid_spec,
        out_shape=out_shape,
        input_output_aliases={first_alias + k: 1 + k for k in range(len(STATE_NAMES))},
        compiler_params=pltpu.CompilerParams(dimension_semantics=("arbitrary", "arbitrary"),
                                             vmem_limit_bytes=VMEM_LIMIT),
        name="mixer",
    )(l, *operands)
    return outs[0].reshape(-1, d), dict(zip(STATE_NAMES, outs[1:]))


def _post_body(l_ref, y_ref, x_ref, g1_ref, sc_ref, sh_ref, g2_ref, lt_ref, wo_ref, wr_ref, br_ref,
               wg_ref, wu_ref, wd_ref, *rest):
    o_ref, xs_ref, cs_ref, os_ref = rest[-4:]
    del l_ref
    tm, d = x_ref.shape
    tms = xs_ref.shape[0]
    x = x_ref[...] + _token_rows(g1_ref[...], tm) * jnp.dot(y_ref[...], wo_ref[...], preferred_element_type=F32)
    h = _rms(x) * (1.0 + _token_rows(sc_ref[...], tm)) + _token_rows(sh_ref[...], tm)
    hb = h.astype(MXU_DTYPE)
    h_lo = (h - hb.astype(F32)).astype(MXU_DTYPE)
    wr = wr_ref[...]
    l12 = jnp.dot(hb, wr, preferred_element_type=F32)
    logits = (l12[:, 0:LANES] + l12[:, LANES:2 * LANES]
              + jnp.dot(h_lo, wr[:, 0:LANES], preferred_element_type=F32) + br_ref[...])
    logits_t = jnp.transpose(logits)
    col = [logits_t[k:k + 1, :] for k in range(N_GROUPS + N_EXPERTS)]

    def first_max(vals, allowed=None):
        neg = jnp.full_like(vals[0], -jnp.inf)
        cand = vals if allowed is None else [jnp.where(al > 0.0, v, neg) for v, al in zip(vals, allowed)]
        m = functools.reduce(jnp.maximum, cand)
        rem = jnp.ones_like(vals[0])
        hot = []
        for v in cand:
            f = jnp.where(v >= m, rem, 0.0)
            rem = rem - f
            hot.append(f)
        return hot, m

    grp, gmax = first_max(col[:N_GROUPS])
    gate = 1.0 / functools.reduce(lambda s, v: s + v, [jnp.exp(v - gmax) for v in col[:N_GROUPS]])
    le = [functools.reduce(lambda s, v: s + v,
                           [grp[gi] * col[N_GROUPS + gi * N_PER_GROUP + j] for gi in range(N_GROUPS)])
          for j in range(N_PER_GROUP)]
    top1, m1 = first_max(le)
    top2, m2 = first_max(le, [1.0 - f for f in top1])
    e2 = jnp.exp(m2 - m1)
    w1 = 1.0 / (1.0 + e2)
    w2 = e2 * w1
    wgrp = [(top1[j] * w1 + top2[j] * w2) * gate for j in range(N_PER_GROUP)]

    sub8 = lax.broadcasted_iota(jnp.int32, (SUBLANES, 1), 0)
    ghot_t = functools.reduce(lambda s, v: s + v, [jnp.where(sub8 == gi, grp[gi], 0.0) for gi in range(N_GROUPS)])
    before = _mm_nt(ghot_t, lt_ref[...])
    count = jnp.sum(ghot_t, axis=1, keepdims=True)
    cnt = [count[gi:gi + 1, :] for gi in range(N_GROUPS)]
    base = [jnp.zeros((1, 1), F32)]
    for gi in range(1, N_GROUPS):
        base.append(jnp.floor((base[-1] + cnt[gi - 1] + (PACK - 1.0)) * (1.0 / PACK)) * PACK)
    pos_row = functools.reduce(lambda s, v: s + v,
                               [grp[gi] * (base[gi] + before[gi:gi + 1, :]) for gi in range(N_GROUPS)])
    pos = jnp.transpose(jnp.broadcast_to(pos_row, (LANES, tm)))[:, 0:1]
    slot = lax.broadcasted_iota(jnp.int32, (tms, tm), 0).astype(F32)
    gather = jnp.where(slot == pos_row, 1.0, 0.0).astype(MXU_DTYPE)
    slot_l = lax.broadcasted_iota(jnp.int32, (tm, tms), 1).astype(F32)
    scatter = jnp.where(slot_l == pos, 1.0, 0.0).astype(MXU_DTYPE)

    xs_ref[...] = jnp.dot(gather, hb, preferred_element_type=F32).astype(xs_ref.dtype)
    sub_e = lax.broadcasted_iota(jnp.int32, (cs_ref.shape[1], 1), 0)
    comb_t = functools.reduce(lambda s, v: s + v,
                              [jnp.where(sub_e == e, grp[e // N_PER_GROUP] * wgrp[e % N_PER_GROUP], 0.0)
                               for e in range(N_EXPERTS)])
    c1, c2, c3 = _split3(comb_t)
    nt = lambda p: lax.dot_general(gather, p, (((1,), (1,)), ((), ())), preferred_element_type=F32)
    cs_ref[...] = nt(c1) + nt(c2) + nt(c3)
    os_ref[...] = jnp.zeros(os_ref.shape, os_ref.dtype)
    for gi in range(N_GROUPS):
        start = base[gi][0, 0].astype(jnp.int32)
        nblk = jnp.floor((cnt[gi][0, 0] + (EXPERT_ROWS - 1.0)) * (1.0 / EXPERT_ROWS)).astype(jnp.int32)

        def block(k, carry, gi=gi, start=start):
            rows = pl.ds(pl.multiple_of(start + k * EXPERT_ROWS, PACK), EXPERT_ROWS)
            xb = xs_ref[rows, :]
            cb = cs_ref[rows, :]
            acc = jnp.zeros((EXPERT_ROWS, d), F32)
            for j in range(N_PER_GROUP):
                e = gi * N_PER_GROUP + j
                gt = jnp.dot(xb, wg_ref[e], preferred_element_type=F32)
                he = gt * _sigmoid(gt) * jnp.dot(xb, wu_ref[e], preferred_element_type=F32)
                acc = acc + jnp.dot((he * cb[:, e:e + 1]).astype(MXU_DTYPE), wd_ref[e], preferred_element_type=F32)
            os_ref[rows, :] += acc
            return carry

        lax.fori_loop(0, nblk, block, 0)
    moe = _select_rows(scatter, os_ref[...], pieces=1)
    out = x + _token_rows(g2_ref[...], tm) * moe
    if len(rest) == 5:
        out = _rms(out) * rest[0][...]
    o_ref[...] = out


def _post(l, y, x, mod, pw, tm, rows_per_seq, final_w=None, in_place=False):
    t, d = x.shape
    final = [] if final_w is None else [final_w.reshape(1, d)]
    tms = tm + EXPERT_ROWS + PACK * N_GROUPS
    ltri = jnp.asarray(np.tril(np.ones((tm, tm), np.float32), -1)).astype(MXU_DTYPE)

    def layer_spec(a):
        return pl.BlockSpec((None,) + a.shape[1:], lambda i, l: (l[0],) + (0,) * (a.ndim - 1),
                            pipeline_mode=pl.Buffered(1))

    grid_spec = pltpu.PrefetchScalarGridSpec(
        num_scalar_prefetch=1,
        grid=(t // tm,),
        in_specs=[
            pl.BlockSpec((tm, d), lambda i, l: (i, 0)),
            pl.BlockSpec((tm, d), lambda i, l: (i, 0)),
            _mod_spec(mod, 2, tm, rows_per_seq),
            _mod_spec(mod, 4, tm, rows_per_seq),
            _mod_spec(mod, 3, tm, rows_per_seq),
            _mod_spec(mod, 5, tm, rows_per_seq),
            pl.BlockSpec((tm, tm), lambda i, l: (0, 0)),
        ] + [layer_spec(pw[k]) for k in ("w_out", "w_route", "b_route", "w_gate", "w_up", "w_down")]
          + [pl.BlockSpec((1, d), lambda i, l: (0, 0)) for _ in final],
        out_specs=pl.BlockSpec((tm, d), lambda i, l: (i, 0)),
        scratch_shapes=[pltpu.VMEM((tms, d), MXU_DTYPE), pltpu.VMEM((tms, 2 * N_EXPERTS), F32),
                        pltpu.VMEM((tms, d), F32)],
    )
    return pl.pallas_call(
        _post_body,
        grid_spec=grid_spec,
        out_shape=jax.ShapeDtypeStruct((t, d), F32),
        input_output_aliases={2: 0} if in_place else {},
        compiler_params=pltpu.CompilerParams(dimension_semantics=("arbitrary",), vmem_limit_bytes=VMEM_LIMIT),
        name="post",
    )(l, y, x, mod, mod, mod, mod, ltri,
      *[pw[k] for k in ("w_out", "w_route", "b_route", "w_gate", "w_up", "w_down")], *final)


def _block_diag(w):
    depth, nblk, bi, bj = w.shape
    eye = jnp.eye(nblk, dtype=w.dtype)
    return (w[:, :, :, None, :] * eye[None, :, None, :, None]).reshape(depth, nblk * bi, nblk * bj)


def _pad_lanes(a, width):
    return jnp.pad(a, [(0, 0)] * (a.ndim - 1) + [(0, width - a.shape[-1])])


def _prep_weights(w_in, lru_conv_w, lru_conv_b, lru_wa, lru_ba, lru_wx, lru_bx, lru_lambda,
                  s5_a_re, s5_a_im, s5_b_re, s5_b_im, s5_c_re, s5_c_im, s5_d, s5_log_dt, s5_w_glu,
                  ssd_conv_w, ssd_conv_b, ssd_dt_bias, ssd_a_log, ssd_d, mix_norm, qt):
    depth, d, _ = w_in.shape
    gw = d // 4
    n_ssd_h = gw // SSD_HD
    xbc_end = 4 * gw + gw + 2 * SSD_G * SSD_N
    dt_cols = jnp.repeat(w_in[..., xbc_end:xbc_end + n_ssd_h], SSD_HD, axis=-1)
    w_in_p = jnp.concatenate([w_in[..., :xbc_end], w_in[..., xbc_end + n_ssd_h:], dt_cols], axis=-1)

    def row(a):
        return _pad_lanes(a.reshape(depth, 1, -1), d)

    vec = jnp.concatenate([
        row(lru_conv_b), row(jnp.concatenate([lru_ba, lru_bx], -1)), row(lru_lambda), row(s5_d),
        row(ssd_conv_b), row(jnp.repeat(ssd_dt_bias, SSD_HD, -1)), row(jnp.repeat(ssd_a_log, SSD_HD, -1)),
        row(jnp.repeat(ssd_d, SSD_HD, -1)), row(mix_norm),
        _pad_lanes(lru_conv_w, d), _pad_lanes(ssd_conv_w, d),
        jnp.zeros((depth, V_ROWS - V_SCW - CONV_K, d), F32)], axis=1)

    wgate = jnp.concatenate([_block_diag(lru_wa), _block_diag(lru_wx)], axis=-1)

    dt = jnp.exp(s5_log_dt)[..., None]
    lr, li = s5_a_re, s5_a_im
    mag = jnp.exp(lr * dt)
    ab_re, ab_im = mag * jnp.cos(li * dt), mag * jnp.sin(li * dt)
    den = lr * lr + li * li
    q_re = ((ab_re - 1.0) * lr + ab_im * li) / den
    q_im = (ab_im * lr - (ab_re - 1.0) * li) / den
    bb_re = q_re[..., None] * s5_b_re - q_im[..., None] * s5_b_im
    bb_im = q_re[..., None] * s5_b_im + q_im[..., None] * s5_b_re
    bbm = jnp.concatenate([_block_diag(jnp.swapaxes(bb_re, -1, -2)), _block_diag(jnp.swapaxes(bb_im, -1, -2))], -1)
    cre = _block_diag(jnp.swapaxes(s5_c_re, -1, -2))
    cim = _block_diag(jnp.swapaxes(s5_c_im, -1, -2))
    a_re, a_im = ab_re.reshape(depth, -1), ab_im.reshape(depth, -1)
    pr, pi = a_re, a_im
    rows = []
    for _ in range(int(math.log2(qt))):
        rows += [pr, pi]
        pr, pi = pr * pr - pi * pi, 2.0 * pr * pi
    pw = jnp.stack(rows, axis=1)
    pw = jnp.pad(pw, ((0, 0), (0, (-pw.shape[1]) % SUBLANES), (0, 0)))
    pr, pi = a_re, a_im
    k_re, k_im = [], []
    for _ in range(qt // SUBLANES):
        k_re.append(pr)
        k_im.append(pi)
        pr, pi = pr * a_re - pi * a_im, pr * a_im + pi * a_re
    pk = jnp.stack(k_re + k_im, axis=1)

    mw = dict(vec=vec, wgate=wgate.astype(MXU_DTYPE), bbm=bbm.astype(MXU_DTYPE), cre=cre.astype(MXU_DTYPE),
              cim=cim.astype(MXU_DTYPE), wglu=s5_w_glu.astype(MXU_DTYPE), pw=pw, pk=pk)
    return w_in_p.astype(MXU_DTYPE), mw


def _states_to_layout(states, layout, ns):
    out = {}
    for nm, s in zip(STATE_NAMES, states):
        if nm in ("lconv", "sconv"):
            s = s.transpose(0, 2, 1, 3)
        out[nm] = s.reshape(layout[nm][0])
    return out


def _states_from_layout(arrs, ns, ref_shapes):
    out = []
    for nm, shape in zip(STATE_NAMES, ref_shapes):
        a = arrs[nm]
        if ns > 1 and nm in ("lconv", "sconv"):
            a = a.transpose(0, 2, 1, 3)
        out.append(a.reshape(shape))
    return tuple(out)


def _trunk(x, mod, states, ref_shapes, pos0, w_in_p, mw, pw, final_norm, *, qt, ns, tm):
    bsz, seq_len, d = x.shape
    depth = w_in_p.shape[0]
    gw = d // 4
    r = qt // ns
    nc = seq_len // r if ns == 1 else 1
    assert (ns == 1 and seq_len % qt == 0 and states is None) or (ns > 1 and r == seq_len and bsz % ns == 0)
    assert r >= CONV_K and (r & (r - 1)) == 0
    nb = bsz // ns
    tok = bsz * seq_len
    consts = _mix_consts(qt, ns, r, nc, pos0)
    layout = _state_layout(depth, bsz, ns, gw, mw["pw"].shape[-1])
    state_in = None if states is None else _states_to_layout(states, layout, ns)
    state_out = {nm: jnp.zeros(layout[nm][0], F32) for nm in STATE_NAMES}
    x2 = x.reshape(tok, d) if ns == 1 else x.transpose(1, 0, 2).reshape(tok, d)
    blocked_chunk = qt if ns == 1 else None

    def layer(carry, l, final_w=None, in_place=True):
        xc, st = carry
        lv = jnp.reshape(l, (1,)).astype(jnp.int32)
        proj = _inproj(lv, xc, mod, w_in_p, tm, seq_len, blocked_chunk)
        y, st = _mixer(lv, proj, consts, mw, state_in, st, nb=nb, nc=nc, qt=qt, ns=ns, r=r, d=d)
        xn = _post(lv, y, xc, mod, pw, tm, seq_len, final_w, in_place)
        return (xn, st), None

    assert depth >= 2
    carry, _ = layer((x2, state_out), jnp.int32(0), in_place=False)
    carry, _ = lax.scan(layer, carry, jnp.arange(1, depth - 1))
    (y, st), _ = layer(carry, jnp.int32(depth - 1), final_norm)
    y = y.reshape(bsz, seq_len, d) if ns == 1 else y.reshape(seq_len, bsz, d).transpose(1, 0, 2)
    return y, _states_from_layout(st, ns, ref_shapes)


def kernel(x_prompt, x_sample, state_lru_conv, state_lru_h, state_s5_re, state_s5_im, state_ssd_conv, state_ssd_h, state_ret, c_prompt, c_sample, w_ada, b_ada, w_in, lru_conv_w, lru_conv_b, lru_wa, lru_ba, lru_wx, lru_bx, lru_lambda, s5_a_re, s5_a_im, s5_b_re, s5_b_im, s5_c_re, s5_c_im, s5_d, s5_log_dt, s5_w_glu, ssd_conv_w, ssd_conv_b, ssd_dt_bias, ssd_a_log, ssd_d, mix_norm, w_out, w_route_group, b_route_group, w_route_exp, b_route_exp, w_exp_gate, w_exp_up, w_exp_down, final_norm):
    bp, lp, d = x_prompt.shape
    bs, ls, _ = x_sample.shape
    depth = w_in.shape[0]
    qt = LANES
    ns_s = qt // ls

    mod_p, mod_s = _modulation(jnp.concatenate([c_prompt, c_sample], axis=0), bp, w_ada, b_ada)
    mod_p = mod_p.reshape(depth, N_MOD, bp, 1, d)

    w_in_p, mw = _prep_weights(w_in, lru_conv_w, lru_conv_b, lru_wa, lru_ba, lru_wx, lru_bx, lru_lambda,
                               s5_a_re, s5_a_im, s5_b_re, s5_b_im, s5_c_re, s5_c_im, s5_d, s5_log_dt, s5_w_glu,
                               ssd_conv_w, ssd_conv_b, ssd_dt_bias, ssd_a_log, ssd_d, mix_norm, qt)
    w_route = _pad_lanes(jnp.concatenate([w_route_group, w_route_exp.reshape(depth, d, N_EXPERTS)], -1), LANES)
    b_route = _pad_lanes(jnp.concatenate([b_route_group, b_route_exp.reshape(depth, N_EXPERTS)], -1), LANES)
    w_route_hi = w_route.astype(MXU_DTYPE)
    w_route_lo = (w_route - w_route_hi.astype(F32)).astype(MXU_DTYPE)
    w_route = jnp.concatenate([w_route_hi, w_route_lo], axis=-1)
    pw = dict(w_out=w_out.astype(MXU_DTYPE), w_route=w_route, b_route=b_route.reshape(depth, 1, LANES),
              w_gate=w_exp_gate.astype(MXU_DTYPE), w_up=w_exp_up.astype(MXU_DTYPE),
              w_down=w_exp_down.astype(MXU_DTYPE))
    w_in_p, mw, pw = lax.optimization_barrier((w_in_p, mw, pw))

    states_s = (state_lru_conv, state_lru_h, state_s5_re, state_s5_im, state_ssd_conv, state_ssd_h, state_ret)
    shapes_s = [s.shape for s in states_s]
    shapes_p = [(depth, bp) + s[2:] for s in shapes_s]
    y_p, new_p = _trunk(x_prompt, mod_p, None, shapes_p, 0, w_in_p, mw, pw, final_norm,
                        qt=qt, ns=1, tm=min(512, lp))
    y_s, new_s = _trunk(x_sample, mod_s, states_s, shapes_s, PAST_LEN, w_in_p, mw, pw, final_norm,
                        qt=qt, ns=ns_s, tm=bs * ls)
    out = [y_p, y_s]
    for a, b in zip(new_p, new_s):
        out += [a, b]
    return tuple(out)
```

```python
import functools
import math

import numpy as np
import jax
import jax.numpy as jnp
from jax import lax
from jax.experimental import pallas as pl
from jax.experimental.pallas import tpu as pltpu

F32 = jnp.float32
MXU_DTYPE = jnp.bfloat16
HIGHEST = lax.Precision.HIGHEST

EPS = 1e-6
CONV_K = 4
LRU_C = 8.0
SSD_HD = 64
SSD_G = 2
SSD_N = 64
RET_H = 4
RET_DK = 32
ROPE_BASE = 10000.0
N_GROUPS = 4
N_PER_GROUP = 4
N_EXPERTS = N_GROUPS * N_PER_GROUP
N_MOD = 6
PAST_LEN = 16384

LANES = 128
SUBLANES = 8
PACK = 16
VMEM_LIMIT = 56 * 1024 * 1024
EXPERT_ROWS = 144


def _mm(a, b):
    return jnp.dot(a.astype(MXU_DTYPE), b.astype(MXU_DTYPE), preferred_element_type=F32)


def _mm_nt(a, b):
    return lax.dot_general(a.astype(MXU_DTYPE), b.astype(MXU_DTYPE), (((1,), (1,)), ((), ())),
                           preferred_element_type=F32)


def _mm_tn(a, b):
    return lax.dot_general(a.astype(MXU_DTYPE), b.astype(MXU_DTYPE), (((0,), (0,)), ((), ())),
                           preferred_element_type=F32)


def _split3(x):
    x1 = x.astype(MXU_DTYPE)
    r1 = x - x1.astype(F32)
    x2 = r1.astype(MXU_DTYPE)
    x3 = (r1 - x2.astype(F32)).astype(MXU_DTYPE)
    return x1, x2, x3


def _select_rows(onehot, x, pieces=3):
    parts = _split3(x)[:pieces]
    out = jnp.dot(onehot, parts[0], preferred_element_type=F32)
    for p in parts[1:]:
        out = out + jnp.dot(onehot, p, preferred_element_type=F32)
    return out


def _sigmoid(x):
    return 0.5 * jnp.tanh(0.5 * x) + 0.5


def _rms(x):
    return x * lax.rsqrt(jnp.mean(x * x, axis=-1, keepdims=True) + EPS)


def _token_rows(v, tm):
    n = v.shape[0]
    return v if n in (1, tm) else jnp.tile(v, (tm // n, 1))


def _mod_body(c_ref, w_ref, b_ref, first_ref, rest_ref):
    c = c_ref[...]
    m = _mm(c * _sigmoid(c), w_ref[...]) + b_ref[...]
    n_first = first_ref.shape[0]
    first_ref[...] = m[0:n_first]
    rest_ref[...] = m[n_first:]


def _modulation(c_all, n_first, w_ada, b_ada):
    depth, d, _ = w_ada.shape
    nb = c_all.shape[0]
    return pl.pallas_call(
        _mod_body,
        grid=(depth, N_MOD),
        in_specs=[
            pl.BlockSpec((nb, d), lambda l, k: (0, 0)),
            pl.BlockSpec((None, d, d), lambda l, k: (l, 0, k)),
            pl.BlockSpec((None, None, 1, d), lambda l, k: (l, k, 0, 0)),
        ],
        out_specs=[pl.BlockSpec((None, None, n_first, d), lambda l, k: (l, k, 0, 0)),
                   pl.BlockSpec((None, None, nb - n_first, d), lambda l, k: (l, k, 0, 0))],
        out_shape=[jax.ShapeDtypeStruct((depth, N_MOD, n_first, d), F32),
                   jax.ShapeDtypeStruct((depth, N_MOD, nb - n_first, d), F32)],
        compiler_params=pltpu.CompilerParams(dimension_semantics=("arbitrary", "arbitrary"),
                                             vmem_limit_bytes=VMEM_LIMIT),
        name="modulation",
    )(c_all, w_ada, b_ada.reshape(depth, N_MOD, 1, d))


def _mod_spec(mod, k, tm, rows_per_seq):
    if mod.ndim == 5:
        tiles_per_seq = rows_per_seq // tm
        return pl.BlockSpec((None, None, None, 1, mod.shape[-1]),
                            lambda i, l: (l[0], k, i // tiles_per_seq, 0, 0))
    return pl.BlockSpec((None, None) + mod.shape[2:], lambda i, l: (l[0], k, 0, 0))


def _inproj_body(l_ref, x_ref, sc_ref, sh_ref, w_ref, o_ref, *, blocked_chunk):
    tm = x_ref.shape[0]
    h = _rms(x_ref[...]) * (1.0 + _token_rows(sc_ref[...], tm)) + _token_rows(sh_ref[...], tm)
    hb = h.astype(MXU_DTYPE)
    if blocked_chunk is not None:
        qt, nk = blocked_chunk, blocked_chunk // SUBLANES
        r_i = lax.broadcasted_iota(jnp.int32, (qt, qt), 0)
        c_i = lax.broadcasted_iota(jnp.int32, (qt, qt), 1)
        to_blocked = jnp.where(c_i == (r_i % SUBLANES) * nk + r_i // SUBLANES, 1.0, 0.0).astype(MXU_DTYPE)
        hb = jnp.concatenate([jnp.dot(to_blocked, hb[j * qt:(j + 1) * qt], preferred_element_type=F32)
                              for j in range(tm // qt)], axis=0).astype(MXU_DTYPE)
    o_ref[...] = jnp.dot(hb, w_ref[...], preferred_element_type=F32)


def _inproj(l, x, mod, w_in, tm, rows_per_seq, blocked_chunk=None):
    t, d = x.shape
    dp = w_in.shape[-1]
    assert blocked_chunk is None or (mod.ndim == 5 and tm % blocked_chunk == 0)
    grid_spec = pltpu.PrefetchScalarGridSpec(
        num_scalar_prefetch=1,
        grid=(t // tm,),
        in_specs=[
            pl.BlockSpec((tm, d), lambda i, l: (i, 0)),
            _mod_spec(mod, 1, tm, rows_per_seq),
            _mod_spec(mod, 0, tm, rows_per_seq),
            pl.BlockSpec((None, d, dp), lambda i, l: (l[0], 0, 0)),
        ],
        out_specs=pl.BlockSpec((tm, dp), lambda i, l: (i, 0)),
    )
    return pl.pallas_call(
        functools.partial(_inproj_body, blocked_chunk=blocked_chunk),
        grid_spec=grid_spec,
        out_shape=jax.ShapeDtypeStruct((t, dp), F32),
        compiler_params=pltpu.CompilerParams(dimension_semantics=("arbitrary",), vmem_limit_bytes=VMEM_LIMIT),
        name="inproj",
    )(l, x, mod, mod, w_in)


V_LCB, V_BGATE, V_LAM, V_S5D, V_SCB, V_DTB, V_ALOG, V_SSDD, V_GAIN, V_LCW, V_SCW = 0, 1, 2, 3, 4, 5, 6, 7, 8, 9, 13
V_ROWS = 24
STATE_NAMES = ("lconv", "lh", "s5r", "s5i", "sconv", "sh", "ret")
MIX_WEIGHTS = ("vec", "wgate", "bbm", "cre", "cim", "wglu", "pw", "pk")


def _mix_body(l_ref, *refs, names, qt, ns, r, nc):
    del l_ref
    g = dict(zip(names, refs))
    gw = g["lh_o"].shape[-1]
    sn = g["s5r_o"].shape[-1]
    nlev = int(math.log2(r))
    nk = qt // SUBLANES
    c = pl.program_id(1)

    if ns == 1:
        @pl.when(c == 0)
        def _zero_carries():
            for nm in ("c_lh", "c_s5r", "c_s5i", "c_sh", "c_ret"):
                g[nm][...] = jnp.zeros(g[nm].shape, F32)
            for nm in ("cb_a", "cb_c"):
                g[nm][...] = jnp.zeros(g[nm].shape, F32)

    proj = g["proj"]
    vec = g["vec"]
    pw = g["pw"]

    def pcols(a, b):
        v = proj[..., a:b]
        return v if ns == 1 else v.reshape(qt, b - a)

    y_parts = []

    def ysave(a, b, val):
        val = val.astype(g["y"].dtype)
        if ns == 1:
            y_parts.append(val)
        else:
            g["y"][..., a:b] = val.reshape(r, ns, b - a)

    row = lax.broadcasted_iota(jnp.int32, (qt, 1), 0)
    t = (row % SUBLANES) * nk + row // SUBLANES if ns == 1 else row // ns
    sub = lax.broadcasted_iota(jnp.int32, (SUBLANES, 1), 0)

    def down(x, d, fill=0.0):
        return jnp.where(t >= d, pltpu.roll(x, d * ns, 0), fill)

    def up(x, d):
        return jnp.where(t + d < r, pltpu.roll(x, qt - d * ns, 0), 0.0)

    def sdown(x, d, fill=0.0):
        return jnp.where(sub >= d, pltpu.roll(x, d, 0), fill)

    def slabs(x):
        return [x[SUBLANES * k:SUBLANES * (k + 1)] for k in range(nk)]

    def first_rows(carry, init):
        if ns == 1:
            return jnp.where(row == 0, g[carry][...], 0.0)
        h0 = g[init][...]
        return jnp.concatenate([h0, jnp.zeros((qt - ns, h0.shape[1]), F32)], axis=0)

    def conv(xraw, nm, w0, b):
        width = xraw.shape[-1]
        acc = vec[b:b + 1, 0:width] + vec[w0 + 3:w0 + 4, 0:width] * xraw
        if ns == 1:
            cb = g["cb_" + nm]
            xs_k = slabs(xraw)
            wrap = [jnp.where(sub >= 1, pltpu.roll(xs_k[nk - j], 1, 0),
                              pltpu.roll(cb[(CONV_K - 1 - j) * SUBLANES:(CONV_K - j) * SUBLANES, :], 1, 0))
                    for j in range(1, CONV_K)]
            for m in range(1, CONV_K):
                sh = jnp.concatenate([wrap[m - k - 1] for k in range(m)] + xs_k[0:nk - m], axis=0)
                acc = acc + vec[w0 + 3 - m:w0 + 4 - m, 0:width] * sh
            cb[...] = xraw[qt - (CONV_K - 1) * SUBLANES:, :]
            g[{"a": "lconv_o", "c": "sconv_o"}[nm]][...] = jnp.concatenate(
                [xs_k[nk - j][SUBLANES - 1:SUBLANES, :] for j in range(CONV_K - 1, 0, -1)], axis=0)
            return acc
        o = g[{"a": "lconv_o", "c": "sconv_o"}[nm]]
        buf = g[{"a": "lconv_i", "c": "sconv_i"}[nm]][...].reshape((CONV_K - 1) * ns, width)
        buf = jnp.concatenate([buf, jnp.zeros((qt - (CONV_K - 1) * ns, width), F32)], axis=0)
        for m in range(1, CONV_K):
            prev = buf if m == 3 else pltpu.roll(buf, qt - (3 - m) * ns, 0)
            sh = jnp.where(t >= m, pltpu.roll(xraw, m * ns, 0), prev)
            acc = acc + vec[w0 + 3 - m:w0 + 4 - m, 0:width] * sh
        o[...] = xraw[(r - (CONV_K - 1)) * ns:, :].reshape(CONV_K - 1, ns, width)
        return acc

    def save_last(nm, h):
        if ns == 1:
            g["c_" + nm][...] = h[qt - 1:qt, :]
            g[nm + "_o"][...] = h[qt - 1:qt, :]
        else:
            g[nm + "_o"][...] = h[qt - ns:, :]

    def seq_mask(width):
        lane_seq = lax.broadcasted_iota(jnp.int32, (qt, ns * width), 1) // width
        row_seq = lax.broadcasted_iota(jnp.int32, (qt, ns * width), 0) % ns
        return (lane_seq == row_seq).astype(F32)

    gain = vec[V_GAIN:V_GAIN + 1, :]

    xc = conv(pcols(0, gw), "a", V_LCW, V_LCB)
    ga = pcols(gw, 2 * gw)
    pre = _mm(xc, g["wgate"][...]) + vec[V_BGATE:V_BGATE + 1, 0:2 * gw]
    rg = _sigmoid(pre[:, 0:gw])
    ig = _sigmoid(pre[:, gw:2 * gw])
    log_a = -LRU_C * rg * jax.nn.softplus(-vec[V_LAM:V_LAM + 1, 0:gw])
    a = jnp.exp(log_a)
    b = jnp.sqrt(-jnp.tanh(log_a) * (a * a + 1.0)) * (ig * xc)
    b = b + a * first_rows("c_lh", "lh_i")
    if ns == 1:
        a_s, b_s = slabs(a), slabs(b)
        hs, ps = [b_s[0]], [a_s[0]]
        for k in range(1, nk):
            hs.append(a_s[k] * hs[-1] + b_s[k])
            ps.append(a_s[k] * ps[-1])
        e, ae = hs[-1], ps[-1]
        for j in range(int(math.log2(SUBLANES))):
            d = 1 << j
            e = ae * sdown(e, d) + e
            ae = ae * sdown(ae, d, 1.0)
        cin = sdown(e, 1)
        b = jnp.concatenate([hs[k] + ps[k] * cin for k in range(nk)], axis=0)
    else:
        for k in range(nlev):
            d = 1 << k
            b = a * down(b, d) + b
            a = a * down(a, d, 1.0)
    save_last("lh", b)
    ya = _rms(b * jax.nn.gelu(ga)) * gain[:, 0:gw]

    u = pcols(2 * gw, 3 * gw)
    bu = _mm(u, g["bbm"][...])
    p_re, p_im = pw[0:1, :], pw[1:2, :]
    h0r, h0i = first_rows("c_s5r", "s5r_i"), first_rows("c_s5i", "s5i_i")
    hr = bu[:, 0:sn] + (p_re * h0r - p_im * h0i)
    hi = bu[:, sn:2 * sn] + (p_re * h0i + p_im * h0r)
    if ns == 1:
        pk = g["pk"]
        br_s, bi_s = slabs(hr), slabs(hi)
        hrs, his = [br_s[0]], [bi_s[0]]
        for k in range(1, nk):
            hrs.append(br_s[k] + (p_re * hrs[-1] - p_im * his[-1]))
            his.append(bi_s[k] + (p_re * his[-1] + p_im * hrs[-2]))
        er, ei = hrs[-1], his[-1]
        lev0 = int(math.log2(nk))
        for j in range(int(math.log2(SUBLANES))):
            d = 1 << j
            q_re, q_im = pw[2 * (lev0 + j):2 * (lev0 + j) + 1, :], pw[2 * (lev0 + j) + 1:2 * (lev0 + j) + 2, :]
            sr, si = sdown(er, d), sdown(ei, d)
            er, ei = er + (q_re * sr - q_im * si), ei + (q_re * si + q_im * sr)
        cr, ci = sdown(er, 1), sdown(ei, 1)
        hr = jnp.concatenate([hrs[k] + (pk[k:k + 1, :] * cr - pk[nk + k:nk + k + 1, :] * ci) for k in range(nk)], 0)
        hi = jnp.concatenate([his[k] + (pk[k:k + 1, :] * ci + pk[nk + k:nk + k + 1, :] * cr) for k in range(nk)], 0)
    else:
        for k in range(nlev):
            d = 1 << k
            q_re, q_im = pw[2 * k:2 * k + 1, :], pw[2 * k + 1:2 * k + 2, :]
            sr, si = down(hr, d), down(hi, d)
            hr, hi = hr + (q_re * sr - q_im * si), hi + (q_re * si + q_im * sr)
    save_last("s5r", hr)
    save_last("s5i", hi)
    yb = _mm(hr, g["cre"][...]) - _mm(hi, g["cim"][...])
    yb = jax.nn.gelu(yb + vec[V_S5D:V_S5D + 1, 0:gw] * u)
    yb = yb * _sigmoid(_mm(yb, g["wglu"][...]))
    yb = _rms(yb) * gain[:, gw:2 * gw]
    ysave(0, 2 * gw, jnp.concatenate([ya, yb], axis=1))

    tri = g["tri"][...] > 0.0
    xbc = conv(pcols(4 * gw, 6 * gw), "c", V_SCW, V_SCB)
    xbc = xbc * _sigmoid(xbc)
    xs, bm, cm = xbc[:, 0:gw], xbc[:, gw:gw + LANES], xbc[:, gw + LANES:2 * gw]
    dt = jax.nn.softplus(pcols(9 * gw, 10 * gw) + vec[V_DTB:V_DTB + 1, 0:gw])
    a_e = -jnp.exp(vec[V_ALOG:V_ALOG + 1, 0:gw])
    dta = dt * a_e
    if ns == 1:
        cs = slabs(dta)
        for k in range(1, nk):
            cs[k] = cs[k] + cs[k - 1]
        e = cs[-1]
        for j in range(int(math.log2(SUBLANES))):
            e = e + sdown(e, 1 << j)
        cin = sdown(e, 1)
        acum = jnp.concatenate([ck + cin for ck in cs], axis=0)
        alast = acum[qt - 1:qt, :]
    else:
        acum = dta
        for k in range(nlev):
            acum = acum + down(acum, 1 << k)
        suf = dta
        for k in range(nlev):
            suf = suf + up(suf, 1 << k)
        alast = acum + suf - dta
    wend = jnp.exp(alast - acum) * dt
    acum_t = jnp.transpose(acum)
    lane = lax.broadcasted_iota(jnp.int32, (1, gw), 1)
    lane_b = lax.broadcasted_iota(jnp.int32, (1, LANES), 1)
    heads_per_group = gw // SSD_HD // SSD_G
    xdt = xs * dt
    xw = xs * wend
    ydiag = jnp.zeros((qt, gw), F32)
    for grp in range(SSD_G):
        gm = ((lane_b // SSD_N) == grp).astype(F32)
        cb = _mm_nt(cm * gm, bm)
        for hh in range(heads_per_group):
            hd = grp * heads_per_group + hh
            col = acum[:, hd * SSD_HD:hd * SSD_HD + 1]
            rowv = acum_t[hd * SSD_HD:hd * SSD_HD + 1, :]
            dec = jnp.where(tri, jnp.exp(col - rowv), 0.0)
            hm = ((lane // SSD_HD) == hd).astype(F32)
            ydiag = ydiag + _mm(cb * dec, xdt * hm)
    if ns == 1:
        st = g["c_sh"][...]
        srow = lax.broadcasted_iota(jnp.int32, st.shape, 0) // SSD_N
        yoff = _mm(cm, st)
        slane = lax.broadcasted_iota(jnp.int32, st.shape, 1) // (SSD_HD * heads_per_group)
        upd = jnp.where(srow == slane, _mm_tn(bm, xw), 0.0)
        st = st * jnp.exp(acum[qt - 1:qt, :]) + upd
        g["c_sh"][...] = st

        @pl.when(c == nc - 1)
        def _final_ssd_state():
            own = jnp.concatenate([st[grp * SSD_N:(grp + 1) * SSD_N, grp * LANES:(grp + 1) * LANES]
                                   for grp in range(SSD_G)], axis=1)
            own = jnp.concatenate([own, jnp.zeros_like(own)], axis=0)
            g["sh_o"][...] = jnp.transpose(own)[:, 0:SSD_N].reshape(g["sh_o"].shape)
    else:
        sh_i, sh_o = g["sh_i"], g["sh_o"]
        per_seq = gw // SSD_HD * SSD_HD
        cols = [[], []]
        for s in range(0, ns, 2):
            two = jnp.concatenate([sh_i[s].reshape(per_seq, SSD_N), sh_i[s + 1].reshape(per_seq, SSD_N)], axis=1)
            two = jnp.transpose(two)
            for grp in range(SSD_G):
                cols[grp] += [two[0:SSD_N, grp * LANES:(grp + 1) * LANES],
                              two[SSD_N:2 * SSD_N, grp * LANES:(grp + 1) * LANES]]
        st = jnp.concatenate([jnp.concatenate(cols[0], axis=1), jnp.concatenate(cols[1], axis=1)], axis=0)
        srow = lax.broadcasted_iota(jnp.int32, st.shape, 0) // SSD_N
        seqm = seq_mask(LANES)
        lastm = jnp.where(t == r - 1, seqm, 0.0)
        yo, upd, dl = [], None, None
        for grp in range(SSD_G):
            gm = ((lane_b // SSD_N) == grp).astype(F32)
            z = _mm(cm * gm, st) * seqm
            zf = z[:, 0:LANES]
            for s in range(1, ns):
                zf = zf + z[:, s * LANES:(s + 1) * LANES]
            yo.append(zf)
            sl = slice(grp * LANES, (grp + 1) * LANES)
            u_g = _mm_tn(bm, jnp.tile(xw[:, sl], (1, ns)) * seqm)
            d_g = jnp.exp(jnp.sum(jnp.tile(acum[:, sl], (1, ns)) * lastm, axis=0, keepdims=True))
            upd = u_g if grp == 0 else jnp.where(srow == grp, u_g, upd)
            dl = d_g if grp == 0 else jnp.where(srow == grp, d_g, dl)
        yoff = jnp.concatenate(yo, axis=1)
        st = st * dl + upd
        for s in range(0, ns, 2):
            two = jnp.concatenate(
                [jnp.concatenate([st[0:SSD_N, q * LANES:(q + 1) * LANES], st[SSD_N:2 * SSD_N, q * LANES:(q + 1) * LANES]],
                                 axis=1) for q in (s, s + 1)], axis=0)
            two = jnp.transpose(two)
            sh_o[s] = two[:, 0:SSD_N].reshape(sh_o.shape[1:])
            sh_o[s + 1] = two[:, SSD_N:2 * SSD_N].reshape(sh_o.shape[1:])
    yc = ydiag + yoff * jnp.exp(acum) + vec[V_SSDD:V_SSDD + 1, 0:gw] * xs
    z = pcols(3 * gw, 4 * gw)
    yc = yc * (z * _sigmoid(z))
    ysave(2 * gw, 3 * gw, _rms(yc) * gain[:, 2 * gw:3 * gw])

    trig = g["trig"]
    cos, sin_a, sin_b = trig[0], trig[1], trig[2]

    def rope(x):
        return x * cos + pltpu.roll(x, LANES - RET_DK // 2, 1) * sin_a + pltpu.roll(x, RET_DK // 2, 1) * sin_b

    q = rope(pcols(6 * gw, 6 * gw + LANES))
    kk = rope(pcols(6 * gw + LANES, 7 * gw)) * (RET_DK ** -0.5)
    v = pcols(7 * gw, 8 * gw)
    dv = gw // RET_H
    intra = jnp.zeros((qt, gw), F32)
    for hd in range(RET_H):
        qm = ((lane_b // RET_DK) == hd).astype(F32)
        sc = _mm_nt(q * qm, kk) * g["dmask"][hd]
        vm = ((lane // dv) == hd).astype(F32)
        intra = intra + _mm(sc, v * vm)
    dec = g["dec"]
    qd, kd = q * dec[0], kk * dec[1]
    if ns == 1:
        rs = g["c_ret"][...]
        cross = _mm(qd, rs)
        rrow = lax.broadcasted_iota(jnp.int32, rs.shape, 0) // RET_DK
        rlane = lax.broadcasted_iota(jnp.int32, rs.shape, 1) // dv
        upd = jnp.where(rrow == rlane, _mm_tn(kd, v), 0.0)
    else:
        ret_i, ret_o = g["ret_i"], g["ret_o"]
        hk = RET_H * RET_DK
        rs = jnp.concatenate([ret_i[s].reshape(hk, dv) for s in range(ns)], axis=1)
        seqm = seq_mask(dv)
        lo = lane_b < dv
        pairs = []
        upd = jnp.zeros(rs.shape, F32)
        for pp in range(RET_H // 2):
            vp = v[:, pp * LANES:(pp + 1) * LANES]
            vsw = pltpu.roll(vp, dv, 1)
            tots = []
            for hh in range(2):
                qm = ((lane_b // RET_DK) == 2 * pp + hh).astype(F32)
                zc = _mm(qd * qm, rs) * seqm
                acc = zc[:, 0:LANES]
                for s2 in range(1, ns // 2):
                    acc = acc + zc[:, s2 * LANES:(s2 + 1) * LANES]
                tots.append(acc + pltpu.roll(acc, dv, 1))
                vh = jnp.where(lo, vp, vsw) if hh == 0 else jnp.where(lo, vsw, vp)
                upd = upd + _mm_tn(kd * qm, jnp.tile(vh, (1, ns // 2)) * seqm)
            pairs.append(jnp.where(lo, tots[0], tots[1]))
        cross = jnp.concatenate(pairs, axis=1)
    rs = rs * dec[2][:, 0:1] + upd
    if ns == 1:
        g["c_ret"][...] = rs

        @pl.when(c == nc - 1)
        def _final_ret_state():
            own = functools.reduce(lambda s, x: s + x, [rs[:, hd * dv:(hd + 1) * dv] for hd in range(RET_H)])
            g["ret_o"][...] = own.reshape(g["ret_o"].shape)
    else:
        for s in range(ns):
            ret_o[s] = rs[:, s * dv:(s + 1) * dv].reshape(ret_o.shape[1:])
    o = intra + cross
    o2 = o * o
    ms = jnp.zeros((qt, gw), F32)
    for hd in range(RET_H):
        vm = ((lane // dv) == hd).astype(F32)
        ms = ms + vm * jnp.sum(o2 * vm, axis=-1, keepdims=True)
    o = o * lax.rsqrt(ms * (1.0 / dv) + EPS) * gain[:, 3 * gw:4 * gw]
    gg = pcols(8 * gw, 9 * gw)
    ysave(3 * gw, 4 * gw, gg * _sigmoid(gg) * o)
    if ns == 1:
        y_all = jnp.concatenate(y_parts, axis=1)
        g["y"][...] = jnp.dot(g["perm"][1], y_all, preferred_element_type=F32).astype(g["y"].dtype)


def _mix_consts(qt, ns, r, nc, pos0):
    assert qt == LANES, "decay tables are stacked as (3, 128, 128)"
    idx = np.arange(qt)
    nk = qt // SUBLANES
    if ns == 1:
        seq, tt = np.zeros_like(idx), (idx % SUBLANES) * nk + idx // SUBLANES
    else:
        seq, tt = idx % ns, idx // ns
    causal = (seq[:, None] == seq[None, :]) & (tt[:, None] >= tt[None, :])
    tri = causal.astype(np.float32)
    lg = np.log1p(-np.exp2(-5.0 - np.arange(RET_H, dtype=np.float64)))
    rel = (tt[:, None] - tt[None, :]).astype(np.float64)
    dmask = np.where(causal[None], np.exp(np.maximum(rel, 0.0)[None] * lg[:, None, None]), 0.0).astype(np.float32)
    lane_h = np.arange(LANES) // RET_DK
    qdec = np.exp((tt[:, None] + 1.0) * lg[lane_h][None, :])
    kdec = np.exp((r - 1.0 - tt[:, None]) * lg[lane_h][None, :])
    rdec = np.broadcast_to(np.exp(r * lg[lane_h])[:, None], (LANES, LANES))
    dec = np.stack([qdec, kdec, rdec]).astype(np.float32)
    half = RET_DK // 2
    pos = pos0 + jnp.asarray((np.arange(nc)[:, None] * qt + tt[None, :]).reshape(-1))
    inv = ROPE_BASE ** (-jnp.arange(half, dtype=F32) / half)
    ang = pos.astype(F32)[:, None] * inv
    reps = LANES // half
    cos, sin = jnp.tile(jnp.cos(ang), (1, reps)), jnp.tile(jnp.sin(ang), (1, reps))
    first = (np.arange(LANES) % RET_DK) < half
    trig = jnp.stack([cos, jnp.where(first, -sin, 0.0), jnp.where(first, 0.0, sin)])
    to_blocked = np.zeros((qt, qt), np.float32)
    to_blocked[idx, (idx % SUBLANES) * nk + idx // SUBLANES] = 1.0
    perm = jnp.asarray(np.stack([to_blocked, to_blocked.T])).astype(MXU_DTYPE)
    return dict(trig=trig, tri=jnp.asarray(tri), dmask=jnp.asarray(dmask), dec=jnp.asarray(dec), perm=perm)


def _state_layout(depth, bsz, ns, gw, sn):
    hk, dv = RET_H * RET_DK, gw // RET_H
    if ns == 1:
        def per_seq(rows, w):
            return (depth, bsz, rows, w), (None, None, rows, w), lambda b, c, l: (l[0], b, 0, 0)
        def matrix(h, rows, w):
            return (depth, bsz, h, rows, w), (None, None, h, rows, w), lambda b, c, l: (l[0], b, 0, 0, 0)
        return dict(lconv=per_seq(CONV_K - 1, gw), lh=per_seq(1, gw), s5r=per_seq(1, sn), s5i=per_seq(1, sn),
                    sconv=per_seq(CONV_K - 1, 2 * gw), sh=matrix(gw // SSD_HD, SSD_HD, SSD_N),
                    ret=matrix(RET_H, RET_DK, dv))

    def flat(per, w):
        return (depth, bsz * per, w), (None, ns * per, w), lambda b, c, l: (l[0], b, 0)

    def conv(w):
        return (depth, CONV_K - 1, bsz, w), (None, CONV_K - 1, ns, w), lambda b, c, l: (l[0], 0, b, 0)
    def matrix(h, rows, w):
        return (depth, bsz, h, rows, w), (None, ns, h, rows, w), lambda b, c, l: (l[0], b, 0, 0, 0)
    return dict(lconv=conv(gw), lh=flat(1, gw), s5r=flat(1, sn), s5i=flat(1, sn), sconv=conv(2 * gw),
                sh=matrix(gw // SSD_HD, SSD_HD, SSD_N), ret=matrix(RET_H, RET_DK, dv))


def _mixer(l, proj, consts, mw, state_in, state_out, *, nb, nc, qt, ns, r, d):
    dp = proj.shape[-1]
    gw = d // 4
    sn = mw["pw"].shape[-1]
    depth = mw["pw"].shape[0]
    layout = _state_layout(depth, nb * ns, ns, gw, sn)
    assert (state_in is None) == (ns == 1)

    def layer_spec(a):
        return pl.BlockSpec((None,) + a.shape[1:], lambda b, c, l: (l[0],) + (0,) * (a.ndim - 1))

    def const_spec(a):
        return pl.BlockSpec(a.shape, lambda b, c, l: (0,) * a.ndim)

    def state_spec(nm):
        _, block, index = layout[nm]
        return pl.BlockSpec(block, index)

    if ns == 1:
        io_spec = lambda w: pl.BlockSpec((qt, w), lambda b, c, l: (b * nc + c, 0))
        y_shape = (nb * nc * qt, d)
        cnames = ("trig", "tri", "dmask", "dec", "perm")
    else:
        proj = proj.reshape(r, nb * ns, dp)
        io_spec = lambda w: pl.BlockSpec((r, ns, w), lambda b, c, l: (0, b, 0))
        y_shape = (r, nb * ns, d)
        cnames = ("trig", "tri", "dmask", "dec")
    names = ("proj",) + cnames + MIX_WEIGHTS
    operands = [proj] + [consts[k] for k in cnames] + [mw[k] for k in MIX_WEIGHTS]
    in_specs = [io_spec(dp), pl.BlockSpec((3, qt, LANES), lambda b, c, l: (0, c, 0))] + \
               [const_spec(consts[k]) for k in cnames[1:]] + [layer_spec(mw[k]) for k in MIX_WEIGHTS]
    first_alias = 1 + len(operands)
    if state_in is not None:
        assert state_in is state_out
        names += tuple(nm + "_i" for nm in STATE_NAMES)
        in_specs += [state_spec(nm) for nm in STATE_NAMES]
    else:
        names += tuple(nm + "_alias" for nm in STATE_NAMES)
        in_specs += [pl.BlockSpec(memory_space=pl.ANY) for _ in STATE_NAMES]
    operands += [state_out[nm] for nm in STATE_NAMES]
    names += ("y",) + tuple(nm + "_o" for nm in STATE_NAMES)
    out_specs = [io_spec(d)] + [state_spec(nm) for nm in STATE_NAMES]
    out_shape = [jax.ShapeDtypeStruct(y_shape, MXU_DTYPE)] + \
                [jax.ShapeDtypeStruct(layout[nm][0], F32) for nm in STATE_NAMES]
    scratch = []
    if ns == 1:
        tail = (CONV_K - 1) * SUBLANES
        scratch_shapes = dict(cb_a=(tail, gw), cb_c=(tail, 2 * gw), c_lh=(1, gw), c_s5r=(1, sn),
                              c_s5i=(1, sn), c_sh=(SSD_G * SSD_N, gw), c_ret=(RET_H * RET_DK, gw))
        names += tuple(scratch_shapes)
        scratch = [pltpu.VMEM(shape, F32) for shape in scratch_shapes.values()]
    grid_spec = pltpu.PrefetchScalarGridSpec(num_scalar_prefetch=1, grid=(nb, nc),
                                             in_specs=in_specs, out_specs=out_specs, scratch_shapes=scratch)
    outs = pl.pallas_call(
        functools.partial(_mix_body, names=names, qt=qt, ns=ns, r=r, nc=nc),
        grid_spec=grid_spec,
        out_shape=out_shape,
        input_output_aliases={first_alias + k: 1 + k for k in range(len(STATE_NAMES))},
        compiler_params=pltpu.CompilerParams(dimension_semantics=("arbitrary", "arbitrary"),
                                             vmem_limit_bytes=VMEM_LIMIT),
        name="mixer",
    )(l, *operands)
    return outs[0].reshape(-1, d), dict(zip(STATE_NAMES, outs[1:]))


def _post_body(l_ref, y_ref, x_ref, g1_ref, sc_ref, sh_ref, g2_ref, lt_ref, wo_ref, wr_ref, br_ref,
               wg_ref, wu_ref, wd_ref, *rest):
    o_ref, xs_ref, cs_ref, os_ref = rest[-4:]
    del l_ref
    tm, d = x_ref.shape
    tms = xs_ref.shape[0]
    x = x_ref[...] + _token_rows(g1_ref[...], tm) * jnp.dot(y_ref[...], wo_ref[...], preferred_element_type=F32)
    h = _rms(x) * (1.0 + _token_rows(sc_ref[...], tm)) + _token_rows(sh_ref[...], tm)
    hb = h.astype(MXU_DTYPE)
    h_lo = (h - hb.astype(F32)).astype(MXU_DTYPE)
    wr = wr_ref[...]
    l12 = jnp.dot(hb, wr, preferred_element_type=F32)
    logits = (l12[:, 0:LANES] + l12[:, LANES:2 * LANES]
              + jnp.dot(h_lo, wr[:, 0:LANES], preferred_element_type=F32) + br_ref[...])
    logits_t = jnp.transpose(logits)
    col = [logits_t[k:k + 1, :] for k in range(N_GROUPS + N_EXPERTS)]

    def first_max(vals, allowed=None):
        neg = jnp.full_like(vals[0], -jnp.inf)
        cand = vals if allowed is None else [jnp.where(al > 0.0, v, neg) for v, al in zip(vals, allowed)]
        m = functools.reduce(jnp.maximum, cand)
        rem = jnp.ones_like(vals[0])
        hot = []
        for v in cand:
            f = jnp.where(v >= m, rem, 0.0)
            rem = rem - f
            hot.append(f)
        return hot, m

    grp, gmax = first_max(col[:N_GROUPS])
    gate = 1.0 / functools.reduce(lambda s, v: s + v, [jnp.exp(v - gmax) for v in col[:N_GROUPS]])
    le = [functools.reduce(lambda s, v: s + v,
                           [grp[gi] * col[N_GROUPS + gi * N_PER_GROUP + j] for gi in range(N_GROUPS)])
          for j in range(N_PER_GROUP)]
    top1, m1 = first_max(le)
    top2, m2 = first_max(le, [1.0 - f for f in top1])
    e2 = jnp.exp(m2 - m1)
    w1 = 1.0 / (1.0 + e2)
    w2 = e2 * w1
    wgrp = [(top1[j] * w1 + top2[j] * w2) * gate for j in range(N_PER_GROUP)]

    sub8 = lax.broadcasted_iota(jnp.int32, (SUBLANES, 1), 0)
    ghot_t = functools.reduce(lambda s, v: s + v, [jnp.where(sub8 == gi, grp[gi], 0.0) for gi in range(N_GROUPS)])
    before = _mm_nt(ghot_t, lt_ref[...])
    count = jnp.sum(ghot_t, axis=1, keepdims=True)
    cnt = [count[gi:gi + 1, :] for gi in range(N_GROUPS)]
    base = [jnp.zeros((1, 1), F32)]
    for gi in range(1, N_GROUPS):
        base.append(jnp.floor((base[-1] + cnt[gi - 1] + (PACK - 1.0)) * (1.0 / PACK)) * PACK)
    pos_row = functools.reduce(lambda s, v: s + v,
                               [grp[gi] * (base[gi] + before[gi:gi + 1, :]) for gi in range(N_GROUPS)])
    pos = jnp.transpose(jnp.broadcast_to(pos_row, (LANES, tm)))[:, 0:1]
    slot = lax.broadcasted_iota(jnp.int32, (tms, tm), 0).astype(F32)
    gather = jnp.where(slot == pos_row, 1.0, 0.0).astype(MXU_DTYPE)
    slot_l = lax.broadcasted_iota(jnp.int32, (tm, tms), 1).astype(F32)
    scatter = jnp.where(slot_l == pos, 1.0, 0.0).astype(MXU_DTYPE)

    xs_ref[...] = jnp.dot(gather, hb, preferred_element_type=F32).astype(xs_ref.dtype)
    sub_e = lax.broadcasted_iota(jnp.int32, (cs_ref.shape[1], 1), 0)
    comb_t = functools.reduce(lambda s, v: s + v,
                              [jnp.where(sub_e == e, grp[e // N_PER_GROUP] * wgrp[e % N_PER_GROUP], 0.0)
                               for e in range(N_EXPERTS)])
    c1, c2, c3 = _split3(comb_t)
    nt = lambda p: lax.dot_general(gather, p, (((1,), (1,)), ((), ())), preferred_element_type=F32)
    cs_ref[...] = nt(c1) + nt(c2) + nt(c3)
    os_ref[...] = jnp.zeros(os_ref.shape, os_ref.dtype)
    for gi in range(N_GROUPS):
        start = base[gi][0, 0].astype(jnp.int32)
        nblk = jnp.floor((cnt[gi][0, 0] + (EXPERT_ROWS - 1.0)) * (1.0 / EXPERT_ROWS)).astype(jnp.int32)

        def block(k, carry, gi=gi, start=start):
            rows = pl.ds(pl.multiple_of(start + k * EXPERT_ROWS, PACK), EXPERT_ROWS)
            xb = xs_ref[rows, :]
            cb = cs_ref[rows, :]
            acc = jnp.zeros((EXPERT_ROWS, d), F32)
            for j in range(N_PER_GROUP):
                e = gi * N_PER_GROUP + j
                gt = jnp.dot(xb, wg_ref[e], preferred_element_type=F32)
                he = gt * _sigmoid(gt) * jnp.dot(xb, wu_ref[e], preferred_element_type=F32)
                acc = acc + jnp.dot((he * cb[:, e:e + 1]).astype(MXU_DTYPE), wd_ref[e], preferred_element_type=F32)
            os_ref[rows, :] += acc
            return carry

        lax.fori_loop(0, nblk, block, 0)
    moe = _select_rows(scatter, os_ref[...], pieces=1)
    out = x + _token_rows(g2_ref[...], tm) * moe
    if len(rest) == 5:
        out = _rms(out) * rest[0][...]
    o_ref[...] = out


def _post(l, y, x, mod, pw, tm, rows_per_seq, final_w=None, in_place=False):
    t, d = x.shape
    final = [] if final_w is None else [final_w.reshape(1, d)]
    tms = tm + EXPERT_ROWS + PACK * N_GROUPS
    ltri = jnp.asarray(np.tril(np.ones((tm, tm), np.float32), -1)).astype(MXU_DTYPE)

    def layer_spec(a):
        return pl.BlockSpec((None,) + a.shape[1:], lambda i, l: (l[0],) + (0,) * (a.ndim - 1),
                            pipeline_mode=pl.Buffered(1))

    grid_spec = pltpu.PrefetchScalarGridSpec(
        num_scalar_prefetch=1,
        grid=(t // tm,),
        in_specs=[
            pl.BlockSpec((tm, d), lambda i, l: (i, 0)),
            pl.BlockSpec((tm, d), lambda i, l: (i, 0)),
            _mod_spec(mod, 2, tm, rows_per_seq),
            _mod_spec(mod, 4, tm, rows_per_seq),
            _mod_spec(mod, 3, tm, rows_per_seq),
            _mod_spec(mod, 5, tm, rows_per_seq),
            pl.BlockSpec((tm, tm), lambda i, l: (0, 0)),
        ] + [layer_spec(pw[k]) for k in ("w_out", "w_route", "b_route", "w_gate", "w_up", "w_down")]
          + [pl.BlockSpec((1, d), lambda i, l: (0, 0)) for _ in final],
        out_specs=pl.BlockSpec((tm, d), lambda i, l: (i, 0)),
        scratch_shapes=[pltpu.VMEM((tms, d), MXU_DTYPE), pltpu.VMEM((tms, 2 * N_EXPERTS), F32),
                        pltpu.VMEM((tms, d), F32)],
    )
    return pl.pallas_call(
        _post_body,
        grid_spec=grid_spec,
        out_shape=jax.ShapeDtypeStruct((t, d), F32),
        input_output_aliases={2: 0} if in_place else {},
        compiler_params=pltpu.CompilerParams(dimension_semantics=("arbitrary",), vmem_limit_bytes=VMEM_LIMIT),
        name="post",
    )(l, y, x, mod, mod, mod, mod, ltri,
      *[pw[k] for k in ("w_out", "w_route", "b_route", "w_gate", "w_up", "w_down")], *final)


def _block_diag(w):
    depth, nblk, bi, bj = w.shape
    eye = jnp.eye(nblk, dtype=w.dtype)
    return (w[:, :, :, None, :] * eye[None, :, None, :, None]).reshape(depth, nblk * bi, nblk * bj)


def _pad_lanes(a, width):
    return jnp.pad(a, [(0, 0)] * (a.ndim - 1) + [(0, width - a.shape[-1])])


def _prep_weights(w_in, lru_conv_w, lru_conv_b, lru_wa, lru_ba, lru_wx, lru_bx, lru_lambda,
                  s5_a_re, s5_a_im, s5_b_re, s5_b_im, s5_c_re, s5_c_im, s5_d, s5_log_dt, s5_w_glu,
                  ssd_conv_w, ssd_conv_b, ssd_dt_bias, ssd_a_log, ssd_d, mix_norm, qt):
    depth, d, _ = w_in.shape
    gw = d // 4
    n_ssd_h = gw // SSD_HD
    xbc_end = 4 * gw + gw + 2 * SSD_G * SSD_N
    dt_cols = jnp.repeat(w_in[..., xbc_end:xbc_end + n_ssd_h], SSD_HD, axis=-1)
    w_in_p = jnp.concatenate([w_in[..., :xbc_end], w_in[..., xbc_end + n_ssd_h:], dt_cols], axis=-1)

    def row(a):
        return _pad_lanes(a.reshape(depth, 1, -1), d)

    vec = jnp.concatenate([
        row(lru_conv_b), row(jnp.concatenate([lru_ba, lru_bx], -1)), row(lru_lambda), row(s5_d),
        row(ssd_conv_b), row(jnp.repeat(ssd_dt_bias, SSD_HD, -1)), row(jnp.repeat(ssd_a_log, SSD_HD, -1)),
        row(jnp.repeat(ssd_d, SSD_HD, -1)), row(mix_norm),
        _pad_lanes(lru_conv_w, d), _pad_lanes(ssd_conv_w, d),
        jnp.zeros((depth, V_ROWS - V_SCW - CONV_K, d), F32)], axis=1)

    wgate = jnp.concatenate([_block_diag(lru_wa), _block_diag(lru_wx)], axis=-1)

    dt = jnp.exp(s5_log_dt)[..., None]
    lr, li = s5_a_re, s5_a_im
    mag = jnp.exp(lr * dt)
    ab_re, ab_im = mag * jnp.cos(li * dt), mag * jnp.sin(li * dt)
    den = lr * lr + li * li
    q_re = ((ab_re - 1.0) * lr + ab_im * li) / den
    q_im = (ab_im * lr - (ab_re - 1.0) * li) / den
    bb_re = q_re[..., None] * s5_b_re - q_im[..., None] * s5_b_im
    bb_im = q_re[..., None] * s5_b_im + q_im[..., None] * s5_b_re
    bbm = jnp.concatenate([_block_diag(jnp.swapaxes(bb_re, -1, -2)), _block_diag(jnp.swapaxes(bb_im, -1, -2))], -1)
    cre = _block_diag(jnp.swapaxes(s5_c_re, -1, -2))
    cim = _block_diag(jnp.swapaxes(s5_c_im, -1, -2))
    a_re, a_im = ab_re.reshape(depth, -1), ab_im.reshape(depth, -1)
    pr, pi = a_re, a_im
    rows = []
    for _ in range(int(math.log2(qt))):
        rows += [pr, pi]
        pr, pi = pr * pr - pi * pi, 2.0 * pr * pi
    pw = jnp.stack(rows, axis=1)
    pw = jnp.pad(pw, ((0, 0), (0, (-pw.shape[1]) % SUBLANES), (0, 0)))
    pr, pi = a_re, a_im
    k_re, k_im = [], []
    for _ in range(qt // SUBLANES):
        k_re.append(pr)
        k_im.append(pi)
        pr, pi = pr * a_re - pi * a_im, pr * a_im + pi * a_re
    pk = jnp.stack(k_re + k_im, axis=1)

    mw = dict(vec=vec, wgate=wgate.astype(MXU_DTYPE), bbm=bbm.astype(MXU_DTYPE), cre=cre.astype(MXU_DTYPE),
              cim=cim.astype(MXU_DTYPE), wglu=s5_w_glu.astype(MXU_DTYPE), pw=pw, pk=pk)
    return w_in_p.astype(MXU_DTYPE), mw


def _states_to_layout(states, layout, ns):
    out = {}
    for nm, s in zip(STATE_NAMES, states):
        if nm in ("lconv", "sconv"):
            s = s.transpose(0, 2, 1, 3)
        out[nm] = s.reshape(layout[nm][0])
    return out


def _states_from_layout(arrs, ns, ref_shapes):
    out = []
    for nm, shape in zip(STATE_NAMES, ref_shapes):
        a = arrs[nm]
        if ns > 1 and nm in ("lconv", "sconv"):
            a = a.transpose(0, 2, 1, 3)
        out.append(a.reshape(shape))
    return tuple(out)


def _trunk(x, mod, states, ref_shapes, pos0, w_in_p, mw, pw, final_norm, *, qt, ns, tm):
    bsz, seq_len, d = x.shape
    depth = w_in_p.shape[0]
    gw = d // 4
    r = qt // ns
    nc = seq_len // r if ns == 1 else 1
    assert (ns == 1 and seq_len % qt == 0 and states is None) or (ns > 1 and r == seq_len and bsz % ns == 0)
    assert r >= CONV_K and (r & (r - 1)) == 0
    nb = bsz // ns
    tok = bsz * seq_len
    consts = _mix_consts(qt, ns, r, nc, pos0)
    layout = _state_layout(depth, bsz, ns, gw, mw["pw"].shape[-1])
    if states is None:
        state_out = {nm: jnp.zeros(layout[nm][0], F32) for nm in STATE_NAMES}
    else:
        state_out = _states_to_layout(states, layout, ns)
    x2 = x.reshape(tok, d) if ns == 1 else x.transpose(1, 0, 2).reshape(tok, d)
    blocked_chunk = qt if ns == 1 else None

    def layer(carry, l, final_w=None, in_place=True):
        xc, st = carry
        lv = jnp.reshape(l, (1,)).astype(jnp.int32)
        proj = _inproj(lv, xc, mod, w_in_p, tm, seq_len, blocked_chunk)
        y, st = _mixer(lv, proj, consts, mw, None if states is None else st, st,
                       nb=nb, nc=nc, qt=qt, ns=ns, r=r, d=d)
        xn = _post(lv, y, xc, mod, pw, tm, seq_len, final_w, in_place)
        return (xn, st), None

    assert depth >= 2
    carry, _ = layer((x2, state_out), jnp.int32(0), in_place=False)
    carry, _ = lax.scan(layer, carry, jnp.arange(1, depth - 1))
    (y, st), _ = layer(carry, jnp.int32(depth - 1), final_norm)
    y = y.reshape(bsz, seq_len, d) if ns == 1 else y.reshape(seq_len, bsz, d).transpose(1, 0, 2)
    return y, _states_from_layout(st, ns, ref_shapes)


def kernel(x_prompt, x_sample, state_lru_conv, state_lru_h, state_s5_re, state_s5_im, state_ssd_conv, state_ssd_h, state_ret, c_prompt, c_sample, w_ada, b_ada, w_in, lru_conv_w, lru_conv_b, lru_wa, lru_ba, lru_wx, lru_bx, lru_lambda, s5_a_re, s5_a_im, s5_b_re, s5_b_im, s5_c_re, s5_c_im, s5_d, s5_log_dt, s5_w_glu, ssd_conv_w, ssd_conv_b, ssd_dt_bias, ssd_a_log, ssd_d, mix_norm, w_out, w_route_group, b_route_group, w_route_exp, b_route_exp, w_exp_gate, w_exp_up, w_exp_down, final_norm):
    bp, lp, d = x_prompt.shape
    bs, ls, _ = x_sample.shape
    depth = w_in.shape[0]
    qt = LANES
    ns_s = qt // ls

    mod_p, mod_s = _modulation(jnp.concatenate([c_prompt, c_sample], axis=0), bp, w_ada, b_ada)
    mod_p = mod_p.reshape(depth, N_MOD, bp, 1, d)

    w_in_p, mw = _prep_weights(w_in, lru_conv_w, lru_conv_b, lru_wa, lru_ba, lru_wx, lru_bx, lru_lambda,
                               s5_a_re, s5_a_im, s5_b_re, s5_b_im, s5_c_re, s5_c_im, s5_d, s5_log_dt, s5_w_glu,
                               ssd_conv_w, ssd_conv_b, ssd_dt_bias, ssd_a_log, ssd_d, mix_norm, qt)
    w_route = _pad_lanes(jnp.concatenate([w_route_group, w_route_exp.reshape(depth, d, N_EXPERTS)], -1), LANES)
    b_route = _pad_lanes(jnp.concatenate([b_route_group, b_route_exp.reshape(depth, N_EXPERTS)], -1), LANES)
    w_route_hi = w_route.astype(MXU_DTYPE)
    w_route_lo = (w_route - w_route_hi.astype(F32)).astype(MXU_DTYPE)
    w_route = jnp.concatenate([w_route_hi, w_route_lo], axis=-1)
    pw = dict(w_out=w_out.astype(MXU_DTYPE), w_route=w_route, b_route=b_route.reshape(depth, 1, LANES),
              w_gate=w_exp_gate.astype(MXU_DTYPE), w_up=w_exp_up.astype(MXU_DTYPE),
              w_down=w_exp_down.astype(MXU_DTYPE))
    w_in_p, mw, pw = lax.optimization_barrier((w_in_p, mw, pw))

    states_s = (state_lru_conv, state_lru_h, state_s5_re, state_s5_im, state_ssd_conv, state_ssd_h, state_ret)
    shapes_s = [s.shape for s in states_s]
    shapes_p = [(depth, bp) + s[2:] for s in shapes_s]
    y_p, new_p = _trunk(x_prompt, mod_p, None, shapes_p, 0, w_in_p, mw, pw, final_norm,
                        qt=qt, ns=1, tm=min(512, lp))
    y_s, new_s = _trunk(x_sample, mod_s, states_s, shapes_s, PAST_LEN, w_in_p, mw, pw, final_norm,
                        qt=qt, ns=ns_s, tm=bs * ls)
    out = [y_p, y_s]
    for a, b in zip(new_p, new_s):
        out += [a, b]
    return tuple(out)
```

```python
import functools
import math

import numpy as np
import jax
import jax.numpy as jnp
from jax import lax
from jax.experimental import pallas as pl
from jax.experimental.pallas import tpu as pltpu

F32 = jnp.float32
MXU_DTYPE = jnp.bfloat16

EPS = 1e-6
CONV_K = 4
LRU_C = 8.0
SSD_HD = 64
SSD_G = 2
SSD_N = 64
RET_H = 4
RET_DK = 32
ROPE_BASE = 10000.0
N_GROUPS = 4
N_PER_GROUP = 4
N_EXPERTS = N_GROUPS * N_PER_GROUP
N_MOD = 6
PAST_LEN = 16384

LANES = 128
SUBLANES = 8
PACK = 16
VMEM_LIMIT = 56 * 1024 * 1024
EXPERT_ROWS = 144


def _mm(a, b):
    return jnp.dot(a.astype(MXU_DTYPE), b.astype(MXU_DTYPE), preferred_element_type=F32)


def _mm_nt(a, b):
    return lax.dot_general(a.astype(MXU_DTYPE), b.astype(MXU_DTYPE), (((1,), (1,)), ((), ())),
                           preferred_element_type=F32)


def _mm_tn(a, b):
    return lax.dot_general(a.astype(MXU_DTYPE), b.astype(MXU_DTYPE), (((0,), (0,)), ((), ())),
                           preferred_element_type=F32)


def _split3(x):
    x1 = x.astype(MXU_DTYPE)
    r1 = x - x1.astype(F32)
    x2 = r1.astype(MXU_DTYPE)
    x3 = (r1 - x2.astype(F32)).astype(MXU_DTYPE)
    return x1, x2, x3


def _sigmoid(x):
    return 0.5 * jnp.tanh(0.5 * x) + 0.5


def _rms(x):
    return x * lax.rsqrt(jnp.mean(x * x, axis=-1, keepdims=True) + EPS)


def _token_rows(v, tm):
    n = v.shape[0]
    return v if n in (1, tm) else jnp.tile(v, (tm // n, 1))


def _mod_body(c_ref, w_ref, b_ref, first_ref, rest_ref):
    c = c_ref[...]
    m = _mm(c * _sigmoid(c), w_ref[...]) + b_ref[...]
    n_first = first_ref.shape[0]
    first_ref[...] = m[0:n_first]
    rest_ref[...] = m[n_first:]


def _modulation(c_all, n_first, w_ada, b_ada):
    depth, d, _ = w_ada.shape
    nb = c_all.shape[0]
    return pl.pallas_call(
        _mod_body,
        grid=(depth, N_MOD),
        in_specs=[
            pl.BlockSpec((nb, d), lambda l, k: (0, 0)),
            pl.BlockSpec((None, d, d), lambda l, k: (l, 0, k)),
            pl.BlockSpec((None, None, 1, d), lambda l, k: (l, k, 0, 0)),
        ],
        out_specs=[pl.BlockSpec((None, None, n_first, d), lambda l, k: (l, k, 0, 0)),
                   pl.BlockSpec((None, None, nb - n_first, d), lambda l, k: (l, k, 0, 0))],
        out_shape=[jax.ShapeDtypeStruct((depth, N_MOD, n_first, d), F32),
                   jax.ShapeDtypeStruct((depth, N_MOD, nb - n_first, d), F32)],
        compiler_params=pltpu.CompilerParams(dimension_semantics=("arbitrary", "arbitrary"),
                                             vmem_limit_bytes=VMEM_LIMIT),
        name="modulation",
    )(c_all, w_ada, b_ada.reshape(depth, N_MOD, 1, d))


def _mod_spec(mod, k, tm, rows_per_seq):
    if mod.ndim == 5:
        tiles_per_seq = rows_per_seq // tm
        return pl.BlockSpec((None, None, None, 1, mod.shape[-1]),
                            lambda i, l: (l[0], k, i // tiles_per_seq, 0, 0))
    return pl.BlockSpec((None, None) + mod.shape[2:], lambda i, l: (l[0], k, 0, 0))


def _inproj_body(l_ref, x_ref, sc_ref, sh_ref, w_ref, o_ref, *, blocked_chunk):
    tm = x_ref.shape[0]
    h = _rms(x_ref[...]) * (1.0 + _token_rows(sc_ref[...], tm)) + _token_rows(sh_ref[...], tm)
    hb = h.astype(MXU_DTYPE)
    if blocked_chunk is not None:
        qt, nk = blocked_chunk, blocked_chunk // SUBLANES
        r_i = lax.broadcasted_iota(jnp.int32, (qt, qt), 0)
        c_i = lax.broadcasted_iota(jnp.int32, (qt, qt), 1)
        to_blocked = jnp.where(c_i == (r_i % SUBLANES) * nk + r_i // SUBLANES, 1.0, 0.0).astype(MXU_DTYPE)
        hb = jnp.concatenate([jnp.dot(to_blocked, hb[j * qt:(j + 1) * qt], preferred_element_type=F32)
                              for j in range(tm // qt)], axis=0).astype(MXU_DTYPE)
    o_ref[...] = jnp.dot(hb, w_ref[...], preferred_element_type=F32)


def _inproj(l, x, mod, w_in, tm, rows_per_seq, blocked_chunk=None):
    t, d = x.shape
    dp = w_in.shape[-1]
    assert blocked_chunk is None or (mod.ndim == 5 and tm % blocked_chunk == 0)
    grid_spec = pltpu.PrefetchScalarGridSpec(
        num_scalar_prefetch=1,
        grid=(t // tm,),
        in_specs=[
            pl.BlockSpec((tm, d), lambda i, l: (i, 0)),
            _mod_spec(mod, 1, tm, rows_per_seq),
            _mod_spec(mod, 0, tm, rows_per_seq),
            pl.BlockSpec((None, d, dp), lambda i, l: (l[0], 0, 0)),
        ],
        out_specs=pl.BlockSpec((tm, dp), lambda i, l: (i, 0)),
    )
    return pl.pallas_call(
        functools.partial(_inproj_body, blocked_chunk=blocked_chunk),
        grid_spec=grid_spec,
        out_shape=jax.ShapeDtypeStruct((t, dp), F32),
        compiler_params=pltpu.CompilerParams(dimension_semantics=("arbitrary",), vmem_limit_bytes=VMEM_LIMIT),
        name="inproj",
    )(l, x, mod, mod, w_in)


V_LCB, V_BGATE, V_LAM, V_S5D, V_SCB, V_DTB, V_ALOG, V_SSDD, V_GAIN, V_LCW, V_SCW = 0, 1, 2, 3, 4, 5, 6, 7, 8, 9, 13
V_ROWS = 24
STATE_NAMES = ("lconv", "lh", "s5r", "s5i", "sconv", "sh", "ret")
MIX_WEIGHTS = ("vec", "wgate", "bbm", "cre", "cim", "wglu", "pw", "pk")


def _mix_body(l_ref, *refs, names, qt, ns, r, nc):
    del l_ref
    g = dict(zip(names, refs))
    gw = g["lh_o"].shape[-1]
    sn = g["s5r_o"].shape[-1]
    nlev = int(math.log2(r))
    nk = qt // SUBLANES
    c = pl.program_id(1)

    if ns == 1:
        @pl.when(c == 0)
        def _zero_carries():
            for nm in ("c_lh", "c_s5r", "c_s5i", "c_sh", "c_ret"):
                g[nm][...] = jnp.zeros(g[nm].shape, F32)
            for nm in ("cb_a", "cb_c"):
                g[nm][...] = jnp.zeros(g[nm].shape, F32)

    proj = g["proj"]
    vec = g["vec"]
    pw = g["pw"]

    def pcols(a, b):
        v = proj[..., a:b]
        return v if ns == 1 else v.reshape(qt, b - a)

    y_parts = []

    def ysave(a, b, val):
        val = val.astype(g["y"].dtype)
        if ns == 1:
            y_parts.append(val)
        else:
            g["y"][..., a:b] = val.reshape(r, ns, b - a)

    row = lax.broadcasted_iota(jnp.int32, (qt, 1), 0)
    t = (row % SUBLANES) * nk + row // SUBLANES if ns == 1 else row // ns
    sub = lax.broadcasted_iota(jnp.int32, (SUBLANES, 1), 0)

    def down(x, d, fill=0.0):
        return jnp.where(t >= d, pltpu.roll(x, d * ns, 0), fill)

    def up(x, d):
        return jnp.where(t + d < r, pltpu.roll(x, qt - d * ns, 0), 0.0)

    def sdown(x, d, fill=0.0):
        return jnp.where(sub >= d, pltpu.roll(x, d, 0), fill)

    def slabs(x):
        return [x[SUBLANES * k:SUBLANES * (k + 1)] for k in range(nk)]

    def first_rows(carry, init):
        if ns == 1:
            return jnp.where(row == 0, g[carry][...], 0.0)
        h0 = g[init][...]
        return jnp.concatenate([h0, jnp.zeros((qt - ns, h0.shape[1]), F32)], axis=0)

    def conv(xraw, nm, w0, b):
        width = xraw.shape[-1]
        acc = vec[b:b + 1, 0:width] + vec[w0 + 3:w0 + 4, 0:width] * xraw
        if ns == 1:
            cb = g["cb_" + nm]
            xs_k = slabs(xraw)
            wrap = [jnp.where(sub >= 1, pltpu.roll(xs_k[nk - j], 1, 0),
                              pltpu.roll(cb[(CONV_K - 1 - j) * SUBLANES:(CONV_K - j) * SUBLANES, :], 1, 0))
                    for j in range(1, CONV_K)]
            for m in range(1, CONV_K):
                sh = jnp.concatenate([wrap[m - k - 1] for k in range(m)] + xs_k[0:nk - m], axis=0)
                acc = acc + vec[w0 + 3 - m:w0 + 4 - m, 0:width] * sh
            cb[...] = xraw[qt - (CONV_K - 1) * SUBLANES:, :]
            g[{"a": "lconv_o", "c": "sconv_o"}[nm]][...] = jnp.concatenate(
                [xs_k[nk - j][SUBLANES - 1:SUBLANES, :] for j in range(CONV_K - 1, 0, -1)], axis=0)
            return acc
        o = g[{"a": "lconv_o", "c": "sconv_o"}[nm]]
        buf = g[{"a": "lconv_i", "c": "sconv_i"}[nm]][...].reshape((CONV_K - 1) * ns, width)
        buf = jnp.concatenate([buf, jnp.zeros((qt - (CONV_K - 1) * ns, width), F32)], axis=0)
        for m in range(1, CONV_K):
            prev = buf if m == 3 else pltpu.roll(buf, qt - (3 - m) * ns, 0)
            sh = jnp.where(t >= m, pltpu.roll(xraw, m * ns, 0), prev)
            acc = acc + vec[w0 + 3 - m:w0 + 4 - m, 0:width] * sh
        o[...] = xraw[(r - (CONV_K - 1)) * ns:, :].reshape(CONV_K - 1, ns, width)
        return acc

    def save_last(nm, h):
        if ns == 1:
            g["c_" + nm][...] = h[qt - 1:qt, :]
            g[nm + "_o"][...] = h[qt - 1:qt, :]
        else:
            g[nm + "_o"][...] = h[qt - ns:, :]

    def seq_mask(width):
        lane_seq = lax.broadcasted_iota(jnp.int32, (qt, ns * width), 1) // width
        row_seq = lax.broadcasted_iota(jnp.int32, (qt, ns * width), 0) % ns
        return (lane_seq == row_seq).astype(F32)

    gain = vec[V_GAIN:V_GAIN + 1, :]

    xc = conv(pcols(0, gw), "a", V_LCW, V_LCB)
    ga = pcols(gw, 2 * gw)
    pre = _mm(xc, g["wgate"][...]) + vec[V_BGATE:V_BGATE + 1, 0:2 * gw]
    rg = _sigmoid(pre[:, 0:gw])
    ig = _sigmoid(pre[:, gw:2 * gw])
    log_a = -LRU_C * rg * jax.nn.softplus(-vec[V_LAM:V_LAM + 1, 0:gw])
    a = jnp.exp(log_a)
    b = jnp.sqrt(-jnp.tanh(log_a) * (a * a + 1.0)) * (ig * xc)
    b = b + a * first_rows("c_lh", "lh_i")
    if ns == 1:
        a_s, b_s = slabs(a), slabs(b)
        hs, ps = [b_s[0]], [a_s[0]]
        for k in range(1, nk):
            hs.append(a_s[k] * hs[-1] + b_s[k])
            ps.append(a_s[k] * ps[-1])
        e, ae = hs[-1], ps[-1]
        for j in range(int(math.log2(SUBLANES))):
            d = 1 << j
            e = ae * sdown(e, d) + e
            ae = ae * sdown(ae, d, 1.0)
        cin = sdown(e, 1)
        b = jnp.concatenate([hs[k] + ps[k] * cin for k in range(nk)], axis=0)
    else:
        for k in range(nlev):
            d = 1 << k
            b = a * down(b, d) + b
            a = a * down(a, d, 1.0)
    save_last("lh", b)
    ya = _rms(b * jax.nn.gelu(ga)) * gain[:, 0:gw]

    u = pcols(2 * gw, 3 * gw)
    bu = _mm(u, g["bbm"][...])
    p_re, p_im = pw[0:1, :], pw[1:2, :]
    h0r, h0i = first_rows("c_s5r", "s5r_i"), first_rows("c_s5i", "s5i_i")
    hr = bu[:, 0:sn] + (p_re * h0r - p_im * h0i)
    hi = bu[:, sn:2 * sn] + (p_re * h0i + p_im * h0r)
    if ns == 1:
        pk = g["pk"]
        br_s, bi_s = slabs(hr), slabs(hi)
        hrs, his = [br_s[0]], [bi_s[0]]
        for k in range(1, nk):
            hrs.append(br_s[k] + (p_re * hrs[-1] - p_im * his[-1]))
            his.append(bi_s[k] + (p_re * his[-1] + p_im * hrs[-2]))
        er, ei = hrs[-1], his[-1]
        lev0 = int(math.log2(nk))
        for j in range(int(math.log2(SUBLANES))):
            d = 1 << j
            q_re, q_im = pw[2 * (lev0 + j):2 * (lev0 + j) + 1, :], pw[2 * (lev0 + j) + 1:2 * (lev0 + j) + 2, :]
            sr, si = sdown(er, d), sdown(ei, d)
            er, ei = er + (q_re * sr - q_im * si), ei + (q_re * si + q_im * sr)
        cr, ci = sdown(er, 1), sdown(ei, 1)
        hr = jnp.concatenate([hrs[k] + (pk[k:k + 1, :] * cr - pk[nk + k:nk + k + 1, :] * ci) for k in range(nk)], 0)
        hi = jnp.concatenate([his[k] + (pk[k:k + 1, :] * ci + pk[nk + k:nk + k + 1, :] * cr) for k in range(nk)], 0)
    else:
        for k in range(nlev):
            d = 1 << k
            q_re, q_im = pw[2 * k:2 * k + 1, :], pw[2 * k + 1:2 * k + 2, :]
            sr, si = down(hr, d), down(hi, d)
            hr, hi = hr + (q_re * sr - q_im * si), hi + (q_re * si + q_im * sr)
    save_last("s5r", hr)
    save_last("s5i", hi)
    yb = _mm(hr, g["cre"][...]) - _mm(hi, g["cim"][...])
    yb = jax.nn.gelu(yb + vec[V_S5D:V_S5D + 1, 0:gw] * u)
    yb = yb * _sigmoid(_mm(yb, g["wglu"][...]))
    yb = _rms(yb) * gain[:, gw:2 * gw]
    ysave(0, 2 * gw, jnp.concatenate([ya, yb], axis=1))

    tri = g["tri"][...] > 0.0
    xbc = conv(pcols(4 * gw, 6 * gw), "c", V_SCW, V_SCB)
    xbc = xbc * _sigmoid(xbc)
    xs, bm, cm = xbc[:, 0:gw], xbc[:, gw:gw + LANES], xbc[:, gw + LANES:2 * gw]
    dt = jax.nn.softplus(pcols(9 * gw, 10 * gw) + vec[V_DTB:V_DTB + 1, 0:gw])
    a_e = -jnp.exp(vec[V_ALOG:V_ALOG + 1, 0:gw])
    dta = dt * a_e
    if ns == 1:
        cs = slabs(dta)
        for k in range(1, nk):
            cs[k] = cs[k] + cs[k - 1]
        e = cs[-1]
        for j in range(int(math.log2(SUBLANES))):
            e = e + sdown(e, 1 << j)
        cin = sdown(e, 1)
        acum = jnp.concatenate([ck + cin for ck in cs], axis=0)
        alast = acum[qt - 1:qt, :]
    else:
        acum = dta
        for k in range(nlev):
            acum = acum + down(acum, 1 << k)
        suf = dta
        for k in range(nlev):
            suf = suf + up(suf, 1 << k)
        alast = acum + suf - dta
    wend = jnp.exp(alast - acum) * dt
    acum_t = jnp.transpose(acum)
    lane = lax.broadcasted_iota(jnp.int32, (1, gw), 1)
    lane_b = lax.broadcasted_iota(jnp.int32, (1, LANES), 1)
    heads_per_group = gw // SSD_HD // SSD_G
    xdt = xs * dt
    xw = xs * wend
    ydiag = jnp.zeros((qt, gw), F32)
    for grp in range(SSD_G):
        gm = ((lane_b // SSD_N) == grp).astype(F32)
        cb = _mm_nt(cm * gm, bm)
        for hh in range(heads_per_group):
            hd = grp * heads_per_group + hh
            col = acum[:, hd * SSD_HD:hd * SSD_HD + 1]
            rowv = acum_t[hd * SSD_HD:hd * SSD_HD + 1, :]
            dec = jnp.where(tri, jnp.exp(col - rowv), 0.0)
            hm = ((lane // SSD_HD) == hd).astype(F32)
            ydiag = ydiag + _mm(cb * dec, xdt * hm)
    if ns == 1:
        st = g["c_sh"][...]
        srow = lax.broadcasted_iota(jnp.int32, st.shape, 0) // SSD_N
        yoff = _mm(cm, st)
        slane = lax.broadcasted_iota(jnp.int32, st.shape, 1) // (SSD_HD * heads_per_group)
        upd = jnp.where(srow == slane, _mm_tn(bm, xw), 0.0)
        st = st * jnp.exp(acum[qt - 1:qt, :]) + upd
        g["c_sh"][...] = st

        @pl.when(c == nc - 1)
        def _final_ssd_state():
            own = jnp.concatenate([st[grp * SSD_N:(grp + 1) * SSD_N, grp * LANES:(grp + 1) * LANES]
                                   for grp in range(SSD_G)], axis=1)
            own = jnp.concatenate([own, jnp.zeros_like(own)], axis=0)
            g["sh_o"][...] = jnp.transpose(own)[:, 0:SSD_N].reshape(g["sh_o"].shape)
    else:
        sh_i, sh_o = g["sh_i"], g["sh_o"]
        per_seq = gw // SSD_HD * SSD_HD
        cols = [[], []]
        for s in range(0, ns, 2):
            two = jnp.concatenate([sh_i[s].reshape(per_seq, SSD_N), sh_i[s + 1].reshape(per_seq, SSD_N)], axis=1)
            two = jnp.transpose(two)
            for grp in range(SSD_G):
                cols[grp] += [two[0:SSD_N, grp * LANES:(grp + 1) * LANES],
                              two[SSD_N:2 * SSD_N, grp * LANES:(grp + 1) * LANES]]
        st = jnp.concatenate([jnp.concatenate(cols[0], axis=1), jnp.concatenate(cols[1], axis=1)], axis=0)
        srow = lax.broadcasted_iota(jnp.int32, st.shape, 0) // SSD_N
        seqm = seq_mask(LANES)
        lastm = jnp.where(t == r - 1, seqm, 0.0)
        yo, upd, dl = [], None, None
        for grp in range(SSD_G):
            gm = ((lane_b // SSD_N) == grp).astype(F32)
            z = _mm(cm * gm, st) * seqm
            zf = z[:, 0:LANES]
            for s in range(1, ns):
                zf = zf + z[:, s * LANES:(s + 1) * LANES]
            yo.append(zf)
            sl = slice(grp * LANES, (grp + 1) * LANES)
            u_g = _mm_tn(bm, jnp.tile(xw[:, sl], (1, ns)) * seqm)
            d_g = jnp.exp(jnp.sum(jnp.tile(acum[:, sl], (1, ns)) * lastm, axis=0, keepdims=True))
            upd = u_g if grp == 0 else jnp.where(srow == grp, u_g, upd)
            dl = d_g if grp == 0 else jnp.where(srow == grp, d_g, dl)
        yoff = jnp.concatenate(yo, axis=1)
        st = st * dl + upd
        for s in range(0, ns, 2):
            two = jnp.concatenate(
                [jnp.concatenate([st[0:SSD_N, q * LANES:(q + 1) * LANES], st[SSD_N:2 * SSD_N, q * LANES:(q + 1) * LANES]],
                                 axis=1) for q in (s, s + 1)], axis=0)
            two = jnp.transpose(two)
            sh_o[s] = two[:, 0:SSD_N].reshape(sh_o.shape[1:])
            sh_o[s + 1] = two[:, SSD_N:2 * SSD_N].reshape(sh_o.shape[1:])
    yc = ydiag + yoff * jnp.exp(acum) + vec[V_SSDD:V_SSDD + 1, 0:gw] * xs
    z = pcols(3 * gw, 4 * gw)
    yc = yc * (z * _sigmoid(z))
    ysave(2 * gw, 3 * gw, _rms(yc) * gain[:, 2 * gw:3 * gw])

    trig = g["trig"]
    cos, sin_a, sin_b = trig[0], trig[1], trig[2]

    def rope(x):
        return x * cos + pltpu.roll(x, LANES - RET_DK // 2, 1) * sin_a + pltpu.roll(x, RET_DK // 2, 1) * sin_b

    q = rope(pcols(6 * gw, 6 * gw + LANES))
    kk = rope(pcols(6 * gw + LANES, 7 * gw)) * (RET_DK ** -0.5)
    v = pcols(7 * gw, 8 * gw)
    dv = gw // RET_H
    intra = jnp.zeros((qt, gw), F32)
    for hd in range(RET_H):
        qm = ((lane_b // RET_DK) == hd).astype(F32)
        sc = _mm_nt(q * qm, kk) * g["dmask"][hd]
        vm = ((lane // dv) == hd).astype(F32)
        intra = intra + _mm(sc, v * vm)
    dec = g["dec"]
    qd, kd = q * dec[0], kk * dec[1]
    if ns == 1:
        rs = g["c_ret"][...]
        cross = _mm(qd, rs)
        rrow = lax.broadcasted_iota(jnp.int32, rs.shape, 0) // RET_DK
        rlane = lax.broadcasted_iota(jnp.int32, rs.shape, 1) // dv
        upd = jnp.where(rrow == rlane, _mm_tn(kd, v), 0.0)
    else:
        ret_i, ret_o = g["ret_i"], g["ret_o"]
        hk = RET_H * RET_DK
        rs = jnp.concatenate([ret_i[s].reshape(hk, dv) for s in range(ns)], axis=1)
        seqm = seq_mask(dv)
        lo = lane_b < dv
        pairs = []
        upd = jnp.zeros(rs.shape, F32)
        for pp in range(RET_H // 2):
            vp = v[:, pp * LANES:(pp + 1) * LANES]
            vsw = pltpu.roll(vp, dv, 1)
            tots = []
            for hh in range(2):
                qm = ((lane_b // RET_DK) == 2 * pp + hh).astype(F32)
                zc = _mm(qd * qm, rs) * seqm
                acc = zc[:, 0:LANES]
                for s2 in range(1, ns // 2):
                    acc = acc + zc[:, s2 * LANES:(s2 + 1) * LANES]
                tots.append(acc + pltpu.roll(acc, dv, 1))
                vh = jnp.where(lo, vp, vsw) if hh == 0 else jnp.where(lo, vsw, vp)
                upd = upd + _mm_tn(kd * qm, jnp.tile(vh, (1, ns // 2)) * seqm)
            pairs.append(jnp.where(lo, tots[0], tots[1]))
        cross = jnp.concatenate(pairs, axis=1)
    rs = rs * dec[2][:, 0:1] + upd
    if ns == 1:
        g["c_ret"][...] = rs

        @pl.when(c == nc - 1)
        def _final_ret_state():
            own = functools.reduce(lambda s, x: s + x, [rs[:, hd * dv:(hd + 1) * dv] for hd in range(RET_H)])
            g["ret_o"][...] = own.reshape(g["ret_o"].shape)
    else:
        for s in range(ns):
            ret_o[s] = rs[:, s * dv:(s + 1) * dv].reshape(ret_o.shape[1:])
    o = intra + cross
    o2 = o * o
    ms = jnp.zeros((qt, gw), F32)
    for hd in range(RET_H):
        vm = ((lane // dv) == hd).astype(F32)
        ms = ms + vm * jnp.sum(o2 * vm, axis=-1, keepdims=True)
    o = o * lax.rsqrt(ms * (1.0 / dv) + EPS) * gain[:, 3 * gw:4 * gw]
    gg = pcols(8 * gw, 9 * gw)
    ysave(3 * gw, 4 * gw, gg * _sigmoid(gg) * o)
    if ns == 1:
        y_all = jnp.concatenate(y_parts, axis=1)
        g["y"][...] = jnp.dot(g["perm"][1], y_all, preferred_element_type=F32).astype(g["y"].dtype)


def _mix_consts(qt, ns, r, nc, pos0):
    assert qt == LANES, "decay tables are stacked as (3, 128, 128)"
    idx = np.arange(qt)
    nk = qt // SUBLANES
    if ns == 1:
        seq, tt = np.zeros_like(idx), (idx % SUBLANES) * nk + idx // SUBLANES
    else:
        seq, tt = idx % ns, idx // ns
    causal = (seq[:, None] == seq[None, :]) & (tt[:, None] >= tt[None, :])
    tri = causal.astype(np.float32)
    lg = np.log1p(-np.exp2(-5.0 - np.arange(RET_H, dtype=np.float64)))
    rel = (tt[:, None] - tt[None, :]).astype(np.float64)
    dmask = np.where(causal[None], np.exp(np.maximum(rel, 0.0)[None] * lg[:, None, None]), 0.0).astype(np.float32)
    lane_h = np.arange(LANES) // RET_DK
    qdec = np.exp((tt[:, None] + 1.0) * lg[lane_h][None, :])
    kdec = np.exp((r - 1.0 - tt[:, None]) * lg[lane_h][None, :])
    rdec = np.broadcast_to(np.exp(r * lg[lane_h])[:, None], (LANES, LANES))
    dec = np.stack([qdec, kdec, rdec]).astype(np.float32)
    half = RET_DK // 2
    pos = pos0 + jnp.asarray((np.arange(nc)[:, None] * qt + tt[None, :]).reshape(-1))
    inv = ROPE_BASE ** (-jnp.arange(half, dtype=F32) / half)
    ang = pos.astype(F32)[:, None] * inv
    reps = LANES // half
    cos, sin = jnp.tile(jnp.cos(ang), (1, reps)), jnp.tile(jnp.sin(ang), (1, reps))
    first = (np.arange(LANES) % RET_DK) < half
    trig = jnp.stack([cos, jnp.where(first, -sin, 0.0), jnp.where(first, 0.0, sin)])
    to_blocked = np.zeros((qt, qt), np.float32)
    to_blocked[idx, (idx % SUBLANES) * nk + idx // SUBLANES] = 1.0
    perm = jnp.asarray(np.stack([to_blocked, to_blocked.T])).astype(MXU_DTYPE)
    return dict(trig=trig, tri=jnp.asarray(tri), dmask=jnp.asarray(dmask), dec=jnp.asarray(dec), perm=perm)


def _state_layout(depth, bsz, ns, gw, sn):
    hk, dv = RET_H * RET_DK, gw // RET_H
    if ns == 1:
        def per_seq(rows, w):
            return (depth, bsz, rows, w), (None, None, rows, w), lambda b, c, l: (l[0], b, 0, 0)
        def matrix(h, rows, w):
            return (depth, bsz, h, rows, w), (None, None, h, rows, w), lambda b, c, l: (l[0], b, 0, 0, 0)
        return dict(lconv=per_seq(CONV_K - 1, gw), lh=per_seq(1, gw), s5r=per_seq(1, sn), s5i=per_seq(1, sn),
                    sconv=per_seq(CONV_K - 1, 2 * gw), sh=matrix(gw // SSD_HD, SSD_HD, SSD_N),
                    ret=matrix(RET_H, RET_DK, dv))

    def flat(per, w):
        return (depth, bsz * per, w), (None, ns * per, w), lambda b, c, l: (l[0], b, 0)

    def conv(w):
        return (depth, CONV_K - 1, bsz, w), (None, CONV_K - 1, ns, w), lambda b, c, l: (l[0], 0, b, 0)
    def matrix(h, rows, w):
        return (depth, bsz, h, rows, w), (None, ns, h, rows, w), lambda b, c, l: (l[0], b, 0, 0, 0)
    return dict(lconv=conv(gw), lh=flat(1, gw), s5r=flat(1, sn), s5i=flat(1, sn), sconv=conv(2 * gw),
                sh=matrix(gw // SSD_HD, SSD_HD, SSD_N), ret=matrix(RET_H, RET_DK, dv))


def _mixer(l, proj, consts, mw, state_in, state_out, *, nb, nc, qt, ns, r, d):
    dp = proj.shape[-1]
    gw = d // 4
    sn = mw["pw"].shape[-1]
    depth = mw["pw"].shape[0]
    layout = _state_layout(depth, nb * ns, ns, gw, sn)
    assert (state_in is None) == (ns == 1)

    def layer_spec(a):
        return pl.BlockSpec((None,) + a.shape[1:], lambda b, c, l: (l[0],) + (0,) * (a.ndim - 1))

    def const_spec(a):
        return pl.BlockSpec(a.shape, lambda b, c, l: (0,) * a.ndim)

    def state_spec(nm):
        _, block, index = layout[nm]
        return pl.BlockSpec(block, index)

    if ns == 1:
        io_spec = lambda w: pl.BlockSpec((qt, w), lambda b, c, l: (b * nc + c, 0))
        y_shape = (nb * nc * qt, d)
        cnames = ("trig", "tri", "dmask", "dec", "perm")
    else:
        proj = proj.reshape(r, nb * ns, dp)
        io_spec = lambda w: pl.BlockSpec((r, ns, w), lambda b, c, l: (0, b, 0))
        y_shape = (r, nb * ns, d)
        cnames = ("trig", "tri", "dmask", "dec")
    names = ("proj",) + cnames + MIX_WEIGHTS
    operands = [proj] + [consts[k] for k in cnames] + [mw[k] for k in MIX_WEIGHTS]
    in_specs = [io_spec(dp), pl.BlockSpec((3, qt, LANES), lambda b, c, l: (0, c, 0))] + \
               [const_spec(consts[k]) for k in cnames[1:]] + [layer_spec(mw[k]) for k in MIX_WEIGHTS]
    first_alias = 1 + len(operands)
    if state_in is not None:
        assert state_in is state_out
        names += tuple(nm + "_i" for nm in STATE_NAMES)
        in_specs += [state_spec(nm) for nm in STATE_NAMES]
    else:
        names += tuple(nm + "_alias" for nm in STATE_NAMES)
        in_specs += [pl.BlockSpec(memory_space=pl.ANY) for _ in STATE_NAMES]
    operands += [state_out[nm] for nm in STATE_NAMES]
    names += ("y",) + tuple(nm + "_o" for nm in STATE_NAMES)
    out_specs = [io_spec(d)] + [state_spec(nm) for nm in STATE_NAMES]
    out_shape = [jax.ShapeDtypeStruct(y_shape, MXU_DTYPE)] + \
                [jax.ShapeDtypeStruct(layout[nm][0], F32) for nm in STATE_NAMES]
    scratch = []
    if ns == 1:
        tail = (CONV_K - 1) * SUBLANES
        scratch_shapes = dict(cb_a=(tail, gw), cb_c=(tail, 2 * gw), c_lh=(1, gw), c_s5r=(1, sn),
                              c_s5i=(1, sn), c_sh=(SSD_G * SSD_N, gw), c_ret=(RET_H * RET_DK, gw))
        names += tuple(scratch_shapes)
        scratch = [pltpu.VMEM(shape, F32) for shape in scratch_shapes.values()]
    grid_spec = pltpu.PrefetchScalarGridSpec(num_scalar_prefetch=1, grid=(nb, nc),
                                             in_specs=in_specs, out_specs=out_specs, scratch_shapes=scratch)
    outs = pl.pallas_call(
        functools.partial(_mix_body, names=names, qt=qt, ns=ns, r=r, nc=nc),
        grid_spec=grid_spec,
        out_shape=out_shape,
        input_output_aliases={first_alias + k: 1 + k for k in range(len(STATE_NAMES))},
        compiler_params=pltpu.CompilerParams(dimension_semantics=("arbitrary", "arbitrary"),
                                             vmem_limit_bytes=VMEM_LIMIT),
        name="mixer",
    )(l, *operands)
    return outs[0].reshape(-1, d), dict(zip(STATE_NAMES, outs[1:]))


def _post_body(l_ref, y_ref, x_ref, g1_ref, sc_ref, sh_ref, g2_ref, lt_ref, wo_ref, wr_ref, br_ref,
               wg_ref, wu_ref, wd_ref, *rest):
    o_ref, xs_ref, cs_ref, os_ref = rest[-4:]
    del l_ref
    tm, d = x_ref.shape
    tms = xs_ref.shape[0]
    x = x_ref[...] + _token_rows(g1_ref[...], tm) * jnp.dot(y_ref[...], wo_ref[...], preferred_element_type=F32)
    h = _rms(x) * (1.0 + _token_rows(sc_ref[...], tm)) + _token_rows(sh_ref[...], tm)
    hb = h.astype(MXU_DTYPE)
    h_lo = (h - hb.astype(F32)).astype(MXU_DTYPE)
    wr = wr_ref[...]
    l12 = jnp.dot(hb, wr, preferred_element_type=F32)
    logits = (l12[:, 0:LANES] + l12[:, LANES:2 * LANES]
              + jnp.dot(h_lo, wr[:, 0:LANES], preferred_element_type=F32) + br_ref[...])
    logits_t = jnp.transpose(logits)
    col = [logits_t[k:k + 1, :] for k in range(N_GROUPS + N_EXPERTS)]

    def first_max(vals, allowed=None):
        neg = jnp.full_like(vals[0], -jnp.inf)
        cand = vals if allowed is None else [jnp.where(al > 0.0, v, neg) for v, al in zip(vals, allowed)]
        m = functools.reduce(jnp.maximum, cand)
        rem = jnp.ones_like(vals[0])
        hot = []
        for v in cand:
            f = jnp.where(v >= m, rem, 0.0)
            rem = rem - f
            hot.append(f)
        return hot, m

    grp, gmax = first_max(col[:N_GROUPS])
    gate = 1.0 / functools.reduce(lambda s, v: s + v, [jnp.exp(v - gmax) for v in col[:N_GROUPS]])
    le = [functools.reduce(lambda s, v: s + v,
                           [grp[gi] * col[N_GROUPS + gi * N_PER_GROUP + j] for gi in range(N_GROUPS)])
          for j in range(N_PER_GROUP)]
    top1, m1 = first_max(le)
    top2, m2 = first_max(le, [1.0 - f for f in top1])
    e2 = jnp.exp(m2 - m1)
    w1 = 1.0 / (1.0 + e2)
    w2 = e2 * w1
    wgrp = [(top1[j] * w1 + top2[j] * w2) * gate for j in range(N_PER_GROUP)]

    sub8 = lax.broadcasted_iota(jnp.int32, (SUBLANES, 1), 0)
    ghot_t = functools.reduce(lambda s, v: s + v, [jnp.where(sub8 == gi, grp[gi], 0.0) for gi in range(N_GROUPS)])
    before = _mm_nt(ghot_t, lt_ref[...])
    count = jnp.sum(ghot_t, axis=1, keepdims=True)
    cnt = [count[gi:gi + 1, :] for gi in range(N_GROUPS)]
    base = [jnp.zeros((1, 1), F32)]
    for gi in range(1, N_GROUPS):
        base.append(jnp.floor((base[-1] + cnt[gi - 1] + (PACK - 1.0)) * (1.0 / PACK)) * PACK)
    pos_row = functools.reduce(lambda s, v: s + v,
                               [grp[gi] * (base[gi] + before[gi:gi + 1, :]) for gi in range(N_GROUPS)])
    pos = jnp.transpose(jnp.broadcast_to(pos_row, (LANES, tm)))[:, 0:1]
    slot = lax.broadcasted_iota(jnp.int32, (tms, tm), 0).astype(F32)
    gather = jnp.where(slot == pos_row, 1.0, 0.0).astype(MXU_DTYPE)
    slot_l = lax.broadcasted_iota(jnp.int32, (tm, tms), 1).astype(F32)
    scatter = jnp.where(slot_l == pos, 1.0, 0.0).astype(MXU_DTYPE)

    xs_ref[...] = jnp.dot(gather, hb, preferred_element_type=F32).astype(xs_ref.dtype)
    sub_e = lax.broadcasted_iota(jnp.int32, (cs_ref.shape[1], 1), 0)
    comb_t = functools.reduce(lambda s, v: s + v,
                              [jnp.where(sub_e == e, grp[e // N_PER_GROUP] * wgrp[e % N_PER_GROUP], 0.0)
                               for e in range(N_EXPERTS)])
    c1, c2, c3 = _split3(comb_t)
    nt = lambda p: lax.dot_general(gather, p, (((1,), (1,)), ((), ())), preferred_element_type=F32)
    cs_ref[...] = nt(c1) + nt(c2) + nt(c3)
    os_ref[...] = jnp.zeros(os_ref.shape, os_ref.dtype)
    for gi in range(N_GROUPS):
        start = base[gi][0, 0].astype(jnp.int32)
        nblk = jnp.floor((cnt[gi][0, 0] + (EXPERT_ROWS - 1.0)) * (1.0 / EXPERT_ROWS)).astype(jnp.int32)

        def block(k, carry, gi=gi, start=start):
            rows = pl.ds(pl.multiple_of(start + k * EXPERT_ROWS, PACK), EXPERT_ROWS)
            xb = xs_ref[rows, :]
            cb = cs_ref[rows, :]
            acc = jnp.zeros((EXPERT_ROWS, d), F32)
            for j in range(N_PER_GROUP):
                e = gi * N_PER_GROUP + j
                gt = jnp.dot(xb, wg_ref[e], preferred_element_type=F32)
                he = gt * _sigmoid(gt) * jnp.dot(xb, wu_ref[e], preferred_element_type=F32)
                acc = acc + jnp.dot((he * cb[:, e:e + 1]).astype(MXU_DTYPE), wd_ref[e], preferred_element_type=F32)
            os_ref[rows, :] += acc
            return carry

        lax.fori_loop(0, nblk, block, 0)
    moe = jnp.dot(scatter, os_ref[...].astype(MXU_DTYPE), preferred_element_type=F32)
    out = x + _token_rows(g2_ref[...], tm) * moe
    if len(rest) == 5:
        out = _rms(out) * rest[0][...]
    o_ref[...] = out


def _post(l, y, x, mod, pw, tm, rows_per_seq, final_w=None, in_place=False):
    t, d = x.shape
    final = [] if final_w is None else [final_w.reshape(1, d)]
    tms = tm + EXPERT_ROWS + PACK * N_GROUPS
    ltri = jnp.asarray(np.tril(np.ones((tm, tm), np.float32), -1)).astype(MXU_DTYPE)

    def layer_spec(a):
        return pl.BlockSpec((None,) + a.shape[1:], lambda i, l: (l[0],) + (0,) * (a.ndim - 1),
                            pipeline_mode=pl.Buffered(1))

    grid_spec = pltpu.PrefetchScalarGridSpec(
        num_scalar_prefetch=1,
        grid=(t // tm,),
        in_specs=[
            pl.BlockSpec((tm, d), lambda i, l: (i, 0)),
            pl.BlockSpec((tm, d), lambda i, l: (i, 0)),
            _mod_spec(mod, 2, tm, rows_per_seq),
            _mod_spec(mod, 4, tm, rows_per_seq),
            _mod_spec(mod, 3, tm, rows_per_seq),
            _mod_spec(mod, 5, tm, rows_per_seq),
            pl.BlockSpec((tm, tm), lambda i, l: (0, 0)),
        ] + [layer_spec(pw[k]) for k in ("w_out", "w_route", "b_route", "w_gate", "w_up", "w_down")]
          + [pl.BlockSpec((1, d), lambda i, l: (0, 0)) for _ in final],
        out_specs=pl.BlockSpec((tm, d), lambda i, l: (i, 0)),
        scratch_shapes=[pltpu.VMEM((tms, d), MXU_DTYPE), pltpu.VMEM((tms, 2 * N_EXPERTS), F32),
                        pltpu.VMEM((tms, d), F32)],
    )
    return pl.pallas_call(
        _post_body,
        grid_spec=grid_spec,
        out_shape=jax.ShapeDtypeStruct((t, d), F32),
        input_output_aliases={2: 0} if in_place else {},
        compiler_params=pltpu.CompilerParams(dimension_semantics=("arbitrary",), vmem_limit_bytes=VMEM_LIMIT),
        name="post",
    )(l, y, x, mod, mod, mod, mod, ltri,
      *[pw[k] for k in ("w_out", "w_route", "b_route", "w_gate", "w_up", "w_down")], *final)


def _block_diag(w):
    depth, nblk, bi, bj = w.shape
    eye = jnp.eye(nblk, dtype=w.dtype)
    return (w[:, :, :, None, :] * eye[None, :, None, :, None]).reshape(depth, nblk * bi, nblk * bj)


def _pad_lanes(a, width):
    return jnp.pad(a, [(0, 0)] * (a.ndim - 1) + [(0, width - a.shape[-1])])


def _prep_weights(w_in, lru_conv_w, lru_conv_b, lru_wa, lru_ba, lru_wx, lru_bx, lru_lambda,
                  s5_a_re, s5_a_im, s5_b_re, s5_b_im, s5_c_re, s5_c_im, s5_d, s5_log_dt, s5_w_glu,
                  ssd_conv_w, ssd_conv_b, ssd_dt_bias, ssd_a_log, ssd_d, mix_norm, qt):
    depth, d, _ = w_in.shape
    gw = d // 4
    n_ssd_h = gw // SSD_HD
    xbc_end = 4 * gw + gw + 2 * SSD_G * SSD_N
    dt_cols = jnp.repeat(w_in[..., xbc_end:xbc_end + n_ssd_h], SSD_HD, axis=-1)
    w_in_p = jnp.concatenate([w_in[..., :xbc_end], w_in[..., xbc_end + n_ssd_h:], dt_cols], axis=-1)

    def row(a):
        return _pad_lanes(a.reshape(depth, 1, -1), d)

    vec = jnp.concatenate([
        row(lru_conv_b), row(jnp.concatenate([lru_ba, lru_bx], -1)), row(lru_lambda), row(s5_d),
        row(ssd_conv_b), row(jnp.repeat(ssd_dt_bias, SSD_HD, -1)), row(jnp.repeat(ssd_a_log, SSD_HD, -1)),
        row(jnp.repeat(ssd_d, SSD_HD, -1)), row(mix_norm),
        _pad_lanes(lru_conv_w, d), _pad_lanes(ssd_conv_w, d),
        jnp.zeros((depth, V_ROWS - V_SCW - CONV_K, d), F32)], axis=1)

    wgate = jnp.concatenate([_block_diag(lru_wa), _block_diag(lru_wx)], axis=-1)

    dt = jnp.exp(s5_log_dt)[..., None]
    lr, li = s5_a_re, s5_a_im
    mag = jnp.exp(lr * dt)
    ab_re, ab_im = mag * jnp.cos(li * dt), mag * jnp.sin(li * dt)
    den = lr * lr + li * li
    q_re = ((ab_re - 1.0) * lr + ab_im * li) / den
    q_im = (ab_im * lr - (ab_re - 1.0) * li) / den
    bb_re = q_re[..., None] * s5_b_re - q_im[..., None] * s5_b_im
    bb_im = q_re[..., None] * s5_b_im + q_im[..., None] * s5_b_re
    bbm = jnp.concatenate([_block_diag(jnp.swapaxes(bb_re, -1, -2)), _block_diag(jnp.swapaxes(bb_im, -1, -2))], -1)
    cre = _block_diag(jnp.swapaxes(s5_c_re, -1, -2))
    cim = _block_diag(jnp.swapaxes(s5_c_im, -1, -2))
    a_re, a_im = ab_re.reshape(depth, -1), ab_im.reshape(depth, -1)
    pr, pi = a_re, a_im
    rows = []
    for _ in range(int(math.log2(qt))):
        rows += [pr, pi]
        pr, pi = pr * pr - pi * pi, 2.0 * pr * pi
    pw = jnp.stack(rows, axis=1)
    pw = jnp.pad(pw, ((0, 0), (0, (-pw.shape[1]) % SUBLANES), (0, 0)))
    pr, pi = a_re, a_im
    k_re, k_im = [], []
    for _ in range(qt // SUBLANES):
        k_re.append(pr)
        k_im.append(pi)
        pr, pi = pr * a_re - pi * a_im, pr * a_im + pi * a_re
    pk = jnp.stack(k_re + k_im, axis=1)

    mw = dict(vec=vec, wgate=wgate.astype(MXU_DTYPE), bbm=bbm.astype(MXU_DTYPE), cre=cre.astype(MXU_DTYPE),
              cim=cim.astype(MXU_DTYPE), wglu=s5_w_glu.astype(MXU_DTYPE), pw=pw, pk=pk)
    return w_in_p.astype(MXU_DTYPE), mw


def _states_to_layout(states, layout, ns):
    out = {}
    for nm, s in zip(STATE_NAMES, states):
        if nm in ("lconv", "sconv"):
            s = s.transpose(0, 2, 1, 3)
        out[nm] = s.reshape(layout[nm][0])
    return out


def _states_from_layout(arrs, ns, ref_shapes):
    out = []
    for nm, shape in zip(STATE_NAMES, ref_shapes):
        a = arrs[nm]
        if ns > 1 and nm in ("lconv", "sconv"):
            a = a.transpose(0, 2, 1, 3)
        out.append(a.reshape(shape))
    return tuple(out)


def _trunk(x, mod, states, ref_shapes, pos0, w_in_p, mw, pw, final_norm, *, qt, ns, tm):
    bsz, seq_len, d = x.shape
    depth = w_in_p.shape[0]
    gw = d // 4
    r = qt // ns
    nc = seq_len // r if ns == 1 else 1
    assert (ns == 1 and seq_len % qt == 0 and states is None) or (ns > 1 and r == seq_len and bsz % ns == 0)
    assert r >= CONV_K and (r & (r - 1)) == 0
    nb = bsz // ns
    tok = bsz * seq_len
    consts = _mix_consts(qt, ns, r, nc, pos0)
    layout = _state_layout(depth, bsz, ns, gw, mw["pw"].shape[-1])
    if states is None:
        state_out = {nm: jnp.zeros(layout[nm][0], F32) for nm in STATE_NAMES}
    else:
        state_out = _states_to_layout(states, layout, ns)
    x2 = x.reshape(tok, d) if ns == 1 else x.transpose(1, 0, 2).reshape(tok, d)
    blocked_chunk = qt if ns == 1 else None

    def layer(carry, l, final_w=None, in_place=True):
        xc, st = carry
        lv = jnp.reshape(l, (1,)).astype(jnp.int32)
        proj = _inproj(lv, xc, mod, w_in_p, tm, seq_len, blocked_chunk)
        y, st = _mixer(lv, proj, consts, mw, None if states is None else st, st,
                       nb=nb, nc=nc, qt=qt, ns=ns, r=r, d=d)
        xn = _post(lv, y, xc, mod, pw, tm, seq_len, final_w, in_place)
        return (xn, st), None

    assert depth >= 2
    carry, _ = layer((x2, state_out), jnp.int32(0), in_place=False)
    carry, _ = lax.scan(layer, carry, jnp.arange(1, depth - 1))
    (y, st), _ = layer(carry, jnp.int32(depth - 1), final_norm)
    y = y.reshape(bsz, seq_len, d) if ns == 1 else y.reshape(seq_len, bsz, d).transpose(1, 0, 2)
    return y, _states_from_layout(st, ns, ref_shapes)


def kernel(x_prompt, x_sample, state_lru_conv, state_lru_h, state_s5_re, state_s5_im, state_ssd_conv, state_ssd_h, state_ret, c_prompt, c_sample, w_ada, b_ada, w_in, lru_conv_w, lru_conv_b, lru_wa, lru_ba, lru_wx, lru_bx, lru_lambda, s5_a_re, s5_a_im, s5_b_re, s5_b_im, s5_c_re, s5_c_im, s5_d, s5_log_dt, s5_w_glu, ssd_conv_w, ssd_conv_b, ssd_dt_bias, ssd_a_log, ssd_d, mix_norm, w_out, w_route_group, b_route_group, w_route_exp, b_route_exp, w_exp_gate, w_exp_up, w_exp_down, final_norm):
    bp, lp, d = x_prompt.shape
    bs, ls, _ = x_sample.shape
    depth = w_in.shape[0]
    qt = LANES
    ns_s = qt // ls

    mod_p, mod_s = _modulation(jnp.concatenate([c_prompt, c_sample], axis=0), bp, w_ada, b_ada)
    mod_p = mod_p.reshape(depth, N_MOD, bp, 1, d)

    w_in_p, mw = _prep_weights(w_in, lru_conv_w, lru_conv_b, lru_wa, lru_ba, lru_wx, lru_bx, lru_lambda,
                               s5_a_re, s5_a_im, s5_b_re, s5_b_im, s5_c_re, s5_c_im, s5_d, s5_log_dt, s5_w_glu,
                               ssd_conv_w, ssd_conv_b, ssd_dt_bias, ssd_a_log, ssd_d, mix_norm, qt)
    w_route = _pad_lanes(jnp.concatenate([w_route_group, w_route_exp.reshape(depth, d, N_EXPERTS)], -1), LANES)
    b_route = _pad_lanes(jnp.concatenate([b_route_group, b_route_exp.reshape(depth, N_EXPERTS)], -1), LANES)
    w_route_hi = w_route.astype(MXU_DTYPE)
    w_route_lo = (w_route - w_route_hi.astype(F32)).astype(MXU_DTYPE)
    w_route = jnp.concatenate([w_route_hi, w_route_lo], axis=-1)
    pw = dict(w_out=w_out.astype(MXU_DTYPE), w_route=w_route, b_route=b_route.reshape(depth, 1, LANES),
              w_gate=w_exp_gate.astype(MXU_DTYPE), w_up=w_exp_up.astype(MXU_DTYPE),
              w_down=w_exp_down.astype(MXU_DTYPE))
    w_in_p, mw, pw = lax.optimization_barrier((w_in_p, mw, pw))

    states_s = (state_lru_conv, state_lru_h, state_s5_re, state_s5_im, state_ssd_conv, state_ssd_h, state_ret)
    shapes_s = [s.shape for s in states_s]
    shapes_p = [(depth, bp) + s[2:] for s in shapes_s]
    y_p, new_p = _trunk(x_prompt, mod_p, None, shapes_p, 0, w_in_p, mw, pw, final_norm,
                        qt=qt, ns=1, tm=min(512, lp))
    y_s, new_s = _trunk(x_sample, mod_s, states_s, shapes_s, PAST_LEN, w_in_p, mw, pw, final_norm,
                        qt=qt, ns=ns_s, tm=bs * ls)
    out = [y_p, y_s]
    for a, b in zip(new_p, new_s):
        out += [a, b]
    return tuple(out)
```

```python
import functools
import math

import numpy as np
import jax
import jax.numpy as jnp
from jax import lax
from jax.experimental import pallas as pl
from jax.experimental.pallas import tpu as pltpu

F32 = jnp.float32
MXU_DTYPE = jnp.bfloat16

EPS = 1e-6
CONV_K = 4
LRU_C = 8.0
SSD_HD = 64
SSD_G = 2
SSD_N = 64
RET_H = 4
RET_DK = 32
ROPE_BASE = 10000.0
N_GROUPS = 4
N_PER_GROUP = 4
N_EXPERTS = N_GROUPS * N_PER_GROUP
N_MOD = 6
PAST_LEN = 16384

LANES = 128
SUBLANES = 8
PACK = 16
VMEM_LIMIT = 56 * 1024 * 1024
EXPERT_ROWS = 144


def _mm(a, b):
    return jnp.dot(a.astype(MXU_DTYPE), b.astype(MXU_DTYPE), preferred_element_type=F32)


def _mm_nt(a, b):
    return lax.dot_general(a.astype(MXU_DTYPE), b.astype(MXU_DTYPE), (((1,), (1,)), ((), ())),
                           preferred_element_type=F32)


def _mm_tn(a, b):
    return lax.dot_general(a.astype(MXU_DTYPE), b.astype(MXU_DTYPE), (((0,), (0,)), ((), ())),
                           preferred_element_type=F32)


def _split3(x):
    x1 = x.astype(MXU_DTYPE)
    r1 = x - x1.astype(F32)
    x2 = r1.astype(MXU_DTYPE)
    x3 = (r1 - x2.astype(F32)).astype(MXU_DTYPE)
    return x1, x2, x3


def _sigmoid(x):
    return 0.5 * jnp.tanh(0.5 * x) + 0.5


def _rms(x):
    return x * lax.rsqrt(jnp.mean(x * x, axis=-1, keepdims=True) + EPS)


def _token_rows(v, tm):
    n = v.shape[0]
    return v if n in (1, tm) else jnp.tile(v, (tm // n, 1))


def _mod_body(c_ref, w_ref, b_ref, first_ref, rest_ref):
    c = c_ref[...]
    m = _mm(c * _sigmoid(c), w_ref[...]) + b_ref[...]
    n_first = first_ref.shape[0]
    first_ref[...] = m[0:n_first]
    rest_ref[...] = m[n_first:]


def _modulation(c_all, n_first, w_ada, b_ada):
    depth, d, _ = w_ada.shape
    nb = c_all.shape[0]
    return pl.pallas_call(
        _mod_body,
        grid=(depth, N_MOD),
        in_specs=[
            pl.BlockSpec((nb, d), lambda l, k: (0, 0)),
            pl.BlockSpec((None, d, d), lambda l, k: (l, 0, k)),
            pl.BlockSpec((None, None, 1, d), lambda l, k: (l, k, 0, 0)),
        ],
        out_specs=[pl.BlockSpec((None, None, n_first, d), lambda l, k: (l, k, 0, 0)),
                   pl.BlockSpec((None, None, nb - n_first, d), lambda l, k: (l, k, 0, 0))],
        out_shape=[jax.ShapeDtypeStruct((depth, N_MOD, n_first, d), F32),
                   jax.ShapeDtypeStruct((depth, N_MOD, nb - n_first, d), F32)],
        compiler_params=pltpu.CompilerParams(dimension_semantics=("arbitrary", "arbitrary"),
                                             vmem_limit_bytes=VMEM_LIMIT),
        name="modulation",
    )(c_all, w_ada, b_ada.reshape(depth, N_MOD, 1, d))


def _mod_spec(mod, k, tm, rows_per_seq):
    if mod.ndim == 5:
        tiles_per_seq = rows_per_seq // tm
        return pl.BlockSpec((None, None, None, 1, mod.shape[-1]),
                            lambda i, l: (l[0], k, i // tiles_per_seq, 0, 0))
    return pl.BlockSpec((None, None) + mod.shape[2:], lambda i, l: (l[0], k, 0, 0))


def _inproj_body(l_ref, x_ref, sc_ref, sh_ref, w_ref, o_ref, *, blocked_chunk):
    tm = x_ref.shape[0]
    h = _rms(x_ref[...]) * (1.0 + _token_rows(sc_ref[...], tm)) + _token_rows(sh_ref[...], tm)
    hb = h.astype(MXU_DTYPE)
    if blocked_chunk is not None:
        qt, nk = blocked_chunk, blocked_chunk // SUBLANES
        r_i = lax.broadcasted_iota(jnp.int32, (qt, qt), 0)
        c_i = lax.broadcasted_iota(jnp.int32, (qt, qt), 1)
        to_blocked = jnp.where(c_i == (r_i % SUBLANES) * nk + r_i // SUBLANES, 1.0, 0.0).astype(MXU_DTYPE)
        hb = jnp.concatenate([jnp.dot(to_blocked, hb[j * qt:(j + 1) * qt], preferred_element_type=F32)
                              for j in range(tm // qt)], axis=0).astype(MXU_DTYPE)
    o_ref[...] = jnp.dot(hb, w_ref[...], preferred_element_type=F32)


def _inproj(l, x, mod, w_in, tm, rows_per_seq, blocked_chunk=None):
    t, d = x.shape
    dp = w_in.shape[-1]
    assert blocked_chunk is None or (mod.ndim == 5 and tm % blocked_chunk == 0)
    grid_spec = pltpu.PrefetchScalarGridSpec(
        num_scalar_prefetch=1,
        grid=(t // tm,),
        in_specs=[
            pl.BlockSpec((tm, d), lambda i, l: (i, 0)),
            _mod_spec(mod, 1, tm, rows_per_seq),
            _mod_spec(mod, 0, tm, rows_per_seq),
            pl.BlockSpec((None, d, dp), lambda i, l: (l[0], 0, 0)),
        ],
        out_specs=pl.BlockSpec((tm, dp), lambda i, l: (i, 0)),
    )
    return pl.pallas_call(
        functools.partial(_inproj_body, blocked_chunk=blocked_chunk),
        grid_spec=grid_spec,
        out_shape=jax.ShapeDtypeStruct((t, dp), F32),
        compiler_params=pltpu.CompilerParams(dimension_semantics=("arbitrary",), vmem_limit_bytes=VMEM_LIMIT),
        name="inproj",
    )(l, x, mod, mod, w_in)


V_LCB, V_BGATE, V_LAM, V_S5D, V_SCB, V_DTB, V_ALOG, V_SSDD, V_GAIN, V_LCW, V_SCW = 0, 1, 2, 3, 4, 5, 6, 7, 8, 9, 13
V_ROWS = 24
STATE_NAMES = ("lconv", "lh", "s5r", "s5i", "sconv", "sh", "ret")
MIX_WEIGHTS = ("vec", "wgate", "bbm", "cre", "cim", "wglu", "pw", "pk")


def _mix_body(l_ref, *refs, names, qt, ns, r, nc):
    del l_ref
    g = dict(zip(names, refs))
    gw = g["lh_o"].shape[-1]
    sn = g["s5r_o"].shape[-1]
    nlev = int(math.log2(r))
    nk = qt // SUBLANES
    c = pl.program_id(1)

    if ns == 1:
        @pl.when(c == 0)
        def _zero_carries():
            for nm in ("c_lh", "c_s5r", "c_s5i", "c_sh", "c_ret"):
                g[nm][...] = jnp.zeros(g[nm].shape, F32)
            for nm in ("cb_a", "cb_c"):
                g[nm][...] = jnp.zeros(g[nm].shape, F32)

    proj = g["proj"]
    vec = g["vec"]
    pw = g["pw"]

    def pcols(a, b):
        v = proj[..., a:b]
        return v if ns == 1 else v.reshape(qt, b - a)

    y_parts = []

    def ysave(a, b, val):
        val = val.astype(g["y"].dtype)
        if ns == 1:
            y_parts.append(val)
        else:
            g["y"][..., a:b] = val.reshape(r, ns, b - a)

    row = lax.broadcasted_iota(jnp.int32, (qt, 1), 0)
    t = (row % SUBLANES) * nk + row // SUBLANES if ns == 1 else row // ns
    sub = lax.broadcasted_iota(jnp.int32, (SUBLANES, 1), 0)

    def down(x, d, fill=0.0):
        return jnp.where(t >= d, pltpu.roll(x, d * ns, 0), fill)

    def up(x, d):
        return jnp.where(t + d < r, pltpu.roll(x, qt - d * ns, 0), 0.0)

    def sdown(x, d, fill=0.0):
        return jnp.where(sub >= d, pltpu.roll(x, d, 0), fill)

    def slabs(x):
        return [x[SUBLANES * k:SUBLANES * (k + 1)] for k in range(nk)]

    def first_rows(carry, init):
        if ns == 1:
            return jnp.where(row == 0, g[carry][...], 0.0)
        h0 = g[init][...]
        return jnp.concatenate([h0, jnp.zeros((qt - ns, h0.shape[1]), F32)], axis=0)

    def conv(xraw, nm, w0, b):
        width = xraw.shape[-1]
        acc = vec[b:b + 1, 0:width] + vec[w0 + 3:w0 + 4, 0:width] * xraw
        if ns == 1:
            cb = g["cb_" + nm]
            xs_k = slabs(xraw)
            wrap = [jnp.where(sub >= 1, pltpu.roll(xs_k[nk - j], 1, 0),
                              pltpu.roll(cb[(CONV_K - 1 - j) * SUBLANES:(CONV_K - j) * SUBLANES, :], 1, 0))
                    for j in range(1, CONV_K)]
            for m in range(1, CONV_K):
                sh = jnp.concatenate([wrap[m - k - 1] for k in range(m)] + xs_k[0:nk - m], axis=0)
                acc = acc + vec[w0 + 3 - m:w0 + 4 - m, 0:width] * sh
            cb[...] = xraw[qt - (CONV_K - 1) * SUBLANES:, :]
            g[{"a": "lconv_o", "c": "sconv_o"}[nm]][...] = jnp.concatenate(
                [xs_k[nk - j][SUBLANES - 1:SUBLANES, :] for j in range(CONV_K - 1, 0, -1)], axis=0)
            return acc
        o = g[{"a": "lconv_o", "c": "sconv_o"}[nm]]
        buf = g[{"a": "lconv_i", "c": "sconv_i"}[nm]][...].reshape((CONV_K - 1) * ns, width)
        buf = jnp.concatenate([buf, jnp.zeros((qt - (CONV_K - 1) * ns, width), F32)], axis=0)
        for m in range(1, CONV_K):
            prev = buf if m == 3 else pltpu.roll(buf, qt - (3 - m) * ns, 0)
            sh = jnp.where(t >= m, pltpu.roll(xraw, m * ns, 0), prev)
            acc = acc + vec[w0 + 3 - m:w0 + 4 - m, 0:width] * sh
        o[...] = xraw[(r - (CONV_K - 1)) * ns:, :].reshape(CONV_K - 1, ns, width)
        return acc

    def save_last(nm, h):
        if ns == 1:
            g["c_" + nm][...] = h[qt - 1:qt, :]
            g[nm + "_o"][...] = h[qt - 1:qt, :]
        else:
            g[nm + "_o"][...] = h[qt - ns:, :]

    def seq_mask(width):
        lane_seq = lax.broadcasted_iota(jnp.int32, (qt, ns * width), 1) // width
        row_seq = lax.broadcasted_iota(jnp.int32, (qt, ns * width), 0) % ns
        return (lane_seq == row_seq).astype(F32)

    gain = vec[V_GAIN:V_GAIN + 1, :]

    xc = conv(pcols(0, gw), "a", V_LCW, V_LCB)
    ga = pcols(gw, 2 * gw)
    pre = _mm(xc, g["wgate"][...]) + vec[V_BGATE:V_BGATE + 1, 0:2 * gw]
    rg = _sigmoid(pre[:, 0:gw])
    ig = _sigmoid(pre[:, gw:2 * gw])
    log_a = -LRU_C * rg * jax.nn.softplus(-vec[V_LAM:V_LAM + 1, 0:gw])
    a = jnp.exp(log_a)
    b = jnp.sqrt(-jnp.tanh(log_a) * (a * a + 1.0)) * (ig * xc)
    b = b + a * first_rows("c_lh", "lh_i")
    if ns == 1:
        a_s, b_s = slabs(a), slabs(b)
        hs, ps = [b_s[0]], [a_s[0]]
        for k in range(1, nk):
            hs.append(a_s[k] * hs[-1] + b_s[k])
            ps.append(a_s[k] * ps[-1])
        e, ae = hs[-1], ps[-1]
        for j in range(int(math.log2(SUBLANES))):
            d = 1 << j
            e = ae * sdown(e, d) + e
            ae = ae * sdown(ae, d, 1.0)
        cin = sdown(e, 1)
        b = jnp.concatenate([hs[k] + ps[k] * cin for k in range(nk)], axis=0)
    else:
        for k in range(nlev):
            d = 1 << k
            b = a * down(b, d) + b
            a = a * down(a, d, 1.0)
    save_last("lh", b)
    ya = _rms(b * jax.nn.gelu(ga)) * gain[:, 0:gw]

    u = pcols(2 * gw, 3 * gw)
    bu = _mm(u, g["bbm"][...])
    p_re, p_im = pw[0:1, :], pw[1:2, :]
    h0r, h0i = first_rows("c_s5r", "s5r_i"), first_rows("c_s5i", "s5i_i")
    hr = bu[:, 0:sn] + (p_re * h0r - p_im * h0i)
    hi = bu[:, sn:2 * sn] + (p_re * h0i + p_im * h0r)
    if ns == 1:
        pk = g["pk"]
        br_s, bi_s = slabs(hr), slabs(hi)
        hrs, his = [br_s[0]], [bi_s[0]]
        for k in range(1, nk):
            hrs.append(br_s[k] + (p_re * hrs[-1] - p_im * his[-1]))
            his.append(bi_s[k] + (p_re * his[-1] + p_im * hrs[-2]))
        er, ei = hrs[-1], his[-1]
        lev0 = int(math.log2(nk))
        for j in range(int(math.log2(SUBLANES))):
            d = 1 << j
            q_re, q_im = pw[2 * (lev0 + j):2 * (lev0 + j) + 1, :], pw[2 * (lev0 + j) + 1:2 * (lev0 + j) + 2, :]
            sr, si = sdown(er, d), sdown(ei, d)
            er, ei = er + (q_re * sr - q_im * si), ei + (q_re * si + q_im * sr)
        cr, ci = sdown(er, 1), sdown(ei, 1)
        hr = jnp.concatenate([hrs[k] + (pk[k:k + 1, :] * cr - pk[nk + k:nk + k + 1, :] * ci) for k in range(nk)], 0)
        hi = jnp.concatenate([his[k] + (pk[k:k + 1, :] * ci + pk[nk + k:nk + k + 1, :] * cr) for k in range(nk)], 0)
    else:
        for k in range(nlev):
            d = 1 << k
            q_re, q_im = pw[2 * k:2 * k + 1, :], pw[2 * k + 1:2 * k + 2, :]
            sr, si = down(hr, d), down(hi, d)
            hr, hi = hr + (q_re * sr - q_im * si), hi + (q_re * si + q_im * sr)
    save_last("s5r", hr)
    save_last("s5i", hi)
    yb = _mm(hr, g["cre"][...]) - _mm(hi, g["cim"][...])
    yb = jax.nn.gelu(yb + vec[V_S5D:V_S5D + 1, 0:gw] * u)
    yb = yb * _sigmoid(_mm(yb, g["wglu"][...]))
    yb = _rms(yb) * gain[:, gw:2 * gw]
    ysave(0, 2 * gw, jnp.concatenate([ya, yb], axis=1))

    tri = g["tri"][...] > 0.0
    xbc = conv(pcols(4 * gw, 6 * gw), "c", V_SCW, V_SCB)
    xbc = xbc * _sigmoid(xbc)
    xs, bm, cm = xbc[:, 0:gw], xbc[:, gw:gw + LANES], xbc[:, gw + LANES:2 * gw]
    dt = jax.nn.softplus(pcols(9 * gw, 10 * gw) + vec[V_DTB:V_DTB + 1, 0:gw])
    a_e = -jnp.exp(vec[V_ALOG:V_ALOG + 1, 0:gw])
    dta = dt * a_e
    if ns == 1:
        cs = slabs(dta)
        for k in range(1, nk):
            cs[k] = cs[k] + cs[k - 1]
        e = cs[-1]
        for j in range(int(math.log2(SUBLANES))):
            e = e + sdown(e, 1 << j)
        cin = sdown(e, 1)
        acum = jnp.concatenate([ck + cin for ck in cs], axis=0)
        alast = acum[qt - 1:qt, :]
    else:
        acum = dta
        for k in range(nlev):
            acum = acum + down(acum, 1 << k)
        suf = dta
        for k in range(nlev):
            suf = suf + up(suf, 1 << k)
        alast = acum + suf - dta
    wend = jnp.exp(alast - acum) * dt
    acum_t = jnp.transpose(acum)
    lane = lax.broadcasted_iota(jnp.int32, (1, gw), 1)
    lane_b = lax.broadcasted_iota(jnp.int32, (1, LANES), 1)
    heads_per_group = gw // SSD_HD // SSD_G
    xdt = xs * dt
    xw = xs * wend
    ydiag = jnp.zeros((qt, gw), F32)
    for grp in range(SSD_G):
        gm = ((lane_b // SSD_N) == grp).astype(F32)
        cb = _mm_nt(cm * gm, bm)
        for hh in range(heads_per_group):
            hd = grp * heads_per_group + hh
            col = acum[:, hd * SSD_HD:hd * SSD_HD + 1]
            rowv = acum_t[hd * SSD_HD:hd * SSD_HD + 1, :]
            dec = jnp.where(tri, jnp.exp(col - rowv), 0.0)
            hm = ((lane // SSD_HD) == hd).astype(F32)
            ydiag = ydiag + _mm(cb * dec, xdt * hm)
    if ns == 1:
        st = g["c_sh"][...]
        srow = lax.broadcasted_iota(jnp.int32, st.shape, 0) // SSD_N
        yoff = _mm(cm, st)
        slane = lax.broadcasted_iota(jnp.int32, st.shape, 1) // (SSD_HD * heads_per_group)
        upd = jnp.where(srow == slane, _mm_tn(bm, xw), 0.0)
        st = st * jnp.exp(acum[qt - 1:qt, :]) + upd
        g["c_sh"][...] = st

        @pl.when(c == nc - 1)
        def _final_ssd_state():
            own = jnp.concatenate([st[grp * SSD_N:(grp + 1) * SSD_N, grp * LANES:(grp + 1) * LANES]
                                   for grp in range(SSD_G)], axis=1)
            own = jnp.concatenate([own, jnp.zeros_like(own)], axis=0)
            g["sh_o"][...] = jnp.transpose(own)[:, 0:SSD_N].reshape(g["sh_o"].shape)
    else:
        sh_i, sh_o = g["sh_i"], g["sh_o"]
        per_seq = gw // SSD_HD * SSD_HD
        cols = [[], []]
        for s in range(0, ns, 2):
            two = jnp.concatenate([sh_i[s].reshape(per_seq, SSD_N), sh_i[s + 1].reshape(per_seq, SSD_N)], axis=1)
            two = jnp.transpose(two)
            for grp in range(SSD_G):
                cols[grp] += [two[0:SSD_N, grp * LANES:(grp + 1) * LANES],
                              two[SSD_N:2 * SSD_N, grp * LANES:(grp + 1) * LANES]]
        st = jnp.concatenate([jnp.concatenate(cols[0], axis=1), jnp.concatenate(cols[1], axis=1)], axis=0)
        srow = lax.broadcasted_iota(jnp.int32, st.shape, 0) // SSD_N
        seqm = seq_mask(LANES)
        lastm = jnp.where(t == r - 1, seqm, 0.0)
        yo, upd, dl = [], None, None
        for grp in range(SSD_G):
            gm = ((lane_b // SSD_N) == grp).astype(F32)
            z = _mm(cm * gm, st) * seqm
            zf = z[:, 0:LANES]
            for s in range(1, ns):
                zf = zf + z[:, s * LANES:(s + 1) * LANES]
            yo.append(zf)
            sl = slice(grp * LANES, (grp + 1) * LANES)
            u_g = _mm_tn(bm, jnp.tile(xw[:, sl], (1, ns)) * seqm)
            d_g = jnp.exp(jnp.sum(jnp.tile(acum[:, sl], (1, ns)) * lastm, axis=0, keepdims=True))
            upd = u_g if grp == 0 else jnp.where(srow == grp, u_g, upd)
            dl = d_g if grp == 0 else jnp.where(srow == grp, d_g, dl)
        yoff = jnp.concatenate(yo, axis=1)
        st = st * dl + upd
        for s in range(0, ns, 2):
            two = jnp.concatenate(
                [jnp.concatenate([st[0:SSD_N, q * LANES:(q + 1) * LANES], st[SSD_N:2 * SSD_N, q * LANES:(q + 1) * LANES]],
                                 axis=1) for q in (s, s + 1)], axis=0)
            two = jnp.transpose(two)
            sh_o[s] = two[:, 0:SSD_N].reshape(sh_o.shape[1:])
            sh_o[s + 1] = two[:, SSD_N:2 * SSD_N].reshape(sh_o.shape[1:])
    yc = ydiag + yoff * jnp.exp(acum) + vec[V_SSDD:V_SSDD + 1, 0:gw] * xs
    z = pcols(3 * gw, 4 * gw)
    yc = yc * (z * _sigmoid(z))
    ysave(2 * gw, 3 * gw, _rms(yc) * gain[:, 2 * gw:3 * gw])

    trig = g["trig"]
    cos, sin_a, sin_b = trig[0], trig[1], trig[2]

    def rope(x):
        return x * cos + pltpu.roll(x, LANES - RET_DK // 2, 1) * sin_a + pltpu.roll(x, RET_DK // 2, 1) * sin_b

    q = rope(pcols(6 * gw, 6 * gw + LANES))
    kk = rope(pcols(6 * gw + LANES, 7 * gw)) * (RET_DK ** -0.5)
    v = pcols(7 * gw, 8 * gw)
    dv = gw // RET_H
    intra = jnp.zeros((qt, gw), F32)
    for hd in range(RET_H):
        qm = ((lane_b // RET_DK) == hd).astype(F32)
        sc = _mm_nt(q * qm, kk) * g["dmask"][hd]
        vm = ((lane // dv) == hd).astype(F32)
        intra = intra + _mm(sc, v * vm)
    dec = g["dec"]
    qd, kd = q * dec[0], kk * dec[1]
    if ns == 1:
        rs = g["c_ret"][...]
        cross = _mm(qd, rs)
        rrow = lax.broadcasted_iota(jnp.int32, rs.shape, 0) // RET_DK
        rlane = lax.broadcasted_iota(jnp.int32, rs.shape, 1) // dv
        upd = jnp.where(rrow == rlane, _mm_tn(kd, v), 0.0)
    else:
        ret_i, ret_o = g["ret_i"], g["ret_o"]
        hk = RET_H * RET_DK
        rs = jnp.concatenate([ret_i[s].reshape(hk, dv) for s in range(ns)], axis=1)
        seqm = seq_mask(dv)
        lo = lane_b < dv
        pairs = []
        upd = jnp.zeros(rs.shape, F32)
        for pp in range(RET_H // 2):
            vp = v[:, pp * LANES:(pp + 1) * LANES]
            vsw = pltpu.roll(vp, dv, 1)
            tots = []
            for hh in range(2):
                qm = ((lane_b // RET_DK) == 2 * pp + hh).astype(F32)
                zc = _mm(qd * qm, rs) * seqm
                acc = zc[:, 0:LANES]
                for s2 in range(1, ns // 2):
                    acc = acc + zc[:, s2 * LANES:(s2 + 1) * LANES]
                tots.append(acc + pltpu.roll(acc, dv, 1))
                vh = jnp.where(lo, vp, vsw) if hh == 0 else jnp.where(lo, vsw, vp)
                upd = upd + _mm_tn(kd * qm, jnp.tile(vh, (1, ns // 2)) * seqm)
            pairs.append(jnp.where(lo, tots[0], tots[1]))
        cross = jnp.concatenate(pairs, axis=1)
    rs = rs * dec[2][:, 0:1] + upd
    if ns == 1:
        g["c_ret"][...] = rs

        @pl.when(c == nc - 1)
        def _final_ret_state():
            own = functools.reduce(lambda s, x: s + x, [rs[:, hd * dv:(hd + 1) * dv] for hd in range(RET_H)])
            g["ret_o"][...] = own.reshape(g["ret_o"].shape)
    else:
        for s in range(ns):
            ret_o[s] = rs[:, s * dv:(s + 1) * dv].reshape(ret_o.shape[1:])
    o = intra + cross
    o2 = o * o
    ms = jnp.zeros((qt, gw), F32)
    for hd in range(RET_H):
        vm = ((lane // dv) == hd).astype(F32)
        ms = ms + vm * jnp.sum(o2 * vm, axis=-1, keepdims=True)
    o = o * lax.rsqrt(ms * (1.0 / dv) + EPS) * gain[:, 3 * gw:4 * gw]
    gg = pcols(8 * gw, 9 * gw)
    ysave(3 * gw, 4 * gw, gg * _sigmoid(gg) * o)
    if ns == 1:
        y_all = jnp.concatenate(y_parts, axis=1)
        g["y"][...] = jnp.dot(g["perm"][1], y_all, preferred_element_type=F32).astype(g["y"].dtype)


def _mix_consts(qt, ns, r, nc, pos0):
    assert qt == LANES, "decay tables are stacked as (3, 128, 128)"
    idx = np.arange(qt)
    nk = qt // SUBLANES
    if ns == 1:
        seq, tt = np.zeros_like(idx), (idx % SUBLANES) * nk + idx // SUBLANES
    else:
        seq, tt = idx % ns, idx // ns
    causal = (seq[:, None] == seq[None, :]) & (tt[:, None] >= tt[None, :])
    tri = causal.astype(np.float32)
    lg = np.log1p(-np.exp2(-5.0 - np.arange(RET_H, dtype=np.float64)))
    rel = (tt[:, None] - tt[None, :]).astype(np.float64)
    dmask = np.where(causal[None], np.exp(np.maximum(rel, 0.0)[None] * lg[:, None, None]), 0.0).astype(np.float32)
    lane_h = np.arange(LANES) // RET_DK
    qdec = np.exp((tt[:, None] + 1.0) * lg[lane_h][None, :])
    kdec = np.exp((r - 1.0 - tt[:, None]) * lg[lane_h][None, :])
    rdec = np.broadcast_to(np.exp(r * lg[lane_h])[:, None], (LANES, LANES))
    dec = np.stack([qdec, kdec, rdec]).astype(np.float32)
    half = RET_DK // 2
    pos = pos0 + jnp.asarray((np.arange(nc)[:, None] * qt + tt[None, :]).reshape(-1))
    inv = ROPE_BASE ** (-jnp.arange(half, dtype=F32) / half)
    ang = pos.astype(F32)[:, None] * inv
    reps = LANES // half
    cos, sin = jnp.tile(jnp.cos(ang), (1, reps)), jnp.tile(jnp.sin(ang), (1, reps))
    first = (np.arange(LANES) % RET_DK) < half
    trig = jnp.stack([cos, jnp.where(first, -sin, 0.0), jnp.where(first, 0.0, sin)])
    to_blocked = np.zeros((qt, qt), np.float32)
    to_blocked[idx, (idx % SUBLANES) * nk + idx // SUBLANES] = 1.0
    perm = jnp.asarray(np.stack([to_blocked, to_blocked.T])).astype(MXU_DTYPE)
    return dict(trig=trig, tri=jnp.asarray(tri), dmask=jnp.asarray(dmask), dec=jnp.asarray(dec), perm=perm)


def _state_layout(depth, bsz, ns, gw, sn):
    hk, dv = RET_H * RET_DK, gw // RET_H
    if ns == 1:
        def per_seq(rows, w):
            return (depth, bsz, rows, w), (None, None, rows, w), lambda b, c, l: (l[0], b, 0, 0)
        def matrix(h, rows, w):
            return (depth, bsz, h, rows, w), (None, None, h, rows, w), lambda b, c, l: (l[0], b, 0, 0, 0)
        return dict(lconv=per_seq(CONV_K - 1, gw), lh=per_seq(1, gw), s5r=per_seq(1, sn), s5i=per_seq(1, sn),
                    sconv=per_seq(CONV_K - 1, 2 * gw), sh=matrix(gw // SSD_HD, SSD_HD, SSD_N),
                    ret=matrix(RET_H, RET_DK, dv))

    def flat(per, w):
        return (depth, bsz * per, w), (None, ns * per, w), lambda b, c, l: (l[0], b, 0)

    def conv(w):
        return (depth, CONV_K - 1, bsz, w), (None, CONV_K - 1, ns, w), lambda b, c, l: (l[0], 0, b, 0)
    def matrix(h, rows, w):
        return (depth, bsz, h, rows, w), (None, ns, h, rows, w), lambda b, c, l: (l[0], b, 0, 0, 0)
    return dict(lconv=conv(gw), lh=flat(1, gw), s5r=flat(1, sn), s5i=flat(1, sn), sconv=conv(2 * gw),
                sh=matrix(gw // SSD_HD, SSD_HD, SSD_N), ret=matrix(RET_H, RET_DK, dv))


def _mixer(l, proj, consts, mw, state_in, state_out, *, nb, nc, qt, ns, r, d):
    dp = proj.shape[-1]
    gw = d // 4
    sn = mw["pw"].shape[-1]
    depth = mw["pw"].shape[0]
    layout = _state_layout(depth, nb * ns, ns, gw, sn)
    assert (state_in is None) == (ns == 1)

    def layer_spec(a):
        return pl.BlockSpec((None,) + a.shape[1:], lambda b, c, l: (l[0],) + (0,) * (a.ndim - 1))

    def const_spec(a):
        return pl.BlockSpec(a.shape, lambda b, c, l: (0,) * a.ndim)

    def state_spec(nm):
        _, block, index = layout[nm]
        return pl.BlockSpec(block, index)

    if ns == 1:
        io_spec = lambda w: pl.BlockSpec((qt, w), lambda b, c, l: (b * nc + c, 0))
        y_shape = (nb * nc * qt, d)
        cnames = ("trig", "tri", "dmask", "dec", "perm")
    else:
        proj = proj.reshape(r, nb * ns, dp)
        io_spec = lambda w: pl.BlockSpec((r, ns, w), lambda b, c, l: (0, b, 0))
        y_shape = (r, nb * ns, d)
        cnames = ("trig", "tri", "dmask", "dec")
    names = ("proj",) + cnames + MIX_WEIGHTS
    operands = [proj] + [consts[k] for k in cnames] + [mw[k] for k in MIX_WEIGHTS]
    in_specs = [io_spec(dp), pl.BlockSpec((3, qt, LANES), lambda b, c, l: (0, c, 0))] + \
               [const_spec(consts[k]) for k in cnames[1:]] + [layer_spec(mw[k]) for k in MIX_WEIGHTS]
    first_alias = 1 + len(operands)
    if state_in is not None:
        assert state_in is state_out
        names += tuple(nm + "_i" for nm in STATE_NAMES)
        in_specs += [state_spec(nm) for nm in STATE_NAMES]
    else:
        names += tuple(nm + "_alias" for nm in STATE_NAMES)
        in_specs += [pl.BlockSpec(memory_space=pl.ANY) for _ in STATE_NAMES]
    operands += [state_out[nm] for nm in STATE_NAMES]
    names += ("y",) + tuple(nm + "_o" for nm in STATE_NAMES)
    out_specs = [io_spec(d)] + [state_spec(nm) for nm in STATE_NAMES]
    out_shape = [jax.ShapeDtypeStruct(y_shape, MXU_DTYPE)] + \
                [jax.ShapeDtypeStruct(layout[nm][0], F32) for nm in STATE_NAMES]
    scratch = []
    if ns == 1:
        tail = (CONV_K - 1) * SUBLANES
        scratch_shapes = dict(cb_a=(tail, gw), cb_c=(tail, 2 * gw), c_lh=(1, gw), c_s5r=(1, sn),
                              c_s5i=(1, sn), c_sh=(SSD_G * SSD_N, gw), c_ret=(RET_H * RET_DK, gw))
        names += tuple(scratch_shapes)
        scratch = [pltpu.VMEM(shape, F32) for shape in scratch_shapes.values()]
    grid_spec = pltpu.PrefetchScalarGridSpec(num_scalar_prefetch=1, grid=(nb, nc),
                                             in_specs=in_specs, out_specs=out_specs, scratch_shapes=scratch)
    outs = pl.pallas_call(
        functools.partial(_mix_body, names=names, qt=qt, ns=ns, r=r, nc=nc),
        grid_spec=grid_spec,
        out_shape=out_shape,
        input_output_aliases={first_alias + k: 1 + k for k in range(len(STATE_NAMES))},
        compiler_params=pltpu.CompilerParams(dimension_semantics=("arbitrary", "arbitrary"),
                                             vmem_limit_bytes=VMEM_LIMIT),
        name="mixer",
    )(l, *operands)
    return outs[0].reshape(-1, d), dict(zip(STATE_NAMES, outs[1:]))


def _post_body(l_ref, y_ref, x_ref, g1_ref, sc_ref, sh_ref, g2_ref, lt_ref, wo_ref, wr_ref, br_ref,
               wg_ref, wu_ref, wd_ref, *rest):
    o_ref, xs_ref, cs_ref, os_ref = rest[-4:]
    del l_ref
    tm, d = x_ref.shape
    tms = xs_ref.shape[0]
    x = x_ref[...] + _token_rows(g1_ref[...], tm) * jnp.dot(y_ref[...], wo_ref[...], preferred_element_type=F32)
    h = _rms(x) * (1.0 + _token_rows(sc_ref[...], tm)) + _token_rows(sh_ref[...], tm)
    hb = h.astype(MXU_DTYPE)
    h_lo = (h - hb.astype(F32)).astype(MXU_DTYPE)
    wr = wr_ref[...]
    l12 = jnp.dot(hb, wr, preferred_element_type=F32)
    logits = (l12[:, 0:LANES] + l12[:, LANES:2 * LANES]
              + jnp.dot(h_lo, wr[:, 0:LANES], preferred_element_type=F32) + br_ref[...])
    logits_t = jnp.transpose(logits)
    col = [logits_t[k:k + 1, :] for k in range(N_GROUPS + N_EXPERTS)]

    def first_max(vals, allowed=None):
        neg = jnp.full_like(vals[0], -jnp.inf)
        cand = vals if allowed is None else [jnp.where(al > 0.0, v, neg) for v, al in zip(vals, allowed)]
        m = functools.reduce(jnp.maximum, cand)
        rem = jnp.ones_like(vals[0])
        hot = []
        for v in cand:
            f = jnp.where(v >= m, rem, 0.0)
            rem = rem - f
            hot.append(f)
        return hot, m

    grp, gmax = first_max(col[:N_GROUPS])
    gate = 1.0 / functools.reduce(lambda s, v: s + v, [jnp.exp(v - gmax) for v in col[:N_GROUPS]])
    le = [functools.reduce(lambda s, v: s + v,
                           [grp[gi] * col[N_GROUPS + gi * N_PER_GROUP + j] for gi in range(N_GROUPS)])
          for j in range(N_PER_GROUP)]
    top1, m1 = first_max(le)
    top2, m2 = first_max(le, [1.0 - f for f in top1])
    e2 = jnp.exp(m2 - m1)
    w1 = 1.0 / (1.0 + e2)
    w2 = e2 * w1
    wgrp = [(top1[j] * w1 + top2[j] * w2) * gate for j in range(N_PER_GROUP)]

    sub8 = lax.broadcasted_iota(jnp.int32, (SUBLANES, 1), 0)
    ghot_t = functools.reduce(lambda s, v: s + v, [jnp.where(sub8 == gi, grp[gi], 0.0) for gi in range(N_GROUPS)])
    before = _mm_nt(ghot_t, lt_ref[...])
    count = jnp.sum(ghot_t, axis=1, keepdims=True)
    cnt = [count[gi:gi + 1, :] for gi in range(N_GROUPS)]
    base = [jnp.zeros((1, 1), F32)]
    for gi in range(1, N_GROUPS):
        base.append(jnp.floor((base[-1] + cnt[gi - 1] + (PACK - 1.0)) * (1.0 / PACK)) * PACK)
    pos_row = functools.reduce(lambda s, v: s + v,
                               [grp[gi] * (base[gi] + before[gi:gi + 1, :]) for gi in range(N_GROUPS)])
    pos = jnp.transpose(jnp.broadcast_to(pos_row, (LANES, tm)))[:, 0:1]
    slot = lax.broadcasted_iota(jnp.int32, (tms, tm), 0).astype(F32)
    gather = jnp.where(slot == pos_row, 1.0, 0.0).astype(MXU_DTYPE)
    slot_l = lax.broadcasted_iota(jnp.int32, (tm, tms), 1).astype(F32)
    scatter = jnp.where(slot_l == pos, 1.0, 0.0).astype(MXU_DTYPE)

    xs_ref[...] = jnp.dot(gather, hb, preferred_element_type=F32).astype(xs_ref.dtype)
    sub_e = lax.broadcasted_iota(jnp.int32, (cs_ref.shape[1], 1), 0)
    comb_t = functools.reduce(lambda s, v: s + v,
                              [jnp.where(sub_e == e, grp[e // N_PER_GROUP] * wgrp[e % N_PER_GROUP], 0.0)
                               for e in range(N_EXPERTS)])
    c1, c2, c3 = _split3(comb_t)
    nt = lambda p: lax.dot_general(gather, p, (((1,), (1,)), ((), ())), preferred_element_type=F32)
    cs_ref[...] = nt(c1) + nt(c2) + nt(c3)
    os_ref[...] = jnp.zeros(os_ref.shape, os_ref.dtype)
    for gi in range(N_GROUPS):
        start = base[gi][0, 0].astype(jnp.int32)
        nblk = jnp.floor((cnt[gi][0, 0] + (EXPERT_ROWS - 1.0)) * (1.0 / EXPERT_ROWS)).astype(jnp.int32)

        def block(k, carry, gi=gi, start=start):
            rows = pl.ds(pl.multiple_of(start + k * EXPERT_ROWS, PACK), EXPERT_ROWS)
            xb = xs_ref[rows, :]
            cb = cs_ref[rows, :]
            acc = jnp.zeros((EXPERT_ROWS, d), F32)
            for j in range(N_PER_GROUP):
                e = gi * N_PER_GROUP + j
                gt = jnp.dot(xb, wg_ref[e], preferred_element_type=F32)
                he = gt * _sigmoid(gt) * jnp.dot(xb, wu_ref[e], preferred_element_type=F32)
                acc = acc + jnp.dot((he * cb[:, e:e + 1]).astype(MXU_DTYPE), wd_ref[e], preferred_element_type=F32)
            os_ref[rows, :] += acc
            return carry

        lax.fori_loop(0, nblk, block, 0)
    moe = jnp.dot(scatter, os_ref[...].astype(MXU_DTYPE), preferred_element_type=F32)
    out = x + _token_rows(g2_ref[...], tm) * moe
    if len(rest) == 5:
        out = _rms(out) * rest[0][...]
    o_ref[...] = out


def _post(l, y, x, mod, pw, tm, rows_per_seq, final_w=None, in_place=False):
    t, d = x.shape
    final = [] if final_w is None else [final_w.reshape(1, d)]
    tms = tm + EXPERT_ROWS + PACK * N_GROUPS
    ltri = jnp.asarray(np.tril(np.ones((tm, tm), np.float32), -1)).astype(MXU_DTYPE)

    def layer_spec(a):
        return pl.BlockSpec((None,) + a.shape[1:], lambda i, l: (l[0],) + (0,) * (a.ndim - 1),
                            pipeline_mode=pl.Buffered(1))

    grid_spec = pltpu.PrefetchScalarGridSpec(
        num_scalar_prefetch=1,
        grid=(t // tm,),
        in_specs=[
            pl.BlockSpec((tm, d), lambda i, l: (i, 0)),
            pl.BlockSpec((tm, d), lambda i, l: (i, 0)),
            _mod_spec(mod, 2, tm, rows_per_seq),
            _mod_spec(mod, 4, tm, rows_per_seq),
            _mod_spec(mod, 3, tm, rows_per_seq),
            _mod_spec(mod, 5, tm, rows_per_seq),
            pl.BlockSpec((tm, tm), lambda i, l: (0, 0)),
        ] + [layer_spec(pw[k]) for k in ("w_out", "w_route", "b_route", "w_gate", "w_up", "w_down")]
          + [pl.BlockSpec((1, d), lambda i, l: (0, 0)) for _ in final],
        out_specs=pl.BlockSpec((tm, d), lambda i, l: (i, 0)),
        scratch_shapes=[pltpu.VMEM((tms, d), MXU_DTYPE), pltpu.VMEM((tms, 2 * N_EXPERTS), F32),
                        pltpu.VMEM((tms, d), F32)],
    )
    return pl.pallas_call(
        _post_body,
        grid_spec=grid_spec,
        out_shape=jax.ShapeDtypeStruct((t, d), F32),
        input_output_aliases={2: 0} if in_place else {},
        compiler_params=pltpu.CompilerParams(dimension_semantics=("arbitrary",), vmem_limit_bytes=VMEM_LIMIT),
        name="post",
    )(l, y, x, mod, mod, mod, mod, ltri,
      *[pw[k] for k in ("w_out", "w_route", "b_route", "w_gate", "w_up", "w_down")], *final)


def _block_diag(w):
    depth, nblk, bi, bj = w.shape
    eye = jnp.eye(nblk, dtype=w.dtype)
    return (w[:, :, :, None, :] * eye[None, :, None, :, None]).reshape(depth, nblk * bi, nblk * bj)


def _pad_lanes(a, width):
    return jnp.pad(a, [(0, 0)] * (a.ndim - 1) + [(0, width - a.shape[-1])])


def _prep_weights(w_in, lru_conv_w, lru_conv_b, lru_wa, lru_ba, lru_wx, lru_bx, lru_lambda,
                  s5_a_re, s5_a_im, s5_b_re, s5_b_im, s5_c_re, s5_c_im, s5_d, s5_log_dt, s5_w_glu,
                  ssd_conv_w, ssd_conv_b, ssd_dt_bias, ssd_a_log, ssd_d, mix_norm, qt):
    depth, d, _ = w_in.shape
    gw = d // 4
    n_ssd_h = gw // SSD_HD
    xbc_end = 4 * gw + gw + 2 * SSD_G * SSD_N
    w_in = w_in.astype(MXU_DTYPE)
    dt_cols = jnp.repeat(w_in[..., xbc_end:xbc_end + n_ssd_h], SSD_HD, axis=-1)
    w_in_p = jnp.concatenate([w_in[..., :xbc_end], w_in[..., xbc_end + n_ssd_h:], dt_cols], axis=-1)

    def row(a):
        return _pad_lanes(a.reshape(depth, 1, -1), d)

    vec = jnp.concatenate([
        row(lru_conv_b), row(jnp.concatenate([lru_ba, lru_bx], -1)), row(lru_lambda), row(s5_d),
        row(ssd_conv_b), row(jnp.repeat(ssd_dt_bias, SSD_HD, -1)), row(jnp.repeat(ssd_a_log, SSD_HD, -1)),
        row(jnp.repeat(ssd_d, SSD_HD, -1)), row(mix_norm),
        _pad_lanes(lru_conv_w, d), _pad_lanes(ssd_conv_w, d),
        jnp.zeros((depth, V_ROWS - V_SCW - CONV_K, d), F32)], axis=1)

    wgate = jnp.concatenate([_block_diag(lru_wa), _block_diag(lru_wx)], axis=-1)

    dt = jnp.exp(s5_log_dt)[..., None]
    lr, li = s5_a_re, s5_a_im
    mag = jnp.exp(lr * dt)
    ab_re, ab_im = mag * jnp.cos(li * dt), mag * jnp.sin(li * dt)
    den = lr * lr + li * li
    q_re = ((ab_re - 1.0) * lr + ab_im * li) / den
    q_im = (ab_im * lr - (ab_re - 1.0) * li) / den
    bb_re = q_re[..., None] * s5_b_re - q_im[..., None] * s5_b_im
    bb_im = q_re[..., None] * s5_b_im + q_im[..., None] * s5_b_re
    bbm = jnp.concatenate([_block_diag(jnp.swapaxes(bb_re, -1, -2)), _block_diag(jnp.swapaxes(bb_im, -1, -2))], -1)
    cre = _block_diag(jnp.swapaxes(s5_c_re, -1, -2))
    cim = _block_diag(jnp.swapaxes(s5_c_im, -1, -2))
    a_re, a_im = ab_re.reshape(depth, -1), ab_im.reshape(depth, -1)
    pr, pi = a_re, a_im
    rows = []
    for _ in range(int(math.log2(qt))):
        rows += [pr, pi]
        pr, pi = pr * pr - pi * pi, 2.0 * pr * pi
    pw = jnp.stack(rows, axis=1)
    pw = jnp.pad(pw, ((0, 0), (0, (-pw.shape[1]) % SUBLANES), (0, 0)))
    pr, pi = a_re, a_im
    k_re, k_im = [], []
    for _ in range(qt // SUBLANES):
        k_re.append(pr)
        k_im.append(pi)
        pr, pi = pr * a_re - pi * a_im, pr * a_im + pi * a_re
    pk = jnp.stack(k_re + k_im, axis=1)

    mw = dict(vec=vec, wgate=wgate.astype(MXU_DTYPE), bbm=bbm.astype(MXU_DTYPE), cre=cre.astype(MXU_DTYPE),
              cim=cim.astype(MXU_DTYPE), wglu=s5_w_glu.astype(MXU_DTYPE), pw=pw, pk=pk)
    return w_in_p, mw


def _states_to_layout(states, layout, ns):
    out = {}
    for nm, s in zip(STATE_NAMES, states):
        if nm in ("lconv", "sconv"):
            s = s.transpose(0, 2, 1, 3)
        out[nm] = s.reshape(layout[nm][0])
    return out


def _states_from_layout(arrs, ns, ref_shapes):
    out = []
    for nm, shape in zip(STATE_NAMES, ref_shapes):
        a = arrs[nm]
        if ns > 1 and nm in ("lconv", "sconv"):
            a = a.transpose(0, 2, 1, 3)
        out.append(a.reshape(shape))
    return tuple(out)


def _trunk(x, mod, states, ref_shapes, pos0, w_in_p, mw, pw, final_norm, *, qt, ns, tm, tm_in):
    bsz, seq_len, d = x.shape
    depth = w_in_p.shape[0]
    gw = d // 4
    r = qt // ns
    nc = seq_len // r if ns == 1 else 1
    assert (ns == 1 and seq_len % qt == 0 and states is None) or (ns > 1 and r == seq_len and bsz % ns == 0)
    assert r >= CONV_K and (r & (r - 1)) == 0
    nb = bsz // ns
    tok = bsz * seq_len
    consts = _mix_consts(qt, ns, r, nc, pos0)
    layout = _state_layout(depth, bsz, ns, gw, mw["pw"].shape[-1])
    if states is None:
        state_out = {nm: jnp.zeros(layout[nm][0], F32) for nm in STATE_NAMES}
    else:
        state_out = _states_to_layout(states, layout, ns)
    x2 = x.reshape(tok, d) if ns == 1 else x.transpose(1, 0, 2).reshape(tok, d)
    blocked_chunk = qt if ns == 1 else None

    def layer(carry, l, final_w=None, in_place=True):
        xc, st = carry
        lv = jnp.reshape(l, (1,)).astype(jnp.int32)
        proj = _inproj(lv, xc, mod, w_in_p, tm_in, seq_len, blocked_chunk)
        y, st = _mixer(lv, proj, consts, mw, None if states is None else st, st,
                       nb=nb, nc=nc, qt=qt, ns=ns, r=r, d=d)
        xn = _post(lv, y, xc, mod, pw, tm, seq_len, final_w, in_place)
        return (xn, st), None

    assert depth >= 2
    carry, _ = layer((x2, state_out), jnp.int32(0), in_place=False)
    carry, _ = lax.scan(layer, carry, jnp.arange(1, depth - 1))
    (y, st), _ = layer(carry, jnp.int32(depth - 1), final_norm)
    y = y.reshape(bsz, seq_len, d) if ns == 1 else y.reshape(seq_len, bsz, d).transpose(1, 0, 2)
    return y, _states_from_layout(st, ns, ref_shapes)


def kernel(x_prompt, x_sample, state_lru_conv, state_lru_h, state_s5_re, state_s5_im, state_ssd_conv, state_ssd_h, state_ret, c_prompt, c_sample, w_ada, b_ada, w_in, lru_conv_w, lru_conv_b, lru_wa, lru_ba, lru_wx, lru_bx, lru_lambda, s5_a_re, s5_a_im, s5_b_re, s5_b_im, s5_c_re, s5_c_im, s5_d, s5_log_dt, s5_w_glu, ssd_conv_w, ssd_conv_b, ssd_dt_bias, ssd_a_log, ssd_d, mix_norm, w_out, w_route_group, b_route_group, w_route_exp, b_route_exp, w_exp_gate, w_exp_up, w_exp_down, final_norm):
    bp, lp, d = x_prompt.shape
    bs, ls, _ = x_sample.shape
    depth = w_in.shape[0]
    qt = LANES
    ns_s = qt // ls

    mod_p, mod_s = _modulation(jnp.concatenate([c_prompt, c_sample], axis=0), bp, w_ada, b_ada)
    mod_p = mod_p.reshape(depth, N_MOD, bp, 1, d)

    w_in_p, mw = _prep_weights(w_in, lru_conv_w, lru_conv_b, lru_wa, lru_ba, lru_wx, lru_bx, lru_lambda,
                               s5_a_re, s5_a_im, s5_b_re, s5_b_im, s5_c_re, s5_c_im, s5_d, s5_log_dt, s5_w_glu,
                               ssd_conv_w, ssd_conv_b, ssd_dt_bias, ssd_a_log, ssd_d, mix_norm, qt)
    w_route = _pad_lanes(jnp.concatenate([w_route_group, w_route_exp.reshape(depth, d, N_EXPERTS)], -1), LANES)
    b_route = _pad_lanes(jnp.concatenate([b_route_group, b_route_exp.reshape(depth, N_EXPERTS)], -1), LANES)
    w_route_hi = w_route.astype(MXU_DTYPE)
    w_route_lo = (w_route - w_route_hi.astype(F32)).astype(MXU_DTYPE)
    w_route = jnp.concatenate([w_route_hi, w_route_lo], axis=-1)
    pw = dict(w_out=w_out.astype(MXU_DTYPE), w_route=w_route, b_route=b_route.reshape(depth, 1, LANES),
              w_gate=w_exp_gate.astype(MXU_DTYPE), w_up=w_exp_up.astype(MXU_DTYPE),
              w_down=w_exp_down.astype(MXU_DTYPE))
    w_in_p, mw, pw = lax.optimization_barrier((w_in_p, mw, pw))

    states_s = (state_lru_conv, state_lru_h, state_s5_re, state_s5_im, state_ssd_conv, state_ssd_h, state_ret)
    shapes_s = [s.shape for s in states_s]
    shapes_p = [(depth, bp) + s[2:] for s in shapes_s]
    y_p, new_p = _trunk(x_prompt, mod_p, None, shapes_p, 0, w_in_p, mw, pw, final_norm,
                        qt=qt, ns=1, tm=min(512, lp), tm_in=min(1024, lp))
    y_s, new_s = _trunk(x_sample, mod_s, states_s, shapes_s, PAST_LEN, w_in_p, mw, pw, final_norm,
                        qt=qt, ns=ns_s, tm=bs * ls, tm_in=bs * ls)
    out = [y_p, y_s]
    for a, b in zip(new_p, new_s):
        out += [a, b]
    return tuple(out)
```

```python
import functools
import math

import numpy as np
import jax
import jax.numpy as jnp
from jax import lax
from jax.experimental import pallas as pl
from jax.experimental.pallas import tpu as pltpu

F32 = jnp.float32
MXU_DTYPE = jnp.bfloat16

EPS = 1e-6
CONV_K = 4
LRU_C = 8.0
SSD_HD = 64
SSD_G = 2
SSD_N = 64
RET_H = 4
RET_DK = 32
ROPE_BASE = 10000.0
N_GROUPS = 4
N_PER_GROUP = 4
N_EXPERTS = N_GROUPS * N_PER_GROUP
N_MOD = 6
PAST_LEN = 16384

LANES = 128
SUBLANES = 8
PACK = 16
VMEM_LIMIT = 56 * 1024 * 1024
EXPERT_ROWS = 160


def _mm(a, b):
    return jnp.dot(a.astype(MXU_DTYPE), b.astype(MXU_DTYPE), preferred_element_type=F32)


def _mm_nt(a, b):
    return lax.dot_general(a.astype(MXU_DTYPE), b.astype(MXU_DTYPE), (((1,), (1,)), ((), ())),
                           preferred_element_type=F32)


def _mm_tn(a, b):
    return lax.dot_general(a.astype(MXU_DTYPE), b.astype(MXU_DTYPE), (((0,), (0,)), ((), ())),
                           preferred_element_type=F32)


def _split3(x):
    x1 = x.astype(MXU_DTYPE)
    r1 = x - x1.astype(F32)
    x2 = r1.astype(MXU_DTYPE)
    x3 = (r1 - x2.astype(F32)).astype(MXU_DTYPE)
    return x1, x2, x3


def _sigmoid(x):
    return 0.5 * jnp.tanh(0.5 * x) + 0.5


def _rms(x):
    return x * lax.rsqrt(jnp.mean(x * x, axis=-1, keepdims=True) + EPS)


def _token_rows(v, tm):
    n = v.shape[0]
    return v if n in (1, tm) else jnp.tile(v, (tm // n, 1))


def _mod_body(c_ref, w_ref, b_ref, first_ref, rest_ref):
    c = c_ref[...]
    m = _mm(c * _sigmoid(c), w_ref[...]) + b_ref[...]
    n_first = first_ref.shape[0]
    first_ref[...] = m[0:n_first]
    rest_ref[...] = m[n_first:]


def _modulation(c_all, n_first, w_ada, b_ada):
    depth, d, _ = w_ada.shape
    nb = c_all.shape[0]
    return pl.pallas_call(
        _mod_body,
        grid=(depth, N_MOD),
        in_specs=[
            pl.BlockSpec((nb, d), lambda l, k: (0, 0)),
            pl.BlockSpec((None, d, d), lambda l, k: (l, 0, k)),
            pl.BlockSpec((None, None, 1, d), lambda l, k: (l, k, 0, 0)),
        ],
        out_specs=[pl.BlockSpec((None, None, n_first, d), lambda l, k: (l, k, 0, 0)),
                   pl.BlockSpec((None, None, nb - n_first, d), lambda l, k: (l, k, 0, 0))],
        out_shape=[jax.ShapeDtypeStruct((depth, N_MOD, n_first, d), F32),
                   jax.ShapeDtypeStruct((depth, N_MOD, nb - n_first, d), F32)],
        compiler_params=pltpu.CompilerParams(dimension_semantics=("arbitrary", "arbitrary"),
                                             vmem_limit_bytes=VMEM_LIMIT),
        name="modulation",
    )(c_all, w_ada, b_ada.reshape(depth, N_MOD, 1, d))


def _mod_spec(mod, k, tm, rows_per_seq):
    if mod.ndim == 5:
        tiles_per_seq = rows_per_seq // tm
        return pl.BlockSpec((None, None, None, 1, mod.shape[-1]),
                            lambda i, l: (l[0], k, i // tiles_per_seq, 0, 0))
    return pl.BlockSpec((None, None) + mod.shape[2:], lambda i, l: (l[0], k, 0, 0))


def _inproj_body(l_ref, x_ref, sc_ref, sh_ref, w_ref, o_ref, *, blocked_chunk):
    tm = x_ref.shape[0]
    h = _rms(x_ref[...]) * (1.0 + _token_rows(sc_ref[...], tm)) + _token_rows(sh_ref[...], tm)
    hb = h.astype(MXU_DTYPE)
    if blocked_chunk is not None:
        qt, nk = blocked_chunk, blocked_chunk // SUBLANES
        r_i = lax.broadcasted_iota(jnp.int32, (qt, qt), 0)
        c_i = lax.broadcasted_iota(jnp.int32, (qt, qt), 1)
        to_blocked = jnp.where(c_i == (r_i % SUBLANES) * nk + r_i // SUBLANES, 1.0, 0.0).astype(MXU_DTYPE)
        hb = jnp.concatenate([jnp.dot(to_blocked, hb[j * qt:(j + 1) * qt], preferred_element_type=F32)
                              for j in range(tm // qt)], axis=0).astype(MXU_DTYPE)
    o_ref[...] = jnp.dot(hb, w_ref[...], preferred_element_type=F32)


def _inproj(l, x, mod, w_in, tm, rows_per_seq, blocked_chunk=None):
    t, d = x.shape
    dp = w_in.shape[-1]
    assert blocked_chunk is None or (mod.ndim == 5 and tm % blocked_chunk == 0)
    grid_spec = pltpu.PrefetchScalarGridSpec(
        num_scalar_prefetch=1,
        grid=(t // tm,),
        in_specs=[
            pl.BlockSpec((tm, d), lambda i, l: (i, 0)),
            _mod_spec(mod, 1, tm, rows_per_seq),
            _mod_spec(mod, 0, tm, rows_per_seq),
            pl.BlockSpec((None, d, dp), lambda i, l: (l[0], 0, 0)),
        ],
        out_specs=pl.BlockSpec((tm, dp), lambda i, l: (i, 0)),
    )
    return pl.pallas_call(
        functools.partial(_inproj_body, blocked_chunk=blocked_chunk),
        grid_spec=grid_spec,
        out_shape=jax.ShapeDtypeStruct((t, dp), F32),
        compiler_params=pltpu.CompilerParams(dimension_semantics=("arbitrary",), vmem_limit_bytes=VMEM_LIMIT),
        name="inproj",
    )(l, x, mod, mod, w_in)


V_LCB, V_BGATE, V_LAM, V_S5D, V_SCB, V_DTB, V_ALOG, V_SSDD, V_GAIN, V_LCW, V_SCW = 0, 1, 2, 3, 4, 5, 6, 7, 8, 9, 13
V_ROWS = 24
STATE_NAMES = ("lconv", "lh", "s5r", "s5i", "sconv", "sh", "ret")
MIX_WEIGHTS = ("vec", "wgate", "bbm", "cre", "cim", "wglu", "pw", "pk")


def _mix_body(l_ref, *refs, names, qt, ns, r, nc):
    del l_ref
    g = dict(zip(names, refs))
    gw = g["lh_o"].shape[-1]
    sn = g["s5r_o"].shape[-1]
    nlev = int(math.log2(r))
    nk = qt // SUBLANES
    c = pl.program_id(1)

    if ns == 1:
        @pl.when(c == 0)
        def _zero_carries():
            for nm in ("c_lh", "c_s5r", "c_s5i", "c_sh", "c_ret"):
                g[nm][...] = jnp.zeros(g[nm].shape, F32)
            for nm in ("cb_a", "cb_c"):
                g[nm][...] = jnp.zeros(g[nm].shape, F32)

    proj = g["proj"]
    vec = g["vec"]
    pw = g["pw"]

    def pcols(a, b):
        v = proj[..., a:b]
        return v if ns == 1 else v.reshape(qt, b - a)

    y_parts = []

    def ysave(a, b, val):
        val = val.astype(g["y"].dtype)
        if ns == 1:
            y_parts.append(val)
        else:
            g["y"][..., a:b] = val.reshape(r, ns, b - a)

    row = lax.broadcasted_iota(jnp.int32, (qt, 1), 0)
    t = (row % SUBLANES) * nk + row // SUBLANES if ns == 1 else row // ns
    sub = lax.broadcasted_iota(jnp.int32, (SUBLANES, 1), 0)

    def down(x, d, fill=0.0):
        return jnp.where(t >= d, pltpu.roll(x, d * ns, 0), fill)

    def up(x, d):
        return jnp.where(t + d < r, pltpu.roll(x, qt - d * ns, 0), 0.0)

    def sdown(x, d, fill=0.0):
        return jnp.where(sub >= d, pltpu.roll(x, d, 0), fill)

    def slabs(x):
        return [x[SUBLANES * k:SUBLANES * (k + 1)] for k in range(nk)]

    def first_rows(carry, init):
        if ns == 1:
            return jnp.where(row == 0, g[carry][...], 0.0)
        h0 = g[init][...]
        return jnp.concatenate([h0, jnp.zeros((qt - ns, h0.shape[1]), F32)], axis=0)

    def conv(xraw, nm, w0, b):
        width = xraw.shape[-1]
        acc = vec[b:b + 1, 0:width] + vec[w0 + 3:w0 + 4, 0:width] * xraw
        if ns == 1:
            cb = g["cb_" + nm]
            xs_k = slabs(xraw)
            wrap = [jnp.where(sub >= 1, pltpu.roll(xs_k[nk - j], 1, 0),
                              pltpu.roll(cb[(CONV_K - 1 - j) * SUBLANES:(CONV_K - j) * SUBLANES, :], 1, 0))
                    for j in range(1, CONV_K)]
            for m in range(1, CONV_K):
                sh = jnp.concatenate([wrap[m - k - 1] for k in range(m)] + xs_k[0:nk - m], axis=0)
                acc = acc + vec[w0 + 3 - m:w0 + 4 - m, 0:width] * sh
            cb[...] = xraw[qt - (CONV_K - 1) * SUBLANES:, :]
            g[{"a": "lconv_o", "c": "sconv_o"}[nm]][...] = jnp.concatenate(
                [xs_k[nk - j][SUBLANES - 1:SUBLANES, :] for j in range(CONV_K - 1, 0, -1)], axis=0)
            return acc
        o = g[{"a": "lconv_o", "c": "sconv_o"}[nm]]
        buf = g[{"a": "lconv_i", "c": "sconv_i"}[nm]][...].reshape((CONV_K - 1) * ns, width)
        buf = jnp.concatenate([buf, jnp.zeros((qt - (CONV_K - 1) * ns, width), F32)], axis=0)
        for m in range(1, CONV_K):
            prev = buf if m == 3 else pltpu.roll(buf, qt - (3 - m) * ns, 0)
            sh = jnp.where(t >= m, pltpu.roll(xraw, m * ns, 0), prev)
            acc = acc + vec[w0 + 3 - m:w0 + 4 - m, 0:width] * sh
        o[...] = xraw[(r - (CONV_K - 1)) * ns:, :].reshape(CONV_K - 1, ns, width)
        return acc

    def save_last(nm, h):
        if ns == 1:
            g["c_" + nm][...] = h[qt - 1:qt, :]
            g[nm + "_o"][...] = h[qt - 1:qt, :]
        else:
            g[nm + "_o"][...] = h[qt - ns:, :]

    def seq_mask(width):
        lane_seq = lax.broadcasted_iota(jnp.int32, (qt, ns * width), 1) // width
        row_seq = lax.broadcasted_iota(jnp.int32, (qt, ns * width), 0) % ns
        return (lane_seq == row_seq).astype(F32)

    gain = vec[V_GAIN:V_GAIN + 1, :]

    xc = conv(pcols(0, gw), "a", V_LCW, V_LCB)
    ga = pcols(gw, 2 * gw)
    pre = _mm(xc, g["wgate"][...]) + vec[V_BGATE:V_BGATE + 1, 0:2 * gw]
    rg = _sigmoid(pre[:, 0:gw])
    ig = _sigmoid(pre[:, gw:2 * gw])
    log_a = -LRU_C * rg * jax.nn.softplus(-vec[V_LAM:V_LAM + 1, 0:gw])
    a = jnp.exp(log_a)
    b = jnp.sqrt(-jnp.tanh(log_a) * (a * a + 1.0)) * (ig * xc)
    b = b + a * first_rows("c_lh", "lh_i")
    if ns == 1:
        a_s, b_s = slabs(a), slabs(b)
        hs, ps = [b_s[0]], [a_s[0]]
        for k in range(1, nk):
            hs.append(a_s[k] * hs[-1] + b_s[k])
            ps.append(a_s[k] * ps[-1])
        e, ae = hs[-1], ps[-1]
        for j in range(int(math.log2(SUBLANES))):
            d = 1 << j
            e = ae * sdown(e, d) + e
            ae = ae * sdown(ae, d, 1.0)
        cin = sdown(e, 1)
        b = jnp.concatenate([hs[k] + ps[k] * cin for k in range(nk)], axis=0)
    else:
        for k in range(nlev):
            d = 1 << k
            b = a * down(b, d) + b
            a = a * down(a, d, 1.0)
    save_last("lh", b)
    ya = _rms(b * jax.nn.gelu(ga)) * gain[:, 0:gw]

    u = pcols(2 * gw, 3 * gw)
    bu = _mm(u, g["bbm"][...])
    p_re, p_im = pw[0:1, :], pw[1:2, :]
    h0r, h0i = first_rows("c_s5r", "s5r_i"), first_rows("c_s5i", "s5i_i")
    hr = bu[:, 0:sn] + (p_re * h0r - p_im * h0i)
    hi = bu[:, sn:2 * sn] + (p_re * h0i + p_im * h0r)
    if ns == 1:
        pk = g["pk"]
        br_s, bi_s = slabs(hr), slabs(hi)
        hrs, his = [br_s[0]], [bi_s[0]]
        for k in range(1, nk):
            hrs.append(br_s[k] + (p_re * hrs[-1] - p_im * his[-1]))
            his.append(bi_s[k] + (p_re * his[-1] + p_im * hrs[-2]))
        er, ei = hrs[-1], his[-1]
        lev0 = int(math.log2(nk))
        for j in range(int(math.log2(SUBLANES))):
            d = 1 << j
            q_re, q_im = pw[2 * (lev0 + j):2 * (lev0 + j) + 1, :], pw[2 * (lev0 + j) + 1:2 * (lev0 + j) + 2, :]
            sr, si = sdown(er, d), sdown(ei, d)
            er, ei = er + (q_re * sr - q_im * si), ei + (q_re * si + q_im * sr)
        cr, ci = sdown(er, 1), sdown(ei, 1)
        hr = jnp.concatenate([hrs[k] + (pk[k:k + 1, :] * cr - pk[nk + k:nk + k + 1, :] * ci) for k in range(nk)], 0)
        hi = jnp.concatenate([his[k] + (pk[k:k + 1, :] * ci + pk[nk + k:nk + k + 1, :] * cr) for k in range(nk)], 0)
    else:
        for k in range(nlev):
            d = 1 << k
            q_re, q_im = pw[2 * k:2 * k + 1, :], pw[2 * k + 1:2 * k + 2, :]
            sr, si = down(hr, d), down(hi, d)
            hr, hi = hr + (q_re * sr - q_im * si), hi + (q_re * si + q_im * sr)
    save_last("s5r", hr)
    save_last("s5i", hi)
    yb = _mm(hr, g["cre"][...]) - _mm(hi, g["cim"][...])
    yb = jax.nn.gelu(yb + vec[V_S5D:V_S5D + 1, 0:gw] * u)
    yb = yb * _sigmoid(_mm(yb, g["wglu"][...]))
    yb = _rms(yb) * gain[:, gw:2 * gw]
    ysave(0, 2 * gw, jnp.concatenate([ya, yb], axis=1))

    tri = g["tri"][...] > 0.0
    xbc = conv(pcols(4 * gw, 6 * gw), "c", V_SCW, V_SCB)
    xbc = xbc * _sigmoid(xbc)
    xs, bm, cm = xbc[:, 0:gw], xbc[:, gw:gw + LANES], xbc[:, gw + LANES:2 * gw]
    dt = jax.nn.softplus(pcols(9 * gw, 10 * gw) + vec[V_DTB:V_DTB + 1, 0:gw])
    a_e = -jnp.exp(vec[V_ALOG:V_ALOG + 1, 0:gw])
    dta = dt * a_e
    if ns == 1:
        cs = slabs(dta)
        for k in range(1, nk):
            cs[k] = cs[k] + cs[k - 1]
        e = cs[-1]
        for j in range(int(math.log2(SUBLANES))):
            e = e + sdown(e, 1 << j)
        cin = sdown(e, 1)
        acum = jnp.concatenate([ck + cin for ck in cs], axis=0)
        alast = acum[qt - 1:qt, :]
    else:
        acum = dta
        for k in range(nlev):
            acum = acum + down(acum, 1 << k)
        suf = dta
        for k in range(nlev):
            suf = suf + up(suf, 1 << k)
        alast = acum + suf - dta
    wend = jnp.exp(alast - acum) * dt
    acum_t = jnp.transpose(acum)
    lane = lax.broadcasted_iota(jnp.int32, (1, gw), 1)
    lane_b = lax.broadcasted_iota(jnp.int32, (1, LANES), 1)
    heads_per_group = gw // SSD_HD // SSD_G
    xdt = xs * dt
    xw = xs * wend
    ydiag = jnp.zeros((qt, gw), F32)
    for grp in range(SSD_G):
        gm = ((lane_b // SSD_N) == grp).astype(F32)
        cb = _mm_nt(cm * gm, bm)
        for hh in range(heads_per_group):
            hd = grp * heads_per_group + hh
            col = acum[:, hd * SSD_HD:hd * SSD_HD + 1]
            rowv = acum_t[hd * SSD_HD:hd * SSD_HD + 1, :]
            dec = jnp.where(tri, jnp.exp(col - rowv), 0.0)
            hm = ((lane // SSD_HD) == hd).astype(F32)
            ydiag = ydiag + _mm(cb * dec, xdt * hm)
    if ns == 1:
        st = g["c_sh"][...]
        srow = lax.broadcasted_iota(jnp.int32, st.shape, 0) // SSD_N
        yoff = _mm(cm, st)
        slane = lax.broadcasted_iota(jnp.int32, st.shape, 1) // (SSD_HD * heads_per_group)
        upd = jnp.where(srow == slane, _mm_tn(bm, xw), 0.0)
        st = st * jnp.exp(acum[qt - 1:qt, :]) + upd
        g["c_sh"][...] = st

        @pl.when(c == nc - 1)
        def _final_ssd_state():
            own = jnp.concatenate([st[grp * SSD_N:(grp + 1) * SSD_N, grp * LANES:(grp + 1) * LANES]
                                   for grp in range(SSD_G)], axis=1)
            own = jnp.concatenate([own, jnp.zeros_like(own)], axis=0)
            g["sh_o"][...] = jnp.transpose(own)[:, 0:SSD_N].reshape(g["sh_o"].shape)
    else:
        sh_i, sh_o = g["sh_i"], g["sh_o"]
        per_seq = gw // SSD_HD * SSD_HD
        cols = [[], []]
        for s in range(0, ns, 2):
            two = jnp.concatenate([sh_i[s].reshape(per_seq, SSD_N), sh_i[s + 1].reshape(per_seq, SSD_N)], axis=1)
            two = jnp.transpose(two)
            for grp in range(SSD_G):
                cols[grp] += [two[0:SSD_N, grp * LANES:(grp + 1) * LANES],
                              two[SSD_N:2 * SSD_N, grp * LANES:(grp + 1) * LANES]]
        st = jnp.concatenate([jnp.concatenate(cols[0], axis=1), jnp.concatenate(cols[1], axis=1)], axis=0)
        srow = lax.broadcasted_iota(jnp.int32, st.shape, 0) // SSD_N
        seqm = seq_mask(LANES)
        lastm = jnp.where(t == r - 1, seqm, 0.0)
        yo, upd, dl = [], None, None
        for grp in range(SSD_G):
            gm = ((lane_b // SSD_N) == grp).astype(F32)
            z = _mm(cm * gm, st) * seqm
            zf = z[:, 0:LANES]
            for s in range(1, ns):
                zf = zf + z[:, s * LANES:(s + 1) * LANES]
            yo.append(zf)
            sl = slice(grp * LANES, (grp + 1) * LANES)
            u_g = _mm_tn(bm, jnp.tile(xw[:, sl], (1, ns)) * seqm)
            d_g = jnp.exp(jnp.sum(jnp.tile(acum[:, sl], (1, ns)) * lastm, axis=0, keepdims=True))
            upd = u_g if grp == 0 else jnp.where(srow == grp, u_g, upd)
            dl = d_g if grp == 0 else jnp.where(srow == grp, d_g, dl)
        yoff = jnp.concatenate(yo, axis=1)
        st = st * dl + upd
        for s in range(0, ns, 2):
            two = jnp.concatenate(
                [jnp.concatenate([st[0:SSD_N, q * LANES:(q + 1) * LANES], st[SSD_N:2 * SSD_N, q * LANES:(q + 1) * LANES]],
                                 axis=1) for q in (s, s + 1)], axis=0)
            two = jnp.transpose(two)
            sh_o[s] = two[:, 0:SSD_N].reshape(sh_o.shape[1:])
            sh_o[s + 1] = two[:, SSD_N:2 * SSD_N].reshape(sh_o.shape[1:])
    yc = ydiag + yoff * jnp.exp(acum) + vec[V_SSDD:V_SSDD + 1, 0:gw] * xs
    z = pcols(3 * gw, 4 * gw)
    yc = yc * (z * _sigmoid(z))
    ysave(2 * gw, 3 * gw, _rms(yc) * gain[:, 2 * gw:3 * gw])

    trig = g["trig"]
    cos, sin_a, sin_b = trig[0], trig[1], trig[2]

    def rope(x):
        return x * cos + pltpu.roll(x, LANES - RET_DK // 2, 1) * sin_a + pltpu.roll(x, RET_DK // 2, 1) * sin_b

    q = rope(pcols(6 * gw, 6 * gw + LANES))
    kk = rope(pcols(6 * gw + LANES, 7 * gw)) * (RET_DK ** -0.5)
    v = pcols(7 * gw, 8 * gw)
    dv = gw // RET_H
    intra = jnp.zeros((qt, gw), F32)
    for hd in range(RET_H):
        qm = ((lane_b // RET_DK) == hd).astype(F32)
        sc = _mm_nt(q * qm, kk) * g["dmask"][hd]
        vm = ((lane // dv) == hd).astype(F32)
        intra = intra + _mm(sc, v * vm)
    dec = g["dec"]
    qd, kd = q * dec[0], kk * dec[1]
    if ns == 1:
        rs = g["c_ret"][...]
        cross = _mm(qd, rs)
        rrow = lax.broadcasted_iota(jnp.int32, rs.shape, 0) // RET_DK
        rlane = lax.broadcasted_iota(jnp.int32, rs.shape, 1) // dv
        upd = jnp.where(rrow == rlane, _mm_tn(kd, v), 0.0)
    else:
        ret_i, ret_o = g["ret_i"], g["ret_o"]
        hk = RET_H * RET_DK
        rs = jnp.concatenate([ret_i[s].reshape(hk, dv) for s in range(ns)], axis=1)
        seqm = seq_mask(dv)
        lo = lane_b < dv
        pairs = []
        upd = jnp.zeros(rs.shape, F32)
        for pp in range(RET_H // 2):
            vp = v[:, pp * LANES:(pp + 1) * LANES]
            vsw = pltpu.roll(vp, dv, 1)
            tots = []
            for hh in range(2):
                qm = ((lane_b // RET_DK) == 2 * pp + hh).astype(F32)
                zc = _mm(qd * qm, rs) * seqm
                acc = zc[:, 0:LANES]
                for s2 in range(1, ns // 2):
                    acc = acc + zc[:, s2 * LANES:(s2 + 1) * LANES]
                tots.append(acc + pltpu.roll(acc, dv, 1))
                vh = jnp.where(lo, vp, vsw) if hh == 0 else jnp.where(lo, vsw, vp)
                upd = upd + _mm_tn(kd * qm, jnp.tile(vh, (1, ns // 2)) * seqm)
            pairs.append(jnp.where(lo, tots[0], tots[1]))
        cross = jnp.concatenate(pairs, axis=1)
    rs = rs * dec[2][:, 0:1] + upd
    if ns == 1:
        g["c_ret"][...] = rs

        @pl.when(c == nc - 1)
        def _final_ret_state():
            own = functools.reduce(lambda s, x: s + x, [rs[:, hd * dv:(hd + 1) * dv] for hd in range(RET_H)])
            g["ret_o"][...] = own.reshape(g["ret_o"].shape)
    else:
        for s in range(ns):
            ret_o[s] = rs[:, s * dv:(s + 1) * dv].reshape(ret_o.shape[1:])
    o = intra + cross
    o2 = o * o
    ms = jnp.zeros((qt, gw), F32)
    for hd in range(RET_H):
        vm = ((lane // dv) == hd).astype(F32)
        ms = ms + vm * jnp.sum(o2 * vm, axis=-1, keepdims=True)
    o = o * lax.rsqrt(ms * (1.0 / dv) + EPS) * gain[:, 3 * gw:4 * gw]
    gg = pcols(8 * gw, 9 * gw)
    ysave(3 * gw, 4 * gw, gg * _sigmoid(gg) * o)
    if ns == 1:
        y_all = jnp.concatenate(y_parts, axis=1)
        g["y"][...] = jnp.dot(g["perm"][1], y_all, preferred_element_type=F32).astype(g["y"].dtype)


def _mix_consts(qt, ns, r, nc, pos0):
    assert qt == LANES, "decay tables are stacked as (3, 128, 128)"
    idx = np.arange(qt)
    nk = qt // SUBLANES
    if ns == 1:
        seq, tt = np.zeros_like(idx), (idx % SUBLANES) * nk + idx // SUBLANES
    else:
        seq, tt = idx % ns, idx // ns
    causal = (seq[:, None] == seq[None, :]) & (tt[:, None] >= tt[None, :])
    tri = causal.astype(np.float32)
    lg = np.log1p(-np.exp2(-5.0 - np.arange(RET_H, dtype=np.float64)))
    rel = (tt[:, None] - tt[None, :]).astype(np.float64)
    dmask = np.where(causal[None], np.exp(np.maximum(rel, 0.0)[None] * lg[:, None, None]), 0.0).astype(np.float32)
    lane_h = np.arange(LANES) // RET_DK
    qdec = np.exp((tt[:, None] + 1.0) * lg[lane_h][None, :])
    kdec = np.exp((r - 1.0 - tt[:, None]) * lg[lane_h][None, :])
    rdec = np.broadcast_to(np.exp(r * lg[lane_h])[:, None], (LANES, LANES))
    dec = np.stack([qdec, kdec, rdec]).astype(np.float32)
    half = RET_DK // 2
    pos = pos0 + jnp.asarray((np.arange(nc)[:, None] * qt + tt[None, :]).reshape(-1))
    inv = ROPE_BASE ** (-jnp.arange(half, dtype=F32) / half)
    ang = pos.astype(F32)[:, None] * inv
    reps = LANES // half
    cos, sin = jnp.tile(jnp.cos(ang), (1, reps)), jnp.tile(jnp.sin(ang), (1, reps))
    first = (np.arange(LANES) % RET_DK) < half
    trig = jnp.stack([cos, jnp.where(first, -sin, 0.0), jnp.where(first, 0.0, sin)])
    to_blocked = np.zeros((qt, qt), np.float32)
    to_blocked[idx, (idx % SUBLANES) * nk + idx // SUBLANES] = 1.0
    perm = jnp.asarray(np.stack([to_blocked, to_blocked.T])).astype(MXU_DTYPE)
    return dict(trig=trig, tri=jnp.asarray(tri), dmask=jnp.asarray(dmask), dec=jnp.asarray(dec), perm=perm)


def _state_layout(depth, bsz, ns, gw, sn):
    hk, dv = RET_H * RET_DK, gw // RET_H
    if ns == 1:
        def per_seq(rows, w):
            return (depth, bsz, rows, w), (None, None, rows, w), lambda b, c, l: (l[0], b, 0, 0)
        def matrix(h, rows, w):
            return (depth, bsz, h, rows, w), (None, None, h, rows, w), lambda b, c, l: (l[0], b, 0, 0, 0)
        return dict(lconv=per_seq(CONV_K - 1, gw), lh=per_seq(1, gw), s5r=per_seq(1, sn), s5i=per_seq(1, sn),
                    sconv=per_seq(CONV_K - 1, 2 * gw), sh=matrix(gw // SSD_HD, SSD_HD, SSD_N),
                    ret=matrix(RET_H, RET_DK, dv))

    def flat(per, w):
        return (depth, bsz * per, w), (None, ns * per, w), lambda b, c, l: (l[0], b, 0)

    def conv(w):
        return (depth, CONV_K - 1, bsz, w), (None, CONV_K - 1, ns, w), lambda b, c, l: (l[0], 0, b, 0)
    def matrix(h, rows, w):
        return (depth, bsz, h, rows, w), (None, ns, h, rows, w), lambda b, c, l: (l[0], b, 0, 0, 0)
    return dict(lconv=conv(gw), lh=flat(1, gw), s5r=flat(1, sn), s5i=flat(1, sn), sconv=conv(2 * gw),
                sh=matrix(gw // SSD_HD, SSD_HD, SSD_N), ret=matrix(RET_H, RET_DK, dv))


def _mixer(l, proj, consts, mw, state_in, state_out, *, nb, nc, qt, ns, r, d):
    dp = proj.shape[-1]
    gw = d // 4
    sn = mw["pw"].shape[-1]
    depth = mw["pw"].shape[0]
    layout = _state_layout(depth, nb * ns, ns, gw, sn)
    assert (state_in is None) == (ns == 1)

    def layer_spec(a):
        return pl.BlockSpec((None,) + a.shape[1:], lambda b, c, l: (l[0],) + (0,) * (a.ndim - 1))

    def const_spec(a):
        return pl.BlockSpec(a.shape, lambda b, c, l: (0,) * a.ndim)

    def state_spec(nm):
        _, block, index = layout[nm]
        return pl.BlockSpec(block, index)

    if ns == 1:
        io_spec = lambda w: pl.BlockSpec((qt, w), lambda b, c, l: (b * nc + c, 0))
        y_shape = (nb * nc * qt, d)
        cnames = ("trig", "tri", "dmask", "dec", "perm")
    else:
        proj = proj.reshape(r, nb * ns, dp)
        io_spec = lambda w: pl.BlockSpec((r, ns, w), lambda b, c, l: (0, b, 0))
        y_shape = (r, nb * ns, d)
        cnames = ("trig", "tri", "dmask", "dec")
    names = ("proj",) + cnames + MIX_WEIGHTS
    operands = [proj] + [consts[k] for k in cnames] + [mw[k] for k in MIX_WEIGHTS]
    in_specs = [io_spec(dp), pl.BlockSpec((3, qt, LANES), lambda b, c, l: (0, c, 0))] + \
               [const_spec(consts[k]) for k in cnames[1:]] + [layer_spec(mw[k]) for k in MIX_WEIGHTS]
    first_alias = 1 + len(operands)
    if state_in is not None:
        assert state_in is state_out
        names += tuple(nm + "_i" for nm in STATE_NAMES)
        in_specs += [state_spec(nm) for nm in STATE_NAMES]
    else:
        names += tuple(nm + "_alias" for nm in STATE_NAMES)
        in_specs += [pl.BlockSpec(memory_space=pl.ANY) for _ in STATE_NAMES]
    operands += [state_out[nm] for nm in STATE_NAMES]
    names += ("y",) + tuple(nm + "_o" for nm in STATE_NAMES)
    out_specs = [io_spec(d)] + [state_spec(nm) for nm in STATE_NAMES]
    out_shape = [jax.ShapeDtypeStruct(y_shape, MXU_DTYPE)] + \
                [jax.ShapeDtypeStruct(layout[nm][0], F32) for nm in STATE_NAMES]
    scratch = []
    if ns == 1:
        tail = (CONV_K - 1) * SUBLANES
        scratch_shapes = dict(cb_a=(tail, gw), cb_c=(tail, 2 * gw), c_lh=(1, gw), c_s5r=(1, sn),
                              c_s5i=(1, sn), c_sh=(SSD_G * SSD_N, gw), c_ret=(RET_H * RET_DK, gw))
        names += tuple(scratch_shapes)
        scratch = [pltpu.VMEM(shape, F32) for shape in scratch_shapes.values()]
    grid_spec = pltpu.PrefetchScalarGridSpec(num_scalar_prefetch=1, grid=(nb, nc),
                                             in_specs=in_specs, out_specs=out_specs, scratch_shapes=scratch)
    outs = pl.pallas_call(
        functools.partial(_mix_body, names=names, qt=qt, ns=ns, r=r, nc=nc),
        grid_spec=grid_spec,
        out_shape=out_shape,
        input_output_aliases={first_alias + k: 1 + k for k in range(len(STATE_NAMES))},
        compiler_params=pltpu.CompilerParams(dimension_semantics=("arbitrary", "arbitrary"),
                                             vmem_limit_bytes=VMEM_LIMIT),
        name="mixer",
    )(l, *operands)
    return outs[0].reshape(-1, d), dict(zip(STATE_NAMES, outs[1:]))


def _post_body(l_ref, y_ref, x_ref, g1_ref, sc_ref, sh_ref, g2_ref, lt_ref, wo_ref, wr_ref, br_ref,
               wg_ref, wu_ref, wd_ref, *rest):
    o_ref, xs_ref, cs_ref, os_ref = rest[-4:]
    del l_ref
    tm, d = x_ref.shape
    tms = xs_ref.shape[0]
    x = x_ref[...] + _token_rows(g1_ref[...], tm) * jnp.dot(y_ref[...], wo_ref[...], preferred_element_type=F32)
    h = _rms(x) * (1.0 + _token_rows(sc_ref[...], tm)) + _token_rows(sh_ref[...], tm)
    hb = h.astype(MXU_DTYPE)
    h_lo = (h - hb.astype(F32)).astype(MXU_DTYPE)
    wr = wr_ref[...]
    l12 = jnp.dot(hb, wr, preferred_element_type=F32)
    logits = (l12[:, 0:LANES] + l12[:, LANES:2 * LANES]
              + jnp.dot(h_lo, wr[:, 0:LANES], preferred_element_type=F32) + br_ref[...])
    logits_t = jnp.transpose(logits)
    col = [logits_t[k:k + 1, :] for k in range(N_GROUPS + N_EXPERTS)]

    def first_max(vals, allowed=None):
        neg = jnp.full_like(vals[0], -jnp.inf)
        cand = vals if allowed is None else [jnp.where(al > 0.0, v, neg) for v, al in zip(vals, allowed)]
        m = functools.reduce(jnp.maximum, cand)
        rem = jnp.ones_like(vals[0])
        hot = []
        for v in cand:
            f = jnp.where(v >= m, rem, 0.0)
            rem = rem - f
            hot.append(f)
        return hot, m

    grp, gmax = first_max(col[:N_GROUPS])
    gate = 1.0 / functools.reduce(lambda s, v: s + v, [jnp.exp(v - gmax) for v in col[:N_GROUPS]])
    le = [functools.reduce(lambda s, v: s + v,
                           [grp[gi] * col[N_GROUPS + gi * N_PER_GROUP + j] for gi in range(N_GROUPS)])
          for j in range(N_PER_GROUP)]
    top1, m1 = first_max(le)
    top2, m2 = first_max(le, [1.0 - f for f in top1])
    e2 = jnp.exp(m2 - m1)
    w1 = 1.0 / (1.0 + e2)
    w2 = e2 * w1
    wgrp = [(top1[j] * w1 + top2[j] * w2) * gate for j in range(N_PER_GROUP)]

    sub8 = lax.broadcasted_iota(jnp.int32, (SUBLANES, 1), 0)
    ghot_t = functools.reduce(lambda s, v: s + v, [jnp.where(sub8 == gi, grp[gi], 0.0) for gi in range(N_GROUPS)])
    before = _mm_nt(ghot_t, lt_ref[...])
    count = jnp.sum(ghot_t, axis=1, keepdims=True)
    cnt = [count[gi:gi + 1, :] for gi in range(N_GROUPS)]
    base = [jnp.zeros((1, 1), F32)]
    for gi in range(1, N_GROUPS):
        base.append(jnp.floor((base[-1] + cnt[gi - 1] + (PACK - 1.0)) * (1.0 / PACK)) * PACK)
    pos_row = functools.reduce(lambda s, v: s + v,
                               [grp[gi] * (base[gi] + before[gi:gi + 1, :]) for gi in range(N_GROUPS)])
    pos = jnp.transpose(jnp.broadcast_to(pos_row, (LANES, tm)))[:, 0:1]
    slot = lax.broadcasted_iota(jnp.int32, (tms, tm), 0).astype(F32)
    gather = jnp.where(slot == pos_row, 1.0, 0.0).astype(MXU_DTYPE)
    slot_l = lax.broadcasted_iota(jnp.int32, (tm, tms), 1).astype(F32)
    scatter = jnp.where(slot_l == pos, 1.0, 0.0).astype(MXU_DTYPE)

    xs_ref[...] = jnp.dot(gather, hb, preferred_element_type=F32).astype(xs_ref.dtype)
    sub_e = lax.broadcasted_iota(jnp.int32, (cs_ref.shape[1], 1), 0)
    comb_t = functools.reduce(lambda s, v: s + v,
                              [jnp.where(sub_e == e, grp[e // N_PER_GROUP] * wgrp[e % N_PER_GROUP], 0.0)
                               for e in range(N_EXPERTS)])
    c1, c2, c3 = _split3(comb_t)
    nt = lambda p: lax.dot_general(gather, p, (((1,), (1,)), ((), ())), preferred_element_type=F32)
    cs_ref[...] = nt(c1) + nt(c2) + nt(c3)
    os_ref[...] = jnp.zeros(os_ref.shape, os_ref.dtype)
    for gi in range(N_GROUPS):
        start = base[gi][0, 0].astype(jnp.int32)
        nblk = jnp.floor((cnt[gi][0, 0] + (EXPERT_ROWS - 1.0)) * (1.0 / EXPERT_ROWS)).astype(jnp.int32)

        def block(k, carry, gi=gi, start=start):
            rows = pl.ds(pl.multiple_of(start + k * EXPERT_ROWS, PACK), EXPERT_ROWS)
            xb = xs_ref[rows, :]
            cb = cs_ref[rows, :]
            acc = jnp.zeros((EXPERT_ROWS, d), F32)
            for j in range(N_PER_GROUP):
                e = gi * N_PER_GROUP + j
                gt = jnp.dot(xb, wg_ref[e], preferred_element_type=F32)
                he = gt * _sigmoid(gt) * jnp.dot(xb, wu_ref[e], preferred_element_type=F32)
                acc = acc + jnp.dot((he * cb[:, e:e + 1]).astype(MXU_DTYPE), wd_ref[e], preferred_element_type=F32)
            os_ref[rows, :] += acc
            return carry

        lax.fori_loop(0, nblk, block, 0)
    moe = jnp.dot(scatter, os_ref[...].astype(MXU_DTYPE), preferred_element_type=F32)
    out = x + _token_rows(g2_ref[...], tm) * moe
    if len(rest) == 5:
        out = _rms(out) * rest[0][...]
    o_ref[...] = out


def _post(l, y, x, mod, pw, tm, rows_per_seq, final_w=None, in_place=False):
    t, d = x.shape
    final = [] if final_w is None else [final_w.reshape(1, d)]
    tms = tm + EXPERT_ROWS + PACK * N_GROUPS
    ltri = jnp.asarray(np.tril(np.ones((tm, tm), np.float32), -1)).astype(MXU_DTYPE)

    def layer_spec(a):
        return pl.BlockSpec((None,) + a.shape[1:], lambda i, l: (l[0],) + (0,) * (a.ndim - 1),
                            pipeline_mode=pl.Buffered(1))

    grid_spec = pltpu.PrefetchScalarGridSpec(
        num_scalar_prefetch=1,
        grid=(t // tm,),
        in_specs=[
            pl.BlockSpec((tm, d), lambda i, l: (i, 0)),
            pl.BlockSpec((tm, d), lambda i, l: (i, 0)),
            _mod_spec(mod, 2, tm, rows_per_seq),
            _mod_spec(mod, 4, tm, rows_per_seq),
            _mod_spec(mod, 3, tm, rows_per_seq),
            _mod_spec(mod, 5, tm, rows_per_seq),
            pl.BlockSpec((tm, tm), lambda i, l: (0, 0)),
        ] + [layer_spec(pw[k]) for k in ("w_out", "w_route", "b_route", "w_gate", "w_up", "w_down")]
          + [pl.BlockSpec((1, d), lambda i, l: (0, 0)) for _ in final],
        out_specs=pl.BlockSpec((tm, d), lambda i, l: (i, 0)),
        scratch_shapes=[pltpu.VMEM((tms, d), MXU_DTYPE), pltpu.VMEM((tms, 2 * N_EXPERTS), F32),
                        pltpu.VMEM((tms, d), F32)],
    )
    return pl.pallas_call(
        _post_body,
        grid_spec=grid_spec,
        out_shape=jax.ShapeDtypeStruct((t, d), F32),
        input_output_aliases={2: 0} if in_place else {},
        compiler_params=pltpu.CompilerParams(dimension_semantics=("arbitrary",), vmem_limit_bytes=VMEM_LIMIT),
        name="post",
    )(l, y, x, mod, mod, mod, mod, ltri,
      *[pw[k] for k in ("w_out", "w_route", "b_route", "w_gate", "w_up", "w_down")], *final)


def _block_diag(w):
    depth, nblk, bi, bj = w.shape
    eye = jnp.eye(nblk, dtype=w.dtype)
    return (w[:, :, :, None, :] * eye[None, :, None, :, None]).reshape(depth, nblk * bi, nblk * bj)


def _pad_lanes(a, width):
    return jnp.pad(a, [(0, 0)] * (a.ndim - 1) + [(0, width - a.shape[-1])])


def _prep_weights(w_in, lru_conv_w, lru_conv_b, lru_wa, lru_ba, lru_wx, lru_bx, lru_lambda,
                  s5_a_re, s5_a_im, s5_b_re, s5_b_im, s5_c_re, s5_c_im, s5_d, s5_log_dt, s5_w_glu,
                  ssd_conv_w, ssd_conv_b, ssd_dt_bias, ssd_a_log, ssd_d, mix_norm, qt):
    depth, d, _ = w_in.shape
    gw = d // 4
    n_ssd_h = gw // SSD_HD
    xbc_end = 4 * gw + gw + 2 * SSD_G * SSD_N
    w_in = w_in.astype(MXU_DTYPE)
    dt_cols = jnp.repeat(w_in[..., xbc_end:xbc_end + n_ssd_h], SSD_HD, axis=-1)
    w_in_p = jnp.concatenate([w_in[..., :xbc_end], w_in[..., xbc_end + n_ssd_h:], dt_cols], axis=-1)

    def row(a):
        return _pad_lanes(a.reshape(depth, 1, -1), d)

    vec = jnp.concatenate([
        row(lru_conv_b), row(jnp.concatenate([lru_ba, lru_bx], -1)), row(lru_lambda), row(s5_d),
        row(ssd_conv_b), row(jnp.repeat(ssd_dt_bias, SSD_HD, -1)), row(jnp.repeat(ssd_a_log, SSD_HD, -1)),
        row(jnp.repeat(ssd_d, SSD_HD, -1)), row(mix_norm),
        _pad_lanes(lru_conv_w, d), _pad_lanes(ssd_conv_w, d),
        jnp.zeros((depth, V_ROWS - V_SCW - CONV_K, d), F32)], axis=1)

    wgate = jnp.concatenate([_block_diag(lru_wa), _block_diag(lru_wx)], axis=-1)

    dt = jnp.exp(s5_log_dt)[..., None]
    lr, li = s5_a_re, s5_a_im
    mag = jnp.exp(lr * dt)
    ab_re, ab_im = mag * jnp.cos(li * dt), mag * jnp.sin(li * dt)
    den = lr * lr + li * li
    q_re = ((ab_re - 1.0) * lr + ab_im * li) / den
    q_im = (ab_im * lr - (ab_re - 1.0) * li) / den
    bb_re = q_re[..., None] * s5_b_re - q_im[..., None] * s5_b_im
    bb_im = q_re[..., None] * s5_b_im + q_im[..., None] * s5_b_re
    bbm = jnp.concatenate([_block_diag(jnp.swapaxes(bb_re, -1, -2)), _block_diag(jnp.swapaxes(bb_im, -1, -2))], -1)
    cre = _block_diag(jnp.swapaxes(s5_c_re, -1, -2))
    cim = _block_diag(jnp.swapaxes(s5_c_im, -1, -2))
    a_re, a_im = ab_re.reshape(depth, -1), ab_im.reshape(depth, -1)
    pr, pi = a_re, a_im
    rows = []
    for _ in range(int(math.log2(qt))):
        rows += [pr, pi]
        pr, pi = pr * pr - pi * pi, 2.0 * pr * pi
    pw = jnp.stack(rows, axis=1)
    pw = jnp.pad(pw, ((0, 0), (0, (-pw.shape[1]) % SUBLANES), (0, 0)))
    pr, pi = a_re, a_im
    k_re, k_im = [], []
    for _ in range(qt // SUBLANES):
        k_re.append(pr)
        k_im.append(pi)
        pr, pi = pr * a_re - pi * a_im, pr * a_im + pi * a_re
    pk = jnp.stack(k_re + k_im, axis=1)

    mw = dict(vec=vec, wgate=wgate.astype(MXU_DTYPE), bbm=bbm.astype(MXU_DTYPE), cre=cre.astype(MXU_DTYPE),
              cim=cim.astype(MXU_DTYPE), wglu=s5_w_glu.astype(MXU_DTYPE), pw=pw, pk=pk)
    return w_in_p, mw


def _states_to_layout(states, layout, ns):
    out = {}
    for nm, s in zip(STATE_NAMES, states):
        if nm in ("lconv", "sconv"):
            s = s.transpose(0, 2, 1, 3)
        out[nm] = s.reshape(layout[nm][0])
    return out


def _states_from_layout(arrs, ns, ref_shapes):
    out = []
    for nm, shape in zip(STATE_NAMES, ref_shapes):
        a = arrs[nm]
        if ns > 1 and nm in ("lconv", "sconv"):
            a = a.transpose(0, 2, 1, 3)
        out.append(a.reshape(shape))
    return tuple(out)


def _trunk(x, mod, states, ref_shapes, pos0, w_in_p, mw, pw, final_norm, *, qt, ns, tm, tm_in):
    bsz, seq_len, d = x.shape
    depth = w_in_p.shape[0]
    gw = d // 4
    r = qt // ns
    nc = seq_len // r if ns == 1 else 1
    assert (ns == 1 and seq_len % qt == 0 and states is None) or (ns > 1 and r == seq_len and bsz % ns == 0)
    assert r >= CONV_K and (r & (r - 1)) == 0
    nb = bsz // ns
    tok = bsz * seq_len
    consts = _mix_consts(qt, ns, r, nc, pos0)
    layout = _state_layout(depth, bsz, ns, gw, mw["pw"].shape[-1])
    if states is None:
        state_out = {nm: jnp.zeros(layout[nm][0], F32) for nm in STATE_NAMES}
    else:
        state_out = _states_to_layout(states, layout, ns)
    x2 = x.reshape(tok, d) if ns == 1 else x.transpose(1, 0, 2).reshape(tok, d)
    blocked_chunk = qt if ns == 1 else None

    def layer(carry, l, final_w=None, in_place=True):
        xc, st = carry
        lv = jnp.reshape(l, (1,)).astype(jnp.int32)
        proj = _inproj(lv, xc, mod, w_in_p, tm_in, seq_len, blocked_chunk)
        y, st = _mixer(lv, proj, consts, mw, None if states is None else st, st,
                       nb=nb, nc=nc, qt=qt, ns=ns, r=r, d=d)
        xn = _post(lv, y, xc, mod, pw, tm, seq_len, final_w, in_place)
        return (xn, st), None

    assert depth >= 2
    carry, _ = layer((x2, state_out), jnp.int32(0), in_place=False)
    carry, _ = lax.scan(layer, carry, jnp.arange(1, depth - 1))
    (y, st), _ = layer(carry, jnp.int32(depth - 1), final_norm)
    y = y.reshape(bsz, seq_len, d) if ns == 1 else y.reshape(seq_len, bsz, d).transpose(1, 0, 2)
    return y, _states_from_layout(st, ns, ref_shapes)


def kernel(x_prompt, x_sample, state_lru_conv, state_lru_h, state_s5_re, state_s5_im, state_ssd_conv, state_ssd_h, state_ret, c_prompt, c_sample, w_ada, b_ada, w_in, lru_conv_w, lru_conv_b, lru_wa, lru_ba, lru_wx, lru_bx, lru_lambda, s5_a_re, s5_a_im, s5_b_re, s5_b_im, s5_c_re, s5_c_im, s5_d, s5_log_dt, s5_w_glu, ssd_conv_w, ssd_conv_b, ssd_dt_bias, ssd_a_log, ssd_d, mix_norm, w_out, w_route_group, b_route_group, w_route_exp, b_route_exp, w_exp_gate, w_exp_up, w_exp_down, final_norm):
    bp, lp, d = x_prompt.shape
    bs, ls, _ = x_sample.shape
    depth = w_in.shape[0]
    qt = LANES
    ns_s = qt // ls

    mod_p, mod_s = _modulation(jnp.concatenate([c_prompt, c_sample], axis=0), bp, w_ada, b_ada)
    mod_p = mod_p.reshape(depth, N_MOD, bp, 1, d)

    w_in_p, mw = _prep_weights(w_in, lru_conv_w, lru_conv_b, lru_wa, lru_ba, lru_wx, lru_bx, lru_lambda,
                               s5_a_re, s5_a_im, s5_b_re, s5_b_im, s5_c_re, s5_c_im, s5_d, s5_log_dt, s5_w_glu,
                               ssd_conv_w, ssd_conv_b, ssd_dt_bias, ssd_a_log, ssd_d, mix_norm, qt)
    w_route = _pad_lanes(jnp.concatenate([w_route_group, w_route_exp.reshape(depth, d, N_EXPERTS)], -1), LANES)
    b_route = _pad_lanes(jnp.concatenate([b_route_group, b_route_exp.reshape(depth, N_EXPERTS)], -1), LANES)
    w_route_hi = w_route.astype(MXU_DTYPE)
    w_route_lo = (w_route - w_route_hi.astype(F32)).astype(MXU_DTYPE)
    w_route = jnp.concatenate([w_route_hi, w_route_lo], axis=-1)
    pw = dict(w_out=w_out.astype(MXU_DTYPE), w_route=w_route, b_route=b_route.reshape(depth, 1, LANES),
              w_gate=w_exp_gate.astype(MXU_DTYPE), w_up=w_exp_up.astype(MXU_DTYPE),
              w_down=w_exp_down.astype(MXU_DTYPE))
    w_in_p, mw, pw = lax.optimization_barrier((w_in_p, mw, pw))

    states_s = (state_lru_conv, state_lru_h, state_s5_re, state_s5_im, state_ssd_conv, state_ssd_h, state_ret)
    shapes_s = [s.shape for s in states_s]
    shapes_p = [(depth, bp) + s[2:] for s in shapes_s]
    y_p, new_p = _trunk(x_prompt, mod_p, None, shapes_p, 0, w_in_p, mw, pw, final_norm,
                        qt=qt, ns=1, tm=min(512, lp), tm_in=min(1024, lp))
    y_s, new_s = _trunk(x_sample, mod_s, states_s, shapes_s, PAST_LEN, w_in_p, mw, pw, final_norm,
                        qt=qt, ns=ns_s, tm=bs * ls, tm_in=bs * ls)
    out = [y_p, y_s]
    for a, b in zip(new_p, new_s):
        out += [a, b]
    return tuple(out)
```

```python
import functools
import math

import numpy as np
import jax
import jax.numpy as jnp
from jax import lax
from jax.experimental import pallas as pl
from jax.experimental.pallas import tpu as pltpu

F32 = jnp.float32
MXU_DTYPE = jnp.bfloat16

EPS = 1e-6
CONV_K = 4
LRU_C = 8.0
SSD_HD = 64
SSD_G = 2
SSD_N = 64
RET_H = 4
RET_DK = 32
ROPE_BASE = 10000.0
N_GROUPS = 4
N_PER_GROUP = 4
N_EXPERTS = N_GROUPS * N_PER_GROUP
N_MOD = 6
PAST_LEN = 16384

LANES = 128
SUBLANES = 8
PACK = 16
VMEM_LIMIT = 56 * 1024 * 1024
EXPERT_ROWS = 176


def _mm(a, b):
    return jnp.dot(a.astype(MXU_DTYPE), b.astype(MXU_DTYPE), preferred_element_type=F32)


def _mm_nt(a, b):
    return lax.dot_general(a.astype(MXU_DTYPE), b.astype(MXU_DTYPE), (((1,), (1,)), ((), ())),
                           preferred_element_type=F32)


def _mm_tn(a, b):
    return lax.dot_general(a.astype(MXU_DTYPE), b.astype(MXU_DTYPE), (((0,), (0,)), ((), ())),
                           preferred_element_type=F32)


def _split3(x):
    x1 = x.astype(MXU_DTYPE)
    r1 = x - x1.astype(F32)
    x2 = r1.astype(MXU_DTYPE)
    x3 = (r1 - x2.astype(F32)).astype(MXU_DTYPE)
    return x1, x2, x3


def _sigmoid(x):
    return 0.5 * jnp.tanh(0.5 * x) + 0.5


def _rms(x):
    return x * lax.rsqrt(jnp.mean(x * x, axis=-1, keepdims=True) + EPS)


def _token_rows(v, tm):
    n = v.shape[0]
    return v if n in (1, tm) else jnp.tile(v, (tm // n, 1))


def _mod_body(c_ref, w_ref, b_ref, first_ref, rest_ref):
    c = c_ref[...]
    m = _mm(c * _sigmoid(c), w_ref[...]) + b_ref[...]
    n_first = first_ref.shape[0]
    first_ref[...] = m[0:n_first]
    rest_ref[...] = m[n_first:]


def _modulation(c_all, n_first, w_ada, b_ada):
    depth, d, _ = w_ada.shape
    nb = c_all.shape[0]
    return pl.pallas_call(
        _mod_body,
        grid=(depth, N_MOD),
        in_specs=[
            pl.BlockSpec((nb, d), lambda l, k: (0, 0)),
            pl.BlockSpec((None, d, d), lambda l, k: (l, 0, k)),
            pl.BlockSpec((None, None, 1, d), lambda l, k: (l, k, 0, 0)),
        ],
        out_specs=[pl.BlockSpec((None, None, n_first, d), lambda l, k: (l, k, 0, 0)),
                   pl.BlockSpec((None, None, nb - n_first, d), lambda l, k: (l, k, 0, 0))],
        out_shape=[jax.ShapeDtypeStruct((depth, N_MOD, n_first, d), F32),
                   jax.ShapeDtypeStruct((depth, N_MOD, nb - n_first, d), F32)],
        compiler_params=pltpu.CompilerParams(dimension_semantics=("arbitrary", "arbitrary"),
                                             vmem_limit_bytes=VMEM_LIMIT),
        name="modulation",
    )(c_all, w_ada, b_ada.reshape(depth, N_MOD, 1, d))


def _mod_spec(mod, k, tm, rows_per_seq):
    if mod.ndim == 5:
        tiles_per_seq = rows_per_seq // tm
        return pl.BlockSpec((None, None, None, 1, mod.shape[-1]),
                            lambda i, l: (l[0], k, i // tiles_per_seq, 0, 0))
    return pl.BlockSpec((None, None) + mod.shape[2:], lambda i, l: (l[0], k, 0, 0))


def _inproj_body(l_ref, x_ref, sc_ref, sh_ref, w_ref, o_ref, *, blocked_chunk):
    tm = x_ref.shape[0]
    h = _rms(x_ref[...]) * (1.0 + _token_rows(sc_ref[...], tm)) + _token_rows(sh_ref[...], tm)
    hb = h.astype(MXU_DTYPE)
    if blocked_chunk is not None:
        qt, nk = blocked_chunk, blocked_chunk // SUBLANES
        r_i = lax.broadcasted_iota(jnp.int32, (qt, qt), 0)
        c_i = lax.broadcasted_iota(jnp.int32, (qt, qt), 1)
        to_blocked = jnp.where(c_i == (r_i % SUBLANES) * nk + r_i // SUBLANES, 1.0, 0.0).astype(MXU_DTYPE)
        hb = jnp.concatenate([jnp.dot(to_blocked, hb[j * qt:(j + 1) * qt], preferred_element_type=F32)
                              for j in range(tm // qt)], axis=0).astype(MXU_DTYPE)
    o_ref[...] = jnp.dot(hb, w_ref[...], preferred_element_type=F32)


def _inproj(l, x, mod, w_in, tm, rows_per_seq, blocked_chunk=None):
    t, d = x.shape
    dp = w_in.shape[-1]
    assert blocked_chunk is None or (mod.ndim == 5 and tm % blocked_chunk == 0)
    grid_spec = pltpu.PrefetchScalarGridSpec(
        num_scalar_prefetch=1,
        grid=(t // tm,),
        in_specs=[
            pl.BlockSpec((tm, d), lambda i, l: (i, 0)),
            _mod_spec(mod, 1, tm, rows_per_seq),
            _mod_spec(mod, 0, tm, rows_per_seq),
            pl.BlockSpec((None, d, dp), lambda i, l: (l[0], 0, 0)),
        ],
        out_specs=pl.BlockSpec((tm, dp), lambda i, l: (i, 0)),
    )
    return pl.pallas_call(
        functools.partial(_inproj_body, blocked_chunk=blocked_chunk),
        grid_spec=grid_spec,
        out_shape=jax.ShapeDtypeStruct((t, dp), F32),
        compiler_params=pltpu.CompilerParams(dimension_semantics=("arbitrary",), vmem_limit_bytes=VMEM_LIMIT),
        name="inproj",
    )(l, x, mod, mod, w_in)


V_LCB, V_BGATE, V_LAM, V_S5D, V_SCB, V_DTB, V_ALOG, V_SSDD, V_GAIN, V_LCW, V_SCW = 0, 1, 2, 3, 4, 5, 6, 7, 8, 9, 13
V_ROWS = 24
STATE_NAMES = ("lconv", "lh", "s5r", "s5i", "sconv", "sh", "ret")
MIX_WEIGHTS = ("vec", "wgate", "bbm", "cre", "cim", "wglu", "pw", "pk")


def _mix_body(l_ref, *refs, names, qt, ns, r, nc):
    del l_ref
    g = dict(zip(names, refs))
    gw = g["lh_o"].shape[-1]
    sn = g["s5r_o"].shape[-1]
    nlev = int(math.log2(r))
    nk = qt // SUBLANES
    c = pl.program_id(1)

    if ns == 1:
        @pl.when(c == 0)
        def _zero_carries():
            for nm in ("c_lh", "c_s5r", "c_s5i", "c_sh", "c_ret"):
                g[nm][...] = jnp.zeros(g[nm].shape, F32)
            for nm in ("cb_a", "cb_c"):
                g[nm][...] = jnp.zeros(g[nm].shape, F32)

    proj = g["proj"]
    vec = g["vec"]
    pw = g["pw"]

    def pcols(a, b):
        v = proj[..., a:b]
        return v if ns == 1 else v.reshape(qt, b - a)

    y_parts = []

    def ysave(a, b, val):
        val = val.astype(g["y"].dtype)
        if ns == 1:
            y_parts.append(val)
        else:
            g["y"][..., a:b] = val.reshape(r, ns, b - a)

    row = lax.broadcasted_iota(jnp.int32, (qt, 1), 0)
    t = (row % SUBLANES) * nk + row // SUBLANES if ns == 1 else row // ns
    sub = lax.broadcasted_iota(jnp.int32, (SUBLANES, 1), 0)

    def down(x, d, fill=0.0):
        return jnp.where(t >= d, pltpu.roll(x, d * ns, 0), fill)

    def up(x, d):
        return jnp.where(t + d < r, pltpu.roll(x, qt - d * ns, 0), 0.0)

    def sdown(x, d, fill=0.0):
        return jnp.where(sub >= d, pltpu.roll(x, d, 0), fill)

    def slabs(x):
        return [x[SUBLANES * k:SUBLANES * (k + 1)] for k in range(nk)]

    def first_rows(carry, init):
        if ns == 1:
            return jnp.where(row == 0, g[carry][...], 0.0)
        h0 = g[init][...]
        return jnp.concatenate([h0, jnp.zeros((qt - ns, h0.shape[1]), F32)], axis=0)

    def conv(xraw, nm, w0, b):
        width = xraw.shape[-1]
        acc = vec[b:b + 1, 0:width] + vec[w0 + 3:w0 + 4, 0:width] * xraw
        if ns == 1:
            cb = g["cb_" + nm]
            xs_k = slabs(xraw)
            wrap = [jnp.where(sub >= 1, pltpu.roll(xs_k[nk - j], 1, 0),
                              pltpu.roll(cb[(CONV_K - 1 - j) * SUBLANES:(CONV_K - j) * SUBLANES, :], 1, 0))
                    for j in range(1, CONV_K)]
            for m in range(1, CONV_K):
                sh = jnp.concatenate([wrap[m - k - 1] for k in range(m)] + xs_k[0:nk - m], axis=0)
                acc = acc + vec[w0 + 3 - m:w0 + 4 - m, 0:width] * sh
            cb[...] = xraw[qt - (CONV_K - 1) * SUBLANES:, :]
            g[{"a": "lconv_o", "c": "sconv_o"}[nm]][...] = jnp.concatenate(
                [xs_k[nk - j][SUBLANES - 1:SUBLANES, :] for j in range(CONV_K - 1, 0, -1)], axis=0)
            return acc
        o = g[{"a": "lconv_o", "c": "sconv_o"}[nm]]
        buf = g[{"a": "lconv_i", "c": "sconv_i"}[nm]][...].reshape((CONV_K - 1) * ns, width)
        buf = jnp.concatenate([buf, jnp.zeros((qt - (CONV_K - 1) * ns, width), F32)], axis=0)
        for m in range(1, CONV_K):
            prev = buf if m == 3 else pltpu.roll(buf, qt - (3 - m) * ns, 0)
            sh = jnp.where(t >= m, pltpu.roll(xraw, m * ns, 0), prev)
            acc = acc + vec[w0 + 3 - m:w0 + 4 - m, 0:width] * sh
        o[...] = xraw[(r - (CONV_K - 1)) * ns:, :].reshape(CONV_K - 1, ns, width)
        return acc

    def save_last(nm, h):
        if ns == 1:
            g["c_" + nm][...] = h[qt - 1:qt, :]
            g[nm + "_o"][...] = h[qt - 1:qt, :]
        else:
            g[nm + "_o"][...] = h[qt - ns:, :]

    def seq_mask(width):
        lane_seq = lax.broadcasted_iota(jnp.int32, (qt, ns * width), 1) // width
        row_seq = lax.broadcasted_iota(jnp.int32, (qt, ns * width), 0) % ns
        return (lane_seq == row_seq).astype(F32)

    gain = vec[V_GAIN:V_GAIN + 1, :]

    xc = conv(pcols(0, gw), "a", V_LCW, V_LCB)
    ga = pcols(gw, 2 * gw)
    pre = _mm(xc, g["wgate"][...]) + vec[V_BGATE:V_BGATE + 1, 0:2 * gw]
    rg = _sigmoid(pre[:, 0:gw])
    ig = _sigmoid(pre[:, gw:2 * gw])
    log_a = -LRU_C * rg * jax.nn.softplus(-vec[V_LAM:V_LAM + 1, 0:gw])
    a = jnp.exp(log_a)
    b = jnp.sqrt(-jnp.tanh(log_a) * (a * a + 1.0)) * (ig * xc)
    b = b + a * first_rows("c_lh", "lh_i")
    if ns == 1:
        a_s, b_s = slabs(a), slabs(b)
        hs, ps = [b_s[0]], [a_s[0]]
        for k in range(1, nk):
            hs.append(a_s[k] * hs[-1] + b_s[k])
            ps.append(a_s[k] * ps[-1])
        e, ae = hs[-1], ps[-1]
        for j in range(int(math.log2(SUBLANES))):
            d = 1 << j
            e = ae * sdown(e, d) + e
            ae = ae * sdown(ae, d, 1.0)
        cin = sdown(e, 1)
        b = jnp.concatenate([hs[k] + ps[k] * cin for k in range(nk)], axis=0)
    else:
        for k in range(nlev):
            d = 1 << k
            b = a * down(b, d) + b
            a = a * down(a, d, 1.0)
    save_last("lh", b)
    ya = _rms(b * jax.nn.gelu(ga)) * gain[:, 0:gw]

    u = pcols(2 * gw, 3 * gw)
    bu = _mm(u, g["bbm"][...])
    p_re, p_im = pw[0:1, :], pw[1:2, :]
    h0r, h0i = first_rows("c_s5r", "s5r_i"), first_rows("c_s5i", "s5i_i")
    hr = bu[:, 0:sn] + (p_re * h0r - p_im * h0i)
    hi = bu[:, sn:2 * sn] + (p_re * h0i + p_im * h0r)
    if ns == 1:
        pk = g["pk"]
        br_s, bi_s = slabs(hr), slabs(hi)
        hrs, his = [br_s[0]], [bi_s[0]]
        for k in range(1, nk):
            hrs.append(br_s[k] + (p_re * hrs[-1] - p_im * his[-1]))
            his.append(bi_s[k] + (p_re * his[-1] + p_im * hrs[-2]))
        er, ei = hrs[-1], his[-1]
        lev0 = int(math.log2(nk))
        for j in range(int(math.log2(SUBLANES))):
            d = 1 << j
            q_re, q_im = pw[2 * (lev0 + j):2 * (lev0 + j) + 1, :], pw[2 * (lev0 + j) + 1:2 * (lev0 + j) + 2, :]
            sr, si = sdown(er, d), sdown(ei, d)
            er, ei = er + (q_re * sr - q_im * si), ei + (q_re * si + q_im * sr)
        cr, ci = sdown(er, 1), sdown(ei, 1)
        hr = jnp.concatenate([hrs[k] + (pk[k:k + 1, :] * cr - pk[nk + k:nk + k + 1, :] * ci) for k in range(nk)], 0)
        hi = jnp.concatenate([his[k] + (pk[k:k + 1, :] * ci + pk[nk + k:nk + k + 1, :] * cr) for k in range(nk)], 0)
    else:
        for k in range(nlev):
            d = 1 << k
            q_re, q_im = pw[2 * k:2 * k + 1, :], pw[2 * k + 1:2 * k + 2, :]
            sr, si = down(hr, d), down(hi, d)
            hr, hi = hr + (q_re * sr - q_im * si), hi + (q_re * si + q_im * sr)
    save_last("s5r", hr)
    save_last("s5i", hi)
    yb = _mm(hr, g["cre"][...]) - _mm(hi, g["cim"][...])
    yb = jax.nn.gelu(yb + vec[V_S5D:V_S5D + 1, 0:gw] * u)
    yb = yb * _sigmoid(_mm(yb, g["wglu"][...]))
    yb = _rms(yb) * gain[:, gw:2 * gw]
    ysave(0, 2 * gw, jnp.concatenate([ya, yb], axis=1))

    tri = g["tri"][...] > 0.0
    xbc = conv(pcols(4 * gw, 6 * gw), "c", V_SCW, V_SCB)
    xbc = xbc * _sigmoid(xbc)
    xs, bm, cm = xbc[:, 0:gw], xbc[:, gw:gw + LANES], xbc[:, gw + LANES:2 * gw]
    dt = jax.nn.softplus(pcols(9 * gw, 10 * gw) + vec[V_DTB:V_DTB + 1, 0:gw])
    a_e = -jnp.exp(vec[V_ALOG:V_ALOG + 1, 0:gw])
    dta = dt * a_e
    if ns == 1:
        cs = slabs(dta)
        for k in range(1, nk):
            cs[k] = cs[k] + cs[k - 1]
        e = cs[-1]
        for j in range(int(math.log2(SUBLANES))):
            e = e + sdown(e, 1 << j)
        cin = sdown(e, 1)
        acum = jnp.concatenate([ck + cin for ck in cs], axis=0)
        alast = acum[qt - 1:qt, :]
    else:
        acum = dta
        for k in range(nlev):
            acum = acum + down(acum, 1 << k)
        suf = dta
        for k in range(nlev):
            suf = suf + up(suf, 1 << k)
        alast = acum + suf - dta
    wend = jnp.exp(alast - acum) * dt
    acum_t = jnp.transpose(acum)
    lane = lax.broadcasted_iota(jnp.int32, (1, gw), 1)
    lane_b = lax.broadcasted_iota(jnp.int32, (1, LANES), 1)
    heads_per_group = gw // SSD_HD // SSD_G
    xdt = xs * dt
    xw = xs * wend
    ydiag = jnp.zeros((qt, gw), F32)
    for grp in range(SSD_G):
        gm = ((lane_b // SSD_N) == grp).astype(F32)
        cb = _mm_nt(cm * gm, bm)
        for hh in range(heads_per_group):
            hd = grp * heads_per_group + hh
            col = acum[:, hd * SSD_HD:hd * SSD_HD + 1]
            rowv = acum_t[hd * SSD_HD:hd * SSD_HD + 1, :]
            dec = jnp.where(tri, jnp.exp(col - rowv), 0.0)
            hm = ((lane // SSD_HD) == hd).astype(F32)
            ydiag = ydiag + _mm(cb * dec, xdt * hm)
    if ns == 1:
        st = g["c_sh"][...]
        srow = lax.broadcasted_iota(jnp.int32, st.shape, 0) // SSD_N
        yoff = _mm(cm, st)
        slane = lax.broadcasted_iota(jnp.int32, st.shape, 1) // (SSD_HD * heads_per_group)
        upd = jnp.where(srow == slane, _mm_tn(bm, xw), 0.0)
        st = st * jnp.exp(acum[qt - 1:qt, :]) + upd
        g["c_sh"][...] = st

        @pl.when(c == nc - 1)
        def _final_ssd_state():
            own = jnp.concatenate([st[grp * SSD_N:(grp + 1) * SSD_N, grp * LANES:(grp + 1) * LANES]
                                   for grp in range(SSD_G)], axis=1)
            own = jnp.concatenate([own, jnp.zeros_like(own)], axis=0)
            g["sh_o"][...] = jnp.transpose(own)[:, 0:SSD_N].reshape(g["sh_o"].shape)
    else:
        sh_i, sh_o = g["sh_i"], g["sh_o"]
        per_seq = gw // SSD_HD * SSD_HD
        cols = [[], []]
        for s in range(0, ns, 2):
            two = jnp.concatenate([sh_i[s].reshape(per_seq, SSD_N), sh_i[s + 1].reshape(per_seq, SSD_N)], axis=1)
            two = jnp.transpose(two)
            for grp in range(SSD_G):
                cols[grp] += [two[0:SSD_N, grp * LANES:(grp + 1) * LANES],
                              two[SSD_N:2 * SSD_N, grp * LANES:(grp + 1) * LANES]]
        st = jnp.concatenate([jnp.concatenate(cols[0], axis=1), jnp.concatenate(cols[1], axis=1)], axis=0)
        srow = lax.broadcasted_iota(jnp.int32, st.shape, 0) // SSD_N
        seqm = seq_mask(LANES)
        lastm = jnp.where(t == r - 1, seqm, 0.0)
        yo, upd, dl = [], None, None
        for grp in range(SSD_G):
            gm = ((lane_b // SSD_N) == grp).astype(F32)
            z = _mm(cm * gm, st) * seqm
            zf = z[:, 0:LANES]
            for s in range(1, ns):
                zf = zf + z[:, s * LANES:(s + 1) * LANES]
            yo.append(zf)
            sl = slice(grp * LANES, (grp + 1) * LANES)
            u_g = _mm_tn(bm, jnp.tile(xw[:, sl], (1, ns)) * seqm)
            d_g = jnp.exp(jnp.sum(jnp.tile(acum[:, sl], (1, ns)) * lastm, axis=0, keepdims=True))
            upd = u_g if grp == 0 else jnp.where(srow == grp, u_g, upd)
            dl = d_g if grp == 0 else jnp.where(srow == grp, d_g, dl)
        yoff = jnp.concatenate(yo, axis=1)
        st = st * dl + upd
        for s in range(0, ns, 2):
            two = jnp.concatenate(
                [jnp.concatenate([st[0:SSD_N, q * LANES:(q + 1) * LANES], st[SSD_N:2 * SSD_N, q * LANES:(q + 1) * LANES]],
                                 axis=1) for q in (s, s + 1)], axis=0)
            two = jnp.transpose(two)
            sh_o[s] = two[:, 0:SSD_N].reshape(sh_o.shape[1:])
            sh_o[s + 1] = two[:, SSD_N:2 * SSD_N].reshape(sh_o.shape[1:])
    yc = ydiag + yoff * jnp.exp(acum) + vec[V_SSDD:V_SSDD + 1, 0:gw] * xs
    z = pcols(3 * gw, 4 * gw)
    yc = yc * (z * _sigmoid(z))
    ysave(2 * gw, 3 * gw, _rms(yc) * gain[:, 2 * gw:3 * gw])

    trig = g["trig"]
    cos, sin_a, sin_b = trig[0], trig[1], trig[2]

    def rope(x):
        return x * cos + pltpu.roll(x, LANES - RET_DK // 2, 1) * sin_a + pltpu.roll(x, RET_DK // 2, 1) * sin_b

    q = rope(pcols(6 * gw, 6 * gw + LANES))
    kk = rope(pcols(6 * gw + LANES, 7 * gw)) * (RET_DK ** -0.5)
    v = pcols(7 * gw, 8 * gw)
    dv = gw // RET_H
    intra = jnp.zeros((qt, gw), F32)
    for hd in range(RET_H):
        qm = ((lane_b // RET_DK) == hd).astype(F32)
        sc = _mm_nt(q * qm, kk) * g["dmask"][hd]
        vm = ((lane // dv) == hd).astype(F32)
        intra = intra + _mm(sc, v * vm)
    dec = g["dec"]
    qd, kd = q * dec[0], kk * dec[1]
    if ns == 1:
        rs = g["c_ret"][...]
        cross = _mm(qd, rs)
        rrow = lax.broadcasted_iota(jnp.int32, rs.shape, 0) // RET_DK
        rlane = lax.broadcasted_iota(jnp.int32, rs.shape, 1) // dv
        upd = jnp.where(rrow == rlane, _mm_tn(kd, v), 0.0)
    else:
        ret_i, ret_o = g["ret_i"], g["ret_o"]
        hk = RET_H * RET_DK
        rs = jnp.concatenate([ret_i[s].reshape(hk, dv) for s in range(ns)], axis=1)
        seqm = seq_mask(dv)
        lo = lane_b < dv
        pairs = []
        upd = jnp.zeros(rs.shape, F32)
        for pp in range(RET_H // 2):
            vp = v[:, pp * LANES:(pp + 1) * LANES]
            vsw = pltpu.roll(vp, dv, 1)
            tots = []
            for hh in range(2):
                qm = ((lane_b // RET_DK) == 2 * pp + hh).astype(F32)
                zc = _mm(qd * qm, rs) * seqm
                acc = zc[:, 0:LANES]
                for s2 in range(1, ns // 2):
                    acc = acc + zc[:, s2 * LANES:(s2 + 1) * LANES]
                tots.append(acc + pltpu.roll(acc, dv, 1))
                vh = jnp.where(lo, vp, vsw) if hh == 0 else jnp.where(lo, vsw, vp)
                upd = upd + _mm_tn(kd * qm, jnp.tile(vh, (1, ns // 2)) * seqm)
            pairs.append(jnp.where(lo, tots[0], tots[1]))
        cross = jnp.concatenate(pairs, axis=1)
    rs = rs * dec[2][:, 0:1] + upd
    if ns == 1:
        g["c_ret"][...] = rs

        @pl.when(c == nc - 1)
        def _final_ret_state():
            own = functools.reduce(lambda s, x: s + x, [rs[:, hd * dv:(hd + 1) * dv] for hd in range(RET_H)])
            g["ret_o"][...] = own.reshape(g["ret_o"].shape)
    else:
        for s in range(ns):
            ret_o[s] = rs[:, s * dv:(s + 1) * dv].reshape(ret_o.shape[1:])
    o = intra + cross
    o2 = o * o
    ms = jnp.zeros((qt, gw), F32)
    for hd in range(RET_H):
        vm = ((lane // dv) == hd).astype(F32)
        ms = ms + vm * jnp.sum(o2 * vm, axis=-1, keepdims=True)
    o = o * lax.rsqrt(ms * (1.0 / dv) + EPS) * gain[:, 3 * gw:4 * gw]
    gg = pcols(8 * gw, 9 * gw)
    ysave(3 * gw, 4 * gw, gg * _sigmoid(gg) * o)
    if ns == 1:
        y_all = jnp.concatenate(y_parts, axis=1)
        g["y"][...] = jnp.dot(g["perm"][1], y_all, preferred_element_type=F32).astype(g["y"].dtype)


def _mix_consts(qt, ns, r, nc, pos0):
    assert qt == LANES, "decay tables are stacked as (3, 128, 128)"
    idx = np.arange(qt)
    nk = qt // SUBLANES
    if ns == 1:
        seq, tt = np.zeros_like(idx), (idx % SUBLANES) * nk + idx // SUBLANES
    else:
        seq, tt = idx % ns, idx // ns
    causal = (seq[:, None] == seq[None, :]) & (tt[:, None] >= tt[None, :])
    tri = causal.astype(np.float32)
    lg = np.log1p(-np.exp2(-5.0 - np.arange(RET_H, dtype=np.float64)))
    rel = (tt[:, None] - tt[None, :]).astype(np.float64)
    dmask = np.where(causal[None], np.exp(np.maximum(rel, 0.0)[None] * lg[:, None, None]), 0.0).astype(np.float32)
    lane_h = np.arange(LANES) // RET_DK
    qdec = np.exp((tt[:, None] + 1.0) * lg[lane_h][None, :])
    kdec = np.exp((r - 1.0 - tt[:, None]) * lg[lane_h][None, :])
    rdec = np.broadcast_to(np.exp(r * lg[lane_h])[:, None], (LANES, LANES))
    dec = np.stack([qdec, kdec, rdec]).astype(np.float32)
    half = RET_DK // 2
    pos = pos0 + jnp.asarray((np.arange(nc)[:, None] * qt + tt[None, :]).reshape(-1))
    inv = ROPE_BASE ** (-jnp.arange(half, dtype=F32) / half)
    ang = pos.astype(F32)[:, None] * inv
    reps = LANES // half
    cos, sin = jnp.tile(jnp.cos(ang), (1, reps)), jnp.tile(jnp.sin(ang), (1, reps))
    first = (np.arange(LANES) % RET_DK) < half
    trig = jnp.stack([cos, jnp.where(first, -sin, 0.0), jnp.where(first, 0.0, sin)])
    to_blocked = np.zeros((qt, qt), np.float32)
    to_blocked[idx, (idx % SUBLANES) * nk + idx // SUBLANES] = 1.0
    perm = jnp.asarray(np.stack([to_blocked, to_blocked.T])).astype(MXU_DTYPE)
    return dict(trig=trig, tri=jnp.asarray(tri), dmask=jnp.asarray(dmask), dec=jnp.asarray(dec), perm=perm)


def _state_layout(depth, bsz, ns, gw, sn):
    hk, dv = RET_H * RET_DK, gw // RET_H
    if ns == 1:
        def per_seq(rows, w):
            return (depth, bsz, rows, w), (None, None, rows, w), lambda b, c, l: (l[0], b, 0, 0)
        def matrix(h, rows, w):
            return (depth, bsz, h, rows, w), (None, None, h, rows, w), lambda b, c, l: (l[0], b, 0, 0, 0)
        return dict(lconv=per_seq(CONV_K - 1, gw), lh=per_seq(1, gw), s5r=per_seq(1, sn), s5i=per_seq(1, sn),
                    sconv=per_seq(CONV_K - 1, 2 * gw), sh=matrix(gw // SSD_HD, SSD_HD, SSD_N),
                    ret=matrix(RET_H, RET_DK, dv))

    def flat(per, w):
        return (depth, bsz * per, w), (None, ns * per, w), lambda b, c, l: (l[0], b, 0)

    def conv(w):
        return (depth, CONV_K - 1, bsz, w), (None, CONV_K - 1, ns, w), lambda b, c, l: (l[0], 0, b, 0)
    def matrix(h, rows, w):
        return (depth, bsz, h, rows, w), (None, ns, h, rows, w), lambda b, c, l: (l[0], b, 0, 0, 0)
    return dict(lconv=conv(gw), lh=flat(1, gw), s5r=flat(1, sn), s5i=flat(1, sn), sconv=conv(2 * gw),
                sh=matrix(gw // SSD_HD, SSD_HD, SSD_N), ret=matrix(RET_H, RET_DK, dv))


def _mixer(l, proj, consts, mw, state_in, state_out, *, nb, nc, qt, ns, r, d):
    dp = proj.shape[-1]
    gw = d // 4
    sn = mw["pw"].shape[-1]
    depth = mw["pw"].shape[0]
    layout = _state_layout(depth, nb * ns, ns, gw, sn)
    assert (state_in is None) == (ns == 1)

    def layer_spec(a):
        return pl.BlockSpec((None,) + a.shape[1:], lambda b, c, l: (l[0],) + (0,) * (a.ndim - 1))

    def const_spec(a):
        return pl.BlockSpec(a.shape, lambda b, c, l: (0,) * a.ndim)

    def state_spec(nm):
        _, block, index = layout[nm]
        return pl.BlockSpec(block, index)

    if ns == 1:
        io_spec = lambda w: pl.BlockSpec((qt, w), lambda b, c, l: (b * nc + c, 0))
        y_shape = (nb * nc * qt, d)
        cnames = ("trig", "tri", "dmask", "dec", "perm")
    else:
        proj = proj.reshape(r, nb * ns, dp)
        io_spec = lambda w: pl.BlockSpec((r, ns, w), lambda b, c, l: (0, b, 0))
        y_shape = (r, nb * ns, d)
        cnames = ("trig", "tri", "dmask", "dec")
    names = ("proj",) + cnames + MIX_WEIGHTS
    operands = [proj] + [consts[k] for k in cnames] + [mw[k] for k in MIX_WEIGHTS]
    in_specs = [io_spec(dp), pl.BlockSpec((3, qt, LANES), lambda b, c, l: (0, c, 0))] + \
               [const_spec(consts[k]) for k in cnames[1:]] + [layer_spec(mw[k]) for k in MIX_WEIGHTS]
    first_alias = 1 + len(operands)
    if state_in is not None:
        assert state_in is state_out
        names += tuple(nm + "_i" for nm in STATE_NAMES)
        in_specs += [state_spec(nm) for nm in STATE_NAMES]
    else:
        names += tuple(nm + "_alias" for nm in STATE_NAMES)
        in_specs += [pl.BlockSpec(memory_space=pl.ANY) for _ in STATE_NAMES]
    operands += [state_out[nm] for nm in STATE_NAMES]
    names += ("y",) + tuple(nm + "_o" for nm in STATE_NAMES)
    out_specs = [io_spec(d)] + [state_spec(nm) for nm in STATE_NAMES]
    out_shape = [jax.ShapeDtypeStruct(y_shape, MXU_DTYPE)] + \
                [jax.ShapeDtypeStruct(layout[nm][0], F32) for nm in STATE_NAMES]
    scratch = []
    if ns == 1:
        tail = (CONV_K - 1) * SUBLANES
        scratch_shapes = dict(cb_a=(tail, gw), cb_c=(tail, 2 * gw), c_lh=(1, gw), c_s5r=(1, sn),
                              c_s5i=(1, sn), c_sh=(SSD_G * SSD_N, gw), c_ret=(RET_H * RET_DK, gw))
        names += tuple(scratch_shapes)
        scratch = [pltpu.VMEM(shape, F32) for shape in scratch_shapes.values()]
    grid_spec = pltpu.PrefetchScalarGridSpec(num_scalar_prefetch=1, grid=(nb, nc),
                                             in_specs=in_specs, out_specs=out_specs, scratch_shapes=scratch)
    outs = pl.pallas_call(
        functools.partial(_mix_body, names=names, qt=qt, ns=ns, r=r, nc=nc),
        grid_spec=grid_spec,
        out_shape=out_shape,
        input_output_aliases={first_alias + k: 1 + k for k in range(len(STATE_NAMES))},
        compiler_params=pltpu.CompilerParams(dimension_semantics=("arbitrary", "arbitrary"),
                                             vmem_limit_bytes=VMEM_LIMIT),
        name="mixer",
    )(l, *operands)
    return outs[0].reshape(-1, d), dict(zip(STATE_NAMES, outs[1:]))


def _post_body(l_ref, y_ref, x_ref, g1_ref, sc_ref, sh_ref, g2_ref, lt_ref, wo_ref, wr_ref, br_ref,
               wg_ref, wu_ref, wd_ref, *rest):
    o_ref, xs_ref, cs_ref, os_ref = rest[-4:]
    del l_ref
    tm, d = x_ref.shape
    tms = xs_ref.shape[0]
    x = x_ref[...] + _token_rows(g1_ref[...], tm) * jnp.dot(y_ref[...], wo_ref[...], preferred_element_type=F32)
    h = _rms(x) * (1.0 + _token_rows(sc_ref[...], tm)) + _token_rows(sh_ref[...], tm)
    hb = h.astype(MXU_DTYPE)
    h_lo = (h - hb.astype(F32)).astype(MXU_DTYPE)
    wr = wr_ref[...]
    l12 = jnp.dot(hb, wr, preferred_element_type=F32)
    logits = (l12[:, 0:LANES] + l12[:, LANES:2 * LANES]
              + jnp.dot(h_lo, wr[:, 0:LANES], preferred_element_type=F32) + br_ref[...])
    logits_t = jnp.transpose(logits)
    col = [logits_t[k:k + 1, :] for k in range(N_GROUPS + N_EXPERTS)]

    def first_max(vals, allowed=None):
        neg = jnp.full_like(vals[0], -jnp.inf)
        cand = vals if allowed is None else [jnp.where(al > 0.0, v, neg) for v, al in zip(vals, allowed)]
        m = functools.reduce(jnp.maximum, cand)
        rem = jnp.ones_like(vals[0])
        hot = []
        for v in cand:
            f = jnp.where(v >= m, rem, 0.0)
            rem = rem - f
            hot.append(f)
        return hot, m

    grp, gmax = first_max(col[:N_GROUPS])
    gate = 1.0 / functools.reduce(lambda s, v: s + v, [jnp.exp(v - gmax) for v in col[:N_GROUPS]])
    le = [functools.reduce(lambda s, v: s + v,
                           [grp[gi] * col[N_GROUPS + gi * N_PER_GROUP + j] for gi in range(N_GROUPS)])
          for j in range(N_PER_GROUP)]
    top1, m1 = first_max(le)
    top2, m2 = first_max(le, [1.0 - f for f in top1])
    e2 = jnp.exp(m2 - m1)
    w1 = 1.0 / (1.0 + e2)
    w2 = e2 * w1
    wgrp = [(top1[j] * w1 + top2[j] * w2) * gate for j in range(N_PER_GROUP)]

    sub8 = lax.broadcasted_iota(jnp.int32, (SUBLANES, 1), 0)
    ghot_t = functools.reduce(lambda s, v: s + v, [jnp.where(sub8 == gi, grp[gi], 0.0) for gi in range(N_GROUPS)])
    before = _mm_nt(ghot_t, lt_ref[...])
    count = jnp.sum(ghot_t, axis=1, keepdims=True)
    cnt = [count[gi:gi + 1, :] for gi in range(N_GROUPS)]
    base = [jnp.zeros((1, 1), F32)]
    for gi in range(1, N_GROUPS):
        base.append(jnp.floor((base[-1] + cnt[gi - 1] + (PACK - 1.0)) * (1.0 / PACK)) * PACK)
    pos_row = functools.reduce(lambda s, v: s + v,
                               [grp[gi] * (base[gi] + before[gi:gi + 1, :]) for gi in range(N_GROUPS)])
    pos = jnp.transpose(jnp.broadcast_to(pos_row, (LANES, tm)))[:, 0:1]
    slot = lax.broadcasted_iota(jnp.int32, (tms, tm), 0).astype(F32)
    gather = jnp.where(slot == pos_row, 1.0, 0.0).astype(MXU_DTYPE)
    slot_l = lax.broadcasted_iota(jnp.int32, (tm, tms), 1).astype(F32)
    scatter = jnp.where(slot_l == pos, 1.0, 0.0).astype(MXU_DTYPE)

    xs_ref[...] = jnp.dot(gather, hb, preferred_element_type=F32).astype(xs_ref.dtype)
    sub_e = lax.broadcasted_iota(jnp.int32, (cs_ref.shape[1], 1), 0)
    comb_t = functools.reduce(lambda s, v: s + v,
                              [jnp.where(sub_e == e, grp[e // N_PER_GROUP] * wgrp[e % N_PER_GROUP], 0.0)
                               for e in range(N_EXPERTS)])
    c1, c2, c3 = _split3(comb_t)
    nt = lambda p: lax.dot_general(gather, p, (((1,), (1,)), ((), ())), preferred_element_type=F32)
    cs_ref[...] = nt(c1) + nt(c2) + nt(c3)
    os_ref[...] = jnp.zeros(os_ref.shape, os_ref.dtype)
    for gi in range(N_GROUPS):
        start = base[gi][0, 0].astype(jnp.int32)
        nblk = jnp.floor((cnt[gi][0, 0] + (EXPERT_ROWS - 1.0)) * (1.0 / EXPERT_ROWS)).astype(jnp.int32)

        def block(k, carry, gi=gi, start=start):
            rows = pl.ds(pl.multiple_of(start + k * EXPERT_ROWS, PACK), EXPERT_ROWS)
            xb = xs_ref[rows, :]
            cb = cs_ref[rows, :]
            acc = jnp.zeros((EXPERT_ROWS, d), F32)
            for j in range(N_PER_GROUP):
                e = gi * N_PER_GROUP + j
                gt = jnp.dot(xb, wg_ref[e], preferred_element_type=F32)
                he = gt * _sigmoid(gt) * jnp.dot(xb, wu_ref[e], preferred_element_type=F32)
                acc = acc + jnp.dot((he * cb[:, e:e + 1]).astype(MXU_DTYPE), wd_ref[e], preferred_element_type=F32)
            os_ref[rows, :] += acc
            return carry

        lax.fori_loop(0, nblk, block, 0)
    moe = jnp.dot(scatter, os_ref[...].astype(MXU_DTYPE), preferred_element_type=F32)
    out = x + _token_rows(g2_ref[...], tm) * moe
    if len(rest) == 5:
        out = _rms(out) * rest[0][...]
    o_ref[...] = out


def _post(l, y, x, mod, pw, tm, rows_per_seq, final_w=None, in_place=False):
    t, d = x.shape
    final = [] if final_w is None else [final_w.reshape(1, d)]
    tms = tm + EXPERT_ROWS + PACK * N_GROUPS
    ltri = jnp.asarray(np.tril(np.ones((tm, tm), np.float32), -1)).astype(MXU_DTYPE)

    def layer_spec(a):
        return pl.BlockSpec((None,) + a.shape[1:], lambda i, l: (l[0],) + (0,) * (a.ndim - 1),
                            pipeline_mode=pl.Buffered(1))

    grid_spec = pltpu.PrefetchScalarGridSpec(
        num_scalar_prefetch=1,
        grid=(t // tm,),
        in_specs=[
            pl.BlockSpec((tm, d), lambda i, l: (i, 0)),
            pl.BlockSpec((tm, d), lambda i, l: (i, 0)),
            _mod_spec(mod, 2, tm, rows_per_seq),
            _mod_spec(mod, 4, tm, rows_per_seq),
            _mod_spec(mod, 3, tm, rows_per_seq),
            _mod_spec(mod, 5, tm, rows_per_seq),
            pl.BlockSpec((tm, tm), lambda i, l: (0, 0)),
        ] + [layer_spec(pw[k]) for k in ("w_out", "w_route", "b_route", "w_gate", "w_up", "w_down")]
          + [pl.BlockSpec((1, d), lambda i, l: (0, 0)) for _ in final],
        out_specs=pl.BlockSpec((tm, d), lambda i, l: (i, 0)),
        scratch_shapes=[pltpu.VMEM((tms, d), MXU_DTYPE), pltpu.VMEM((tms, 2 * N_EXPERTS), F32),
                        pltpu.VMEM((tms, d), F32)],
    )
    return pl.pallas_call(
        _post_body,
        grid_spec=grid_spec,
        out_shape=jax.ShapeDtypeStruct((t, d), F32),
        input_output_aliases={2: 0} if in_place else {},
        compiler_params=pltpu.CompilerParams(dimension_semantics=("arbitrary",), vmem_limit_bytes=VMEM_LIMIT),
        name="post",
    )(l, y, x, mod, mod, mod, mod, ltri,
      *[pw[k] for k in ("w_out", "w_route", "b_route", "w_gate", "w_up", "w_down")], *final)


def _block_diag(w):
    depth, nblk, bi, bj = w.shape
    eye = jnp.eye(nblk, dtype=w.dtype)
    return (w[:, :, :, None, :] * eye[None, :, None, :, None]).reshape(depth, nblk * bi, nblk * bj)


def _pad_lanes(a, width):
    return jnp.pad(a, [(0, 0)] * (a.ndim - 1) + [(0, width - a.shape[-1])])


def _prep_weights(w_in, lru_conv_w, lru_conv_b, lru_wa, lru_ba, lru_wx, lru_bx, lru_lambda,
                  s5_a_re, s5_a_im, s5_b_re, s5_b_im, s5_c_re, s5_c_im, s5_d, s5_log_dt, s5_w_glu,
                  ssd_conv_w, ssd_conv_b, ssd_dt_bias, ssd_a_log, ssd_d, mix_norm, qt):
    depth, d, _ = w_in.shape
    gw = d // 4
    n_ssd_h = gw // SSD_HD
    xbc_end = 4 * gw + gw + 2 * SSD_G * SSD_N
    w_in = w_in.astype(MXU_DTYPE)
    dt_cols = jnp.repeat(w_in[..., xbc_end:xbc_end + n_ssd_h], SSD_HD, axis=-1)
    w_in_p = jnp.concatenate([w_in[..., :xbc_end], w_in[..., xbc_end + n_ssd_h:], dt_cols], axis=-1)

    def row(a):
        return _pad_lanes(a.reshape(depth, 1, -1), d)

    vec = jnp.concatenate([
        row(lru_conv_b), row(jnp.concatenate([lru_ba, lru_bx], -1)), row(lru_lambda), row(s5_d),
        row(ssd_conv_b), row(jnp.repeat(ssd_dt_bias, SSD_HD, -1)), row(jnp.repeat(ssd_a_log, SSD_HD, -1)),
        row(jnp.repeat(ssd_d, SSD_HD, -1)), row(mix_norm),
        _pad_lanes(lru_conv_w, d), _pad_lanes(ssd_conv_w, d),
        jnp.zeros((depth, V_ROWS - V_SCW - CONV_K, d), F32)], axis=1)

    wgate = jnp.concatenate([_block_diag(lru_wa), _block_diag(lru_wx)], axis=-1)

    dt = jnp.exp(s5_log_dt)[..., None]
    lr, li = s5_a_re, s5_a_im
    mag = jnp.exp(lr * dt)
    ab_re, ab_im = mag * jnp.cos(li * dt), mag * jnp.sin(li * dt)
    den = lr * lr + li * li
    q_re = ((ab_re - 1.0) * lr + ab_im * li) / den
    q_im = (ab_im * lr - (ab_re - 1.0) * li) / den
    bb_re = q_re[..., None] * s5_b_re - q_im[..., None] * s5_b_im
    bb_im = q_re[..., None] * s5_b_im + q_im[..., None] * s5_b_re
    bbm = jnp.concatenate([_block_diag(jnp.swapaxes(bb_re, -1, -2)), _block_diag(jnp.swapaxes(bb_im, -1, -2))], -1)
    cre = _block_diag(jnp.swapaxes(s5_c_re, -1, -2))
    cim = _block_diag(jnp.swapaxes(s5_c_im, -1, -2))
    a_re, a_im = ab_re.reshape(depth, -1), ab_im.reshape(depth, -1)
    pr, pi = a_re, a_im
    rows = []
    for _ in range(int(math.log2(qt))):
        rows += [pr, pi]
        pr, pi = pr * pr - pi * pi, 2.0 * pr * pi
    pw = jnp.stack(rows, axis=1)
    pw = jnp.pad(pw, ((0, 0), (0, (-pw.shape[1]) % SUBLANES), (0, 0)))
    pr, pi = a_re, a_im
    k_re, k_im = [], []
    for _ in range(qt // SUBLANES):
        k_re.append(pr)
        k_im.append(pi)
        pr, pi = pr * a_re - pi * a_im, pr * a_im + pi * a_re
    pk = jnp.stack(k_re + k_im, axis=1)

    mw = dict(vec=vec, wgate=wgate.astype(MXU_DTYPE), bbm=bbm.astype(MXU_DTYPE), cre=cre.astype(MXU_DTYPE),
              cim=cim.astype(MXU_DTYPE), wglu=s5_w_glu.astype(MXU_DTYPE), pw=pw, pk=pk)
    return w_in_p, mw


def _states_to_layout(states, layout, ns):
    out = {}
    for nm, s in zip(STATE_NAMES, states):
        if nm in ("lconv", "sconv"):
            s = s.transpose(0, 2, 1, 3)
        out[nm] = s.reshape(layout[nm][0])
    return out


def _states_from_layout(arrs, ns, ref_shapes):
    out = []
    for nm, shape in zip(STATE_NAMES, ref_shapes):
        a = arrs[nm]
        if ns > 1 and nm in ("lconv", "sconv"):
            a = a.transpose(0, 2, 1, 3)
        out.append(a.reshape(shape))
    return tuple(out)


def _trunk(x, mod, states, ref_shapes, pos0, w_in_p, mw, pw, final_norm, *, qt, ns, tm, tm_in):
    bsz, seq_len, d = x.shape
    depth = w_in_p.shape[0]
    gw = d // 4
    r = qt // ns
    nc = seq_len // r if ns == 1 else 1
    assert (ns == 1 and seq_len % qt == 0 and states is None) or (ns > 1 and r == seq_len and bsz % ns == 0)
    assert r >= CONV_K and (r & (r - 1)) == 0
    nb = bsz // ns
    tok = bsz * seq_len
    consts = _mix_consts(qt, ns, r, nc, pos0)
    layout = _state_layout(depth, bsz, ns, gw, mw["pw"].shape[-1])
    if states is None:
        state_out = {nm: jnp.zeros(layout[nm][0], F32) for nm in STATE_NAMES}
    else:
        state_out = _states_to_layout(states, layout, ns)
    x2 = x.reshape(tok, d) if ns == 1 else x.transpose(1, 0, 2).reshape(tok, d)
    blocked_chunk = qt if ns == 1 else None

    def layer(carry, l, final_w=None, in_place=True):
        xc, st = carry
        lv = jnp.reshape(l, (1,)).astype(jnp.int32)
        proj = _inproj(lv, xc, mod, w_in_p, tm_in, seq_len, blocked_chunk)
        y, st = _mixer(lv, proj, consts, mw, None if states is None else st, st,
                       nb=nb, nc=nc, qt=qt, ns=ns, r=r, d=d)
        xn = _post(lv, y, xc, mod, pw, tm, seq_len, final_w, in_place)
        return (xn, st), None

    assert depth >= 2
    carry, _ = layer((x2, state_out), jnp.int32(0), in_place=False)
    carry, _ = lax.scan(layer, carry, jnp.arange(1, depth - 1))
    (y, st), _ = layer(carry, jnp.int32(depth - 1), final_norm)
    y = y.reshape(bsz, seq_len, d) if ns == 1 else y.reshape(seq_len, bsz, d).transpose(1, 0, 2)
    return y, _states_from_layout(st, ns, ref_shapes)


def kernel(x_prompt, x_sample, state_lru_conv, state_lru_h, state_s5_re, state_s5_im, state_ssd_conv, state_ssd_h, state_ret, c_prompt, c_sample, w_ada, b_ada, w_in, lru_conv_w, lru_conv_b, lru_wa, lru_ba, lru_wx, lru_bx, lru_lambda, s5_a_re, s5_a_im, s5_b_re, s5_b_im, s5_c_re, s5_c_im, s5_d, s5_log_dt, s5_w_glu, ssd_conv_w, ssd_conv_b, ssd_dt_bias, ssd_a_log, ssd_d, mix_norm, w_out, w_route_group, b_route_group, w_route_exp, b_route_exp, w_exp_gate, w_exp_up, w_exp_down, final_norm):
    bp, lp, d = x_prompt.shape
    bs, ls, _ = x_sample.shape
    depth = w_in.shape[0]
    qt = LANES
    ns_s = qt // ls

    mod_p, mod_s = _modulation(jnp.concatenate([c_prompt, c_sample], axis=0), bp, w_ada, b_ada)
    mod_p = mod_p.reshape(depth, N_MOD, bp, 1, d)

    w_in_p, mw = _prep_weights(w_in, lru_conv_w, lru_conv_b, lru_wa, lru_ba, lru_wx, lru_bx, lru_lambda,
                               s5_a_re, s5_a_im, s5_b_re, s5_b_im, s5_c_re, s5_c_im, s5_d, s5_log_dt, s5_w_glu,
                               ssd_conv_w, ssd_conv_b, ssd_dt_bias, ssd_a_log, ssd_d, mix_norm, qt)
    w_route = _pad_lanes(jnp.concatenate([w_route_group, w_route_exp.reshape(depth, d, N_EXPERTS)], -1), LANES)
    b_route = _pad_lanes(jnp.concatenate([b_route_group, b_route_exp.reshape(depth, N_EXPERTS)], -1), LANES)
    w_route_hi = w_route.astype(MXU_DTYPE)
    w_route_lo = (w_route - w_route_hi.astype(F32)).astype(MXU_DTYPE)
    w_route = jnp.concatenate([w_route_hi, w_route_lo], axis=-1)
    pw = dict(w_out=w_out.astype(MXU_DTYPE), w_route=w_route, b_route=b_route.reshape(depth, 1, LANES),
              w_gate=w_exp_gate.astype(MXU_DTYPE), w_up=w_exp_up.astype(MXU_DTYPE),
              w_down=w_exp_down.astype(MXU_DTYPE))
    w_in_p, mw, pw = lax.optimization_barrier((w_in_p, mw, pw))

    states_s = (state_lru_conv, state_lru_h, state_s5_re, state_s5_im, state_ssd_conv, state_ssd_h, state_ret)
    shapes_s = [s.shape for s in states_s]
    shapes_p = [(depth, bp) + s[2:] for s in shapes_s]
    y_p, new_p = _trunk(x_prompt, mod_p, None, shapes_p, 0, w_in_p, mw, pw, final_norm,
                        qt=qt, ns=1, tm=min(512, lp), tm_in=min(1024, lp))
    y_s, new_s = _trunk(x_sample, mod_s, states_s, shapes_s, PAST_LEN, w_in_p, mw, pw, final_norm,
                        qt=qt, ns=ns_s, tm=bs * ls, tm_in=bs * ls)
    out = [y_p, y_s]
    for a, b in zip(new_p, new_s):
        out += [a, b]
    return tuple(out)
```

```python
import functools
import math

import numpy as np
import jax
import jax.numpy as jnp
from jax import lax
from jax.experimental import pallas as pl
from jax.experimental.pallas import tpu as pltpu

F32 = jnp.float32
MXU_DTYPE = jnp.bfloat16

EPS = 1e-6
CONV_K = 4
LRU_C = 8.0
SSD_HD = 64
SSD_G = 2
SSD_N = 64
RET_H = 4
RET_DK = 32
ROPE_BASE = 10000.0
N_GROUPS = 4
N_PER_GROUP = 4
N_EXPERTS = N_GROUPS * N_PER_GROUP
N_MOD = 6
PAST_LEN = 16384

LANES = 128
SUBLANES = 8
PACK = 16
VMEM_LIMIT = 56 * 1024 * 1024
EXPERT_ROWS = 192


def _mm(a, b):
    return jnp.dot(a.astype(MXU_DTYPE), b.astype(MXU_DTYPE), preferred_element_type=F32)


def _mm_nt(a, b):
    return lax.dot_general(a.astype(MXU_DTYPE), b.astype(MXU_DTYPE), (((1,), (1,)), ((), ())),
                           preferred_element_type=F32)


def _mm_tn(a, b):
    return lax.dot_general(a.astype(MXU_DTYPE), b.astype(MXU_DTYPE), (((0,), (0,)), ((), ())),
                           preferred_element_type=F32)


def _split3(x):
    x1 = x.astype(MXU_DTYPE)
    r1 = x - x1.astype(F32)
    x2 = r1.astype(MXU_DTYPE)
    x3 = (r1 - x2.astype(F32)).astype(MXU_DTYPE)
    return x1, x2, x3


def _sigmoid(x):
    return 0.5 * jnp.tanh(0.5 * x) + 0.5


def _rms(x):
    return x * lax.rsqrt(jnp.mean(x * x, axis=-1, keepdims=True) + EPS)


def _token_rows(v, tm):
    n = v.shape[0]
    return v if n in (1, tm) else jnp.tile(v, (tm // n, 1))


def _mod_body(c_ref, w_ref, b_ref, first_ref, rest_ref):
    c = c_ref[...]
    m = _mm(c * _sigmoid(c), w_ref[...]) + b_ref[...]
    n_first = first_ref.shape[0]
    first_ref[...] = m[0:n_first]
    rest_ref[...] = m[n_first:]


def _modulation(c_all, n_first, w_ada, b_ada):
    depth, d, _ = w_ada.shape
    nb = c_all.shape[0]
    return pl.pallas_call(
        _mod_body,
        grid=(depth, N_MOD),
        in_specs=[
            pl.BlockSpec((nb, d), lambda l, k: (0, 0)),
            pl.BlockSpec((None, d, d), lambda l, k: (l, 0, k)),
            pl.BlockSpec((None, None, 1, d), lambda l, k: (l, k, 0, 0)),
        ],
        out_specs=[pl.BlockSpec((None, None, n_first, d), lambda l, k: (l, k, 0, 0)),
                   pl.BlockSpec((None, None, nb - n_first, d), lambda l, k: (l, k, 0, 0))],
        out_shape=[jax.ShapeDtypeStruct((depth, N_MOD, n_first, d), F32),
                   jax.ShapeDtypeStruct((depth, N_MOD, nb - n_first, d), F32)],
        compiler_params=pltpu.CompilerParams(dimension_semantics=("arbitrary", "arbitrary"),
                                             vmem_limit_bytes=VMEM_LIMIT),
        name="modulation",
    )(c_all, w_ada, b_ada.reshape(depth, N_MOD, 1, d))


def _mod_spec(mod, k, tm, rows_per_seq):
    if mod.ndim == 5:
        tiles_per_seq = rows_per_seq // tm
        return pl.BlockSpec((None, None, None, 1, mod.shape[-1]),
                            lambda i, l: (l[0], k, i // tiles_per_seq, 0, 0))
    return pl.BlockSpec((None, None) + mod.shape[2:], lambda i, l: (l[0], k, 0, 0))


def _inproj_body(l_ref, x_ref, sc_ref, sh_ref, w_ref, o_ref, *, blocked_chunk):
    tm = x_ref.shape[0]
    h = _rms(x_ref[...]) * (1.0 + _token_rows(sc_ref[...], tm)) + _token_rows(sh_ref[...], tm)
    hb = h.astype(MXU_DTYPE)
    if blocked_chunk is not None:
        qt, nk = blocked_chunk, blocked_chunk // SUBLANES
        r_i = lax.broadcasted_iota(jnp.int32, (qt, qt), 0)
        c_i = lax.broadcasted_iota(jnp.int32, (qt, qt), 1)
        to_blocked = jnp.where(c_i == (r_i % SUBLANES) * nk + r_i // SUBLANES, 1.0, 0.0).astype(MXU_DTYPE)
        hb = jnp.concatenate([jnp.dot(to_blocked, hb[j * qt:(j + 1) * qt], preferred_element_type=F32)
                              for j in range(tm // qt)], axis=0).astype(MXU_DTYPE)
    o_ref[...] = jnp.dot(hb, w_ref[...], preferred_element_type=F32)


def _inproj(l, x, mod, w_in, tm, rows_per_seq, blocked_chunk=None):
    t, d = x.shape
    dp = w_in.shape[-1]
    assert blocked_chunk is None or (mod.ndim == 5 and tm % blocked_chunk == 0)
    grid_spec = pltpu.PrefetchScalarGridSpec(
        num_scalar_prefetch=1,
        grid=(t // tm,),
        in_specs=[
            pl.BlockSpec((tm, d), lambda i, l: (i, 0)),
            _mod_spec(mod, 1, tm, rows_per_seq),
            _mod_spec(mod, 0, tm, rows_per_seq),
            pl.BlockSpec((None, d, dp), lambda i, l: (l[0], 0, 0)),
        ],
        out_specs=pl.BlockSpec((tm, dp), lambda i, l: (i, 0)),
    )
    return pl.pallas_call(
        functools.partial(_inproj_body, blocked_chunk=blocked_chunk),
        grid_spec=grid_spec,
        out_shape=jax.ShapeDtypeStruct((t, dp), F32),
        compiler_params=pltpu.CompilerParams(dimension_semantics=("arbitrary",), vmem_limit_bytes=VMEM_LIMIT),
        name="inproj",
    )(l, x, mod, mod, w_in)


V_LCB, V_BGATE, V_LAM, V_S5D, V_SCB, V_DTB, V_ALOG, V_SSDD, V_GAIN, V_LCW, V_SCW = 0, 1, 2, 3, 4, 5, 6, 7, 8, 9, 13
V_ROWS = 24
STATE_NAMES = ("lconv", "lh", "s5r", "s5i", "sconv", "sh", "ret")
MIX_WEIGHTS = ("vec", "wgate", "bbm", "cre", "cim", "wglu", "pw", "pk")


def _mix_body(l_ref, *refs, names, qt, ns, r, nc):
    del l_ref
    g = dict(zip(names, refs))
    gw = g["lh_o"].shape[-1]
    sn = g["s5r_o"].shape[-1]
    nlev = int(math.log2(r))
    nk = qt // SUBLANES
    c = pl.program_id(1)

    if ns == 1:
        @pl.when(c == 0)
        def _zero_carries():
            for nm in ("c_lh", "c_s5r", "c_s5i", "c_sh", "c_ret"):
                g[nm][...] = jnp.zeros(g[nm].shape, F32)
            for nm in ("cb_a", "cb_c"):
                g[nm][...] = jnp.zeros(g[nm].shape, F32)

    proj = g["proj"]
    vec = g["vec"]
    pw = g["pw"]

    def pcols(a, b):
        v = proj[..., a:b]
        return v if ns == 1 else v.reshape(qt, b - a)

    y_parts = []

    def ysave(a, b, val):
        val = val.astype(g["y"].dtype)
        if ns == 1:
            y_parts.append(val)
        else:
            g["y"][..., a:b] = val.reshape(r, ns, b - a)

    row = lax.broadcasted_iota(jnp.int32, (qt, 1), 0)
    t = (row % SUBLANES) * nk + row // SUBLANES if ns == 1 else row // ns
    sub = lax.broadcasted_iota(jnp.int32, (SUBLANES, 1), 0)

    def down(x, d, fill=0.0):
        return jnp.where(t >= d, pltpu.roll(x, d * ns, 0), fill)

    def up(x, d):
        return jnp.where(t + d < r, pltpu.roll(x, qt - d * ns, 0), 0.0)

    def sdown(x, d, fill=0.0):
        return jnp.where(sub >= d, pltpu.roll(x, d, 0), fill)

    def slabs(x):
        return [x[SUBLANES * k:SUBLANES * (k + 1)] for k in range(nk)]

    def first_rows(carry, init):
        if ns == 1:
            return jnp.where(row == 0, g[carry][...], 0.0)
        h0 = g[init][...]
        return jnp.concatenate([h0, jnp.zeros((qt - ns, h0.shape[1]), F32)], axis=0)

    def conv(xraw, nm, w0, b):
        width = xraw.shape[-1]
        acc = vec[b:b + 1, 0:width] + vec[w0 + 3:w0 + 4, 0:width] * xraw
        if ns == 1:
            cb = g["cb_" + nm]
            xs_k = slabs(xraw)
            wrap = [jnp.where(sub >= 1, pltpu.roll(xs_k[nk - j], 1, 0),
                              pltpu.roll(cb[(CONV_K - 1 - j) * SUBLANES:(CONV_K - j) * SUBLANES, :], 1, 0))
                    for j in range(1, CONV_K)]
            for m in range(1, CONV_K):
                sh = jnp.concatenate([wrap[m - k - 1] for k in range(m)] + xs_k[0:nk - m], axis=0)
                acc = acc + vec[w0 + 3 - m:w0 + 4 - m, 0:width] * sh
            cb[...] = xraw[qt - (CONV_K - 1) * SUBLANES:, :]
            g[{"a": "lconv_o", "c": "sconv_o"}[nm]][...] = jnp.concatenate(
                [xs_k[nk - j][SUBLANES - 1:SUBLANES, :] for j in range(CONV_K - 1, 0, -1)], axis=0)
            return acc
        o = g[{"a": "lconv_o", "c": "sconv_o"}[nm]]
        buf = g[{"a": "lconv_i", "c": "sconv_i"}[nm]][...].reshape((CONV_K - 1) * ns, width)
        buf = jnp.concatenate([buf, jnp.zeros((qt - (CONV_K - 1) * ns, width), F32)], axis=0)
        for m in range(1, CONV_K):
            prev = buf if m == 3 else pltpu.roll(buf, qt - (3 - m) * ns, 0)
            sh = jnp.where(t >= m, pltpu.roll(xraw, m * ns, 0), prev)
            acc = acc + vec[w0 + 3 - m:w0 + 4 - m, 0:width] * sh
        o[...] = xraw[(r - (CONV_K - 1)) * ns:, :].reshape(CONV_K - 1, ns, width)
        return acc

    def save_last(nm, h):
        if ns == 1:
            g["c_" + nm][...] = h[qt - 1:qt, :]
            g[nm + "_o"][...] = h[qt - 1:qt, :]
        else:
            g[nm + "_o"][...] = h[qt - ns:, :]

    def seq_mask(width):
        lane_seq = lax.broadcasted_iota(jnp.int32, (qt, ns * width), 1) // width
        row_seq = lax.broadcasted_iota(jnp.int32, (qt, ns * width), 0) % ns
        return (lane_seq == row_seq).astype(F32)

    gain = vec[V_GAIN:V_GAIN + 1, :]

    xc = conv(pcols(0, gw), "a", V_LCW, V_LCB)
    ga = pcols(gw, 2 * gw)
    pre = _mm(xc, g["wgate"][...]) + vec[V_BGATE:V_BGATE + 1, 0:2 * gw]
    rg = _sigmoid(pre[:, 0:gw])
    ig = _sigmoid(pre[:, gw:2 * gw])
    log_a = -LRU_C * rg * jax.nn.softplus(-vec[V_LAM:V_LAM + 1, 0:gw])
    a = jnp.exp(log_a)
    b = jnp.sqrt(-jnp.tanh(log_a) * (a * a + 1.0)) * (ig * xc)
    b = b + a * first_rows("c_lh", "lh_i")
    if ns == 1:
        a_s, b_s = slabs(a), slabs(b)
        hs, ps = [b_s[0]], [a_s[0]]
        for k in range(1, nk):
            hs.append(a_s[k] * hs[-1] + b_s[k])
            ps.append(a_s[k] * ps[-1])
        e, ae = hs[-1], ps[-1]
        for j in range(int(math.log2(SUBLANES))):
            d = 1 << j
            e = ae * sdown(e, d) + e
            ae = ae * sdown(ae, d, 1.0)
        cin = sdown(e, 1)
        b = jnp.concatenate([hs[k] + ps[k] * cin for k in range(nk)], axis=0)
    else:
        for k in range(nlev):
            d = 1 << k
            b = a * down(b, d) + b
            a = a * down(a, d, 1.0)
    save_last("lh", b)
    ya = _rms(b * jax.nn.gelu(ga)) * gain[:, 0:gw]

    u = pcols(2 * gw, 3 * gw)
    bu = _mm(u, g["bbm"][...])
    p_re, p_im = pw[0:1, :], pw[1:2, :]
    h0r, h0i = first_rows("c_s5r", "s5r_i"), first_rows("c_s5i", "s5i_i")
    hr = bu[:, 0:sn] + (p_re * h0r - p_im * h0i)
    hi = bu[:, sn:2 * sn] + (p_re * h0i + p_im * h0r)
    if ns == 1:
        pk = g["pk"]
        br_s, bi_s = slabs(hr), slabs(hi)
        hrs, his = [br_s[0]], [bi_s[0]]
        for k in range(1, nk):
            hrs.append(br_s[k] + (p_re * hrs[-1] - p_im * his[-1]))
            his.append(bi_s[k] + (p_re * his[-1] + p_im * hrs[-2]))
        er, ei = hrs[-1], his[-1]
        lev0 = int(math.log2(nk))
        for j in range(int(math.log2(SUBLANES))):
            d = 1 << j
            q_re, q_im = pw[2 * (lev0 + j):2 * (lev0 + j) + 1, :], pw[2 * (lev0 + j) + 1:2 * (lev0 + j) + 2, :]
            sr, si = sdown(er, d), sdown(ei, d)
            er, ei = er + (q_re * sr - q_im * si), ei + (q_re * si + q_im * sr)
        cr, ci = sdown(er, 1), sdown(ei, 1)
        hr = jnp.concatenate([hrs[k] + (pk[k:k + 1, :] * cr - pk[nk + k:nk + k + 1, :] * ci) for k in range(nk)], 0)
        hi = jnp.concatenate([his[k] + (pk[k:k + 1, :] * ci + pk[nk + k:nk + k + 1, :] * cr) for k in range(nk)], 0)
    else:
        for k in range(nlev):
            d = 1 << k
            q_re, q_im = pw[2 * k:2 * k + 1, :], pw[2 * k + 1:2 * k + 2, :]
            sr, si = down(hr, d), down(hi, d)
            hr, hi = hr + (q_re * sr - q_im * si), hi + (q_re * si + q_im * sr)
    save_last("s5r", hr)
    save_last("s5i", hi)
    yb = _mm(hr, g["cre"][...]) - _mm(hi, g["cim"][...])
    yb = jax.nn.gelu(yb + vec[V_S5D:V_S5D + 1, 0:gw] * u)
    yb = yb * _sigmoid(_mm(yb, g["wglu"][...]))
    yb = _rms(yb) * gain[:, gw:2 * gw]
    ysave(0, 2 * gw, jnp.concatenate([ya, yb], axis=1))

    tri = g["tri"][...] > 0.0
    xbc = conv(pcols(4 * gw, 6 * gw), "c", V_SCW, V_SCB)
    xbc = xbc * _sigmoid(xbc)
    xs, bm, cm = xbc[:, 0:gw], xbc[:, gw:gw + LANES], xbc[:, gw + LANES:2 * gw]
    dt = jax.nn.softplus(pcols(9 * gw, 10 * gw) + vec[V_DTB:V_DTB + 1, 0:gw])
    a_e = -jnp.exp(vec[V_ALOG:V_ALOG + 1, 0:gw])
    dta = dt * a_e
    if ns == 1:
        cs = slabs(dta)
        for k in range(1, nk):
            cs[k] = cs[k] + cs[k - 1]
        e = cs[-1]
        for j in range(int(math.log2(SUBLANES))):
            e = e + sdown(e, 1 << j)
        cin = sdown(e, 1)
        acum = jnp.concatenate([ck + cin for ck in cs], axis=0)
        alast = acum[qt - 1:qt, :]
    else:
        acum = dta
        for k in range(nlev):
            acum = acum + down(acum, 1 << k)
        suf = dta
        for k in range(nlev):
            suf = suf + up(suf, 1 << k)
        alast = acum + suf - dta
    wend = jnp.exp(alast - acum) * dt
    acum_t = jnp.transpose(acum)
    lane = lax.broadcasted_iota(jnp.int32, (1, gw), 1)
    lane_b = lax.broadcasted_iota(jnp.int32, (1, LANES), 1)
    heads_per_group = gw // SSD_HD // SSD_G
    xdt = xs * dt
    xw = xs * wend
    ydiag = jnp.zeros((qt, gw), F32)
    for grp in range(SSD_G):
        gm = ((lane_b // SSD_N) == grp).astype(F32)
        cb = _mm_nt(cm * gm, bm)
        for hh in range(heads_per_group):
            hd = grp * heads_per_group + hh
            col = acum[:, hd * SSD_HD:hd * SSD_HD + 1]
            rowv = acum_t[hd * SSD_HD:hd * SSD_HD + 1, :]
            dec = jnp.where(tri, jnp.exp(col - rowv), 0.0)
            hm = ((lane // SSD_HD) == hd).astype(F32)
            ydiag = ydiag + _mm(cb * dec, xdt * hm)
    if ns == 1:
        st = g["c_sh"][...]
        srow = lax.broadcasted_iota(jnp.int32, st.shape, 0) // SSD_N
        yoff = _mm(cm, st)
        slane = lax.broadcasted_iota(jnp.int32, st.shape, 1) // (SSD_HD * heads_per_group)
        upd = jnp.where(srow == slane, _mm_tn(bm, xw), 0.0)
        st = st * jnp.exp(acum[qt - 1:qt, :]) + upd
        g["c_sh"][...] = st

        @pl.when(c == nc - 1)
        def _final_ssd_state():
            own = jnp.concatenate([st[grp * SSD_N:(grp + 1) * SSD_N, grp * LANES:(grp + 1) * LANES]
                                   for grp in range(SSD_G)], axis=1)
            own = jnp.concatenate([own, jnp.zeros_like(own)], axis=0)
            g["sh_o"][...] = jnp.transpose(own)[:, 0:SSD_N].reshape(g["sh_o"].shape)
    else:
        sh_i, sh_o = g["sh_i"], g["sh_o"]
        per_seq = gw // SSD_HD * SSD_HD
        cols = [[], []]
        for s in range(0, ns, 2):
            two = jnp.concatenate([sh_i[s].reshape(per_seq, SSD_N), sh_i[s + 1].reshape(per_seq, SSD_N)], axis=1)
            two = jnp.transpose(two)
            for grp in range(SSD_G):
                cols[grp] += [two[0:SSD_N, grp * LANES:(grp + 1) * LANES],
                              two[SSD_N:2 * SSD_N, grp * LANES:(grp + 1) * LANES]]
        st = jnp.concatenate([jnp.concatenate(cols[0], axis=1), jnp.concatenate(cols[1], axis=1)], axis=0)
        srow = lax.broadcasted_iota(jnp.int32, st.shape, 0) // SSD_N
        seqm = seq_mask(LANES)
        lastm = jnp.where(t == r - 1, seqm, 0.0)
        yo, upd, dl = [], None, None
        for grp in range(SSD_G):
            gm = ((lane_b // SSD_N) == grp).astype(F32)
            z = _mm(cm * gm, st) * seqm
            zf = z[:, 0:LANES]
            for s in range(1, ns):
                zf = zf + z[:, s * LANES:(s + 1) * LANES]
            yo.append(zf)
            sl = slice(grp * LANES, (grp + 1) * LANES)
            u_g = _mm_tn(bm, jnp.tile(xw[:, sl], (1, ns)) * seqm)
            d_g = jnp.exp(jnp.sum(jnp.tile(acum[:, sl], (1, ns)) * lastm, axis=0, keepdims=True))
            upd = u_g if grp == 0 else jnp.where(srow == grp, u_g, upd)
            dl = d_g if grp == 0 else jnp.where(srow == grp, d_g, dl)
        yoff = jnp.concatenate(yo, axis=1)
        st = st * dl + upd
        for s in range(0, ns, 2):
            two = jnp.concatenate(
                [jnp.concatenate([st[0:SSD_N, q * LANES:(q + 1) * LANES], st[SSD_N:2 * SSD_N, q * LANES:(q + 1) * LANES]],
                                 axis=1) for q in (s, s + 1)], axis=0)
            two = jnp.transpose(two)
            sh_o[s] = two[:, 0:SSD_N].reshape(sh_o.shape[1:])
            sh_o[s + 1] = two[:, SSD_N:2 * SSD_N].reshape(sh_o.shape[1:])
    yc = ydiag + yoff * jnp.exp(acum) + vec[V_SSDD:V_SSDD + 1, 0:gw] * xs
    z = pcols(3 * gw, 4 * gw)
    yc = yc * (z * _sigmoid(z))
    ysave(2 * gw, 3 * gw, _rms(yc) * gain[:, 2 * gw:3 * gw])

    trig = g["trig"]
    cos, sin_a, sin_b = trig[0], trig[1], trig[2]

    def rope(x):
        return x * cos + pltpu.roll(x, LANES - RET_DK // 2, 1) * sin_a + pltpu.roll(x, RET_DK // 2, 1) * sin_b

    q = rope(pcols(6 * gw, 6 * gw + LANES))
    kk = rope(pcols(6 * gw + LANES, 7 * gw)) * (RET_DK ** -0.5)
    v = pcols(7 * gw, 8 * gw)
    dv = gw // RET_H
    intra = jnp.zeros((qt, gw), F32)
    for hd in range(RET_H):
        qm = ((lane_b // RET_DK) == hd).astype(F32)
        sc = _mm_nt(q * qm, kk) * g["dmask"][hd]
        vm = ((lane // dv) == hd).astype(F32)
        intra = intra + _mm(sc, v * vm)
    dec = g["dec"]
    qd, kd = q * dec[0], kk * dec[1]
    if ns == 1:
        rs = g["c_ret"][...]
        cross = _mm(qd, rs)
        rrow = lax.broadcasted_iota(jnp.int32, rs.shape, 0) // RET_DK
        rlane = lax.broadcasted_iota(jnp.int32, rs.shape, 1) // dv
        upd = jnp.where(rrow == rlane, _mm_tn(kd, v), 0.0)
    else:
        ret_i, ret_o = g["ret_i"], g["ret_o"]
        hk = RET_H * RET_DK
        rs = jnp.concatenate([ret_i[s].reshape(hk, dv) for s in range(ns)], axis=1)
        seqm = seq_mask(dv)
        lo = lane_b < dv
        pairs = []
        upd = jnp.zeros(rs.shape, F32)
        for pp in range(RET_H // 2):
            vp = v[:, pp * LANES:(pp + 1) * LANES]
            vsw = pltpu.roll(vp, dv, 1)
            tots = []
            for hh in range(2):
                qm = ((lane_b // RET_DK) == 2 * pp + hh).astype(F32)
                zc = _mm(qd * qm, rs) * seqm
                acc = zc[:, 0:LANES]
                for s2 in range(1, ns // 2):
                    acc = acc + zc[:, s2 * LANES:(s2 + 1) * LANES]
                tots.append(acc + pltpu.roll(acc, dv, 1))
                vh = jnp.where(lo, vp, vsw) if hh == 0 else jnp.where(lo, vsw, vp)
                upd = upd + _mm_tn(kd * qm, jnp.tile(vh, (1, ns // 2)) * seqm)
            pairs.append(jnp.where(lo, tots[0], tots[1]))
        cross = jnp.concatenate(pairs, axis=1)
    rs = rs * dec[2][:, 0:1] + upd
    if ns == 1:
        g["c_ret"][...] = rs

        @pl.when(c == nc - 1)
        def _final_ret_state():
            own = functools.reduce(lambda s, x: s + x, [rs[:, hd * dv:(hd + 1) * dv] for hd in range(RET_H)])
            g["ret_o"][...] = own.reshape(g["ret_o"].shape)
    else:
        for s in range(ns):
            ret_o[s] = rs[:, s * dv:(s + 1) * dv].reshape(ret_o.shape[1:])
    o = intra + cross
    o2 = o * o
    ms = jnp.zeros((qt, gw), F32)
    for hd in range(RET_H):
        vm = ((lane // dv) == hd).astype(F32)
        ms = ms + vm * jnp.sum(o2 * vm, axis=-1, keepdims=True)
    o = o * lax.rsqrt(ms * (1.0 / dv) + EPS) * gain[:, 3 * gw:4 * gw]
    gg = pcols(8 * gw, 9 * gw)
    ysave(3 * gw, 4 * gw, gg * _sigmoid(gg) * o)
    if ns == 1:
        y_all = jnp.concatenate(y_parts, axis=1)
        g["y"][...] = jnp.dot(g["perm"][1], y_all, preferred_element_type=F32).astype(g["y"].dtype)


def _mix_consts(qt, ns, r, nc, pos0):
    assert qt == LANES, "decay tables are stacked as (3, 128, 128)"
    idx = np.arange(qt)
    nk = qt // SUBLANES
    if ns == 1:
        seq, tt = np.zeros_like(idx), (idx % SUBLANES) * nk + idx // SUBLANES
    else:
        seq, tt = idx % ns, idx // ns
    causal = (seq[:, None] == seq[None, :]) & (tt[:, None] >= tt[None, :])
    tri = causal.astype(np.float32)
    lg = np.log1p(-np.exp2(-5.0 - np.arange(RET_H, dtype=np.float64)))
    rel = (tt[:, None] - tt[None, :]).astype(np.float64)
    dmask = np.where(causal[None], np.exp(np.maximum(rel, 0.0)[None] * lg[:, None, None]), 0.0).astype(np.float32)
    lane_h = np.arange(LANES) // RET_DK
    qdec = np.exp((tt[:, None] + 1.0) * lg[lane_h][None, :])
    kdec = np.exp((r - 1.0 - tt[:, None]) * lg[lane_h][None, :])
    rdec = np.broadcast_to(np.exp(r * lg[lane_h])[:, None], (LANES, LANES))
    dec = np.stack([qdec, kdec, rdec]).astype(np.float32)
    half = RET_DK // 2
    pos = pos0 + jnp.asarray((np.arange(nc)[:, None] * qt + tt[None, :]).reshape(-1))
    inv = ROPE_BASE ** (-jnp.arange(half, dtype=F32) / half)
    ang = pos.astype(F32)[:, None] * inv
    reps = LANES // half
    cos, sin = jnp.tile(jnp.cos(ang), (1, reps)), jnp.tile(jnp.sin(ang), (1, reps))
    first = (np.arange(LANES) % RET_DK) < half
    trig = jnp.stack([cos, jnp.where(first, -sin, 0.0), jnp.where(first, 0.0, sin)])
    to_blocked = np.zeros((qt, qt), np.float32)
    to_blocked[idx, (idx % SUBLANES) * nk + idx // SUBLANES] = 1.0
    perm = jnp.asarray(np.stack([to_blocked, to_blocked.T])).astype(MXU_DTYPE)
    return dict(trig=trig, tri=jnp.asarray(tri), dmask=jnp.asarray(dmask), dec=jnp.asarray(dec), perm=perm)


def _state_layout(depth, bsz, ns, gw, sn):
    hk, dv = RET_H * RET_DK, gw // RET_H
    if ns == 1:
        def per_seq(rows, w):
            return (depth, bsz, rows, w), (None, None, rows, w), lambda b, c, l: (l[0], b, 0, 0)
        def matrix(h, rows, w):
            return (depth, bsz, h, rows, w), (None, None, h, rows, w), lambda b, c, l: (l[0], b, 0, 0, 0)
        return dict(lconv=per_seq(CONV_K - 1, gw), lh=per_seq(1, gw), s5r=per_seq(1, sn), s5i=per_seq(1, sn),
                    sconv=per_seq(CONV_K - 1, 2 * gw), sh=matrix(gw // SSD_HD, SSD_HD, SSD_N),
                    ret=matrix(RET_H, RET_DK, dv))

    def flat(per, w):
        return (depth, bsz * per, w), (None, ns * per, w), lambda b, c, l: (l[0], b, 0)

    def conv(w):
        return (depth, CONV_K - 1, bsz, w), (None, CONV_K - 1, ns, w), lambda b, c, l: (l[0], 0, b, 0)
    def matrix(h, rows, w):
        return (depth, bsz, h, rows, w), (None, ns, h, rows, w), lambda b, c, l: (l[0], b, 0, 0, 0)
    return dict(lconv=conv(gw), lh=flat(1, gw), s5r=flat(1, sn), s5i=flat(1, sn), sconv=conv(2 * gw),
                sh=matrix(gw // SSD_HD, SSD_HD, SSD_N), ret=matrix(RET_H, RET_DK, dv))


def _mixer(l, proj, consts, mw, state_in, state_out, *, nb, nc, qt, ns, r, d):
    dp = proj.shape[-1]
    gw = d // 4
    sn = mw["pw"].shape[-1]
    depth = mw["pw"].shape[0]
    layout = _state_layout(depth, nb * ns, ns, gw, sn)
    assert (state_in is None) == (ns == 1)

    def layer_spec(a):
        return pl.BlockSpec((None,) + a.shape[1:], lambda b, c, l: (l[0],) + (0,) * (a.ndim - 1))

    def const_spec(a):
        return pl.BlockSpec(a.shape, lambda b, c, l: (0,) * a.ndim)

    def state_spec(nm):
        _, block, index = layout[nm]
        return pl.BlockSpec(block, index)

    if ns == 1:
        io_spec = lambda w: pl.BlockSpec((qt, w), lambda b, c, l: (b * nc + c, 0))
        y_shape = (nb * nc * qt, d)
        cnames = ("trig", "tri", "dmask", "dec", "perm")
    else:
        proj = proj.reshape(r, nb * ns, dp)
        io_spec = lambda w: pl.BlockSpec((r, ns, w), lambda b, c, l: (0, b, 0))
        y_shape = (r, nb * ns, d)
        cnames = ("trig", "tri", "dmask", "dec")
    names = ("proj",) + cnames + MIX_WEIGHTS
    operands = [proj] + [consts[k] for k in cnames] + [mw[k] for k in MIX_WEIGHTS]
    in_specs = [io_spec(dp), pl.BlockSpec((3, qt, LANES), lambda b, c, l: (0, c, 0))] + \
               [const_spec(consts[k]) for k in cnames[1:]] + [layer_spec(mw[k]) for k in MIX_WEIGHTS]
    first_alias = 1 + len(operands)
    if state_in is not None:
        assert state_in is state_out
        names += tuple(nm + "_i" for nm in STATE_NAMES)
        in_specs += [state_spec(nm) for nm in STATE_NAMES]
    else:
        names += tuple(nm + "_alias" for nm in STATE_NAMES)
        in_specs += [pl.BlockSpec(memory_space=pl.ANY) for _ in STATE_NAMES]
    operands += [state_out[nm] for nm in STATE_NAMES]
    names += ("y",) + tuple(nm + "_o" for nm in STATE_NAMES)
    out_specs = [io_spec(d)] + [state_spec(nm) for nm in STATE_NAMES]
    out_shape = [jax.ShapeDtypeStruct(y_shape, MXU_DTYPE)] + \
                [jax.ShapeDtypeStruct(layout[nm][0], F32) for nm in STATE_NAMES]
    scratch = []
    if ns == 1:
        tail = (CONV_K - 1) * SUBLANES
        scratch_shapes = dict(cb_a=(tail, gw), cb_c=(tail, 2 * gw), c_lh=(1, gw), c_s5r=(1, sn),
                              c_s5i=(1, sn), c_sh=(SSD_G * SSD_N, gw), c_ret=(RET_H * RET_DK, gw))
        names += tuple(scratch_shapes)
        scratch = [pltpu.VMEM(shape, F32) for shape in scratch_shapes.values()]
    grid_spec = pltpu.PrefetchScalarGridSpec(num_scalar_prefetch=1, grid=(nb, nc),
                                             in_specs=in_specs, out_specs=out_specs, scratch_shapes=scratch)
    outs = pl.pallas_call(
        functools.partial(_mix_body, names=names, qt=qt, ns=ns, r=r, nc=nc),
        grid_spec=grid_spec,
        out_shape=out_shape,
        input_output_aliases={first_alias + k: 1 + k for k in range(len(STATE_NAMES))},
        compiler_params=pltpu.CompilerParams(dimension_semantics=("arbitrary", "arbitrary"),
                                             vmem_limit_bytes=VMEM_LIMIT),
        name="mixer",
    )(l, *operands)
    return outs[0].reshape(-1, d), dict(zip(STATE_NAMES, outs[1:]))


def _post_body(l_ref, y_ref, x_ref, g1_ref, sc_ref, sh_ref, g2_ref, lt_ref, wo_ref, wr_ref, br_ref,
               wg_ref, wu_ref, wd_ref, *rest):
    o_ref, xs_ref, cs_ref, os_ref = rest[-4:]
    del l_ref
    tm, d = x_ref.shape
    tms = xs_ref.shape[0]
    x = x_ref[...] + _token_rows(g1_ref[...], tm) * jnp.dot(y_ref[...], wo_ref[...], preferred_element_type=F32)
    h = _rms(x) * (1.0 + _token_rows(sc_ref[...], tm)) + _token_rows(sh_ref[...], tm)
    hb = h.astype(MXU_DTYPE)
    h_lo = (h - hb.astype(F32)).astype(MXU_DTYPE)
    wr = wr_ref[...]
    l12 = jnp.dot(hb, wr, preferred_element_type=F32)
    logits = (l12[:, 0:LANES] + l12[:, LANES:2 * LANES]
              + jnp.dot(h_lo, wr[:, 0:LANES], preferred_element_type=F32) + br_ref[...])
    logits_t = jnp.transpose(logits)
    col = [logits_t[k:k + 1, :] for k in range(N_GROUPS + N_EXPERTS)]

    def first_max(vals, allowed=None):
        neg = jnp.full_like(vals[0], -jnp.inf)
        cand = vals if allowed is None else [jnp.where(al > 0.0, v, neg) for v, al in zip(vals, allowed)]
        m = functools.reduce(jnp.maximum, cand)
        rem = jnp.ones_like(vals[0])
        hot = []
        for v in cand:
            f = jnp.where(v >= m, rem, 0.0)
            rem = rem - f
            hot.append(f)
        return hot, m

    grp, gmax = first_max(col[:N_GROUPS])
    gate = 1.0 / functools.reduce(lambda s, v: s + v, [jnp.exp(v - gmax) for v in col[:N_GROUPS]])
    le = [functools.reduce(lambda s, v: s + v,
                           [grp[gi] * col[N_GROUPS + gi * N_PER_GROUP + j] for gi in range(N_GROUPS)])
          for j in range(N_PER_GROUP)]
    top1, m1 = first_max(le)
    top2, m2 = first_max(le, [1.0 - f for f in top1])
    e2 = jnp.exp(m2 - m1)
    w1 = 1.0 / (1.0 + e2)
    w2 = e2 * w1
    wgrp = [(top1[j] * w1 + top2[j] * w2) * gate for j in range(N_PER_GROUP)]

    sub8 = lax.broadcasted_iota(jnp.int32, (SUBLANES, 1), 0)
    ghot_t = functools.reduce(lambda s, v: s + v, [jnp.where(sub8 == gi, grp[gi], 0.0) for gi in range(N_GROUPS)])
    before = _mm_nt(ghot_t, lt_ref[...])
    count = jnp.sum(ghot_t, axis=1, keepdims=True)
    cnt = [count[gi:gi + 1, :] for gi in range(N_GROUPS)]
    base = [jnp.zeros((1, 1), F32)]
    for gi in range(1, N_GROUPS):
        base.append(jnp.floor((base[-1] + cnt[gi - 1] + (PACK - 1.0)) * (1.0 / PACK)) * PACK)
    pos_row = functools.reduce(lambda s, v: s + v,
                               [grp[gi] * (base[gi] + before[gi:gi + 1, :]) for gi in range(N_GROUPS)])
    pos = jnp.transpose(jnp.broadcast_to(pos_row, (LANES, tm)))[:, 0:1]
    slot = lax.broadcasted_iota(jnp.int32, (tms, tm), 0).astype(F32)
    gather = jnp.where(slot == pos_row, 1.0, 0.0).astype(MXU_DTYPE)
    slot_l = lax.broadcasted_iota(jnp.int32, (tm, tms), 1).astype(F32)
    scatter = jnp.where(slot_l == pos, 1.0, 0.0).astype(MXU_DTYPE)

    xs_ref[...] = jnp.dot(gather, hb, preferred_element_type=F32).astype(xs_ref.dtype)
    sub_e = lax.broadcasted_iota(jnp.int32, (cs_ref.shape[1], 1), 0)
    comb_t = functools.reduce(lambda s, v: s + v,
                              [jnp.where(sub_e == e, grp[e // N_PER_GROUP] * wgrp[e % N_PER_GROUP], 0.0)
                               for e in range(N_EXPERTS)])
    c1, c2, c3 = _split3(comb_t)
    nt = lambda p: lax.dot_general(gather, p, (((1,), (1,)), ((), ())), preferred_element_type=F32)
    cs_ref[...] = nt(c1) + nt(c2) + nt(c3)
    os_ref[...] = jnp.zeros(os_ref.shape, os_ref.dtype)
    for gi in range(N_GROUPS):
        start = base[gi][0, 0].astype(jnp.int32)
        nblk = jnp.floor((cnt[gi][0, 0] + (EXPERT_ROWS - 1.0)) * (1.0 / EXPERT_ROWS)).astype(jnp.int32)

        def block(k, carry, gi=gi, start=start):
            rows = pl.ds(pl.multiple_of(start + k * EXPERT_ROWS, PACK), EXPERT_ROWS)
            xb = xs_ref[rows, :]
            cb = cs_ref[rows, :]
            acc = jnp.zeros((EXPERT_ROWS, d), F32)
            for j in range(N_PER_GROUP):
                e = gi * N_PER_GROUP + j
                gt = jnp.dot(xb, wg_ref[e], preferred_element_type=F32)
                he = gt * _sigmoid(gt) * jnp.dot(xb, wu_ref[e], preferred_element_type=F32)
                acc = acc + jnp.dot((he * cb[:, e:e + 1]).astype(MXU_DTYPE), wd_ref[e], preferred_element_type=F32)
            os_ref[rows, :] += acc
            return carry

        lax.fori_loop(0, nblk, block, 0)
    moe = jnp.dot(scatter, os_ref[...].astype(MXU_DTYPE), preferred_element_type=F32)
    out = x + _token_rows(g2_ref[...], tm) * moe
    if len(rest) == 5:
        out = _rms(out) * rest[0][...]
    o_ref[...] = out


def _post(l, y, x, mod, pw, tm, rows_per_seq, final_w=None, in_place=False):
    t, d = x.shape
    final = [] if final_w is None else [final_w.reshape(1, d)]
    tms = tm + EXPERT_ROWS + PACK * N_GROUPS
    ltri = jnp.asarray(np.tril(np.ones((tm, tm), np.float32), -1)).astype(MXU_DTYPE)

    def layer_spec(a):
        return pl.BlockSpec((None,) + a.shape[1:], lambda i, l: (l[0],) + (0,) * (a.ndim - 1),
                            pipeline_mode=pl.Buffered(1))

    grid_spec = pltpu.PrefetchScalarGridSpec(
        num_scalar_prefetch=1,
        grid=(t // tm,),
        in_specs=[
            pl.BlockSpec((tm, d), lambda i, l: (i, 0)),
            pl.BlockSpec((tm, d), lambda i, l: (i, 0)),
            _mod_spec(mod, 2, tm, rows_per_seq),
            _mod_spec(mod, 4, tm, rows_per_seq),
            _mod_spec(mod, 3, tm, rows_per_seq),
            _mod_spec(mod, 5, tm, rows_per_seq),
            pl.BlockSpec((tm, tm), lambda i, l: (0, 0)),
        ] + [layer_spec(pw[k]) for k in ("w_out", "w_route", "b_route", "w_gate", "w_up", "w_down")]
          + [pl.BlockSpec((1, d), lambda i, l: (0, 0)) for _ in final],
        out_specs=pl.BlockSpec((tm, d), lambda i, l: (i, 0)),
        scratch_shapes=[pltpu.VMEM((tms, d), MXU_DTYPE), pltpu.VMEM((tms, 2 * N_EXPERTS), F32),
                        pltpu.VMEM((tms, d), F32)],
    )
    return pl.pallas_call(
        _post_body,
        grid_spec=grid_spec,
        out_shape=jax.ShapeDtypeStruct((t, d), F32),
        input_output_aliases={2: 0} if in_place else {},
        compiler_params=pltpu.CompilerParams(dimension_semantics=("arbitrary",), vmem_limit_bytes=VMEM_LIMIT),
        name="post",
    )(l, y, x, mod, mod, mod, mod, ltri,
      *[pw[k] for k in ("w_out", "w_route", "b_route", "w_gate", "w_up", "w_down")], *final)


def _block_diag(w):
    depth, nblk, bi, bj = w.shape
    eye = jnp.eye(nblk, dtype=w.dtype)
    return (w[:, :, :, None, :] * eye[None, :, None, :, None]).reshape(depth, nblk * bi, nblk * bj)


def _pad_lanes(a, width):
    return jnp.pad(a, [(0, 0)] * (a.ndim - 1) + [(0, width - a.shape[-1])])


def _prep_weights(w_in, lru_conv_w, lru_conv_b, lru_wa, lru_ba, lru_wx, lru_bx, lru_lambda,
                  s5_a_re, s5_a_im, s5_b_re, s5_b_im, s5_c_re, s5_c_im, s5_d, s5_log_dt, s5_w_glu,
                  ssd_conv_w, ssd_conv_b, ssd_dt_bias, ssd_a_log, ssd_d, mix_norm, qt):
    depth, d, _ = w_in.shape
    gw = d // 4
    n_ssd_h = gw // SSD_HD
    xbc_end = 4 * gw + gw + 2 * SSD_G * SSD_N
    w_in = w_in.astype(MXU_DTYPE)
    dt_cols = jnp.repeat(w_in[..., xbc_end:xbc_end + n_ssd_h], SSD_HD, axis=-1)
    w_in_p = jnp.concatenate([w_in[..., :xbc_end], w_in[..., xbc_end + n_ssd_h:], dt_cols], axis=-1)

    def row(a):
        return _pad_lanes(a.reshape(depth, 1, -1), d)

    vec = jnp.concatenate([
        row(lru_conv_b), row(jnp.concatenate([lru_ba, lru_bx], -1)), row(lru_lambda), row(s5_d),
        row(ssd_conv_b), row(jnp.repeat(ssd_dt_bias, SSD_HD, -1)), row(jnp.repeat(ssd_a_log, SSD_HD, -1)),
        row(jnp.repeat(ssd_d, SSD_HD, -1)), row(mix_norm),
        _pad_lanes(lru_conv_w, d), _pad_lanes(ssd_conv_w, d),
        jnp.zeros((depth, V_ROWS - V_SCW - CONV_K, d), F32)], axis=1)

    wgate = jnp.concatenate([_block_diag(lru_wa), _block_diag(lru_wx)], axis=-1)

    dt = jnp.exp(s5_log_dt)[..., None]
    lr, li = s5_a_re, s5_a_im
    mag = jnp.exp(lr * dt)
    ab_re, ab_im = mag * jnp.cos(li * dt), mag * jnp.sin(li * dt)
    den = lr * lr + li * li
    q_re = ((ab_re - 1.0) * lr + ab_im * li) / den
    q_im = (ab_im * lr - (ab_re - 1.0) * li) / den
    bb_re = q_re[..., None] * s5_b_re - q_im[..., None] * s5_b_im
    bb_im = q_re[..., None] * s5_b_im + q_im[..., None] * s5_b_re
    bbm = jnp.concatenate([_block_diag(jnp.swapaxes(bb_re, -1, -2)), _block_diag(jnp.swapaxes(bb_im, -1, -2))], -1)
    cre = _block_diag(jnp.swapaxes(s5_c_re, -1, -2))
    cim = _block_diag(jnp.swapaxes(s5_c_im, -1, -2))
    a_re, a_im = ab_re.reshape(depth, -1), ab_im.reshape(depth, -1)
    pr, pi = a_re, a_im
    rows = []
    for _ in range(int(math.log2(qt))):
        rows += [pr, pi]
        pr, pi = pr * pr - pi * pi, 2.0 * pr * pi
    pw = jnp.stack(rows, axis=1)
    pw = jnp.pad(pw, ((0, 0), (0, (-pw.shape[1]) % SUBLANES), (0, 0)))
    pr, pi = a_re, a_im
    k_re, k_im = [], []
    for _ in range(qt // SUBLANES):
        k_re.append(pr)
        k_im.append(pi)
        pr, pi = pr * a_re - pi * a_im, pr * a_im + pi * a_re
    pk = jnp.stack(k_re + k_im, axis=1)

    mw = dict(vec=vec, wgate=wgate.astype(MXU_DTYPE), bbm=bbm.astype(MXU_DTYPE), cre=cre.astype(MXU_DTYPE),
              cim=cim.astype(MXU_DTYPE), wglu=s5_w_glu.astype(MXU_DTYPE), pw=pw, pk=pk)
    return w_in_p, mw


def _states_to_layout(states, layout, ns):
    out = {}
    for nm, s in zip(STATE_NAMES, states):
        if nm in ("lconv", "sconv"):
            s = s.transpose(0, 2, 1, 3)
        out[nm] = s.reshape(layout[nm][0])
    return out


def _states_from_layout(arrs, ns, ref_shapes):
    out = []
    for nm, shape in zip(STATE_NAMES, ref_shapes):
        a = arrs[nm]
        if ns > 1 and nm in ("lconv", "sconv"):
            a = a.transpose(0, 2, 1, 3)
        out.append(a.reshape(shape))
    return tuple(out)


def _trunk(x, mod, states, ref_shapes, pos0, w_in_p, mw, pw, final_norm, *, qt, ns, tm, tm_in):
    bsz, seq_len, d = x.shape
    depth = w_in_p.shape[0]
    gw = d // 4
    r = qt // ns
    nc = seq_len // r if ns == 1 else 1
    assert (ns == 1 and seq_len % qt == 0 and states is None) or (ns > 1 and r == seq_len and bsz % ns == 0)
    assert r >= CONV_K and (r & (r - 1)) == 0
    nb = bsz // ns
    tok = bsz * seq_len
    consts = _mix_consts(qt, ns, r, nc, pos0)
    layout = _state_layout(depth, bsz, ns, gw, mw["pw"].shape[-1])
    if states is None:
        state_out = {nm: jnp.zeros(layout[nm][0], F32) for nm in STATE_NAMES}
    else:
        state_out = _states_to_layout(states, layout, ns)
    x2 = x.reshape(tok, d) if ns == 1 else x.transpose(1, 0, 2).reshape(tok, d)
    blocked_chunk = qt if ns == 1 else None

    def layer(carry, l, final_w=None, in_place=True):
        xc, st = carry
        lv = jnp.reshape(l, (1,)).astype(jnp.int32)
        proj = _inproj(lv, xc, mod, w_in_p, tm_in, seq_len, blocked_chunk)
        y, st = _mixer(lv, proj, consts, mw, None if states is None else st, st,
                       nb=nb, nc=nc, qt=qt, ns=ns, r=r, d=d)
        xn = _post(lv, y, xc, mod, pw, tm, seq_len, final_w, in_place)
        return (xn, st), None

    assert depth >= 2
    carry, _ = layer((x2, state_out), jnp.int32(0), in_place=False)
    carry, _ = lax.scan(layer, carry, jnp.arange(1, depth - 1))
    (y, st), _ = layer(carry, jnp.int32(depth - 1), final_norm)
    y = y.reshape(bsz, seq_len, d) if ns == 1 else y.reshape(seq_len, bsz, d).transpose(1, 0, 2)
    return y, _states_from_layout(st, ns, ref_shapes)


def kernel(x_prompt, x_sample, state_lru_conv, state_lru_h, state_s5_re, state_s5_im, state_ssd_conv, state_ssd_h, state_ret, c_prompt, c_sample, w_ada, b_ada, w_in, lru_conv_w, lru_conv_b, lru_wa, lru_ba, lru_wx, lru_bx, lru_lambda, s5_a_re, s5_a_im, s5_b_re, s5_b_im, s5_c_re, s5_c_im, s5_d, s5_log_dt, s5_w_glu, ssd_conv_w, ssd_conv_b, ssd_dt_bias, ssd_a_log, ssd_d, mix_norm, w_out, w_route_group, b_route_group, w_route_exp, b_route_exp, w_exp_gate, w_exp_up, w_exp_down, final_norm):
    bp, lp, d = x_prompt.shape
    bs, ls, _ = x_sample.shape
    depth = w_in.shape[0]
    qt = LANES
    ns_s = qt // ls

    mod_p, mod_s = _modulation(jnp.concatenate([c_prompt, c_sample], axis=0), bp, w_ada, b_ada)
    mod_p = mod_p.reshape(depth, N_MOD, bp, 1, d)

    w_in_p, mw = _prep_weights(w_in, lru_conv_w, lru_conv_b, lru_wa, lru_ba, lru_wx, lru_bx, lru_lambda,
                               s5_a_re, s5_a_im, s5_b_re, s5_b_im, s5_c_re, s5_c_im, s5_d, s5_log_dt, s5_w_glu,
                               ssd_conv_w, ssd_conv_b, ssd_dt_bias, ssd_a_log, ssd_d, mix_norm, qt)
    w_route = _pad_lanes(jnp.concatenate([w_route_group, w_route_exp.reshape(depth, d, N_EXPERTS)], -1), LANES)
    b_route = _pad_lanes(jnp.concatenate([b_route_group, b_route_exp.reshape(depth, N_EXPERTS)], -1), LANES)
    w_route_hi = w_route.astype(MXU_DTYPE)
    w_route_lo = (w_route - w_route_hi.astype(F32)).astype(MXU_DTYPE)
    w_route = jnp.concatenate([w_route_hi, w_route_lo], axis=-1)
    pw = dict(w_out=w_out.astype(MXU_DTYPE), w_route=w_route, b_route=b_route.reshape(depth, 1, LANES),
              w_gate=w_exp_gate.astype(MXU_DTYPE), w_up=w_exp_up.astype(MXU_DTYPE),
              w_down=w_exp_down.astype(MXU_DTYPE))
    w_in_p, mw, pw = lax.optimization_barrier((w_in_p, mw, pw))

    states_s = (state_lru_conv, state_lru_h, state_s5_re, state_s5_im, state_ssd_conv, state_ssd_h, state_ret)
    shapes_s = [s.shape for s in states_s]
    shapes_p = [(depth, bp) + s[2:] for s in shapes_s]
    y_p, new_p = _trunk(x_prompt, mod_p, None, shapes_p, 0, w_in_p, mw, pw, final_norm,
                        qt=qt, ns=1, tm=min(512, lp), tm_in=min(1024, lp))
    y_s, new_s = _trunk(x_sample, mod_s, states_s, shapes_s, PAST_LEN, w_in_p, mw, pw, final_norm,
                        qt=qt, ns=ns_s, tm=bs * ls, tm_in=bs * ls)
    out = [y_p, y_s]
    for a, b in zip(new_p, new_s):
        out += [a, b]
    return tuple(out)
```
